```python
import math
import jax, jax.numpy as jnp
from jax import lax
import numpy as np

D_MODEL = 1024
BATCH = 8
SEQ = 2048
DEPTH = 2

RET_HEADS = 4
RET_DK = D_MODEL // RET_HEADS
RET_DV = 2 * RET_DK
RET_CHUNK = 128
ROPE_BASE = 10000.0
HGRN_HEADS = 8
HGRN_DK = D_MODEL // HGRN_HEADS
HGRN_DV = D_MODEL // HGRN_HEADS
HGRN_CHUNK = 64
LB_FLOOR = 1e-30
FNET_GROUPS = 4
FNET_WIDTH = D_MODEL
FNET_GROUP_DIM = FNET_WIDTH // FNET_GROUPS
D_FF = 2816
CONV_W = 3
N_BRANCH = 3
EPS = 1e-6

RET_QK_W = RET_HEADS * RET_DK
RET_V_W = RET_HEADS * RET_DV
HGRN_K_W = HGRN_HEADS * HGRN_DK
HGRN_V_W = HGRN_HEADS * HGRN_DV
D_IN = 2 * RET_QK_W + 2 * RET_V_W + 3 * HGRN_K_W + 2 * HGRN_V_W + FNET_WIDTH + N_BRANCH * D_MODEL

kernel_name = "hybrid_retention_hgrn2_fnet_encoder"


def _split_points():
    widths = (RET_QK_W, RET_QK_W, RET_V_W, RET_V_W,
              HGRN_K_W, HGRN_K_W, HGRN_K_W, HGRN_V_W, HGRN_V_W,
              FNET_WIDTH, N_BRANCH * D_MODEL)
    return tuple(int(c) for c in np.cumsum(widths)[:-1])


def _rms(x):
    xf = x.astype(jnp.float32)
    return xf * lax.rsqrt(jnp.mean(xf * xf, axis=-1, keepdims=True) + EPS)


def rms_norm(x, w):
    return (_rms(x) * w.astype(jnp.float32)).astype(x.dtype)


def rotary(x, positions):
    half = x.shape[-1] // 2
    inv_freq = ROPE_BASE ** (-jnp.arange(half, dtype=jnp.float32) / half)
    ang = positions.astype(jnp.float32)[..., None] * inv_freq
    cos = jnp.cos(ang)[:, :, None, :]
    sin = jnp.sin(ang)[:, :, None, :]
    x1 = x[..., :half].astype(jnp.float32)
    x2 = x[..., half:].astype(jnp.float32)
    return jnp.concatenate([x1 * cos - x2 * sin, x1 * sin + x2 * cos], axis=-1)


def retention_past(qc, kc, vc, log_gamma):
    B, H, nC, C, dk = qc.shape
    dv = vc.shape[-1]
    pos = jnp.arange(C, dtype=jnp.float32)
    q_dec = jnp.exp(log_gamma[:, None] * (pos + 1.0))[None, :, :, None]
    k_dec = jnp.exp(log_gamma[:, None] * (C - 1.0 - pos))[None, :, :, None]
    chunk_dec = jnp.exp(log_gamma * C)[None, :, None, None]

    def step(state, xs):
        q, k, v = xs
        out = jnp.einsum('bhcd,bhde->bhce', q * q_dec, state)
        state = state * chunk_dec + jnp.einsum('bhcd,bhce->bhde', k * k_dec, v)
        return state, out

    init = jnp.zeros((B, H, dk, dv), jnp.float32)
    xs = (jnp.moveaxis(qc, 2, 0), jnp.moveaxis(kc, 2, 0), jnp.moveaxis(vc, 2, 0))
    _, out = lax.scan(step, init, xs)
    return jnp.moveaxis(out, 0, 2)


def retention(q, k, v, log_gamma):
    B, S, H, _ = q.shape
    C = RET_CHUNK
    nC = S // C

    def chunk(t):
        return t.astype(jnp.float32).reshape(B, nC, C, H, -1).transpose(0, 3, 1, 2, 4)

    qc, kc, vc = chunk(q), chunk(k), chunk(v)
    pos = jnp.arange(C, dtype=jnp.float32)
    dist = jnp.abs(pos[:, None] - pos[None, :])
    decay = jnp.exp(log_gamma[:, None, None] * dist)
    scores = jnp.einsum('bhnid,bhnjd->bhnij', qc, kc) * decay[None, :, None]
    intra = jnp.einsum('bhnij,bhnje->bhnie', scores, vc)

    def flip(t):
        return t[:, :, ::-1, ::-1]

    past = retention_past(qc, kc, vc, log_gamma)
    future = flip(retention_past(flip(qc), flip(kc), flip(vc), log_gamma))
    out = intra + past + future
    return out.transpose(0, 2, 3, 1, 4).reshape(B, S, H, -1)


def hgrn2_scan(q, k, log_f, v):
    B, S, H, dk = q.shape
    dv = v.shape[-1]
    C = HGRN_CHUNK
    nC = S // C

    def chunk(t):
        return t.reshape(B, nC, C, H, -1).transpose(1, 0, 3, 2, 4)

    causal = jnp.tril(jnp.ones((C, C), dtype=bool))[None, None, :, :, None]

    def step(state, xs):
        qc, kc, gc, vc = xs
        b = jnp.cumsum(gc, axis=2)
        diff = b[:, :, :, None, :] - b[:, :, None, :, :]
        pair_dec = jnp.where(causal, jnp.exp(jnp.where(causal, diff, 0.0)), 0.0)
        attn = jnp.einsum('bhtsd,bhtd,bhsd->bhts', pair_dec, qc, kc)
        intra = jnp.einsum('bhts,bhse->bhte', attn, vc)
        inter = jnp.einsum('bhtd,bhde->bhte', qc * jnp.exp(b), state)
        b_last = b[:, :, -1:, :]
        state = (state * jnp.exp(b_last)[:, :, 0, :, None]
                 + jnp.einsum('bhsd,bhse->bhde', kc * jnp.exp(b_last - b), vc))
        return state, intra + inter

    init = jnp.zeros((B, H, dk, dv), jnp.float32)
    _, out = lax.scan(step, init, (chunk(q), chunk(k), chunk(log_f), chunk(v)))
    return out.transpose(1, 0, 3, 2, 4).reshape(B, S, H, dv)


def hgrn2_gate(z, lb):
    zf = z.astype(jnp.float32)
    log_lb = jnp.log(jnp.maximum(lb, LB_FLOOR))
    log_f = jnp.logaddexp(jax.nn.log_sigmoid(zf), log_lb + jax.nn.log_sigmoid(-zf))
    k = (1.0 - lb) * jax.nn.sigmoid(-zf)
    return log_f, k


def hgrn2_lower_bounds(lb_logits):
    p = jax.nn.softmax(lb_logits.astype(jnp.float32), axis=1)
    return jnp.cumsum(p, axis=1) - p[:, :1]


def fourier_mix(u):
    B, S, _ = u.shape
    ug = u.astype(jnp.float32).reshape(B, S, FNET_GROUPS, FNET_GROUP_DIM)
    y = jnp.fft.fft2(ug, axes=(1, 3), norm='ortho').real
    return y.reshape(B, S, FNET_WIDTH).astype(u.dtype)


def conv_ffn(x, w_up, conv_w, conv_b, w_down):
    h = x @ w_up
    pad = CONV_W // 2
    hp = jnp.pad(h, ((0, 0), (pad, pad), (0, 0)))
    S = h.shape[1]
    hc = conv_b
    for j in range(CONV_W):
        hc = hc + hp[:, j:j + S] * conv_w[j]
    gate, up = jnp.split(hc, 2, axis=-1)
    return (jax.nn.gelu(gate, approximate=True) * up) @ w_down


def setup_inputs(seed: int = 0) -> dict:
    key = jax.random.key(seed)
    ks = jax.random.split(key, 14)
    f32 = jnp.float32

    def nrm(k, shape, fan_in):
        return jax.random.normal(k, shape, f32) * (fan_in ** -0.5)

    x = jax.random.normal(ks[0], (BATCH, SEQ, D_MODEL), f32)
    positions = jnp.tile(jnp.arange(SEQ, dtype=jnp.int32)[None, :], (BATCH, 1))
    norm_w = 1.0 + 0.05 * jax.random.normal(ks[1], (DEPTH, 4, D_MODEL), f32)
    w_in = nrm(ks[2], (DEPTH, D_MODEL, D_IN), D_MODEL)
    hgrn_lb_logits = jax.random.normal(ks[3], (2, DEPTH, HGRN_K_W), f32)
    hgrn_norm_w = 1.0 + 0.05 * jax.random.normal(ks[4], (DEPTH, HGRN_DV), f32)
    w_ret_o = nrm(ks[5], (DEPTH, RET_V_W, D_MODEL), RET_V_W)
    w_hgrn_o = nrm(ks[6], (DEPTH, HGRN_V_W, D_MODEL), HGRN_V_W)
    w_fnet = nrm(ks[7], (DEPTH, FNET_WIDTH, D_MODEL), FNET_WIDTH)
    w_out = nrm(ks[8], (DEPTH, D_MODEL, D_MODEL), D_MODEL)
    w_up = nrm(ks[9], (DEPTH, D_MODEL, 2 * D_FF), D_MODEL)
    conv_w = nrm(ks[10], (DEPTH, CONV_W, 2 * D_FF), CONV_W)
    conv_b = 0.01 * jax.random.normal(ks[11], (DEPTH, 2 * D_FF), f32)
    w_down = nrm(ks[12], (DEPTH, D_FF, D_MODEL), D_FF)
    return {"x": x, "positions": positions, "norm_w": norm_w, "w_in": w_in,
            "hgrn_lb_logits": hgrn_lb_logits, "hgrn_norm_w": hgrn_norm_w,
            "w_ret_o": w_ret_o, "w_hgrn_o": w_hgrn_o, "w_fnet": w_fnet, "w_out": w_out,
            "w_up": w_up, "conv_w": conv_w, "conv_b": conv_b, "w_down": w_down}


def reference(x, positions, norm_w, w_in, hgrn_lb_logits, hgrn_norm_w,
              w_ret_o, w_hgrn_o, w_fnet, w_out, w_up, conv_w, conv_b, w_down):
    B, S, _ = x.shape
    dt = x.dtype
    split_points = _split_points()
    log_gamma = jnp.log(1.0 - 2.0 ** (-5.0 - jnp.arange(RET_HEADS, dtype=jnp.float32)))
    lower_bounds = hgrn2_lower_bounds(hgrn_lb_logits)
    hgrn_scale = HGRN_DK ** -0.5
    ret_scale = RET_DK ** -0.5

    def flip(t):
        return t[:, ::-1]

    for l in range(DEPTH):
        xn = rms_norm(x, norm_w[l, 0])
        u = xn @ w_in[l]
        (rq, rk, rv, rg, hq, hz_f, hz_b, hi, hg, fu, ga) = jnp.split(u, split_points, axis=-1)

        q = rotary(rq.reshape(B, S, RET_HEADS, RET_DK), positions)
        k = rotary(rk.reshape(B, S, RET_HEADS, RET_DK), positions) * ret_scale
        ro = retention(q, k, rv.reshape(B, S, RET_HEADS, RET_DV), log_gamma)
        ro = _rms(ro).reshape(B, S, RET_V_W).astype(dt) * jax.nn.silu(rg)
        y_ret = ro @ w_ret_o[l]

        hqh = jax.nn.silu(hq.astype(jnp.float32)).reshape(B, S, HGRN_HEADS, HGRN_DK) * hgrn_scale
        hih = hi.astype(jnp.float32).reshape(B, S, HGRN_HEADS, HGRN_DV)
        lf_f, k_f = hgrn2_gate(hz_f.reshape(B, S, HGRN_HEADS, HGRN_DK),
                               lower_bounds[0, l].reshape(HGRN_HEADS, HGRN_DK))
        lf_b, k_b = hgrn2_gate(hz_b.reshape(B, S, HGRN_HEADS, HGRN_DK),
                               lower_bounds[1, l].reshape(HGRN_HEADS, HGRN_DK))
        ho_f = hgrn2_scan(hqh, k_f, lf_f, hih)
        ho_b = flip(hgrn2_scan(flip(hqh), flip(k_b), flip(lf_b), flip(hih)))
        ho = _rms(ho_f + ho_b) * hgrn_norm_w[l].astype(jnp.float32)
        ho = ho.reshape(B, S, HGRN_V_W).astype(dt) * jax.nn.silu(hg)
        y_hgrn = ho @ w_hgrn_o[l]

        y_fft = fourier_mix(fu) @ w_fnet[l]

        g_ret, g_hgrn, g_fft = jnp.split(jax.nn.sigmoid(ga), N_BRANCH, axis=-1)
        mix = (g_ret * y_ret + g_hgrn * y_hgrn + g_fft * y_fft) @ w_out[l]
        x = x + rms_norm(mix, norm_w[l, 1])

        hn = rms_norm(x, norm_w[l, 2])
        x = x + rms_norm(conv_ffn(hn, w_up[l], conv_w[l], conv_b[l], w_down[l]), norm_w[l, 3])
    return x
```

```python
import functools
import math

import numpy as np
import jax
import jax.numpy as jnp
from jax import lax
from jax.experimental import pallas as pl
from jax.experimental.pallas import tpu as pltpu

F32 = jnp.float32
BF16 = jnp.bfloat16

RET_HEADS = 4
HGRN_HEADS = 8
FNET_GROUPS = 4
N_BRANCH = 3
CONV_W = 3
ROPE_BASE = 10000.0
LB_FLOOR = 1e-30
EPS = 1e-6

V7X_VMEM_LIMIT_BYTES = 56 * 1024 * 1024
SUBLANES = 8
BF16_ROWS = 16

RET_CHUNK = 256
HGRN_CHUNK = 128
ROW_TILE = 256


def _dot(a, b):
    return jnp.dot(a, b, preferred_element_type=F32)


def _dot_nt(a, b):
    return lax.dot_general(a, b, (((1,), (1,)), ((), ())), preferred_element_type=F32)


def _dot_tn(a, b):
    return lax.dot_general(a, b, (((0,), (0,)), ((), ())), preferred_element_type=F32)


def _rms(x):
    return x * lax.rsqrt(jnp.mean(x * x, axis=-1, keepdims=True) + EPS)


def _params(*sem):
    return pltpu.CompilerParams(dimension_semantics=sem,
                                vmem_limit_bytes=V7X_VMEM_LIMIT_BYTES)


def _resident(shape, index_map):
    return pl.BlockSpec(shape, index_map, pipeline_mode=pl.Buffered(1))


def _rope_kernel(pos_ref, invf_ref, cos_ref, sin_ref):
    ang = pos_ref[...] * invf_ref[...]
    cos_ref[...] = jnp.cos(ang)
    sin_ref[...] = jnp.sin(ang)


def rope_tables(positions, half):
    B, S = positions.shape
    pos = positions.astype(F32).reshape(B, S, 1)
    inv_freq = (ROPE_BASE ** (-jnp.arange(half, dtype=F32) / half)).reshape(1, half)
    out = jax.ShapeDtypeStruct((B, S, half), F32)
    return pl.pallas_call(
        _rope_kernel,
        out_shape=(out, out),
        grid=(B,),
        in_specs=[pl.BlockSpec((None, S, 1), lambda b: (b, 0, 0)),
                  pl.BlockSpec((1, half), lambda b: (0, 0))],
        out_specs=(pl.BlockSpec((None, S, half), lambda b: (b, 0, 0)),
                   pl.BlockSpec((None, S, half), lambda b: (b, 0, 0))),
        compiler_params=_params("parallel"),
        name="rope_tables",
    )(pos, inv_freq)


def _norm_kernel(x_ref, w_ref, o_ref):
    o_ref[...] = (_rms(x_ref[...]) * w_ref[...]).astype(o_ref.dtype)


def rms_norm_bf16(x2, w, tm=1024):
    T, D = x2.shape
    return pl.pallas_call(
        _norm_kernel,
        out_shape=jax.ShapeDtypeStruct((T, D), BF16),
        grid=(T // tm,),
        in_specs=[pl.BlockSpec((tm, D), lambda i: (i, 0)),
                  pl.BlockSpec((1, D), lambda i: (0, 0))],
        out_specs=pl.BlockSpec((tm, D), lambda i: (i, 0)),
        compiler_params=_params("parallel"),
        name="rms_norm",
    )(x2, w.reshape(1, D))


def _ret_kernel(lg_ref, xn_ref, wq_ref, wk_ref, wv_ref, wg_ref, cos_ref, sin_ref,
                o_ref, q_s, k_s, v_s, g_s, gst_s, sf_s, *, seq, dk, dv):
    C = RET_CHUNK
    R = seq // C
    half = dk // 2
    lg = lg_ref[pl.program_id(1)]
    ret_scale = dk ** -0.5

    def rows_of(n):
        return pl.ds(pl.multiple_of(n * C, C), C)

    def proj(n, carry):
        rows = rows_of(n)
        xc = xn_ref[rows, :]
        cos = cos_ref[rows, :]
        sin = sin_ref[rows, :]
        q = _dot(xc, wq_ref[...])
        q1, q2 = q[:, :half], q[:, half:]
        q_s[rows, :] = jnp.concatenate([q1 * cos - q2 * sin, q1 * sin + q2 * cos], axis=-1)
        k = _dot(xc, wk_ref[...]) * ret_scale
        k1, k2 = k[:, :half], k[:, half:]
        k_s[rows, :] = jnp.concatenate([k1 * cos - k2 * sin, k1 * sin + k2 * cos], axis=-1)
        v_s[rows, :] = _dot(xc, wv_ref[...]).astype(BF16)
        g = _dot(xc, wg_ref[...])
        g_s[rows, :] = (g * jax.nn.sigmoid(g)).astype(BF16)
        return carry

    lax.fori_loop(0, R, proj, 0)

    pos = lax.broadcasted_iota(jnp.int32, (C, 1), 0).astype(F32)
    qdec_f = jnp.exp(lg * (pos + 1.0))
    qdec_b = jnp.exp(lg * (C - pos))
    kdec_f = jnp.exp(lg * (C - 1.0 - pos))
    kdec_b = jnp.exp(lg * pos)
    chunk_dec = jnp.exp(lg * C)
    ii = lax.broadcasted_iota(jnp.int32, (C, C), 0)
    jj = lax.broadcasted_iota(jnp.int32, (C, C), 1)
    decay = jnp.exp(lg * jnp.abs(ii - jj).astype(F32))

    def bwd(t, g_state):
        n = R - 1 - t
        rows = rows_of(n)
        gst_s[n] = g_state.astype(BF16)
        kb = (k_s[rows, :] * kdec_b).astype(BF16)
        return g_state * chunk_dec + _dot_tn(kb, v_s[rows, :])

    lax.fori_loop(0, R, bwd, jnp.zeros((dk, dv), F32))

    sf_s[...] = jnp.zeros_like(sf_s)

    def fwd(n, carry):
        rows = rows_of(n)
        q = q_s[rows, :]
        k = k_s[rows, :]
        v = v_s[rows, :]
        s = _dot_nt(q.astype(BF16), k.astype(BF16)) * decay
        out = _dot(s.astype(BF16), v)
        out += _dot((q * qdec_f).astype(BF16), sf_s[...].astype(BF16))
        out += _dot((q * qdec_b).astype(BF16), gst_s[n])
        sf_s[...] = sf_s[...] * chunk_dec + _dot_tn((k * kdec_f).astype(BF16), v)
        o_ref[rows, :] = (_rms(out) * g_s[rows, :].astype(F32)).astype(o_ref.dtype)
        return carry

    lax.fori_loop(0, R, fwd, 0)


def retention_branch(xn3, w_in_b, layer, cos, sin, log_gamma):
    B, S, D = xn3.shape
    dk = D // RET_HEADS
    dv = 2 * dk
    H = RET_HEADS
    qk_blocks = D // dk
    v_off = 2 * D // dv
    g_off = v_off + H
    kern = functools.partial(_ret_kernel, seq=S, dk=dk, dv=dv)
    return pl.pallas_call(
        kern,
        out_shape=jax.ShapeDtypeStruct((B, S, H * dv), BF16),
        grid=(B, H),
        in_specs=[
            pl.BlockSpec(memory_space=pltpu.SMEM),
            pl.BlockSpec((None, S, D), lambda b, h: (b, 0, 0)),
            pl.BlockSpec((None, D, dk), lambda b, h: (layer, 0, h)),
            pl.BlockSpec((None, D, dk), lambda b, h: (layer, 0, qk_blocks + h)),
            pl.BlockSpec((None, D, dv), lambda b, h: (layer, 0, v_off + h)),
            pl.BlockSpec((None, D, dv), lambda b, h: (layer, 0, g_off + h)),
            pl.BlockSpec((None, S, dk // 2), lambda b, h: (b, 0, 0)),
            pl.BlockSpec((None, S, dk // 2), lambda b, h: (b, 0, 0)),
        ],
        out_specs=pl.BlockSpec((None, S, dv), lambda b, h: (b, 0, h)),
        scratch_shapes=[
            pltpu.VMEM((S, dk), F32),
            pltpu.VMEM((S, dk), F32),
            pltpu.VMEM((S, dv), BF16),
            pltpu.VMEM((S, dv), BF16),
            pltpu.VMEM((S // RET_CHUNK, dk, dv), BF16),
            pltpu.VMEM((dk, dv), F32),
        ],
        compiler_params=_params("parallel", "arbitrary"),
        name="retention",
    )(log_gamma, xn3, w_in_b, w_in_b, w_in_b, w_in_b, cos, sin)


def _hgrn_gate(z, lb):
    e = jnp.exp(-jnp.abs(z))
    log_sig = jnp.minimum(z, 0.0) - jnp.log1p(e)
    log_sig_neg = log_sig - z
    c = jnp.log(jnp.maximum(lb, LB_FLOOR)) + log_sig_neg
    log_f = jnp.maximum(log_sig, c) + jnp.log1p(jnp.exp(-jnp.abs(log_sig - c)))
    sig_neg = jnp.where(z >= 0.0, e, 1.0) / (1.0 + e)
    return log_f, (1.0 - lb) * sig_neg


def _split3_bf16(x):
    hi = x.astype(BF16)
    r = x - hi.astype(F32)
    mid = r.astype(BF16)
    lo = (r - mid.astype(F32)).astype(BF16)
    return hi, mid, lo


def _hgrn_chunk(q, k, lf, v, state_t, b_s, consts, reverse):
    C = HGRN_CHUNK
    tri, row_in_group, level_masks = consts
    hi, mid, lo = _split3_bf16(lf)
    cum = _dot(tri, hi) + _dot(tri, mid) + _dot(tri, lo)
    b_s[...] = cum
    edge = 0 if reverse else C - 1
    total = b_s[pl.ds(edge, 1), :]

    o = _dot_nt((q * jnp.exp(cum)).astype(BF16), state_t.astype(BF16))
    k_tail = (k * jnp.exp(total - cum)).astype(BF16)
    new_state = state_t * jnp.exp(total) + _dot_tn(v.astype(BF16), k_tail)

    diag = jnp.sum(q * k, axis=-1, keepdims=True) * v
    for j in range(1, SUBLANES):
        shift = (C - j) if reverse else j
        cum_p = pltpu.roll(cum, shift, 0)
        k_p = pltpu.roll(k, shift, 0)
        v_p = pltpu.roll(v, shift, 0)
        if reverse:
            valid = row_in_group < (SUBLANES - j)
        else:
            valid = row_in_group >= j
        w = jnp.where(valid, jnp.exp(jnp.where(valid, cum - cum_p, 0.0)), 0.0)
        diag += jnp.sum(q * k_p * w, axis=-1, keepdims=True) * v_p
    o += diag

    attn = jnp.zeros((C, C), F32)
    m = SUBLANES
    level = 0
    while 2 * m <= C:
        is_query, pair_mask = level_masks[level]
        pieces = []
        for blk in range(C // (2 * m)):
            row = blk * 2 * m + (m if reverse else m - 1)
            pieces.append(jnp.broadcast_to(b_s[pl.ds(row, 1), :], (2 * m, cum.shape[1])))
        ref_pt = jnp.concatenate(pieces, axis=0) if len(pieces) > 1 else pieces[0]
        expo = jnp.where(is_query, cum - ref_pt, ref_pt - cum)
        x = (jnp.where(is_query, q, k) * jnp.exp(expo)).astype(BF16)
        attn += jnp.where(pair_mask, _dot_nt(x, x), 0.0)
        m *= 2
        level += 1
    o += _dot(attn.astype(BF16), v.astype(BF16))
    return o, new_state


def _hgrn_consts(reverse):
    C = HGRN_CHUNK
    ii = lax.broadcasted_iota(jnp.int32, (C, C), 0)
    jj = lax.broadcasted_iota(jnp.int32, (C, C), 1)
    tri = jnp.where((jj >= ii) if reverse else (jj <= ii), 1.0, 0.0).astype(BF16)
    r1 = lax.broadcasted_iota(jnp.int32, (C, 1), 0)
    row_in_group = r1 % SUBLANES
    level_masks = []
    m = SUBLANES
    while 2 * m <= C:
        second_r = (r1 // m) % 2 == 1
        is_query = jnp.logical_not(second_r) if reverse else second_r
        same_blk = (ii // (2 * m)) == (jj // (2 * m))
        i_second = (ii // m) % 2 == 1
        j_second = (jj // m) % 2 == 1
        if reverse:
            pair = same_blk & jnp.logical_not(i_second) & j_second
        else:
            pair = same_blk & i_second & jnp.logical_not(j_second)
        level_masks.append((is_query, pair))
        m *= 2
    return tri, row_in_group, level_masks


def _hgrn_kernel(xn_ref, w_ref, lbf_ref, lbb_ref, nw_ref, o_ref,
                 q_s, lff_s, kf_s, lfb_s, kb_s, v_s, g_s, acc_s, bf_s, bb_s, *, seq, dk):
    C = HGRN_CHUNK
    R = seq // C
    scale = dk ** -0.5

    def proj(n, carry):
        rows = pl.ds(pl.multiple_of(n * ROW_TILE, ROW_TILE), ROW_TILE)
        u = _dot(xn_ref[rows, :], w_ref[...])
        hq = u[:, 0 * dk:1 * dk]
        q_s[rows, :] = hq * jax.nn.sigmoid(hq) * scale
        lf, kk = _hgrn_gate(u[:, 1 * dk:2 * dk], lbf_ref[...])
        lff_s[rows, :] = lf
        kf_s[rows, :] = kk
        lf, kk = _hgrn_gate(u[:, 2 * dk:3 * dk], lbb_ref[...])
        lfb_s[rows, :] = lf
        kb_s[rows, :] = kk
        v_s[rows, :] = u[:, 3 * dk:4 * dk]
        hg = u[:, 4 * dk:5 * dk]
        g_s[rows, :] = hg * jax.nn.sigmoid(hg)
        return carry

    lax.fori_loop(0, seq // ROW_TILE, proj, 0)

    consts_f = _hgrn_consts(False)
    consts_b = _hgrn_consts(True)

    def step(n, carry):
        st_f, st_b = carry
        rows = pl.ds(pl.multiple_of(n * C, C), C)
        o, st_f = _hgrn_chunk(q_s[rows, :], kf_s[rows, :], lff_s[rows, :], v_s[rows, :],
                              st_f, bf_s, consts_f, False)
        acc_s[rows, :] = o
        return st_f, st_b

    zero = jnp.zeros((dk, dk), F32)
    lax.fori_loop(0, R, step, (zero, zero))

    def step_b(t, st_b):
        n = R - 1 - t
        rows = pl.ds(pl.multiple_of(n * C, C), C)
        o, st_b = _hgrn_chunk(q_s[rows, :], kb_s[rows, :], lfb_s[rows, :], v_s[rows, :],
                              st_b, bb_s, consts_b, True)
        ho = acc_s[rows, :] + o
        o_ref[rows, :] = (_rms(ho) * nw_ref[...] * g_s[rows, :]).astype(o_ref.dtype)
        return st_b

    lax.fori_loop(0, R, step_b, zero)


def hgrn_branch(xn3, w_hgrn_b, lb_f, lb_b, norm_w):
    B, S, D = xn3.shape
    H = HGRN_HEADS
    dk = D // H
    kern = functools.partial(_hgrn_kernel, seq=S, dk=dk)
    vec = lambda: pltpu.VMEM((S, dk), F32)
    return pl.pallas_call(
        kern,
        out_shape=jax.ShapeDtypeStruct((B, S, D), BF16),
        grid=(B, H),
        in_specs=[
            pl.BlockSpec((None, S, D), lambda b, h: (b, 0, 0)),
            pl.BlockSpec((D, 5 * dk), lambda b, h: (0, h)),
            pl.BlockSpec((1, dk), lambda b, h: (0, h)),
            pl.BlockSpec((1, dk), lambda b, h: (0, h)),
            pl.BlockSpec((1, dk), lambda b, h: (0, 0)),
        ],
        out_specs=pl.BlockSpec((None, S, dk), lambda b, h: (b, 0, h)),
        scratch_shapes=[vec(), vec(), vec(), vec(), vec(), vec(), vec(), vec(),
                        pltpu.VMEM((HGRN_CHUNK, dk), F32),
                        pltpu.VMEM((HGRN_CHUNK, dk), F32)],
        compiler_params=_params("parallel", "arbitrary"),
        name="hgrn2",
    )(xn3, w_hgrn_b, lb_f.reshape(1, D), lb_b.reshape(1, D), norm_w.reshape(1, dk))


def _fnet_proj_kernel(xn_ref, w_ref, cs_ref, o_ref, *, gdim):
    fu = _dot(xn_ref[...], w_ref[...]).astype(BF16)
    for g in range(FNET_GROUPS):
        t = _dot(fu[:, g * gdim:(g + 1) * gdim], cs_ref[...])
        o_ref[0, :, g * gdim:(g + 1) * gdim] = t[:, :gdim].astype(o_ref.dtype)
        o_ref[1, :, g * gdim:(g + 1) * gdim] = t[:, gdim:].astype(o_ref.dtype)


def _seq_dft_kernel(dft_ref, rhs_ref, o_ref):
    o_ref[...] = _dot(dft_ref[...], rhs_ref[...]).astype(o_ref.dtype)


def _dft_tables(n):
    idx = np.arange(n, dtype=np.int64)
    ang = 2.0 * np.pi * ((idx[:, None] * idx[None, :]) % n).astype(np.float64) / n
    s = 1.0 / math.sqrt(n)
    return np.cos(ang) * s, np.sin(ang) * s


def fourier_branch(xn3, w_in_b, layer, fu_off_blocks, tm=512):
    B, S, D = xn3.shape
    W = D
    gdim = W // FNET_GROUPS
    c_small, s_small = _dft_tables(gdim)
    cs_small = jnp.asarray(np.concatenate([c_small, s_small], axis=1), dtype=BF16)
    c_seq, s_seq = _dft_tables(S)
    dft_seq = jnp.asarray(np.concatenate([c_seq, -s_seq], axis=1), dtype=BF16)
    tiles = S // tm
    rhs = pl.pallas_call(
        functools.partial(_fnet_proj_kernel, gdim=gdim),
        out_shape=jax.ShapeDtypeStruct((B, 2, S, W), BF16),
        grid=(B, tiles),
        in_specs=[pl.BlockSpec((None, tm, D), lambda b, r: (b, r, 0)),
                  pl.BlockSpec((None, D, W), lambda b, r: (layer, 0, fu_off_blocks)),
                  pl.BlockSpec((gdim, 2 * gdim), lambda b, r: (0, 0))],
        out_specs=pl.BlockSpec((None, 2, tm, W), lambda b, r: (b, 0, r, 0)),
        compiler_params=_params("parallel", "parallel"),
        name="fnet_proj",
    )(xn3, w_in_b, cs_small)
    rhs = rhs.reshape(B, 2 * S, W)
    return pl.pallas_call(
        _seq_dft_kernel,
        out_shape=jax.ShapeDtypeStruct((B, S, W), BF16),
        grid=(B, tiles),
        in_specs=[pl.BlockSpec((tm, 2 * S), lambda b, r: (r, 0)),
                  pl.BlockSpec((None, 2 * S, W), lambda b, r: (b, 0, 0))],
        out_specs=pl.BlockSpec((None, tm, W), lambda b, r: (b, r, 0)),
        compiler_params=_params("parallel", "arbitrary"),
        name="fnet_seq_dft",
    )(dft_seq, rhs)


def _merge_kernel(x_ref, xn_ref, ro_ref, ho_ref, fo_ref, wga_ref, wro_ref, who_ref,
                  wf_ref, wout_ref, nw_ref, o_ref, *, d):
    xn = xn_ref[...]

    def gate(i):
        return jax.nn.sigmoid(_dot(xn, wga_ref[:, i * d:(i + 1) * d]))

    mix = gate(0) * _dot(ro_ref[...], wro_ref[...])
    mix += gate(1) * _dot(ho_ref[...], who_ref[...])
    mix += gate(2) * _dot(fo_ref[...], wf_ref[...])
    y = _dot(mix.astype(BF16), wout_ref[...])
    o_ref[...] = x_ref[...] + _rms(y) * nw_ref[...]


def merge_branches(x2, xn2, ro2, ho2, fo2, w_in_b, layer, ga_off_blocks,
                   w_ret_o, w_hgrn_o, w_fnet, w_out, norm_w, tm=512):
    T, D = x2.shape
    RV = ro2.shape[1]
    tile = lambda w: pl.BlockSpec((tm, w), lambda i: (i, 0))
    return pl.pallas_call(
        functools.partial(_merge_kernel, d=D),
        out_shape=jax.ShapeDtypeStruct((T, D), F32),
        grid=(T // tm,),
        in_specs=[tile(D), tile(D), tile(RV), tile(D), tile(D),
                  _resident((None, D, N_BRANCH * D), lambda i: (layer, 0, ga_off_blocks)),
                  _resident((None, RV, D), lambda i: (layer, 0, 0)),
                  _resident((None, D, D), lambda i: (layer, 0, 0)),
                  _resident((None, D, D), lambda i: (layer, 0, 0)),
                  _resident((None, D, D), lambda i: (layer, 0, 0)),
                  pl.BlockSpec((1, D), lambda i: (0, 0))],
        out_specs=tile(D),
        compiler_params=_params("parallel"),
        name="merge",
    )(x2, xn2, ro2, ho2, fo2, w_in_b, w_ret_o, w_hgrn_o, w_fnet, w_out, norm_w.reshape(1, D))


def _ffn_kernel(x_ref, xp_ref, xnx_ref, nw_in_ref, wup_ref, cw_ref, cb_ref, wdn_ref, nw_out_ref,
                o_ref, *, tm, tiles_per_seq, d_ff, fc):
    i = pl.program_id(0)
    r = i % tiles_per_seq
    halo = BF16_ROWS
    x = x_ref[...]
    nw = nw_in_ref[...]
    hp = jnp.where(r == 0, 0.0, _rms(xp_ref[...]) * nw)
    hx = jnp.where(r == tiles_per_seq - 1, 0.0, _rms(xnx_ref[...]) * nw)
    hn = jnp.concatenate([hp, _rms(x) * nw, hx], axis=0).astype(BF16)
    n_ext = tm + 2 * halo
    acc = jnp.zeros((tm, x.shape[1]), F32)
    for c in range(d_ff // fc):
        def conv(col):
            h = _dot(hn, wup_ref[:, col:col + fc])
            cw = cw_ref[:, col:col + fc]
            prev = pltpu.roll(h, 1, 0)[halo:halo + tm]
            nxt = pltpu.roll(h, n_ext - 1, 0)[halo:halo + tm]
            return (cb_ref[:, col:col + fc] + prev * cw[0:1] + h[halo:halo + tm] * cw[1:2]
                    + nxt * cw[2:3])
        gate = conv(c * fc)
        up = conv(d_ff + c * fc)
        act = (jax.nn.gelu(gate, approximate=True) * up).astype(BF16)
        acc += _dot(act, wdn_ref[c * fc:(c + 1) * fc, :])
    o_ref[...] = x + _rms(acc) * nw_out_ref[...]


def conv_ffn_block(x2, seq, w_up, conv_w, conv_b, w_down, nw_in, nw_out, layer, tm=512, fc=256):
    T, D = x2.shape
    d_ff = w_down.shape[1]
    halo = BF16_ROWS
    tps = seq // tm
    hb = tm // halo
    n_hb = T // halo
    kern = functools.partial(_ffn_kernel, tm=tm, tiles_per_seq=tps, d_ff=d_ff, fc=fc)
    return pl.pallas_call(
        kern,
        out_shape=jax.ShapeDtypeStruct((T, D), F32),
        grid=(T // tm,),
        in_specs=[pl.BlockSpec((tm, D), lambda i: (i, 0)),
                  pl.BlockSpec((halo, D), lambda i: (jnp.maximum(i * hb - 1, 0), 0)),
                  pl.BlockSpec((halo, D), lambda i: (jnp.minimum((i + 1) * hb, n_hb - 1), 0)),
                  pl.BlockSpec((1, D), lambda i: (0, 0)),
                  _resident((None, D, 2 * d_ff), lambda i: (layer, 0, 0)),
                  pl.BlockSpec((None, CONV_W, 2 * d_ff), lambda i: (layer, 0, 0)),
                  pl.BlockSpec((None, 1, 2 * d_ff), lambda i: (layer, 0, 0)),
                  _resident((None, d_ff, D), lambda i: (layer, 0, 0)),
                  pl.BlockSpec((1, D), lambda i: (0, 0))],
        out_specs=pl.BlockSpec((tm, D), lambda i: (i, 0)),
        compiler_params=_params("parallel"),
        name="conv_ffn",
    )(x2, x2, x2, nw_in.reshape(1, D), w_up, conv_w, conv_b, w_down, nw_out.reshape(1, D))


def kernel(x, positions, norm_w, w_in, hgrn_lb_logits, hgrn_norm_w, w_ret_o, w_hgrn_o,
           w_fnet, w_out, w_up, conv_w, conv_b, w_down):
    B, S, D = x.shape
    depth = w_in.shape[0]
    T = B * S
    dk_h = D // HGRN_HEADS

    hgrn_off = 2 * D + 2 * 2 * D
    fu_off = hgrn_off + 5 * D
    ga_off = fu_off + D

    w_in_b = w_in.astype(BF16)
    w_hgrn = w_in_b[:, :, hgrn_off:fu_off].reshape(depth, D, 5, HGRN_HEADS, dk_h)
    w_hgrn = w_hgrn.transpose(0, 1, 3, 2, 4).reshape(depth, D, 5 * D)
    w_ret_o_b = w_ret_o.astype(BF16)
    w_hgrn_o_b = w_hgrn_o.astype(BF16)
    w_fnet_b = w_fnet.astype(BF16)
    w_out_b = w_out.astype(BF16)
    w_up_b = w_up.astype(BF16)
    w_down_b = w_down.astype(BF16)
    conv_b3 = conv_b.reshape(depth, 1, -1)

    log_gamma = jnp.log(1.0 - 2.0 ** (-5.0 - jnp.arange(RET_HEADS, dtype=F32)))
    p = jax.nn.softmax(hgrn_lb_logits.astype(F32), axis=1)
    lower_bounds = jnp.cumsum(p, axis=1) - p[:, :1]

    cos, sin = rope_tables(positions, D // RET_HEADS // 2)

    x2 = x.reshape(T, D)
    for l in range(depth):
        xn2 = rms_norm_bf16(x2, norm_w[l, 0])
        xn3 = xn2.reshape(B, S, D)
        ro = retention_branch(xn3, w_in_b, l, cos, sin, log_gamma)
        ho = hgrn_branch(xn3, w_hgrn[l], lower_bounds[0, l], lower_bounds[1, l], hgrn_norm_w[l])
        fo = fourier_branch(xn3, w_in_b, l, fu_off // D)
        x2 = merge_branches(x2, xn2, ro.reshape(T, -1), ho.reshape(T, D), fo.reshape(T, D),
                            w_in_b, l, ga_off // (N_BRANCH * D),
                            w_ret_o_b, w_hgrn_o_b, w_fnet_b, w_out_b, norm_w[l, 1])
        x2 = conv_ffn_block(x2, S, w_up_b, conv_w, conv_b3, w_down_b,
                            norm_w[l, 2], norm_w[l, 3], l)
    return x2.reshape(B, S, D)
```

```python
import functools
import math

import numpy as np
import jax
import jax.numpy as jnp
from jax import lax
from jax.experimental import pallas as pl
from jax.experimental.pallas import tpu as pltpu

F32 = jnp.float32
BF16 = jnp.bfloat16

RET_HEADS = 4
HGRN_HEADS = 8
FNET_GROUPS = 4
N_BRANCH = 3
CONV_W = 3
ROPE_BASE = 10000.0
LB_FLOOR = 1e-30
EPS = 1e-6

V7X_VMEM_LIMIT_BYTES = 56 * 1024 * 1024
SUBLANES = 8
BF16_ROWS = 16

RET_CHUNK = 256
HGRN_CHUNK = 128
ROW_TILE = 256


def _dot(a, b):
    return jnp.dot(a, b, preferred_element_type=F32)


def _dot_nt(a, b):
    return lax.dot_general(a, b, (((1,), (1,)), ((), ())), preferred_element_type=F32)


def _dot_tn(a, b):
    return lax.dot_general(a, b, (((0,), (0,)), ((), ())), preferred_element_type=F32)


def _rms(x):
    return x * lax.rsqrt(jnp.mean(x * x, axis=-1, keepdims=True) + EPS)


def _params(*sem):
    return pltpu.CompilerParams(dimension_semantics=sem,
                                vmem_limit_bytes=V7X_VMEM_LIMIT_BYTES)


def _resident(shape, index_map):
    return pl.BlockSpec(shape, index_map, pipeline_mode=pl.Buffered(1))


def _rope_kernel(pos_ref, invf_ref, cos_ref, sin_ref):
    ang = pos_ref[...] * invf_ref[...]
    cos_ref[...] = jnp.cos(ang)
    sin_ref[...] = jnp.sin(ang)


def rope_tables(positions, half):
    B, S = positions.shape
    pos = positions.astype(F32).reshape(B, S, 1)
    inv_freq = (ROPE_BASE ** (-jnp.arange(half, dtype=F32) / half)).reshape(1, half)
    out = jax.ShapeDtypeStruct((B, S, half), F32)
    return pl.pallas_call(
        _rope_kernel,
        out_shape=(out, out),
        grid=(B,),
        in_specs=[pl.BlockSpec((None, S, 1), lambda b: (b, 0, 0)),
                  pl.BlockSpec((1, half), lambda b: (0, 0))],
        out_specs=(pl.BlockSpec((None, S, half), lambda b: (b, 0, 0)),
                   pl.BlockSpec((None, S, half), lambda b: (b, 0, 0))),
        compiler_params=_params("parallel"),
        name="rope_tables",
    )(pos, inv_freq)


def _norm_kernel(x_ref, w_ref, o_ref):
    o_ref[...] = (_rms(x_ref[...]) * w_ref[...]).astype(o_ref.dtype)


def rms_norm_bf16(x2, w, tm=1024):
    T, D = x2.shape
    return pl.pallas_call(
        _norm_kernel,
        out_shape=jax.ShapeDtypeStruct((T, D), BF16),
        grid=(T // tm,),
        in_specs=[pl.BlockSpec((tm, D), lambda i: (i, 0)),
                  pl.BlockSpec((1, D), lambda i: (0, 0))],
        out_specs=pl.BlockSpec((tm, D), lambda i: (i, 0)),
        compiler_params=_params("parallel"),
        name="rms_norm",
    )(x2, w.reshape(1, D))


def _ret_kernel(lg_ref, xn_ref, wq_ref, wk_ref, wv_ref, wg_ref, cos_ref, sin_ref,
                o_ref, q_s, k_s, v_s, g_s, gst_s, sf_s, *, seq, dk, dv):
    C = RET_CHUNK
    R = seq // C
    half = dk // 2
    lg = lg_ref[pl.program_id(1)]
    ret_scale = dk ** -0.5

    def rows_of(n):
        return pl.ds(pl.multiple_of(n * C, C), C)

    def proj(n, carry):
        rows = rows_of(n)
        xc = xn_ref[rows, :]
        cos = cos_ref[rows, :]
        sin = sin_ref[rows, :]
        q = _dot(xc, wq_ref[...])
        q1, q2 = q[:, :half], q[:, half:]
        q_s[rows, :] = jnp.concatenate([q1 * cos - q2 * sin, q1 * sin + q2 * cos], axis=-1)
        k = _dot(xc, wk_ref[...]) * ret_scale
        k1, k2 = k[:, :half], k[:, half:]
        k_s[rows, :] = jnp.concatenate([k1 * cos - k2 * sin, k1 * sin + k2 * cos], axis=-1)
        v_s[rows, :] = _dot(xc, wv_ref[...]).astype(BF16)
        g = _dot(xc, wg_ref[...])
        g_s[rows, :] = (g * jax.nn.sigmoid(g)).astype(BF16)
        return carry

    lax.fori_loop(0, R, proj, 0)

    pos = lax.broadcasted_iota(jnp.int32, (C, 1), 0).astype(F32)
    qdec_f = jnp.exp(lg * (pos + 1.0))
    qdec_b = jnp.exp(lg * (C - pos))
    kdec_f = jnp.exp(lg * (C - 1.0 - pos))
    kdec_b = jnp.exp(lg * pos)
    chunk_dec = jnp.exp(lg * C)
    ii = lax.broadcasted_iota(jnp.int32, (C, C), 0)
    jj = lax.broadcasted_iota(jnp.int32, (C, C), 1)
    decay = jnp.exp(lg * jnp.abs(ii - jj).astype(F32))

    def bwd(t, g_state):
        n = R - 1 - t
        rows = rows_of(n)
        gst_s[n] = g_state.astype(BF16)
        kb = (k_s[rows, :] * kdec_b).astype(BF16)
        return g_state * chunk_dec + _dot_tn(kb, v_s[rows, :])

    lax.fori_loop(0, R, bwd, jnp.zeros((dk, dv), F32))

    sf_s[...] = jnp.zeros_like(sf_s)

    def fwd(n, carry):
        rows = rows_of(n)
        q = q_s[rows, :]
        k = k_s[rows, :]
        v = v_s[rows, :]
        s = _dot_nt(q.astype(BF16), k.astype(BF16)) * decay
        out = _dot(s.astype(BF16), v)
        out += _dot((q * qdec_f).astype(BF16), sf_s[...].astype(BF16))
        out += _dot((q * qdec_b).astype(BF16), gst_s[n])
        sf_s[...] = sf_s[...] * chunk_dec + _dot_tn((k * kdec_f).astype(BF16), v)
        o_ref[rows, :] = (_rms(out) * g_s[rows, :].astype(F32)).astype(o_ref.dtype)
        return carry

    lax.fori_loop(0, R, fwd, 0)


def retention_branch(xn3, w_in_b, layer, cos, sin, log_gamma):
    B, S, D = xn3.shape
    dk = D // RET_HEADS
    dv = 2 * dk
    H = RET_HEADS
    qk_blocks = D // dk
    v_off = 2 * D // dv
    g_off = v_off + H
    kern = functools.partial(_ret_kernel, seq=S, dk=dk, dv=dv)
    return pl.pallas_call(
        kern,
        out_shape=jax.ShapeDtypeStruct((B, S, H * dv), BF16),
        grid=(B, H),
        in_specs=[
            pl.BlockSpec(memory_space=pltpu.SMEM),
            pl.BlockSpec((None, S, D), lambda b, h: (b, 0, 0)),
            pl.BlockSpec((None, D, dk), lambda b, h: (layer, 0, h)),
            pl.BlockSpec((None, D, dk), lambda b, h: (layer, 0, qk_blocks + h)),
            pl.BlockSpec((None, D, dv), lambda b, h: (layer, 0, v_off + h)),
            pl.BlockSpec((None, D, dv), lambda b, h: (layer, 0, g_off + h)),
            pl.BlockSpec((None, S, dk // 2), lambda b, h: (b, 0, 0)),
            pl.BlockSpec((None, S, dk // 2), lambda b, h: (b, 0, 0)),
        ],
        out_specs=pl.BlockSpec((None, S, dv), lambda b, h: (b, 0, h)),
        scratch_shapes=[
            pltpu.VMEM((S, dk), F32),
            pltpu.VMEM((S, dk), F32),
            pltpu.VMEM((S, dv), BF16),
            pltpu.VMEM((S, dv), BF16),
            pltpu.VMEM((S // RET_CHUNK, dk, dv), BF16),
            pltpu.VMEM((dk, dv), F32),
        ],
        compiler_params=_params("parallel", "arbitrary"),
        name="retention",
    )(log_gamma, xn3, w_in_b, w_in_b, w_in_b, w_in_b, cos, sin)


def _hgrn_gate(z, lb):
    e = jnp.exp(-jnp.abs(z))
    pos = z >= 0.0
    sig_neg_num = jnp.where(pos, e, 1.0)
    num = jnp.where(pos, 1.0, e) + jnp.maximum(lb, LB_FLOOR) * sig_neg_num
    log_f = jnp.log(num) - jnp.log(1.0 + e)
    return log_f, (1.0 - lb) * sig_neg_num / (1.0 + e)


def _split3_bf16(x):
    hi = x.astype(BF16)
    r = x - hi.astype(F32)
    mid = r.astype(BF16)
    lo = (r - mid.astype(F32)).astype(BF16)
    return hi, mid, lo


def _boundary_rows(cum_ref, base, m, reverse, row_in_group):
    C = HGRN_CHUNK
    d = cum_ref.shape[1]
    blk = 2 * m
    off = m if reverse else m - 1
    pieces = []
    if blk >= SUBLANES:
        for b in range(C // blk):
            pieces.append(jnp.broadcast_to(cum_ref[pl.ds(base + (b * blk + off), 1), :], (blk, d)))
    else:
        for g in range(C // SUBLANES):
            val = None
            for u in range(SUBLANES // blk):
                row = g * SUBLANES + u * blk + off
                piece = jnp.broadcast_to(cum_ref[pl.ds(base + row, 1), :], (SUBLANES, d))
                val = piece if val is None else jnp.where(row_in_group >= u * blk, piece, val)
            pieces.append(val)
    return jnp.concatenate(pieces, axis=0) if len(pieces) > 1 else pieces[0]


def _hgrn_intra(q, k, v, cum, cum_ref, base, consts, reverse):
    C = HGRN_CHUNK
    level_id, row_in_group, signs = consts
    attn = jnp.zeros((C, C), F32)
    for level, sign in enumerate(signs):
        ref_pt = _boundary_rows(cum_ref, base, 2 ** level, reverse, row_in_group)
        x = (jnp.where(sign > 0.0, q, k) * jnp.exp((cum - ref_pt) * sign)).astype(BF16)
        attn = jnp.where(level_id == level, _dot_nt(x, x), attn)
    o = _dot(attn.astype(BF16), v.astype(BF16))
    return o + jnp.sum(q * k, axis=-1, keepdims=True) * v


def _hgrn_consts(reverse, d):
    C = HGRN_CHUNK
    ii = lax.broadcasted_iota(jnp.int32, (C, C), 0)
    jj = lax.broadcasted_iota(jnp.int32, (C, C), 1)
    diff = ii ^ jj
    level_id = jnp.full((C, C), -1, jnp.int32)
    n_levels = C.bit_length() - 1
    for level in range(n_levels):
        level_id = jnp.where((diff >> level) == 1, level, level_id)
    level_id = jnp.where((ii < jj) if reverse else (ii > jj), level_id, -1)
    rows = lax.broadcasted_iota(jnp.int32, (C, d), 0)
    row_in_group = lax.broadcasted_iota(jnp.int32, (SUBLANES, d), 0)
    signs = []
    for level in range(n_levels):
        second = ((rows >> level) & 1) == 1
        is_query = jnp.logical_not(second) if reverse else second
        signs.append(jnp.where(is_query, 1.0, -1.0))
    return level_id, row_in_group, signs


def _chunk_cumsum_matrix(reverse):
    ii = lax.broadcasted_iota(jnp.int32, (ROW_TILE, ROW_TILE), 0)
    jj = lax.broadcasted_iota(jnp.int32, (ROW_TILE, ROW_TILE), 1)
    same = (ii // HGRN_CHUNK) == (jj // HGRN_CHUNK)
    order = (jj >= ii) if reverse else (jj <= ii)
    return jnp.where(same & order, 1.0, 0.0).astype(BF16)


def _hgrn_kernel(xn_ref, w_ref, lbf_ref, lbb_ref, nw_ref, o_ref,
                 q_s, v_s, g_s, acc_s, kf_s, kb_s, cumf_s, cumb_s,
                 qef_s, qeb_s, ktf_s, ktb_s, decf_s, decb_s, *, seq, dk):
    C = HGRN_CHUNK
    R = seq // C
    per_tile = ROW_TILE // C
    scale = dk ** -0.5

    def tile_rows(n):
        return pl.ds(pl.multiple_of(n * ROW_TILE, ROW_TILE), ROW_TILE)

    dirs = ((False, lbf_ref, kf_s, cumf_s, qef_s, ktf_s, decf_s, _chunk_cumsum_matrix(False)),
            (True, lbb_ref, kb_s, cumb_s, qeb_s, ktb_s, decb_s, _chunk_cumsum_matrix(True)))

    def proj(n, carry):
        rows = tile_rows(n)
        u = _dot(xn_ref[rows, :], w_ref[...])
        hq = u[:, 0 * dk:1 * dk]
        q = hq * jax.nn.sigmoid(hq) * scale
        q_s[rows, :] = q
        for idx, (reverse, lb_ref, k_s, cum_s, qe_s, kt_s, dec_s, cs_mat) in enumerate(dirs):
            lf, kk = _hgrn_gate(u[:, (1 + idx) * dk:(2 + idx) * dk], lb_ref[...])
            k_s[rows, :] = kk
            parts = _dot(cs_mat, jnp.concatenate(_split3_bf16(lf), axis=1))
            cum = parts[:, :dk] + parts[:, dk:2 * dk] + parts[:, 2 * dk:]
            cum_s[rows, :] = cum
            for j in range(per_tile):
                sl = slice(j * C, (j + 1) * C)
                edge = j * C if reverse else (j + 1) * C - 1
                total = cum[edge:edge + 1, :]
                r0 = pl.multiple_of(n * ROW_TILE + j * C, C)
                qe_s[pl.ds(r0, C), :] = (q[sl] * jnp.exp(cum[sl])).astype(BF16)
                kt_s[pl.ds(r0, C), :] = (kk[sl] * jnp.exp(total - cum[sl])).astype(BF16)
                dec_s[n * per_tile + j] = jnp.broadcast_to(jnp.exp(total), (SUBLANES, dk))
        v_s[rows, :] = u[:, 3 * dk:4 * dk]
        hg = u[:, 4 * dk:5 * dk]
        g_s[rows, :] = hg * jax.nn.sigmoid(hg)
        acc_s[rows, :] = jnp.zeros((ROW_TILE, dk), F32)
        return carry

    lax.fori_loop(0, seq // ROW_TILE, proj, 0)

    consts = (_hgrn_consts(False, dk), _hgrn_consts(True, dk))
    group = 2

    def chunk(c, state_t, idx):
        reverse, _, k_s, cum_s, qe_s, kt_s, dec_s, _ = dirs[idx]
        base = pl.multiple_of(c * C, C)
        rows = pl.ds(base, C)
        v = v_s[rows, :]
        o = _hgrn_intra(q_s[rows, :], k_s[rows, :], v, cum_s[rows, :], cum_s, base,
                        consts[idx], reverse)
        o += _dot_nt(qe_s[rows, :], state_t.astype(BF16))
        acc_s[rows, :] += o
        dec = jnp.tile(dec_s[c], (dk // SUBLANES, 1))
        return state_t * dec + _dot_tn(v.astype(BF16), kt_s[rows, :])

    def step(i, carry):
        st_f, st_b = carry
        for j in range(group):
            st_f = chunk(i * group + j, st_f, 0)
            st_b = chunk(R - 1 - (i * group + j), st_b, 1)
        return st_f, st_b

    zero = jnp.zeros((dk, dk), F32)
    lax.fori_loop(0, R // group, step, (zero, zero))

    def finish(n, carry):
        rows = tile_rows(n)
        o_ref[rows, :] = (_rms(acc_s[rows, :]) * nw_ref[...] * g_s[rows, :]).astype(o_ref.dtype)
        return carry

    lax.fori_loop(0, seq // ROW_TILE, finish, 0)


def hgrn_branch(xn3, w_hgrn_b, lb_f, lb_b, norm_w):
    B, S, D = xn3.shape
    H = HGRN_HEADS
    dk = D // H
    kern = functools.partial(_hgrn_kernel, seq=S, dk=dk)
    vec = lambda: pltpu.VMEM((S, dk), F32)
    half = lambda: pltpu.VMEM((S, dk), BF16)
    dec = lambda: pltpu.VMEM((S // HGRN_CHUNK, SUBLANES, dk), F32)
    return pl.pallas_call(
        kern,
        out_shape=jax.ShapeDtypeStruct((B, S, D), BF16),
        grid=(B, H),
        in_specs=[
            pl.BlockSpec((None, S, D), lambda b, h: (b, 0, 0)),
            pl.BlockSpec((D, 5 * dk), lambda b, h: (0, h)),
            pl.BlockSpec((1, dk), lambda b, h: (0, h)),
            pl.BlockSpec((1, dk), lambda b, h: (0, h)),
            pl.BlockSpec((1, dk), lambda b, h: (0, 0)),
        ],
        out_specs=pl.BlockSpec((None, S, dk), lambda b, h: (b, 0, h)),
        scratch_shapes=[vec(), vec(), vec(), vec(), vec(), vec(), vec(), vec(),
                        half(), half(), half(), half(), dec(), dec()],
        compiler_params=_params("parallel", "arbitrary"),
        name="hgrn2",
    )(xn3, w_hgrn_b, lb_f.reshape(1, D), lb_b.reshape(1, D), norm_w.reshape(1, dk))


def _fnet_proj_kernel(xn_ref, w_ref, cs_ref, o_ref, *, gdim):
    fu = _dot(xn_ref[...], w_ref[...]).astype(BF16)
    for g in range(FNET_GROUPS):
        t = _dot(fu[:, g * gdim:(g + 1) * gdim], cs_ref[...])
        o_ref[0, :, g * gdim:(g + 1) * gdim] = t[:, :gdim].astype(o_ref.dtype)
        o_ref[1, :, g * gdim:(g + 1) * gdim] = t[:, gdim:].astype(o_ref.dtype)


def _seq_dft_kernel(dft_ref, rhs_ref, o_ref):
    o_ref[...] = _dot(dft_ref[...], rhs_ref[...]).astype(o_ref.dtype)


def _dft_tables(n):
    idx = np.arange(n, dtype=np.int64)
    ang = 2.0 * np.pi * ((idx[:, None] * idx[None, :]) % n).astype(np.float64) / n
    s = 1.0 / math.sqrt(n)
    return np.cos(ang) * s, np.sin(ang) * s


def fourier_branch(xn3, w_in_b, layer, fu_off_blocks, tm=512):
    B, S, D = xn3.shape
    W = D
    gdim = W // FNET_GROUPS
    c_small, s_small = _dft_tables(gdim)
    cs_small = jnp.asarray(np.concatenate([c_small, s_small], axis=1), dtype=BF16)
    c_seq, s_seq = _dft_tables(S)
    dft_seq = jnp.asarray(np.concatenate([c_seq, -s_seq], axis=1), dtype=BF16)
    tiles = S // tm
    rhs = pl.pallas_call(
        functools.partial(_fnet_proj_kernel, gdim=gdim),
        out_shape=jax.ShapeDtypeStruct((B, 2, S, W), BF16),
        grid=(B, tiles),
        in_specs=[pl.BlockSpec((None, tm, D), lambda b, r: (b, r, 0)),
                  pl.BlockSpec((None, D, W), lambda b, r: (layer, 0, fu_off_blocks)),
                  pl.BlockSpec((gdim, 2 * gdim), lambda b, r: (0, 0))],
        out_specs=pl.BlockSpec((None, 2, tm, W), lambda b, r: (b, 0, r, 0)),
        compiler_params=_params("parallel", "parallel"),
        name="fnet_proj",
    )(xn3, w_in_b, cs_small)
    rhs = rhs.reshape(B, 2 * S, W)
    return pl.pallas_call(
        _seq_dft_kernel,
        out_shape=jax.ShapeDtypeStruct((B, S, W), BF16),
        grid=(B, tiles),
        in_specs=[pl.BlockSpec((tm, 2 * S), lambda b, r: (r, 0)),
                  pl.BlockSpec((None, 2 * S, W), lambda b, r: (b, 0, 0))],
        out_specs=pl.BlockSpec((None, tm, W), lambda b, r: (b, r, 0)),
        compiler_params=_params("parallel", "arbitrary"),
        name="fnet_seq_dft",
    )(dft_seq, rhs)


def _merge_kernel(x_ref, xn_ref, ro_ref, ho_ref, fo_ref, wga_ref, wro_ref, who_ref,
                  wf_ref, wout_ref, nw_ref, o_ref, *, d):
    xn = xn_ref[...]

    def gate(i):
        return jax.nn.sigmoid(_dot(xn, wga_ref[:, i * d:(i + 1) * d]))

    mix = gate(0) * _dot(ro_ref[...], wro_ref[...])
    mix += gate(1) * _dot(ho_ref[...], who_ref[...])
    mix += gate(2) * _dot(fo_ref[...], wf_ref[...])
    y = _dot(mix.astype(BF16), wout_ref[...])
    o_ref[...] = x_ref[...] + _rms(y) * nw_ref[...]


def merge_branches(x2, xn2, ro2, ho2, fo2, w_in_b, layer, ga_off_blocks,
                   w_ret_o, w_hgrn_o, w_fnet, w_out, norm_w, tm=512):
    T, D = x2.shape
    RV = ro2.shape[1]
    tile = lambda w: pl.BlockSpec((tm, w), lambda i: (i, 0))
    return pl.pallas_call(
        functools.partial(_merge_kernel, d=D),
        out_shape=jax.ShapeDtypeStruct((T, D), F32),
        grid=(T // tm,),
        in_specs=[tile(D), tile(D), tile(RV), tile(D), tile(D),
                  _resident((None, D, N_BRANCH * D), lambda i: (layer, 0, ga_off_blocks)),
                  _resident((None, RV, D), lambda i: (layer, 0, 0)),
                  _resident((None, D, D), lambda i: (layer, 0, 0)),
                  _resident((None, D, D), lambda i: (layer, 0, 0)),
                  _resident((None, D, D), lambda i: (layer, 0, 0)),
                  pl.BlockSpec((1, D), lambda i: (0, 0))],
        out_specs=tile(D),
        compiler_params=_params("parallel"),
        name="merge",
    )(x2, xn2, ro2, ho2, fo2, w_in_b, w_ret_o, w_hgrn_o, w_fnet, w_out, norm_w.reshape(1, D))


def _ffn_kernel(x_ref, xp_ref, xnx_ref, nw_in_ref, wup_ref, cw_ref, cb_ref, wdn_ref, nw_out_ref,
                o_ref, *, tm, tiles_per_seq, d_ff, fc):
    i = pl.program_id(0)
    r = i % tiles_per_seq
    halo = BF16_ROWS
    x = x_ref[...]
    nw = nw_in_ref[...]
    hp = jnp.where(r == 0, 0.0, _rms(xp_ref[...]) * nw)
    hx = jnp.where(r == tiles_per_seq - 1, 0.0, _rms(xnx_ref[...]) * nw)
    hn = jnp.concatenate([hp, _rms(x) * nw, hx], axis=0).astype(BF16)
    n_ext = tm + 2 * halo
    acc = jnp.zeros((tm, x.shape[1]), F32)
    for c in range(d_ff // fc):
        def conv(col):
            h = _dot(hn, wup_ref[:, col:col + fc])
            cw = cw_ref[:, col:col + fc]
            prev = pltpu.roll(h, 1, 0)[halo:halo + tm]
            nxt = pltpu.roll(h, n_ext - 1, 0)[halo:halo + tm]
            return (cb_ref[:, col:col + fc] + prev * cw[0:1] + h[halo:halo + tm] * cw[1:2]
                    + nxt * cw[2:3])
        gate = conv(c * fc)
        up = conv(d_ff + c * fc)
        act = (jax.nn.gelu(gate, approximate=True) * up).astype(BF16)
        acc += _dot(act, wdn_ref[c * fc:(c + 1) * fc, :])
    o_ref[...] = x + _rms(acc) * nw_out_ref[...]


def conv_ffn_block(x2, seq, w_up, conv_w, conv_b, w_down, nw_in, nw_out, layer, tm=512, fc=256):
    T, D = x2.shape
    d_ff = w_down.shape[1]
    halo = BF16_ROWS
    tps = seq // tm
    hb = tm // halo
    n_hb = T // halo
    kern = functools.partial(_ffn_kernel, tm=tm, tiles_per_seq=tps, d_ff=d_ff, fc=fc)
    return pl.pallas_call(
        kern,
        out_shape=jax.ShapeDtypeStruct((T, D), F32),
        grid=(T // tm,),
        in_specs=[pl.BlockSpec((tm, D), lambda i: (i, 0)),
                  pl.BlockSpec((halo, D), lambda i: (jnp.maximum(i * hb - 1, 0), 0)),
                  pl.BlockSpec((halo, D), lambda i: (jnp.minimum((i + 1) * hb, n_hb - 1), 0)),
                  pl.BlockSpec((1, D), lambda i: (0, 0)),
                  _resident((None, D, 2 * d_ff), lambda i: (layer, 0, 0)),
                  pl.BlockSpec((None, CONV_W, 2 * d_ff), lambda i: (layer, 0, 0)),
                  pl.BlockSpec((None, 1, 2 * d_ff), lambda i: (layer, 0, 0)),
                  _resident((None, d_ff, D), lambda i: (layer, 0, 0)),
                  pl.BlockSpec((1, D), lambda i: (0, 0))],
        out_specs=pl.BlockSpec((tm, D), lambda i: (i, 0)),
        compiler_params=_params("parallel"),
        name="conv_ffn",
    )(x2, x2, x2, nw_in.reshape(1, D), w_up, conv_w, conv_b, w_down, nw_out.reshape(1, D))


def kernel(x, positions, norm_w, w_in, hgrn_lb_logits, hgrn_norm_w, w_ret_o, w_hgrn_o,
           w_fnet, w_out, w_up, conv_w, conv_b, w_down):
    B, S, D = x.shape
    depth = w_in.shape[0]
    T = B * S
    dk_h = D // HGRN_HEADS

    hgrn_off = 2 * D + 2 * 2 * D
    fu_off = hgrn_off + 5 * D
    ga_off = fu_off + D

    w_in_b = w_in.astype(BF16)
    w_hgrn = w_in_b[:, :, hgrn_off:fu_off].reshape(depth, D, 5, HGRN_HEADS, dk_h)
    w_hgrn = w_hgrn.transpose(0, 1, 3, 2, 4).reshape(depth, D, 5 * D)
    w_ret_o_b = w_ret_o.astype(BF16)
    w_hgrn_o_b = w_hgrn_o.astype(BF16)
    w_fnet_b = w_fnet.astype(BF16)
    w_out_b = w_out.astype(BF16)
    w_up_b = w_up.astype(BF16)
    w_down_b = w_down.astype(BF16)
    conv_b3 = conv_b.reshape(depth, 1, -1)

    log_gamma = jnp.log(1.0 - 2.0 ** (-5.0 - jnp.arange(RET_HEADS, dtype=F32)))
    p = jax.nn.softmax(hgrn_lb_logits.astype(F32), axis=1)
    lower_bounds = jnp.cumsum(p, axis=1) - p[:, :1]

    cos, sin = rope_tables(positions, D // RET_HEADS // 2)

    x2 = x.reshape(T, D)
    for l in range(depth):
        xn2 = rms_norm_bf16(x2, norm_w[l, 0])
        xn3 = xn2.reshape(B, S, D)
        ro = retention_branch(xn3, w_in_b, l, cos, sin, log_gamma)
        ho = hgrn_branch(xn3, w_hgrn[l], lower_bounds[0, l], lower_bounds[1, l], hgrn_norm_w[l])
        fo = fourier_branch(xn3, w_in_b, l, fu_off // D)
        x2 = merge_branches(x2, xn2, ro.reshape(T, -1), ho.reshape(T, D), fo.reshape(T, D),
                            w_in_b, l, ga_off // (N_BRANCH * D),
                            w_ret_o_b, w_hgrn_o_b, w_fnet_b, w_out_b, norm_w[l, 1])
        x2 = conv_ffn_block(x2, S, w_up_b, conv_w, conv_b3, w_down_b,
                            norm_w[l, 2], norm_w[l, 3], l)
    return x2.reshape(B, S, D)
```

```python
import functools
import math

import numpy as np
import jax
import jax.numpy as jnp
from jax import lax
from jax.experimental import pallas as pl
from jax.experimental.pallas import tpu as pltpu

F32 = jnp.float32
BF16 = jnp.bfloat16

RET_HEADS = 4
HGRN_HEADS = 8
FNET_GROUPS = 4
N_BRANCH = 3
CONV_W = 3
ROPE_BASE = 10000.0
LB_FLOOR = 1e-30
EPS = 1e-6
LOG2_E = 1.4426950408889634

V7X_VMEM_LIMIT_BYTES = 56 * 1024 * 1024
SUBLANES = 8
BF16_ROWS = 16

RET_CHUNK = 256
HGRN_CHUNK = 128
ROW_TILE = 512


def _dot(a, b):
    return jnp.dot(a, b, preferred_element_type=F32)


def _dot_nt(a, b):
    return lax.dot_general(a, b, (((1,), (1,)), ((), ())), preferred_element_type=F32)


def _dot_tn(a, b):
    return lax.dot_general(a, b, (((0,), (0,)), ((), ())), preferred_element_type=F32)


def _rms(x):
    return x * lax.rsqrt(jnp.mean(x * x, axis=-1, keepdims=True) + EPS)


def _params(*sem):
    return pltpu.CompilerParams(dimension_semantics=sem,
                                vmem_limit_bytes=V7X_VMEM_LIMIT_BYTES)


def _resident(shape, index_map):
    return pl.BlockSpec(shape, index_map, pipeline_mode=pl.Buffered(1))


def _rope_kernel(pos_ref, invf_ref, cos_ref, sin_ref):
    ang = pos_ref[...] * invf_ref[...]
    cos_ref[...] = jnp.cos(ang)
    sin_ref[...] = jnp.sin(ang)


def rope_tables(positions, half):
    B, S = positions.shape
    pos = positions.astype(F32).reshape(B, S, 1)
    inv_freq = (ROPE_BASE ** (-jnp.arange(half, dtype=F32) / half)).reshape(1, half)
    out = jax.ShapeDtypeStruct((B, S, half), F32)
    return pl.pallas_call(
        _rope_kernel,
        out_shape=(out, out),
        grid=(B,),
        in_specs=[pl.BlockSpec((None, S, 1), lambda b: (b, 0, 0)),
                  pl.BlockSpec((1, half), lambda b: (0, 0))],
        out_specs=(pl.BlockSpec((None, S, half), lambda b: (b, 0, 0)),
                   pl.BlockSpec((None, S, half), lambda b: (b, 0, 0))),
        compiler_params=_params("parallel"),
        name="rope_tables",
    )(pos, inv_freq)


def _norm_kernel(x_ref, w_ref, o_ref):
    o_ref[...] = (_rms(x_ref[...]) * w_ref[...]).astype(o_ref.dtype)


def rms_norm_bf16(x2, w, tm=1024):
    T, D = x2.shape
    return pl.pallas_call(
        _norm_kernel,
        out_shape=jax.ShapeDtypeStruct((T, D), BF16),
        grid=(T // tm,),
        in_specs=[pl.BlockSpec((tm, D), lambda i: (i, 0)),
                  pl.BlockSpec((1, D), lambda i: (0, 0))],
        out_specs=pl.BlockSpec((tm, D), lambda i: (i, 0)),
        compiler_params=_params("parallel"),
        name="rms_norm",
    )(x2, w.reshape(1, D))


def _ret_kernel(lg_ref, xn_ref, wq_ref, wk_ref, wv_ref, wg_ref, cos_ref, sin_ref,
                o_ref, qi_s, qd_s, ki_s, v_s, g_s, st_s, kvb_s, run_s, *, seq, dk, dv):
    C = RET_CHUNK
    R = seq // C
    half = dk // 2
    lg = lg_ref[pl.program_id(1)]
    ret_scale = dk ** -0.5

    def rows_of(n):
        return pl.ds(pl.multiple_of(n * C, C), C)

    pos = lax.broadcasted_iota(jnp.int32, (C, 1), 0).astype(F32)
    qdec_f = jnp.exp(lg * (pos + 1.0))
    qdec_b = jnp.exp(lg * (C - pos))
    kdec_f = jnp.exp(lg * (C - 1.0 - pos))
    kdec_b = jnp.exp(lg * pos)
    chunk_dec = jnp.exp(lg * C)
    ii = lax.broadcasted_iota(jnp.int32, (C, C), 0)
    jj = lax.broadcasted_iota(jnp.int32, (C, C), 1)
    decay = jnp.exp(lg * jnp.abs(ii - jj).astype(F32))

    run_s[...] = jnp.zeros_like(run_s)

    def proj(n, carry):
        rows = rows_of(n)
        xc = xn_ref[rows, :]
        cos = cos_ref[rows, :]
        sin = sin_ref[rows, :]
        q = _dot(xc, wq_ref[...])
        q1, q2 = q[:, :half], q[:, half:]
        q = jnp.concatenate([q1 * cos - q2 * sin, q1 * sin + q2 * cos], axis=-1)
        qi_s[rows, :] = q.astype(BF16)
        qd_s[rows, :] = jnp.concatenate([q * qdec_f, q * qdec_b], axis=-1).astype(BF16)
        k = _dot(xc, wk_ref[...]) * ret_scale
        k1, k2 = k[:, :half], k[:, half:]
        k = jnp.concatenate([k1 * cos - k2 * sin, k1 * sin + k2 * cos], axis=-1)
        ki_s[rows, :] = k.astype(BF16)
        v = _dot(xc, wv_ref[...]).astype(BF16)
        v_s[rows, :] = v
        g = _dot(xc, wg_ref[...])
        g_s[rows, :] = (g * jax.nn.sigmoid(g)).astype(BF16)
        st_s[n, pl.ds(0, dk), :] = run_s[...].astype(BF16)
        run_s[...] = run_s[...] * chunk_dec + _dot_tn((k * kdec_f).astype(BF16), v)
        kvb_s[n] = _dot_tn((k * kdec_b).astype(BF16), v)
        return carry

    lax.fori_loop(0, R, proj, 0)

    run_s[...] = jnp.zeros_like(run_s)

    def bwd(t, carry):
        n = R - 1 - t
        st_s[n, pl.ds(dk, dk), :] = run_s[...].astype(BF16)
        run_s[...] = run_s[...] * chunk_dec + kvb_s[n]
        return carry

    lax.fori_loop(0, R, bwd, 0)

    def out(n, carry):
        rows = rows_of(n)
        s = _dot_nt(qi_s[rows, :], ki_s[rows, :]) * decay
        o = _dot(s.astype(BF16), v_s[rows, :]) + _dot(qd_s[rows, :], st_s[n])
        o_ref[rows, :] = (_rms(o) * g_s[rows, :].astype(F32)).astype(o_ref.dtype)
        return carry

    lax.fori_loop(0, R, out, 0, unroll=2)


def retention_branch(xn3, w_in_b, layer, cos, sin, log_gamma):
    B, S, D = xn3.shape
    dk = D // RET_HEADS
    dv = 2 * dk
    H = RET_HEADS
    qk_blocks = D // dk
    v_off = 2 * D // dv
    g_off = v_off + H
    kern = functools.partial(_ret_kernel, seq=S, dk=dk, dv=dv)
    return pl.pallas_call(
        kern,
        out_shape=jax.ShapeDtypeStruct((B, S, H * dv), BF16),
        grid=(B, H),
        in_specs=[
            pl.BlockSpec(memory_space=pltpu.SMEM),
            pl.BlockSpec((None, S, D), lambda b, h: (b, 0, 0)),
            pl.BlockSpec((None, D, dk), lambda b, h: (layer, 0, h)),
            pl.BlockSpec((None, D, dk), lambda b, h: (layer, 0, qk_blocks + h)),
            pl.BlockSpec((None, D, dv), lambda b, h: (layer, 0, v_off + h)),
            pl.BlockSpec((None, D, dv), lambda b, h: (layer, 0, g_off + h)),
            pl.BlockSpec((None, S, dk // 2), lambda b, h: (b, 0, 0)),
            pl.BlockSpec((None, S, dk // 2), lambda b, h: (b, 0, 0)),
        ],
        out_specs=pl.BlockSpec((None, S, dv), lambda b, h: (b, 0, h)),
        scratch_shapes=[
            pltpu.VMEM((S, dk), BF16),
            pltpu.VMEM((S, 2 * dk), BF16),
            pltpu.VMEM((S, dk), BF16),
            pltpu.VMEM((S, dv), BF16),
            pltpu.VMEM((S, dv), BF16),
            pltpu.VMEM((S // RET_CHUNK, 2 * dk, dv), BF16),
            pltpu.VMEM((S // RET_CHUNK, dk, dv), F32),
            pltpu.VMEM((dk, dv), F32),
        ],
        compiler_params=_params("parallel", "arbitrary"),
        name="retention",
    )(log_gamma, xn3, w_in_b, w_in_b, w_in_b, w_in_b, cos, sin)


def _hgrn_gate(z, lb):
    e = jnp.exp(-jnp.abs(z))
    pos = z >= 0.0
    sig_neg_num = jnp.where(pos, e, 1.0)
    num = jnp.where(pos, 1.0, e) + jnp.maximum(lb, LB_FLOOR) * sig_neg_num
    log_f = jnp.log(num) - jnp.log(1.0 + e)
    return log_f, (1.0 - lb) * sig_neg_num / (1.0 + e)


def _boundary_rows(cum_ref, base, m, reverse, row_in_group):
    C = HGRN_CHUNK
    d = cum_ref.shape[1]
    blk = 2 * m
    off = m if reverse else m - 1
    pieces = []
    if blk >= SUBLANES:
        for b in range(C // blk):
            pieces.append(jnp.broadcast_to(cum_ref[pl.ds(base + (b * blk + off), 1), :], (blk, d)))
    else:
        for g in range(C // SUBLANES):
            val = None
            for u in range(SUBLANES // blk):
                row = g * SUBLANES + u * blk + off
                piece = jnp.broadcast_to(cum_ref[pl.ds(base + row, 1), :], (SUBLANES, d))
                val = piece if val is None else jnp.where(row_in_group >= u * blk, piece, val)
            pieces.append(val)
    return jnp.concatenate(pieces, axis=0) if len(pieces) > 1 else pieces[0]


def _level_operands(level, q, k, cum, cum_ref, base, consts, reverse):
    C = HGRN_CHUNK
    _, row_in_group, signs = consts
    m = 2 ** level
    if 2 * m <= SUBLANES:
        sign = signs[level]
        ref_pt = _boundary_rows(cum_ref, base, m, reverse, row_in_group)
        x = (jnp.where(sign > 0.0, q, k) * jnp.exp2((cum - ref_pt) * sign)).astype(BF16)
        return x, x, list(range(C // SUBLANES))
    xq, xall, q_groups = [], [], []
    for b in range(C // (2 * m)):
        first = slice(b * 2 * m, b * 2 * m + m)
        second = slice(b * 2 * m + m, (b + 1) * 2 * m)
        q_rows, k_rows = (first, second) if reverse else (second, first)
        edge = k_rows.start if reverse else k_rows.stop - 1
        ref_pt = cum_ref[pl.ds(base + edge, 1), :]
        xq_b = q[q_rows] * jnp.exp2(cum[q_rows] - ref_pt)
        xk_b = k[k_rows] * jnp.exp2(ref_pt - cum[k_rows])
        xq.append(xq_b)
        xall.extend([xq_b, xk_b] if reverse else [xk_b, xq_b])
        q_groups.extend(range(q_rows.start // SUBLANES, q_rows.stop // SUBLANES))
    return (jnp.concatenate(xq, axis=0).astype(BF16), jnp.concatenate(xall, axis=0).astype(BF16),
            q_groups)


def _paired_dot_nt(lhs_a, rhs_a, lhs_b, rhs_b):
    rhs = jnp.concatenate([rhs_a, rhs_b], axis=1)
    lhs = jnp.concatenate(
        [jnp.concatenate([lhs_a, jnp.zeros_like(lhs_a)], axis=1),
         jnp.concatenate([jnp.zeros_like(lhs_b), lhs_b], axis=1)], axis=0)
    s = _dot_nt(lhs, rhs)
    return s[:lhs_a.shape[0]], s[lhs_a.shape[0]:]


def _assemble_scores(scores, level_id):
    C = HGRN_CHUNK
    rows = [jnp.zeros((SUBLANES, C), F32) for _ in range(C // SUBLANES)]
    for level, (s, q_groups) in enumerate(scores):
        for i, g in enumerate(q_groups):
            lid = level_id[g * SUBLANES:(g + 1) * SUBLANES]
            rows[g] = jnp.where(lid == level, s[i * SUBLANES:(i + 1) * SUBLANES], rows[g])
    return jnp.concatenate(rows, axis=0)


def _hgrn_intra_pair(fwd, bwd, consts):
    C = HGRN_CHUNK
    args = ((fwd, consts[0], False), (bwd, consts[1], True))
    scores = ([], [])
    for level in range(C.bit_length() - 1):
        ops = [_level_operands(level, q, k, cum, cum_ref, base, cst, rev)
               for (q, k, _, cum, cum_ref, base), cst, rev in args]
        s_f, s_b = _paired_dot_nt(ops[0][0], ops[0][1], ops[1][0], ops[1][1])
        scores[0].append((s_f, ops[0][2]))
        scores[1].append((s_b, ops[1][2]))
    outs = []
    for idx, ((q, k, v, _, _, _), cst, _) in enumerate(args):
        attn = _assemble_scores(scores[idx], cst[0])
        o = _dot(attn.astype(BF16), v.astype(BF16))
        outs.append(o + jnp.sum(q * k, axis=-1, keepdims=True) * v)
    return outs


def _hgrn_consts(reverse, d):
    C = HGRN_CHUNK
    ii = lax.broadcasted_iota(jnp.int32, (C, C), 0)
    jj = lax.broadcasted_iota(jnp.int32, (C, C), 1)
    diff = ii ^ jj
    level_id = jnp.full((C, C), -1, jnp.int32)
    n_levels = C.bit_length() - 1
    for level in range(n_levels):
        level_id = jnp.where((diff >> level) == 1, level, level_id)
    level_id = jnp.where((ii < jj) if reverse else (ii > jj), level_id, -1)
    rows = lax.broadcasted_iota(jnp.int32, (C, d), 0)
    row_in_group = lax.broadcasted_iota(jnp.int32, (SUBLANES, d), 0)
    signs = []
    for level in range(SUBLANES.bit_length() - 1):
        second = ((rows >> level) & 1) == 1
        is_query = jnp.logical_not(second) if reverse else second
        signs.append(jnp.where(is_query, 1.0, -1.0))
    return level_id, row_in_group, signs


def _chunk_cumsum(x, reverse, row_in_group):
    rows, d = x.shape
    groups = rows // SUBLANES
    per_chunk = HGRN_CHUNK // SUBLANES
    y = x.reshape(groups, SUBLANES, d)
    step = 1
    while step < SUBLANES:
        rolled = pltpu.roll(y, (SUBLANES - step) if reverse else step, 1)
        valid = (row_in_group < SUBLANES - step) if reverse else (row_in_group >= step)
        y = y + jnp.where(valid, rolled, 0.0)
        step *= 2
    out = [None] * groups
    for c in range(rows // HGRN_CHUNK):
        order = range(c * per_chunk, (c + 1) * per_chunk)
        carry = None
        for g in (reversed(order) if reverse else order):
            yg = y[g] if carry is None else y[g] + carry
            out[g] = yg
            edge = 0 if reverse else SUBLANES - 1
            carry = jnp.broadcast_to(yg[edge:edge + 1, :], (SUBLANES, d))
    return jnp.concatenate(out, axis=0)


def _hgrn_kernel(xn_ref, w_ref, lbf_ref, lbb_ref, nw_ref, o_ref,
                 q_s, v_s, g_s, acc_s, kf_s, kb_s, cumf_s, cumb_s,
                 qef_s, qeb_s, ktf_s, ktb_s, decf_s, decb_s, *, seq, dk):
    C = HGRN_CHUNK
    R = seq // C
    per_tile = ROW_TILE // C
    scale = dk ** -0.5

    def tile_rows(n):
        return pl.ds(pl.multiple_of(n * ROW_TILE, ROW_TILE), ROW_TILE)

    dirs = ((False, lbf_ref, kf_s, cumf_s, qef_s, ktf_s, decf_s),
            (True, lbb_ref, kb_s, cumb_s, qeb_s, ktb_s, decb_s))
    row_in_group = lax.broadcasted_iota(jnp.int32, (1, SUBLANES, dk), 1)

    def proj(n, carry):
        rows = tile_rows(n)
        u = _dot(xn_ref[rows, :], w_ref[...])
        hq = u[:, 0 * dk:1 * dk]
        q = hq * jax.nn.sigmoid(hq) * scale
        q_s[rows, :] = q
        for idx, (reverse, lb_ref, k_s, cum_s, qe_s, kt_s, dec_s) in enumerate(dirs):
            lf, kk = _hgrn_gate(u[:, (1 + idx) * dk:(2 + idx) * dk], lb_ref[...])
            k_s[rows, :] = kk
            cum = _chunk_cumsum(lf * LOG2_E, reverse, row_in_group)
            cum_s[rows, :] = cum
            for j in range(per_tile):
                sl = slice(j * C, (j + 1) * C)
                edge = j * C if reverse else (j + 1) * C - 1
                total = cum[edge:edge + 1, :]
                r0 = pl.multiple_of(n * ROW_TILE + j * C, C)
                qe_s[pl.ds(r0, C), :] = (q[sl] * jnp.exp2(cum[sl])).astype(BF16)
                kt_s[pl.ds(r0, C), :] = (kk[sl] * jnp.exp2(total - cum[sl])).astype(BF16)
                dec_s[n * per_tile + j] = jnp.broadcast_to(jnp.exp2(total), (SUBLANES, dk))
        v_s[rows, :] = u[:, 3 * dk:4 * dk]
        hg = u[:, 4 * dk:5 * dk]
        g_s[rows, :] = hg * jax.nn.sigmoid(hg)
        acc_s[rows, :] = jnp.zeros((ROW_TILE, dk), F32)
        return carry

    lax.fori_loop(0, seq // ROW_TILE, proj, 0)

    consts = (_hgrn_consts(False, dk), _hgrn_consts(True, dk))
    group = 2

    def chunk_pair(cf, cb, states):
        data, rows = [], []
        for c, (_, _, k_s, cum_s, _, _, _) in zip((cf, cb), dirs):
            base = pl.multiple_of(c * C, C)
            r = pl.ds(base, C)
            data.append((q_s[r, :], k_s[r, :], v_s[r, :], cum_s[r, :], cum_s, base))
            rows.append(r)
        o_f, o_b = _hgrn_intra_pair(data[0], data[1], consts)
        i_f, i_b = _paired_dot_nt(qef_s[rows[0], :], states[0].astype(BF16),
                                  qeb_s[rows[1], :], states[1].astype(BF16))
        acc_s[rows[0], :] += o_f + i_f
        acc_s[rows[1], :] += o_b + i_b
        new_states = []
        for c, r, st, (_, _, v, _, _, _), (_, _, _, _, _, kt_s, dec_s) in zip(
                (cf, cb), rows, states, data, dirs):
            dec = jnp.tile(dec_s[c], (dk // SUBLANES, 1))
            new_states.append(st * dec + _dot_tn(v.astype(BF16), kt_s[r, :]))
        return tuple(new_states)

    def step(i, states):
        for j in range(group):
            states = chunk_pair(i * group + j, R - 1 - (i * group + j), states)
        return states

    zero = jnp.zeros((dk, dk), F32)
    lax.fori_loop(0, R // group, step, (zero, zero))

    def finish(n, carry):
        rows = tile_rows(n)
        o_ref[rows, :] = (_rms(acc_s[rows, :]) * nw_ref[...] * g_s[rows, :]).astype(o_ref.dtype)
        return carry

    lax.fori_loop(0, seq // ROW_TILE, finish, 0)


def hgrn_branch(xn3, w_hgrn_b, lb_f, lb_b, norm_w):
    B, S, D = xn3.shape
    H = HGRN_HEADS
    dk = D // H
    kern = functools.partial(_hgrn_kernel, seq=S, dk=dk)
    vec = lambda: pltpu.VMEM((S, dk), F32)
    half = lambda: pltpu.VMEM((S, dk), BF16)
    dec = lambda: pltpu.VMEM((S // HGRN_CHUNK, SUBLANES, dk), F32)
    return pl.pallas_call(
        kern,
        out_shape=jax.ShapeDtypeStruct((B, S, D), BF16),
        grid=(B, H),
        in_specs=[
            pl.BlockSpec((None, S, D), lambda b, h: (b, 0, 0)),
            pl.BlockSpec((D, 5 * dk), lambda b, h: (0, h)),
            pl.BlockSpec((1, dk), lambda b, h: (0, h)),
            pl.BlockSpec((1, dk), lambda b, h: (0, h)),
            pl.BlockSpec((1, dk), lambda b, h: (0, 0)),
        ],
        out_specs=pl.BlockSpec((None, S, dk), lambda b, h: (b, 0, h)),
        scratch_shapes=[vec(), vec(), vec(), vec(), vec(), vec(), vec(), vec(),
                        half(), half(), half(), half(), dec(), dec()],
        compiler_params=_params("parallel", "arbitrary"),
        name="hgrn2",
    )(xn3, w_hgrn_b, lb_f.reshape(1, D), lb_b.reshape(1, D), norm_w.reshape(1, dk))


def _fnet_proj_kernel(xn_ref, w_ref, cs_ref, o_ref, *, gdim):
    fu = _dot(xn_ref[...], w_ref[...]).astype(BF16)
    for g in range(FNET_GROUPS):
        t = _dot(fu[:, g * gdim:(g + 1) * gdim], cs_ref[...])
        o_ref[0, :, g * gdim:(g + 1) * gdim] = t[:, :gdim].astype(o_ref.dtype)
        o_ref[1, :, g * gdim:(g + 1) * gdim] = t[:, gdim:].astype(o_ref.dtype)


def _seq_dft_kernel(dft_ref, rhs_ref, o_ref):
    o_ref[...] = _dot(dft_ref[...], rhs_ref[...]).astype(o_ref.dtype)


def _dft_tables(n):
    idx = np.arange(n, dtype=np.int64)
    ang = 2.0 * np.pi * ((idx[:, None] * idx[None, :]) % n).astype(np.float64) / n
    s = 1.0 / math.sqrt(n)
    return np.cos(ang) * s, np.sin(ang) * s


def fourier_branch(xn3, w_in_b, layer, fu_off_blocks, tm=512):
    B, S, D = xn3.shape
    W = D
    gdim = W // FNET_GROUPS
    c_small, s_small = _dft_tables(gdim)
    cs_small = jnp.asarray(np.concatenate([c_small, s_small], axis=1), dtype=BF16)
    c_seq, s_seq = _dft_tables(S)
    dft_seq = jnp.asarray(np.concatenate([c_seq, -s_seq], axis=1), dtype=BF16)
    tiles = S // tm
    rhs = pl.pallas_call(
        functools.partial(_fnet_proj_kernel, gdim=gdim),
        out_shape=jax.ShapeDtypeStruct((B, 2, S, W), BF16),
        grid=(B, tiles),
        in_specs=[pl.BlockSpec((None, tm, D), lambda b, r: (b, r, 0)),
                  pl.BlockSpec((None, D, W), lambda b, r: (layer, 0, fu_off_blocks)),
                  pl.BlockSpec((gdim, 2 * gdim), lambda b, r: (0, 0))],
        out_specs=pl.BlockSpec((None, 2, tm, W), lambda b, r: (b, 0, r, 0)),
        compiler_params=_params("parallel", "parallel"),
        name="fnet_proj",
    )(xn3, w_in_b, cs_small)
    rhs = rhs.reshape(B, 2 * S, W)
    return pl.pallas_call(
        _seq_dft_kernel,
        out_shape=jax.ShapeDtypeStruct((B, S, W), BF16),
        grid=(B, tiles),
        in_specs=[pl.BlockSpec((tm, 2 * S), lambda b, r: (r, 0)),
                  pl.BlockSpec((None, 2 * S, W), lambda b, r: (b, 0, 0))],
        out_specs=pl.BlockSpec((None, tm, W), lambda b, r: (b, r, 0)),
        compiler_params=_params("parallel", "arbitrary"),
        name="fnet_seq_dft",
    )(dft_seq, rhs)


def _merge_kernel(x_ref, xn_ref, ro_ref, ho_ref, fo_ref, wga_ref, wro_ref, who_ref,
                  wf_ref, wout_ref, nw_ref, o_ref, *, d):
    xn = xn_ref[...]

    def gate(i):
        return jax.nn.sigmoid(_dot(xn, wga_ref[:, i * d:(i + 1) * d]))

    mix = gate(0) * _dot(ro_ref[...], wro_ref[...])
    mix += gate(1) * _dot(ho_ref[...], who_ref[...])
    mix += gate(2) * _dot(fo_ref[...], wf_ref[...])
    y = _dot(mix.astype(BF16), wout_ref[...])
    o_ref[...] = x_ref[...] + _rms(y) * nw_ref[...]


def merge_branches(x2, xn2, ro2, ho2, fo2, w_in_b, layer, ga_off_blocks,
                   w_ret_o, w_hgrn_o, w_fnet, w_out, norm_w, tm=512):
    T, D = x2.shape
    RV = ro2.shape[1]
    tile = lambda w: pl.BlockSpec((tm, w), lambda i: (i, 0))
    return pl.pallas_call(
        functools.partial(_merge_kernel, d=D),
        out_shape=jax.ShapeDtypeStruct((T, D), F32),
        grid=(T // tm,),
        in_specs=[tile(D), tile(D), tile(RV), tile(D), tile(D),
                  _resident((None, D, N_BRANCH * D), lambda i: (layer, 0, ga_off_blocks)),
                  _resident((None, RV, D), lambda i: (layer, 0, 0)),
                  _resident((None, D, D), lambda i: (layer, 0, 0)),
                  _resident((None, D, D), lambda i: (layer, 0, 0)),
                  _resident((None, D, D), lambda i: (layer, 0, 0)),
                  pl.BlockSpec((1, D), lambda i: (0, 0))],
        out_specs=tile(D),
        compiler_params=_params("parallel"),
        name="merge",
    )(x2, xn2, ro2, ho2, fo2, w_in_b, w_ret_o, w_hgrn_o, w_fnet, w_out, norm_w.reshape(1, D))


def _ffn_kernel(x_ref, xp_ref, xnx_ref, nw_in_ref, wup_ref, cw_ref, cb_ref, wdn_ref, nw_out_ref,
                o_ref, *, tm, tiles_per_seq, d_ff, fc):
    i = pl.program_id(0)
    r = i % tiles_per_seq
    halo = BF16_ROWS
    x = x_ref[...]
    nw = nw_in_ref[...]
    hp = jnp.where(r == 0, 0.0, _rms(xp_ref[...]) * nw)
    hx = jnp.where(r == tiles_per_seq - 1, 0.0, _rms(xnx_ref[...]) * nw)
    hn = jnp.concatenate([hp, _rms(x) * nw, hx], axis=0).astype(BF16)
    n_ext = tm + 2 * halo
    acc = jnp.zeros((tm, x.shape[1]), F32)
    for c in range(d_ff // fc):
        def conv(col):
            h = _dot(hn, wup_ref[:, col:col + fc])
            cw = cw_ref[:, col:col + fc]
            prev = pltpu.roll(h, 1, 0)[halo:halo + tm]
            nxt = pltpu.roll(h, n_ext - 1, 0)[halo:halo + tm]
            return (cb_ref[:, col:col + fc] + prev * cw[0:1] + h[halo:halo + tm] * cw[1:2]
                    + nxt * cw[2:3])
        gate = conv(c * fc)
        up = conv(d_ff + c * fc)
        act = (jax.nn.gelu(gate, approximate=True) * up).astype(BF16)
        acc += _dot(act, wdn_ref[c * fc:(c + 1) * fc, :])
    o_ref[...] = x + _rms(acc) * nw_out_ref[...]


def conv_ffn_block(x2, seq, w_up, conv_w, conv_b, w_down, nw_in, nw_out, layer, tm=512, fc=256):
    T, D = x2.shape
    d_ff = w_down.shape[1]
    halo = BF16_ROWS
    tps = seq // tm
    hb = tm // halo
    n_hb = T // halo
    kern = functools.partial(_ffn_kernel, tm=tm, tiles_per_seq=tps, d_ff=d_ff, fc=fc)
    return pl.pallas_call(
        kern,
        out_shape=jax.ShapeDtypeStruct((T, D), F32),
        grid=(T // tm,),
        in_specs=[pl.BlockSpec((tm, D), lambda i: (i, 0)),
                  pl.BlockSpec((halo, D), lambda i: (jnp.maximum(i * hb - 1, 0), 0)),
                  pl.BlockSpec((halo, D), lambda i: (jnp.minimum((i + 1) * hb, n_hb - 1), 0)),
                  pl.BlockSpec((1, D), lambda i: (0, 0)),
                  _resident((None, D, 2 * d_ff), lambda i: (layer, 0, 0)),
                  pl.BlockSpec((None, CONV_W, 2 * d_ff), lambda i: (layer, 0, 0)),
                  pl.BlockSpec((None, 1, 2 * d_ff), lambda i: (layer, 0, 0)),
                  _resident((None, d_ff, D), lambda i: (layer, 0, 0)),
                  pl.BlockSpec((1, D), lambda i: (0, 0))],
        out_specs=pl.BlockSpec((tm, D), lambda i: (i, 0)),
        compiler_params=_params("parallel"),
        name="conv_ffn",
    )(x2, x2, x2, nw_in.reshape(1, D), w_up, conv_w, conv_b, w_down, nw_out.reshape(1, D))


def kernel(x, positions, norm_w, w_in, hgrn_lb_logits, hgrn_norm_w, w_ret_o, w_hgrn_o,
           w_fnet, w_out, w_up, conv_w, conv_b, w_down):
    B, S, D = x.shape
    depth = w_in.shape[0]
    T = B * S
    dk_h = D // HGRN_HEADS

    hgrn_off = 2 * D + 2 * 2 * D
    fu_off = hgrn_off + 5 * D
    ga_off = fu_off + D

    w_in_b = w_in.astype(BF16)
    w_hgrn = w_in_b[:, :, hgrn_off:fu_off].reshape(depth, D, 5, HGRN_HEADS, dk_h)
    w_hgrn = w_hgrn.transpose(0, 1, 3, 2, 4).reshape(depth, D, 5 * D)
    w_ret_o_b = w_ret_o.astype(BF16)
    w_hgrn_o_b = w_hgrn_o.astype(BF16)
    w_fnet_b = w_fnet.astype(BF16)
    w_out_b = w_out.astype(BF16)
    w_up_b = w_up.astype(BF16)
    w_down_b = w_down.astype(BF16)
    conv_b3 = conv_b.reshape(depth, 1, -1)

    log_gamma = jnp.log(1.0 - 2.0 ** (-5.0 - jnp.arange(RET_HEADS, dtype=F32)))
    p = jax.nn.softmax(hgrn_lb_logits.astype(F32), axis=1)
    lower_bounds = jnp.cumsum(p, axis=1) - p[:, :1]

    cos, sin = rope_tables(positions, D // RET_HEADS // 2)

    x2 = x.reshape(T, D)
    for l in range(depth):
        xn2 = rms_norm_bf16(x2, norm_w[l, 0])
        xn3 = xn2.reshape(B, S, D)
        ro = retention_branch(xn3, w_in_b, l, cos, sin, log_gamma)
        ho = hgrn_branch(xn3, w_hgrn[l], lower_bounds[0, l], lower_bounds[1, l], hgrn_norm_w[l])
        fo = fourier_branch(xn3, w_in_b, l, fu_off // D)
        x2 = merge_branches(x2, xn2, ro.reshape(T, -1), ho.reshape(T, D), fo.reshape(T, D),
                            w_in_b, l, ga_off // (N_BRANCH * D),
                            w_ret_o_b, w_hgrn_o_b, w_fnet_b, w_out_b, norm_w[l, 1])
        x2 = conv_ffn_block(x2, S, w_up_b, conv_w, conv_b3, w_down_b,
                            norm_w[l, 2], norm_w[l, 3], l)
    return x2.reshape(B, S, D)
```

```python
import functools
import math

import numpy as np
import jax
import jax.numpy as jnp
from jax import lax
from jax.experimental import pallas as pl
from jax.experimental.pallas import tpu as pltpu

F32 = jnp.float32
BF16 = jnp.bfloat16

RET_HEADS = 4
HGRN_HEADS = 8
FNET_GROUPS = 4
N_BRANCH = 3
CONV_W = 3
ROPE_BASE = 10000.0
LB_FLOOR = 1e-30
EPS = 1e-6
LOG2_E = 1.4426950408889634
GELU_C0 = math.sqrt(2.0 / math.pi)
GELU_C1 = GELU_C0 * 0.044715

V7X_VMEM_LIMIT_BYTES = 56 * 1024 * 1024
SUBLANES = 8
BF16_ROWS = 16

RET_CHUNK = 256
HGRN_CHUNK = 128
ROW_TILE = 512


def _dot(a, b):
    return jnp.dot(a, b, preferred_element_type=F32)


def _dot_nt(a, b):
    return lax.dot_general(a, b, (((1,), (1,)), ((), ())), preferred_element_type=F32)


def _dot_tn(a, b):
    return lax.dot_general(a, b, (((0,), (0,)), ((), ())), preferred_element_type=F32)


def _rms(x):
    return x * lax.rsqrt(jnp.mean(x * x, axis=-1, keepdims=True) + EPS)


def _params(*sem):
    return pltpu.CompilerParams(dimension_semantics=sem,
                                vmem_limit_bytes=V7X_VMEM_LIMIT_BYTES)


def _resident(shape, index_map):
    return pl.BlockSpec(shape, index_map, pipeline_mode=pl.Buffered(1))


def _rope_kernel(pos_ref, invf_ref, cos_ref, sin_ref):
    ang = pos_ref[...] * invf_ref[...]
    cos_ref[...] = jnp.cos(ang)
    sin_ref[...] = jnp.sin(ang)


def rope_tables(positions, half):
    B, S = positions.shape
    pos = positions.astype(F32).reshape(B, S, 1)
    inv_freq = (ROPE_BASE ** (-jnp.arange(half, dtype=F32) / half)).reshape(1, half)
    out = jax.ShapeDtypeStruct((B, S, half), F32)
    return pl.pallas_call(
        _rope_kernel,
        out_shape=(out, out),
        grid=(B,),
        in_specs=[pl.BlockSpec((None, S, 1), lambda b: (b, 0, 0)),
                  pl.BlockSpec((1, half), lambda b: (0, 0))],
        out_specs=(pl.BlockSpec((None, S, half), lambda b: (b, 0, 0)),
                   pl.BlockSpec((None, S, half), lambda b: (b, 0, 0))),
        compiler_params=_params("parallel"),
        name="rope_tables",
    )(pos, inv_freq)


def _norm_kernel(x_ref, w_ref, o_ref):
    o_ref[...] = (_rms(x_ref[...]) * w_ref[...]).astype(o_ref.dtype)


def rms_norm_bf16(x2, w, tm=1024):
    T, D = x2.shape
    return pl.pallas_call(
        _norm_kernel,
        out_shape=jax.ShapeDtypeStruct((T, D), BF16),
        grid=(T // tm,),
        in_specs=[pl.BlockSpec((tm, D), lambda i: (i, 0)),
                  pl.BlockSpec((1, D), lambda i: (0, 0))],
        out_specs=pl.BlockSpec((tm, D), lambda i: (i, 0)),
        compiler_params=_params("parallel"),
        name="rms_norm",
    )(x2, w.reshape(1, D))


def _ret_kernel(lg_ref, xn_ref, wq_ref, wk_ref, wv_ref, wg_ref, cos_ref, sin_ref,
                o_ref, qi_s, qd_s, ki_s, v_s, g_s, st_s, kvb_s, run_s, *, seq, dk, dv):
    C = RET_CHUNK
    R = seq // C
    half = dk // 2
    lg = lg_ref[pl.program_id(1)]
    ret_scale = dk ** -0.5

    def rows_of(n):
        return pl.ds(pl.multiple_of(n * C, C), C)

    pos = lax.broadcasted_iota(jnp.int32, (C, 1), 0).astype(F32)
    qdec_f = jnp.exp(lg * (pos + 1.0))
    qdec_b = jnp.exp(lg * (C - pos))
    kdec_f = jnp.exp(lg * (C - 1.0 - pos))
    kdec_b = jnp.exp(lg * pos)
    chunk_dec = jnp.exp(lg * C)
    ii = lax.broadcasted_iota(jnp.int32, (C, C), 0)
    jj = lax.broadcasted_iota(jnp.int32, (C, C), 1)
    decay = jnp.exp(lg * jnp.abs(ii - jj).astype(F32))

    run_s[...] = jnp.zeros_like(run_s)

    def proj(t, carry):
        rows = pl.ds(pl.multiple_of(t * ROW_TILE, ROW_TILE), ROW_TILE)
        xc = xn_ref[rows, :]
        cos = cos_ref[rows, :]
        sin = sin_ref[rows, :]
        q = _dot(xc, wq_ref[...])
        q1, q2 = q[:, :half], q[:, half:]
        q = jnp.concatenate([q1 * cos - q2 * sin, q1 * sin + q2 * cos], axis=-1)
        k = _dot(xc, wk_ref[...]) * ret_scale
        k1, k2 = k[:, :half], k[:, half:]
        k = jnp.concatenate([k1 * cos - k2 * sin, k1 * sin + k2 * cos], axis=-1)
        v = _dot(xc, wv_ref[...]).astype(BF16)
        g = _dot(xc, wg_ref[...])
        qi_s[rows, :] = q.astype(BF16)
        ki_s[rows, :] = k.astype(BF16)
        v_s[rows, :] = v
        g_s[rows, :] = (g * jax.nn.sigmoid(g)).astype(BF16)
        for j in range(ROW_TILE // C):
            n = t * (ROW_TILE // C) + j
            sl = slice(j * C, (j + 1) * C)
            qd_s[rows_of(n), :] = jnp.concatenate([q[sl] * qdec_f, q[sl] * qdec_b],
                                                  axis=-1).astype(BF16)
            st_s[n, pl.ds(0, dk), :] = run_s[...].astype(BF16)
            run_s[...] = run_s[...] * chunk_dec + _dot_tn((k[sl] * kdec_f).astype(BF16), v[sl])
            kvb_s[n] = _dot_tn((k[sl] * kdec_b).astype(BF16), v[sl])
        return carry

    lax.fori_loop(0, seq // ROW_TILE, proj, 0)

    run_s[...] = jnp.zeros_like(run_s)

    def bwd(t, carry):
        n = R - 1 - t
        st_s[n, pl.ds(dk, dk), :] = run_s[...].astype(BF16)
        run_s[...] = run_s[...] * chunk_dec + kvb_s[n]
        return carry

    lax.fori_loop(0, R, bwd, 0)

    def out(n, carry):
        rows = rows_of(n)
        s = _dot_nt(qi_s[rows, :], ki_s[rows, :]) * decay
        o = _dot(s.astype(BF16), v_s[rows, :]) + _dot(qd_s[rows, :], st_s[n])
        o_ref[rows, :] = (_rms(o) * g_s[rows, :].astype(F32)).astype(o_ref.dtype)
        return carry

    lax.fori_loop(0, R, out, 0, unroll=2)


def retention_branch(xn3, w_in_b, layer, cos, sin, log_gamma):
    B, S, D = xn3.shape
    dk = D // RET_HEADS
    dv = 2 * dk
    H = RET_HEADS
    qk_blocks = D // dk
    v_off = 2 * D // dv
    g_off = v_off + H
    kern = functools.partial(_ret_kernel, seq=S, dk=dk, dv=dv)
    return pl.pallas_call(
        kern,
        out_shape=jax.ShapeDtypeStruct((B, S, H * dv), BF16),
        grid=(B, H),
        in_specs=[
            pl.BlockSpec(memory_space=pltpu.SMEM),
            pl.BlockSpec((None, S, D), lambda b, h: (b, 0, 0)),
            pl.BlockSpec((None, D, dk), lambda b, h: (layer, 0, h)),
            pl.BlockSpec((None, D, dk), lambda b, h: (layer, 0, qk_blocks + h)),
            pl.BlockSpec((None, D, dv), lambda b, h: (layer, 0, v_off + h)),
            pl.BlockSpec((None, D, dv), lambda b, h: (layer, 0, g_off + h)),
            pl.BlockSpec((None, S, dk // 2), lambda b, h: (b, 0, 0)),
            pl.BlockSpec((None, S, dk // 2), lambda b, h: (b, 0, 0)),
        ],
        out_specs=pl.BlockSpec((None, S, dv), lambda b, h: (b, 0, h)),
        scratch_shapes=[
            pltpu.VMEM((S, dk), BF16),
            pltpu.VMEM((S, 2 * dk), BF16),
            pltpu.VMEM((S, dk), BF16),
            pltpu.VMEM((S, dv), BF16),
            pltpu.VMEM((S, dv), BF16),
            pltpu.VMEM((S // RET_CHUNK, 2 * dk, dv), BF16),
            pltpu.VMEM((S // RET_CHUNK, dk, dv), F32),
            pltpu.VMEM((dk, dv), F32),
        ],
        compiler_params=_params("parallel", "arbitrary"),
        name="retention",
    )(log_gamma, xn3, w_in_b, w_in_b, w_in_b, w_in_b, cos, sin)


def _hgrn_gate(z, lb):
    e = jnp.exp(-jnp.abs(z))
    pos = z >= 0.0
    sig_neg_num = jnp.where(pos, e, 1.0)
    num = jnp.where(pos, 1.0, e) + jnp.maximum(lb, LB_FLOOR) * sig_neg_num
    log_f = jnp.log(num) - jnp.log(1.0 + e)
    return log_f, (1.0 - lb) * sig_neg_num / (1.0 + e)


def _boundary_rows(cum_ref, base, m, reverse, row_in_group):
    C = HGRN_CHUNK
    d = cum_ref.shape[1]
    blk = 2 * m
    off = m if reverse else m - 1
    pieces = []
    if blk >= SUBLANES:
        for b in range(C // blk):
            pieces.append(jnp.broadcast_to(cum_ref[pl.ds(base + (b * blk + off), 1), :], (blk, d)))
    else:
        for g in range(C // SUBLANES):
            val = None
            for u in range(SUBLANES // blk):
                row = g * SUBLANES + u * blk + off
                piece = jnp.broadcast_to(cum_ref[pl.ds(base + row, 1), :], (SUBLANES, d))
                val = piece if val is None else jnp.where(row_in_group >= u * blk, piece, val)
            pieces.append(val)
    return jnp.concatenate(pieces, axis=0) if len(pieces) > 1 else pieces[0]


def _level_operands(level, q, k, cum, cum_ref, base, consts, reverse):
    C = HGRN_CHUNK
    _, row_in_group, signs = consts
    m = 2 ** level
    if 2 * m <= SUBLANES:
        sign = signs[level]
        ref_pt = _boundary_rows(cum_ref, base, m, reverse, row_in_group)
        x = (jnp.where(sign > 0.0, q, k) * jnp.exp2((cum - ref_pt) * sign)).astype(BF16)
        return x, x, list(range(C // SUBLANES))
    xq, xall, q_groups = [], [], []
    for b in range(C // (2 * m)):
        first = slice(b * 2 * m, b * 2 * m + m)
        second = slice(b * 2 * m + m, (b + 1) * 2 * m)
        q_rows, k_rows = (first, second) if reverse else (second, first)
        edge = k_rows.start if reverse else k_rows.stop - 1
        ref_pt = cum_ref[pl.ds(base + edge, 1), :]
        xq_b = q[q_rows] * jnp.exp2(cum[q_rows] - ref_pt)
        xk_b = k[k_rows] * jnp.exp2(ref_pt - cum[k_rows])
        xq.append(xq_b)
        xall.extend([xq_b, xk_b] if reverse else [xk_b, xq_b])
        q_groups.extend(range(q_rows.start // SUBLANES, q_rows.stop // SUBLANES))
    return (jnp.concatenate(xq, axis=0).astype(BF16), jnp.concatenate(xall, axis=0).astype(BF16),
            q_groups)


def _paired_dot_nt(lhs_a, rhs_a, lhs_b, rhs_b):
    rhs = jnp.concatenate([rhs_a, rhs_b], axis=1)
    lhs = jnp.concatenate(
        [jnp.concatenate([lhs_a, jnp.zeros_like(lhs_a)], axis=1),
         jnp.concatenate([jnp.zeros_like(lhs_b), lhs_b], axis=1)], axis=0)
    s = _dot_nt(lhs, rhs)
    return s[:lhs_a.shape[0]], s[lhs_a.shape[0]:]


def _assemble_scores(scores, level_id):
    C = HGRN_CHUNK
    rows = [jnp.zeros((SUBLANES, C), F32) for _ in range(C // SUBLANES)]
    for level, (s, q_groups) in enumerate(scores):
        for i, g in enumerate(q_groups):
            lid = level_id[g * SUBLANES:(g + 1) * SUBLANES]
            rows[g] = jnp.where(lid == level, s[i * SUBLANES:(i + 1) * SUBLANES], rows[g])
    return jnp.concatenate(rows, axis=0)


def _hgrn_intra_pair(fwd, bwd, consts):
    C = HGRN_CHUNK
    args = ((fwd, consts[0], False), (bwd, consts[1], True))
    scores = ([], [])
    for level in range(C.bit_length() - 1):
        ops = [_level_operands(level, q, k, cum, cum_ref, base, cst, rev)
               for (q, k, _, cum, cum_ref, base), cst, rev in args]
        s_f, s_b = _paired_dot_nt(ops[0][0], ops[0][1], ops[1][0], ops[1][1])
        scores[0].append((s_f, ops[0][2]))
        scores[1].append((s_b, ops[1][2]))
    outs = []
    for idx, ((q, k, v, _, _, _), cst, _) in enumerate(args):
        attn = _assemble_scores(scores[idx], cst[0])
        o = _dot(attn.astype(BF16), v.astype(BF16))
        outs.append(o + jnp.sum(q * k, axis=-1, keepdims=True) * v)
    return outs


def _hgrn_consts(reverse, d):
    C = HGRN_CHUNK
    ii = lax.broadcasted_iota(jnp.int32, (C, C), 0)
    jj = lax.broadcasted_iota(jnp.int32, (C, C), 1)
    diff = ii ^ jj
    level_id = jnp.full((C, C), -1, jnp.int32)
    n_levels = C.bit_length() - 1
    for level in range(n_levels):
        level_id = jnp.where((diff >> level) == 1, level, level_id)
    level_id = jnp.where((ii < jj) if reverse else (ii > jj), level_id, -1)
    rows = lax.broadcasted_iota(jnp.int32, (C, d), 0)
    row_in_group = lax.broadcasted_iota(jnp.int32, (SUBLANES, d), 0)
    signs = []
    for level in range(SUBLANES.bit_length() - 1):
        second = ((rows >> level) & 1) == 1
        is_query = jnp.logical_not(second) if reverse else second
        signs.append(jnp.where(is_query, 1.0, -1.0))
    return level_id, row_in_group, signs


def _chunk_cumsum(x, reverse, row_in_group):
    rows, d = x.shape
    groups = rows // SUBLANES
    per_chunk = HGRN_CHUNK // SUBLANES
    y = x.reshape(groups, SUBLANES, d)
    step = 1
    while step < SUBLANES:
        rolled = pltpu.roll(y, (SUBLANES - step) if reverse else step, 1)
        valid = (row_in_group < SUBLANES - step) if reverse else (row_in_group >= step)
        y = y + jnp.where(valid, rolled, 0.0)
        step *= 2
    out = [None] * groups
    for c in range(rows // HGRN_CHUNK):
        order = range(c * per_chunk, (c + 1) * per_chunk)
        carry = None
        for g in (reversed(order) if reverse else order):
            yg = y[g] if carry is None else y[g] + carry
            out[g] = yg
            edge = 0 if reverse else SUBLANES - 1
            carry = jnp.broadcast_to(yg[edge:edge + 1, :], (SUBLANES, d))
    return jnp.concatenate(out, axis=0)


def _hgrn_kernel(xn_ref, w_ref, lbf_ref, lbb_ref, nw_ref, o_ref,
                 q_s, v_s, g_s, acc_s, kf_s, kb_s, cumf_s, cumb_s,
                 qef_s, qeb_s, ktf_s, ktb_s, decf_s, decb_s, *, seq, dk):
    C = HGRN_CHUNK
    R = seq // C
    per_tile = ROW_TILE // C
    scale = dk ** -0.5

    def tile_rows(n):
        return pl.ds(pl.multiple_of(n * ROW_TILE, ROW_TILE), ROW_TILE)

    dirs = ((False, lbf_ref, kf_s, cumf_s, qef_s, ktf_s, decf_s),
            (True, lbb_ref, kb_s, cumb_s, qeb_s, ktb_s, decb_s))
    row_in_group = lax.broadcasted_iota(jnp.int32, (1, SUBLANES, dk), 1)

    def proj(n, carry):
        rows = tile_rows(n)
        u = _dot(xn_ref[rows, :], w_ref[...])
        hq = u[:, 0 * dk:1 * dk]
        q = hq * jax.nn.sigmoid(hq) * scale
        q_s[rows, :] = q
        for idx, (reverse, lb_ref, k_s, cum_s, qe_s, kt_s, dec_s) in enumerate(dirs):
            lf, kk = _hgrn_gate(u[:, (1 + idx) * dk:(2 + idx) * dk], lb_ref[...])
            k_s[rows, :] = kk
            cum = _chunk_cumsum(lf * LOG2_E, reverse, row_in_group)
            cum_s[rows, :] = cum
            for j in range(per_tile):
                sl = slice(j * C, (j + 1) * C)
                edge = j * C if reverse else (j + 1) * C - 1
                total = cum[edge:edge + 1, :]
                r0 = pl.multiple_of(n * ROW_TILE + j * C, C)
                qe_s[pl.ds(r0, C), :] = (q[sl] * jnp.exp2(cum[sl])).astype(BF16)
                kt_s[pl.ds(r0, C), :] = (kk[sl] * jnp.exp2(total - cum[sl])).astype(BF16)
                dec_s[n * per_tile + j] = jnp.broadcast_to(jnp.exp2(total), (SUBLANES, dk))
        v_s[rows, :] = u[:, 3 * dk:4 * dk]
        hg = u[:, 4 * dk:5 * dk]
        g_s[rows, :] = hg * jax.nn.sigmoid(hg)
        acc_s[rows, :] = jnp.zeros((ROW_TILE, dk), F32)
        return carry

    lax.fori_loop(0, seq // ROW_TILE, proj, 0)

    consts = (_hgrn_consts(False, dk), _hgrn_consts(True, dk))
    group = 2

    def chunk_pair(cf, cb, states):
        data, rows = [], []
        for c, (_, _, k_s, cum_s, _, _, _) in zip((cf, cb), dirs):
            base = pl.multiple_of(c * C, C)
            r = pl.ds(base, C)
            data.append((q_s[r, :], k_s[r, :], v_s[r, :], cum_s[r, :], cum_s, base))
            rows.append(r)
        o_f, o_b = _hgrn_intra_pair(data[0], data[1], consts)
        i_f, i_b = _paired_dot_nt(qef_s[rows[0], :], states[0].astype(BF16),
                                  qeb_s[rows[1], :], states[1].astype(BF16))
        acc_s[rows[0], :] += o_f + i_f
        acc_s[rows[1], :] += o_b + i_b
        new_states = []
        for c, r, st, (_, _, v, _, _, _), (_, _, _, _, _, kt_s, dec_s) in zip(
                (cf, cb), rows, states, data, dirs):
            dec = jnp.tile(dec_s[c], (dk // SUBLANES, 1))
            new_states.append(st * dec + _dot_tn(v.astype(BF16), kt_s[r, :]))
        return tuple(new_states)

    def step(i, states):
        for j in range(group):
            states = chunk_pair(i * group + j, R - 1 - (i * group + j), states)
        return states

    zero = jnp.zeros((dk, dk), F32)
    lax.fori_loop(0, R // group, step, (zero, zero))

    def finish(n, carry):
        rows = tile_rows(n)
        o_ref[rows, :] = (_rms(acc_s[rows, :]) * nw_ref[...] * g_s[rows, :]).astype(o_ref.dtype)
        return carry

    lax.fori_loop(0, seq // ROW_TILE, finish, 0)


def hgrn_branch(xn3, w_hgrn_b, lb_f, lb_b, norm_w):
    B, S, D = xn3.shape
    H = HGRN_HEADS
    dk = D // H
    kern = functools.partial(_hgrn_kernel, seq=S, dk=dk)
    vec = lambda: pltpu.VMEM((S, dk), F32)
    half = lambda: pltpu.VMEM((S, dk), BF16)
    dec = lambda: pltpu.VMEM((S // HGRN_CHUNK, SUBLANES, dk), F32)
    return pl.pallas_call(
        kern,
        out_shape=jax.ShapeDtypeStruct((B, S, D), BF16),
        grid=(B, H),
        in_specs=[
            pl.BlockSpec((None, S, D), lambda b, h: (b, 0, 0)),
            pl.BlockSpec((D, 5 * dk), lambda b, h: (0, h)),
            pl.BlockSpec((1, dk), lambda b, h: (0, h)),
            pl.BlockSpec((1, dk), lambda b, h: (0, h)),
            pl.BlockSpec((1, dk), lambda b, h: (0, 0)),
        ],
        out_specs=pl.BlockSpec((None, S, dk), lambda b, h: (b, 0, h)),
        scratch_shapes=[vec(), vec(), vec(), vec(), vec(), vec(), vec(), vec(),
                        half(), half(), half(), half(), dec(), dec()],
        compiler_params=_params("parallel", "arbitrary"),
        name="hgrn2",
    )(xn3, w_hgrn_b, lb_f.reshape(1, D), lb_b.reshape(1, D), norm_w.reshape(1, dk))


def _fnet_proj_kernel(xn_ref, w_ref, cs_ref, o_ref, *, gdim):
    fu = _dot(xn_ref[...], w_ref[...]).astype(BF16)
    for g in range(FNET_GROUPS):
        t = _dot(fu[:, g * gdim:(g + 1) * gdim], cs_ref[...])
        o_ref[0, :, g * gdim:(g + 1) * gdim] = t[:, :gdim].astype(o_ref.dtype)
        o_ref[1, :, g * gdim:(g + 1) * gdim] = t[:, gdim:].astype(o_ref.dtype)


def _seq_dft_kernel(dft_ref, rhs_ref, o_ref):
    o_ref[...] = _dot(dft_ref[...], rhs_ref[...]).astype(o_ref.dtype)


def _dft_tables(n):
    idx = np.arange(n, dtype=np.int64)
    ang = 2.0 * np.pi * ((idx[:, None] * idx[None, :]) % n).astype(np.float64) / n
    s = 1.0 / math.sqrt(n)
    return np.cos(ang) * s, np.sin(ang) * s


def fourier_branch(xn3, w_in_b, layer, fu_off_blocks, tm=512):
    B, S, D = xn3.shape
    W = D
    gdim = W // FNET_GROUPS
    c_small, s_small = _dft_tables(gdim)
    cs_small = jnp.asarray(np.concatenate([c_small, s_small], axis=1), dtype=BF16)
    c_seq, s_seq = _dft_tables(S)
    dft_seq = jnp.asarray(np.concatenate([c_seq, -s_seq], axis=1), dtype=BF16)
    tiles = S // tm
    rhs = pl.pallas_call(
        functools.partial(_fnet_proj_kernel, gdim=gdim),
        out_shape=jax.ShapeDtypeStruct((B, 2, S, W), BF16),
        grid=(B, tiles),
        in_specs=[pl.BlockSpec((None, tm, D), lambda b, r: (b, r, 0)),
                  pl.BlockSpec((None, D, W), lambda b, r: (layer, 0, fu_off_blocks)),
                  pl.BlockSpec((gdim, 2 * gdim), lambda b, r: (0, 0))],
        out_specs=pl.BlockSpec((None, 2, tm, W), lambda b, r: (b, 0, r, 0)),
        compiler_params=_params("parallel", "parallel"),
        name="fnet_proj",
    )(xn3, w_in_b, cs_small)
    rhs = rhs.reshape(B, 2 * S, W)
    return pl.pallas_call(
        _seq_dft_kernel,
        out_shape=jax.ShapeDtypeStruct((B, S, W), BF16),
        grid=(B, tiles),
        in_specs=[pl.BlockSpec((tm, 2 * S), lambda b, r: (r, 0)),
                  pl.BlockSpec((None, 2 * S, W), lambda b, r: (b, 0, 0))],
        out_specs=pl.BlockSpec((None, tm, W), lambda b, r: (b, r, 0)),
        compiler_params=_params("parallel", "arbitrary"),
        name="fnet_seq_dft",
    )(dft_seq, rhs)


def _merge_kernel(x_ref, xn_ref, ro_ref, ho_ref, fo_ref, wga_ref, wro_ref, who_ref,
                  wf_ref, wout_ref, nw_ref, o_ref, *, d):
    xn = xn_ref[...]

    def gate(i):
        return jax.nn.sigmoid(_dot(xn, wga_ref[:, i * d:(i + 1) * d]))

    mix = gate(0) * _dot(ro_ref[...], wro_ref[...])
    mix += gate(1) * _dot(ho_ref[...], who_ref[...])
    mix += gate(2) * _dot(fo_ref[...], wf_ref[...])
    y = _dot(mix.astype(BF16), wout_ref[...])
    o_ref[...] = x_ref[...] + _rms(y) * nw_ref[...]


def merge_branches(x2, xn2, ro2, ho2, fo2, w_in_b, layer, ga_off_blocks,
                   w_ret_o, w_hgrn_o, w_fnet, w_out, norm_w, tm=512):
    T, D = x2.shape
    RV = ro2.shape[1]
    tile = lambda w: pl.BlockSpec((tm, w), lambda i: (i, 0))
    return pl.pallas_call(
        functools.partial(_merge_kernel, d=D),
        out_shape=jax.ShapeDtypeStruct((T, D), F32),
        grid=(T // tm,),
        in_specs=[tile(D), tile(D), tile(RV), tile(D), tile(D),
                  _resident((None, D, N_BRANCH * D), lambda i: (layer, 0, ga_off_blocks)),
                  _resident((None, RV, D), lambda i: (layer, 0, 0)),
                  _resident((None, D, D), lambda i: (layer, 0, 0)),
                  _resident((None, D, D), lambda i: (layer, 0, 0)),
                  _resident((None, D, D), lambda i: (layer, 0, 0)),
                  pl.BlockSpec((1, D), lambda i: (0, 0))],
        out_specs=tile(D),
        compiler_params=_params("parallel"),
        name="merge",
    )(x2, xn2, ro2, ho2, fo2, w_in_b, w_ret_o, w_hgrn_o, w_fnet, w_out, norm_w.reshape(1, D))


def _ffn_kernel(x_ref, xp_ref, xnx_ref, nw_in_ref, wup_ref, cw_ref, cb_ref, wdn_ref, nw_out_ref,
                *rest, tm, tiles_per_seq, d_ff, fc, emit_next):
    if emit_next:
        nw_next_ref, o_ref, xn_ref, hn_s, acc_s = rest
    else:
        (o_ref, hn_s, acc_s), nw_next_ref, xn_ref = rest, None, None
    i = pl.program_id(0)
    r = i % tiles_per_seq
    halo = BF16_ROWS
    x = x_ref[...]
    nw = nw_in_ref[...]
    hp = jnp.where(r == 0, 0.0, _rms(xp_ref[...]) * nw)
    hx = jnp.where(r == tiles_per_seq - 1, 0.0, _rms(xnx_ref[...]) * nw)
    hn = jnp.concatenate([hp, _rms(x) * nw, hx], axis=0).astype(BF16)
    n_ext = tm + 2 * halo
    hn_s[...] = hn
    acc_s[...] = jnp.zeros_like(acc_s)

    def conv(col, scale):
        cols = pl.ds(pl.multiple_of(col, fc), fc)
        h = _dot(hn_s[...], wup_ref[:, cols])
        cw = cw_ref[:, cols] * scale
        prev = pltpu.roll(h, 1, 0)[halo:halo + tm]
        nxt = pltpu.roll(h, n_ext - 1, 0)[halo:halo + tm]
        return (cb_ref[:, cols] * scale + prev * cw[0:1] + h[halo:halo + tm] * cw[1:2]
                + nxt * cw[2:3])

    def chunk(c, carry):
        gate = conv(c * fc, 1.0)
        half_up = conv(d_ff + c * fc, 0.5)
        inner = gate * (GELU_C0 + GELU_C1 * (gate * gate))
        act = (gate * (1.0 + jnp.tanh(inner)) * half_up).astype(BF16)
        acc_s[...] += _dot(act, wdn_ref[pl.ds(pl.multiple_of(c * fc, fc), fc), :])
        return carry

    lax.fori_loop(0, d_ff // fc, chunk, 0, unroll=2)
    y = x + _rms(acc_s[...]) * nw_out_ref[...]
    o_ref[...] = y
    if xn_ref is not None:
        xn_ref[...] = (_rms(y) * nw_next_ref[...]).astype(xn_ref.dtype)


def conv_ffn_block(x2, seq, w_up, conv_w, conv_b, w_down, nw_in, nw_out, layer, nw_next=None,
                   tm=1024, fc=256):
    T, D = x2.shape
    d_ff = w_down.shape[1]
    halo = BF16_ROWS
    tps = seq // tm
    hb = tm // halo
    n_hb = T // halo
    emit_next = nw_next is not None
    kern = functools.partial(_ffn_kernel, tm=tm, tiles_per_seq=tps, d_ff=d_ff, fc=fc,
                             emit_next=emit_next)
    vec = pl.BlockSpec((1, D), lambda i: (0, 0))
    tile = pl.BlockSpec((tm, D), lambda i: (i, 0))
    in_specs = [tile,
                pl.BlockSpec((halo, D), lambda i: (jnp.maximum(i * hb - 1, 0), 0)),
                pl.BlockSpec((halo, D), lambda i: (jnp.minimum((i + 1) * hb, n_hb - 1), 0)),
                vec,
                _resident((None, D, 2 * d_ff), lambda i: (layer, 0, 0)),
                pl.BlockSpec((None, CONV_W, 2 * d_ff), lambda i: (layer, 0, 0)),
                pl.BlockSpec((None, 1, 2 * d_ff), lambda i: (layer, 0, 0)),
                _resident((None, d_ff, D), lambda i: (layer, 0, 0)),
                vec]
    args = [x2, x2, x2, nw_in.reshape(1, D), w_up, conv_w, conv_b, w_down, nw_out.reshape(1, D)]
    out_shape = jax.ShapeDtypeStruct((T, D), F32)
    out_specs = tile
    if emit_next:
        in_specs.append(vec)
        args.append(nw_next.reshape(1, D))
        out_shape = (out_shape, jax.ShapeDtypeStruct((T, D), BF16))
        out_specs = (tile, tile)
    return pl.pallas_call(
        kern,
        out_shape=out_shape,
        grid=(T // tm,),
        in_specs=in_specs,
        out_specs=out_specs,
        scratch_shapes=[pltpu.VMEM((tm + 2 * halo, D), BF16),
                        pltpu.VMEM((tm, D), F32)],
        compiler_params=_params("parallel"),
        name="conv_ffn",
    )(*args)


def kernel(x, positions, norm_w, w_in, hgrn_lb_logits, hgrn_norm_w, w_ret_o, w_hgrn_o,
           w_fnet, w_out, w_up, conv_w, conv_b, w_down):
    B, S, D = x.shape
    depth = w_in.shape[0]
    T = B * S
    dk_h = D // HGRN_HEADS

    hgrn_off = 2 * D + 2 * 2 * D
    fu_off = hgrn_off + 5 * D
    ga_off = fu_off + D

    w_in_b = w_in.astype(BF16)
    w_hgrn = w_in_b[:, :, hgrn_off:fu_off].reshape(depth, D, 5, HGRN_HEADS, dk_h)
    w_hgrn = w_hgrn.transpose(0, 1, 3, 2, 4).reshape(depth, D, 5 * D)
    w_ret_o_b = w_ret_o.astype(BF16)
    w_hgrn_o_b = w_hgrn_o.astype(BF16)
    w_fnet_b = w_fnet.astype(BF16)
    w_out_b = w_out.astype(BF16)
    w_up_b = w_up.astype(BF16)
    w_down_b = w_down.astype(BF16)
    conv_b3 = conv_b.reshape(depth, 1, -1)

    log_gamma = jnp.log(1.0 - 2.0 ** (-5.0 - jnp.arange(RET_HEADS, dtype=F32)))
    p = jax.nn.softmax(hgrn_lb_logits.astype(F32), axis=1)
    lower_bounds = jnp.cumsum(p, axis=1) - p[:, :1]

    cos, sin = rope_tables(positions, D // RET_HEADS // 2)

    x2 = x.reshape(T, D)
    xn2 = rms_norm_bf16(x2, norm_w[0, 0])
    for l in range(depth):
        xn3 = xn2.reshape(B, S, D)
        ro = retention_branch(xn3, w_in_b, l, cos, sin, log_gamma)
        ho = hgrn_branch(xn3, w_hgrn[l], lower_bounds[0, l], lower_bounds[1, l], hgrn_norm_w[l])
        fo = fourier_branch(xn3, w_in_b, l, fu_off // D)
        x2 = merge_branches(x2, xn2, ro.reshape(T, -1), ho.reshape(T, D), fo.reshape(T, D),
                            w_in_b, l, ga_off // (N_BRANCH * D),
                            w_ret_o_b, w_hgrn_o_b, w_fnet_b, w_out_b, norm_w[l, 1])
        if l + 1 < depth:
            x2, xn2 = conv_ffn_block(x2, S, w_up_b, conv_w, conv_b3, w_down_b,
                                     norm_w[l, 2], norm_w[l, 3], l, nw_next=norm_w[l + 1, 0])
        else:
            x2 = conv_ffn_block(x2, S, w_up_b, conv_w, conv_b3, w_down_b,
                                norm_w[l, 2], norm_w[l, 3], l)
    return x2.reshape(B, S, D)
```

```python
import functools
import math

import numpy as np
import jax
import jax.numpy as jnp
from jax import lax
from jax.experimental import pallas as pl
from jax.experimental.pallas import tpu as pltpu

F32 = jnp.float32
BF16 = jnp.bfloat16

RET_HEADS = 4
HGRN_HEADS = 8
FNET_GROUPS = 4
N_BRANCH = 3
CONV_W = 3
ROPE_BASE = 10000.0
LB_FLOOR = 1e-30
EPS = 1e-6
LOG2_E = 1.4426950408889634
GELU_C0 = math.sqrt(2.0 / math.pi)
GELU_C1 = GELU_C0 * 0.044715

V7X_VMEM_LIMIT_BYTES = 56 * 1024 * 1024
SUBLANES = 8
BF16_ROWS = 16

RET_CHUNK = 256
HGRN_CHUNK = 128
HGRN_HEADS_PER_STEP = 2
ROW_TILE = 512


def _dot(a, b):
    return jnp.dot(a, b, preferred_element_type=F32)


def _dot_nt(a, b):
    return lax.dot_general(a, b, (((1,), (1,)), ((), ())), preferred_element_type=F32)


def _dot_tn(a, b):
    return lax.dot_general(a, b, (((0,), (0,)), ((), ())), preferred_element_type=F32)


def _rms(x):
    return x * lax.rsqrt(jnp.mean(x * x, axis=-1, keepdims=True) + EPS)


def _params(*sem):
    return pltpu.CompilerParams(dimension_semantics=sem,
                                vmem_limit_bytes=V7X_VMEM_LIMIT_BYTES)


def _resident(shape, index_map):
    return pl.BlockSpec(shape, index_map, pipeline_mode=pl.Buffered(1))


def _rope_kernel(pos_ref, invf_ref, cos_ref, sin_ref):
    ang = pos_ref[...] * invf_ref[...]
    cos_ref[...] = jnp.cos(ang)
    sin_ref[...] = jnp.sin(ang)


def rope_tables(positions, half):
    B, S = positions.shape
    pos = positions.astype(F32).reshape(B, S, 1)
    inv_freq = (ROPE_BASE ** (-jnp.arange(half, dtype=F32) / half)).reshape(1, half)
    out = jax.ShapeDtypeStruct((B, S, half), F32)
    return pl.pallas_call(
        _rope_kernel,
        out_shape=(out, out),
        grid=(B,),
        in_specs=[pl.BlockSpec((None, S, 1), lambda b: (b, 0, 0)),
                  pl.BlockSpec((1, half), lambda b: (0, 0))],
        out_specs=(pl.BlockSpec((None, S, half), lambda b: (b, 0, 0)),
                   pl.BlockSpec((None, S, half), lambda b: (b, 0, 0))),
        compiler_params=_params("parallel"),
        name="rope_tables",
    )(pos, inv_freq)


def _norm_kernel(x_ref, w_ref, o_ref):
    o_ref[...] = (_rms(x_ref[...]) * w_ref[...]).astype(o_ref.dtype)


def rms_norm_bf16(x2, w, tm=1024):
    T, D = x2.shape
    return pl.pallas_call(
        _norm_kernel,
        out_shape=jax.ShapeDtypeStruct((T, D), BF16),
        grid=(T // tm,),
        in_specs=[pl.BlockSpec((tm, D), lambda i: (i, 0)),
                  pl.BlockSpec((1, D), lambda i: (0, 0))],
        out_specs=pl.BlockSpec((tm, D), lambda i: (i, 0)),
        compiler_params=_params("parallel"),
        name="rms_norm",
    )(x2, w.reshape(1, D))


def _ret_kernel(lg_ref, xn_ref, wq_ref, wk_ref, wv_ref, wg_ref, cos_ref, sin_ref,
                o_ref, qi_s, qd_s, ki_s, v_s, g_s, st_s, kvb_s, run_s, *, seq, dk, dv):
    C = RET_CHUNK
    R = seq // C
    half = dk // 2
    lg = lg_ref[pl.program_id(1)]
    ret_scale = dk ** -0.5

    def rows_of(n):
        return pl.ds(pl.multiple_of(n * C, C), C)

    pos = lax.broadcasted_iota(jnp.int32, (C, 1), 0).astype(F32)
    qdec_f = jnp.exp(lg * (pos + 1.0))
    qdec_b = jnp.exp(lg * (C - pos))
    kdec_f = jnp.exp(lg * (C - 1.0 - pos))
    kdec_b = jnp.exp(lg * pos)
    chunk_dec = jnp.exp(lg * C)
    ii = lax.broadcasted_iota(jnp.int32, (C, C), 0)
    jj = lax.broadcasted_iota(jnp.int32, (C, C), 1)
    decay = jnp.exp(lg * jnp.abs(ii - jj).astype(F32))

    run_s[...] = jnp.zeros_like(run_s)

    def proj(t, carry):
        rows = pl.ds(pl.multiple_of(t * ROW_TILE, ROW_TILE), ROW_TILE)
        xc = xn_ref[rows, :]
        cos = cos_ref[rows, :]
        sin = sin_ref[rows, :]
        q = _dot(xc, wq_ref[...])
        q1, q2 = q[:, :half], q[:, half:]
        q = jnp.concatenate([q1 * cos - q2 * sin, q1 * sin + q2 * cos], axis=-1)
        k = _dot(xc, wk_ref[...]) * ret_scale
        k1, k2 = k[:, :half], k[:, half:]
        k = jnp.concatenate([k1 * cos - k2 * sin, k1 * sin + k2 * cos], axis=-1)
        v = _dot(xc, wv_ref[...]).astype(BF16)
        g = _dot(xc, wg_ref[...])
        qi_s[rows, :] = q.astype(BF16)
        ki_s[rows, :] = k.astype(BF16)
        v_s[rows, :] = v
        g_s[rows, :] = (g * jax.nn.sigmoid(g)).astype(BF16)
        for j in range(ROW_TILE // C):
            n = t * (ROW_TILE // C) + j
            sl = slice(j * C, (j + 1) * C)
            qd_s[rows_of(n), :] = jnp.concatenate([q[sl] * qdec_f, q[sl] * qdec_b],
                                                  axis=-1).astype(BF16)
            st_s[n, pl.ds(0, dk), :] = run_s[...].astype(BF16)
            run_s[...] = run_s[...] * chunk_dec + _dot_tn((k[sl] * kdec_f).astype(BF16), v[sl])
            kvb_s[n] = _dot_tn((k[sl] * kdec_b).astype(BF16), v[sl])
        return carry

    lax.fori_loop(0, seq // ROW_TILE, proj, 0)

    run_s[...] = jnp.zeros_like(run_s)

    def bwd(t, carry):
        n = R - 1 - t
        st_s[n, pl.ds(dk, dk), :] = run_s[...].astype(BF16)
        run_s[...] = run_s[...] * chunk_dec + kvb_s[n]
        return carry

    lax.fori_loop(0, R, bwd, 0)

    def out(n, carry):
        rows = rows_of(n)
        s = _dot_nt(qi_s[rows, :], ki_s[rows, :]) * decay
        o = _dot(s.astype(BF16), v_s[rows, :]) + _dot(qd_s[rows, :], st_s[n])
        o_ref[rows, :] = (_rms(o) * g_s[rows, :].astype(F32)).astype(o_ref.dtype)
        return carry

    lax.fori_loop(0, R, out, 0, unroll=2)


def retention_branch(xn3, w_in_b, layer, cos, sin, log_gamma):
    B, S, D = xn3.shape
    dk = D // RET_HEADS
    dv = 2 * dk
    H = RET_HEADS
    qk_blocks = D // dk
    v_off = 2 * D // dv
    g_off = v_off + H
    kern = functools.partial(_ret_kernel, seq=S, dk=dk, dv=dv)
    return pl.pallas_call(
        kern,
        out_shape=jax.ShapeDtypeStruct((B, S, H * dv), BF16),
        grid=(B, H),
        in_specs=[
            pl.BlockSpec(memory_space=pltpu.SMEM),
            pl.BlockSpec((None, S, D), lambda b, h: (b, 0, 0)),
            pl.BlockSpec((None, D, dk), lambda b, h: (layer, 0, h)),
            pl.BlockSpec((None, D, dk), lambda b, h: (layer, 0, qk_blocks + h)),
            pl.BlockSpec((None, D, dv), lambda b, h: (layer, 0, v_off + h)),
            pl.BlockSpec((None, D, dv), lambda b, h: (layer, 0, g_off + h)),
            pl.BlockSpec((None, S, dk // 2), lambda b, h: (b, 0, 0)),
            pl.BlockSpec((None, S, dk // 2), lambda b, h: (b, 0, 0)),
        ],
        out_specs=pl.BlockSpec((None, S, dv), lambda b, h: (b, 0, h)),
        scratch_shapes=[
            pltpu.VMEM((S, dk), BF16),
            pltpu.VMEM((S, 2 * dk), BF16),
            pltpu.VMEM((S, dk), BF16),
            pltpu.VMEM((S, dv), BF16),
            pltpu.VMEM((S, dv), BF16),
            pltpu.VMEM((S // RET_CHUNK, 2 * dk, dv), BF16),
            pltpu.VMEM((S // RET_CHUNK, dk, dv), F32),
            pltpu.VMEM((dk, dv), F32),
        ],
        compiler_params=_params("parallel", "arbitrary"),
        name="retention",
    )(log_gamma, xn3, w_in_b, w_in_b, w_in_b, w_in_b, cos, sin)


def _hgrn_gate(z, lb):
    e = jnp.exp(-jnp.abs(z))
    pos = z >= 0.0
    sig_neg_num = jnp.where(pos, e, 1.0)
    num = jnp.where(pos, 1.0, e) + jnp.maximum(lb, LB_FLOOR) * sig_neg_num
    log_f = jnp.log(num) - jnp.log(1.0 + e)
    return log_f, (1.0 - lb) * sig_neg_num / (1.0 + e)


def _boundary_rows(cum_ref, base, m, reverse, row_in_group):
    C = HGRN_CHUNK
    d = cum_ref.shape[1]
    blk = 2 * m
    off = m if reverse else m - 1
    pieces = []
    if blk >= SUBLANES:
        for b in range(C // blk):
            pieces.append(jnp.broadcast_to(cum_ref[pl.ds(base + (b * blk + off), 1), :], (blk, d)))
    else:
        for g in range(C // SUBLANES):
            val = None
            for u in range(SUBLANES // blk):
                row = g * SUBLANES + u * blk + off
                piece = jnp.broadcast_to(cum_ref[pl.ds(base + row, 1), :], (SUBLANES, d))
                val = piece if val is None else jnp.where(row_in_group >= u * blk, piece, val)
            pieces.append(val)
    return jnp.concatenate(pieces, axis=0) if len(pieces) > 1 else pieces[0]


def _level_operands(level, q, k, cum, cum_ref, base, consts, reverse):
    C = HGRN_CHUNK
    _, row_in_group, signs = consts
    m = 2 ** level
    if 2 * m <= SUBLANES:
        sign = signs[level]
        ref_pt = _boundary_rows(cum_ref, base, m, reverse, row_in_group)
        x = (jnp.where(sign > 0.0, q, k) * jnp.exp2((cum - ref_pt) * sign)).astype(BF16)
        return x, x, list(range(C // SUBLANES))
    xq, xall, q_groups = [], [], []
    for b in range(C // (2 * m)):
        first = slice(b * 2 * m, b * 2 * m + m)
        second = slice(b * 2 * m + m, (b + 1) * 2 * m)
        q_rows, k_rows = (first, second) if reverse else (second, first)
        edge = k_rows.start if reverse else k_rows.stop - 1
        ref_pt = cum_ref[pl.ds(base + edge, 1), :]
        xq_b = q[q_rows] * jnp.exp2(cum[q_rows] - ref_pt)
        xk_b = k[k_rows] * jnp.exp2(ref_pt - cum[k_rows])
        xq.append(xq_b)
        xall.extend([xq_b, xk_b] if reverse else [xk_b, xq_b])
        q_groups.extend(range(q_rows.start // SUBLANES, q_rows.stop // SUBLANES))
    return (jnp.concatenate(xq, axis=0).astype(BF16), jnp.concatenate(xall, axis=0).astype(BF16),
            q_groups)


def _paired_dot_nt(lhs_a, rhs_a, lhs_b, rhs_b):
    rhs = jnp.concatenate([rhs_a, rhs_b], axis=1)
    lhs = jnp.concatenate(
        [jnp.concatenate([lhs_a, jnp.zeros_like(lhs_a)], axis=1),
         jnp.concatenate([jnp.zeros_like(lhs_b), lhs_b], axis=1)], axis=0)
    s = _dot_nt(lhs, rhs)
    return s[:lhs_a.shape[0]], s[lhs_a.shape[0]:]


def _assemble_scores(scores, level_id):
    C = HGRN_CHUNK
    rows = [jnp.zeros((SUBLANES, C), F32) for _ in range(C // SUBLANES)]
    for level, (s, q_groups) in enumerate(scores):
        for i, g in enumerate(q_groups):
            lid = level_id[g * SUBLANES:(g + 1) * SUBLANES]
            rows[g] = jnp.where(lid == level, s[i * SUBLANES:(i + 1) * SUBLANES], rows[g])
    return jnp.concatenate(rows, axis=0)


def _hgrn_intra_pair(fwd, bwd, consts):
    C = HGRN_CHUNK
    args = ((fwd, consts[0], False), (bwd, consts[1], True))
    scores = ([], [])
    for level in range(C.bit_length() - 1):
        ops = [_level_operands(level, q, k, cum, cum_ref, base, cst, rev)
               for (q, k, _, cum, cum_ref, base), cst, rev in args]
        s_f, s_b = _paired_dot_nt(ops[0][0], ops[0][1], ops[1][0], ops[1][1])
        scores[0].append((s_f, ops[0][2]))
        scores[1].append((s_b, ops[1][2]))
    outs = []
    for idx, ((q, k, v, _, _, _), cst, _) in enumerate(args):
        attn = _assemble_scores(scores[idx], cst[0])
        o = _dot(attn.astype(BF16), v.astype(BF16))
        outs.append(o + jnp.sum(q * k, axis=-1, keepdims=True) * v)
    return outs


def _hgrn_consts(reverse, d):
    C = HGRN_CHUNK
    ii = lax.broadcasted_iota(jnp.int32, (C, C), 0)
    jj = lax.broadcasted_iota(jnp.int32, (C, C), 1)
    diff = ii ^ jj
    level_id = jnp.full((C, C), -1, jnp.int32)
    n_levels = C.bit_length() - 1
    for level in range(n_levels):
        level_id = jnp.where((diff >> level) == 1, level, level_id)
    level_id = jnp.where((ii < jj) if reverse else (ii > jj), level_id, -1)
    rows = lax.broadcasted_iota(jnp.int32, (C, d), 0)
    row_in_group = lax.broadcasted_iota(jnp.int32, (SUBLANES, d), 0)
    signs = []
    for level in range(SUBLANES.bit_length() - 1):
        second = ((rows >> level) & 1) == 1
        is_query = jnp.logical_not(second) if reverse else second
        signs.append(jnp.where(is_query, 1.0, -1.0))
    return level_id, row_in_group, signs


def _chunk_cumsum(x, reverse, row_in_group):
    rows, d = x.shape
    groups = rows // SUBLANES
    per_chunk = HGRN_CHUNK // SUBLANES
    y = x.reshape(groups, SUBLANES, d)
    step = 1
    while step < SUBLANES:
        rolled = pltpu.roll(y, (SUBLANES - step) if reverse else step, 1)
        valid = (row_in_group < SUBLANES - step) if reverse else (row_in_group >= step)
        y = y + jnp.where(valid, rolled, 0.0)
        step *= 2
    out = [None] * groups
    for c in range(rows // HGRN_CHUNK):
        order = range(c * per_chunk, (c + 1) * per_chunk)
        carry = None
        for g in (reversed(order) if reverse else order):
            yg = y[g] if carry is None else y[g] + carry
            out[g] = yg
            edge = 0 if reverse else SUBLANES - 1
            carry = jnp.broadcast_to(yg[edge:edge + 1, :], (SUBLANES, d))
    return jnp.concatenate(out, axis=0)


def _hgrn_kernel(xn_ref, wq_ref, wzf_ref, wzb_ref, wi_ref, wg_ref, lbf_ref, lbb_ref, nw_ref, o_ref,
                 q_s, v_s, g_s, acc_s, kf_s, kb_s, cumf_s, cumb_s,
                 qef_s, qeb_s, ktf_s, ktb_s, decf_s, decb_s, *, seq, dk, heads):
    C = HGRN_CHUNK
    R = seq // C
    per_tile = ROW_TILE // C
    scale = dk ** -0.5

    def tile_rows(n):
        return pl.ds(pl.multiple_of(n * ROW_TILE, ROW_TILE), ROW_TILE)

    def head_cols(h):
        return slice(h * dk, (h + 1) * dk)

    dirs = [((False, lbf_ref, kf_s.at[h], cumf_s.at[h], qef_s.at[h], ktf_s.at[h], decf_s.at[h]),
             (True, lbb_ref, kb_s.at[h], cumb_s.at[h], qeb_s.at[h], ktb_s.at[h], decb_s.at[h]))
            for h in range(heads)]
    row_in_group = lax.broadcasted_iota(jnp.int32, (1, SUBLANES, dk), 1)

    def proj(n, carry):
        rows = tile_rows(n)
        xc = xn_ref[rows, :]
        hq_all = _dot(xc, wq_ref[...])
        z_all = (_dot(xc, wzf_ref[...]), _dot(xc, wzb_ref[...]))
        v_all = _dot(xc, wi_ref[...])
        hg_all = _dot(xc, wg_ref[...])
        for h in range(heads):
            cols = head_cols(h)
            hq = hq_all[:, cols]
            q = hq * jax.nn.sigmoid(hq) * scale
            q_s[h, rows, :] = q
            for idx, (reverse, lb_ref, k_s, cum_s, qe_s, kt_s, dec_s) in enumerate(dirs[h]):
                lf, kk = _hgrn_gate(z_all[idx][:, cols], lb_ref[:, cols])
                k_s[rows, :] = kk
                cum = _chunk_cumsum(lf * LOG2_E, reverse, row_in_group)
                cum_s[rows, :] = cum
                for j in range(per_tile):
                    sl = slice(j * C, (j + 1) * C)
                    edge = j * C if reverse else (j + 1) * C - 1
                    total = cum[edge:edge + 1, :]
                    r0 = pl.multiple_of(n * ROW_TILE + j * C, C)
                    qe_s[pl.ds(r0, C), :] = (q[sl] * jnp.exp2(cum[sl])).astype(BF16)
                    kt_s[pl.ds(r0, C), :] = (kk[sl] * jnp.exp2(total - cum[sl])).astype(BF16)
                    dec_s[n * per_tile + j] = jnp.broadcast_to(jnp.exp2(total), (SUBLANES, dk))
            v_s[h, rows, :] = v_all[:, cols]
            hg = hg_all[:, cols]
            g_s[h, rows, :] = hg * jax.nn.sigmoid(hg)
            acc_s[h, rows, :] = jnp.zeros((ROW_TILE, dk), F32)
        return carry

    lax.fori_loop(0, seq // ROW_TILE, proj, 0, unroll=2)

    consts = (_hgrn_consts(False, dk), _hgrn_consts(True, dk))

    def chunk_pair(h, cf, cb, states):
        data, rows = [], []
        for c, (_, _, k_s, cum_s, _, _, _) in zip((cf, cb), dirs[h]):
            base = pl.multiple_of(c * C, C)
            r = pl.ds(base, C)
            data.append((q_s[h, r, :], k_s[r, :], v_s[h, r, :], cum_s[r, :], cum_s, base))
            rows.append(r)
        o_f, o_b = _hgrn_intra_pair(data[0], data[1], consts)
        i_f, i_b = _paired_dot_nt(dirs[h][0][4][rows[0], :], states[0].astype(BF16),
                                  dirs[h][1][4][rows[1], :], states[1].astype(BF16))
        acc_s[h, rows[0], :] += o_f + i_f
        acc_s[h, rows[1], :] += o_b + i_b
        new_states = []
        for c, r, st, (_, _, v, _, _, _), (_, _, _, _, _, kt_s, dec_s) in zip(
                (cf, cb), rows, states, data, dirs[h]):
            dec = jnp.tile(dec_s[c], (dk // SUBLANES, 1))
            new_states.append(st * dec + _dot_tn(v.astype(BF16), kt_s[r, :]))
        return tuple(new_states)

    def step(i, states):
        return tuple(chunk_pair(h, i, R - 1 - i, states[h]) for h in range(heads))

    zero = jnp.zeros((dk, dk), F32)
    lax.fori_loop(0, R, step, tuple((zero, zero) for _ in range(heads)))

    def finish(n, carry):
        rows = tile_rows(n)
        for h in range(heads):
            o_ref[rows, head_cols(h)] = (_rms(acc_s[h, rows, :]) * nw_ref[...]
                                         * g_s[h, rows, :]).astype(o_ref.dtype)
        return carry

    lax.fori_loop(0, seq // ROW_TILE, finish, 0)


def hgrn_branch(xn3, w_in_b, layer, hgrn_off, lb_f, lb_b, norm_w):
    B, S, D = xn3.shape
    dk = D // HGRN_HEADS
    hp = HGRN_HEADS_PER_STEP
    wide = hp * dk
    kern = functools.partial(_hgrn_kernel, seq=S, dk=dk, heads=hp)
    vec = lambda: pltpu.VMEM((hp, S, dk), F32)
    half = lambda: pltpu.VMEM((hp, S, dk), BF16)
    dec = lambda: pltpu.VMEM((hp, S // HGRN_CHUNK, SUBLANES, dk), F32)

    def w_spec(group):
        first = (hgrn_off + group * D) // wide
        return pl.BlockSpec((None, D, wide), lambda b, j: (layer, 0, first + j))

    lb_spec = pl.BlockSpec((1, wide), lambda b, j: (0, j))
    return pl.pallas_call(
        kern,
        out_shape=jax.ShapeDtypeStruct((B, S, D), BF16),
        grid=(B, HGRN_HEADS // hp),
        in_specs=[pl.BlockSpec((None, S, D), lambda b, j: (b, 0, 0)),
                  w_spec(0), w_spec(1), w_spec(2), w_spec(3), w_spec(4),
                  lb_spec, lb_spec,
                  pl.BlockSpec((1, dk), lambda b, j: (0, 0))],
        out_specs=pl.BlockSpec((None, S, wide), lambda b, j: (b, 0, j)),
        scratch_shapes=[vec(), vec(), vec(), vec(), vec(), vec(), vec(), vec(),
                        half(), half(), half(), half(), dec(), dec()],
        compiler_params=_params("parallel", "arbitrary"),
        name="hgrn2",
    )(xn3, w_in_b, w_in_b, w_in_b, w_in_b, w_in_b,
      lb_f.reshape(1, D), lb_b.reshape(1, D), norm_w.reshape(1, dk))


def _fnet_proj_kernel(xn_ref, w_ref, cs_ref, o_ref, *, gdim):
    fu = _dot(xn_ref[...], w_ref[...]).astype(BF16)
    for g in range(FNET_GROUPS):
        t = _dot(fu[:, g * gdim:(g + 1) * gdim], cs_ref[...])
        o_ref[0, :, g * gdim:(g + 1) * gdim] = t[:, :gdim].astype(o_ref.dtype)
        o_ref[1, :, g * gdim:(g + 1) * gdim] = t[:, gdim:].astype(o_ref.dtype)


def _seq_dft_kernel(dft_ref, rhs_ref, o_ref):
    o_ref[...] = _dot(dft_ref[...], rhs_ref[...]).astype(o_ref.dtype)


def _dft_tables(n):
    idx = np.arange(n, dtype=np.int64)
    ang = 2.0 * np.pi * ((idx[:, None] * idx[None, :]) % n).astype(np.float64) / n
    s = 1.0 / math.sqrt(n)
    return np.cos(ang) * s, np.sin(ang) * s


def fourier_branch(xn3, w_in_b, layer, fu_off_blocks, tm=512):
    B, S, D = xn3.shape
    W = D
    gdim = W // FNET_GROUPS
    c_small, s_small = _dft_tables(gdim)
    cs_small = jnp.asarray(np.concatenate([c_small, s_small], axis=1), dtype=BF16)
    c_seq, s_seq = _dft_tables(S)
    dft_seq = jnp.asarray(np.concatenate([c_seq, -s_seq], axis=1), dtype=BF16)
    tiles = S // tm
    rhs = pl.pallas_call(
        functools.partial(_fnet_proj_kernel, gdim=gdim),
        out_shape=jax.ShapeDtypeStruct((B, 2, S, W), BF16),
        grid=(B, tiles),
        in_specs=[pl.BlockSpec((None, tm, D), lambda b, r: (b, r, 0)),
                  pl.BlockSpec((None, D, W), lambda b, r: (layer, 0, fu_off_blocks)),
                  pl.BlockSpec((gdim, 2 * gdim), lambda b, r: (0, 0))],
        out_specs=pl.BlockSpec((None, 2, tm, W), lambda b, r: (b, 0, r, 0)),
        compiler_params=_params("parallel", "parallel"),
        name="fnet_proj",
    )(xn3, w_in_b, cs_small)
    rhs = rhs.reshape(B, 2 * S, W)
    return pl.pallas_call(
        _seq_dft_kernel,
        out_shape=jax.ShapeDtypeStruct((B, S, W), BF16),
        grid=(B, tiles),
        in_specs=[pl.BlockSpec((tm, 2 * S), lambda b, r: (r, 0)),
                  pl.BlockSpec((None, 2 * S, W), lambda b, r: (b, 0, 0))],
        out_specs=pl.BlockSpec((None, tm, W), lambda b, r: (b, r, 0)),
        compiler_params=_params("parallel", "arbitrary"),
        name="fnet_seq_dft",
    )(dft_seq, rhs)


def _merge_kernel(x_ref, xn_ref, ro_ref, ho_ref, fo_ref, wga_ref, wro_ref, who_ref,
                  wf_ref, wout_ref, nw_ref, o_ref, *, d):
    xn = xn_ref[...]

    def gate(i):
        return jax.nn.sigmoid(_dot(xn, wga_ref[:, i * d:(i + 1) * d]))

    mix = gate(0) * _dot(ro_ref[...], wro_ref[...])
    mix += gate(1) * _dot(ho_ref[...], who_ref[...])
    mix += gate(2) * _dot(fo_ref[...], wf_ref[...])
    y = _dot(mix.astype(BF16), wout_ref[...])
    o_ref[...] = x_ref[...] + _rms(y) * nw_ref[...]


def merge_branches(x2, xn2, ro2, ho2, fo2, w_in_b, layer, ga_off_blocks,
                   w_ret_o, w_hgrn_o, w_fnet, w_out, norm_w, tm=512):
    T, D = x2.shape
    RV = ro2.shape[1]
    tile = lambda w: pl.BlockSpec((tm, w), lambda i: (i, 0))
    return pl.pallas_call(
        functools.partial(_merge_kernel, d=D),
        out_shape=jax.ShapeDtypeStruct((T, D), F32),
        grid=(T // tm,),
        in_specs=[tile(D), tile(D), tile(RV), tile(D), tile(D),
                  _resident((None, D, N_BRANCH * D), lambda i: (layer, 0, ga_off_blocks)),
                  _resident((None, RV, D), lambda i: (layer, 0, 0)),
                  _resident((None, D, D), lambda i: (layer, 0, 0)),
                  _resident((None, D, D), lambda i: (layer, 0, 0)),
                  _resident((None, D, D), lambda i: (layer, 0, 0)),
                  pl.BlockSpec((1, D), lambda i: (0, 0))],
        out_specs=tile(D),
        compiler_params=_params("parallel"),
        name="merge",
    )(x2, xn2, ro2, ho2, fo2, w_in_b, w_ret_o, w_hgrn_o, w_fnet, w_out, norm_w.reshape(1, D))


def _ffn_kernel(x_ref, xp_ref, xnx_ref, nw_in_ref, wup_ref, cw_ref, cb_ref, wdn_ref, nw_out_ref,
                *rest, tm, tiles_per_seq, d_ff, fc, emit_next):
    if emit_next:
        nw_next_ref, o_ref, xn_ref, hn_s, acc_s = rest
    else:
        (o_ref, hn_s, acc_s), nw_next_ref, xn_ref = rest, None, None
    i = pl.program_id(0)
    r = i % tiles_per_seq
    halo = BF16_ROWS
    x = x_ref[...]
    nw = nw_in_ref[...]
    hp = jnp.where(r == 0, 0.0, _rms(xp_ref[...]) * nw)
    hx = jnp.where(r == tiles_per_seq - 1, 0.0, _rms(xnx_ref[...]) * nw)
    hn = jnp.concatenate([hp, _rms(x) * nw, hx], axis=0).astype(BF16)
    n_ext = tm + 2 * halo
    hn_s[...] = hn
    acc_s[...] = jnp.zeros_like(acc_s)

    def conv(col, scale):
        cols = pl.ds(pl.multiple_of(col, fc), fc)
        h = _dot(hn_s[...], wup_ref[:, cols])
        cw = cw_ref[:, cols] * scale
        prev = pltpu.roll(h, 1, 0)[halo:halo + tm]
        nxt = pltpu.roll(h, n_ext - 1, 0)[halo:halo + tm]
        return (cb_ref[:, cols] * scale + prev * cw[0:1] + h[halo:halo + tm] * cw[1:2]
                + nxt * cw[2:3])

    def chunk(c, carry):
        gate = conv(c * fc, 1.0)
        half_up = conv(d_ff + c * fc, 0.5)
        inner = gate * (GELU_C0 + GELU_C1 * (gate * gate))
        act = (gate * (1.0 + jnp.tanh(inner)) * half_up).astype(BF16)
        acc_s[...] += _dot(act, wdn_ref[pl.ds(pl.multiple_of(c * fc, fc), fc), :])
        return carry

    lax.fori_loop(0, d_ff // fc, chunk, 0, unroll=2)
    y = x + _rms(acc_s[...]) * nw_out_ref[...]
    o_ref[...] = y
    if xn_ref is not None:
        xn_ref[...] = (_rms(y) * nw_next_ref[...]).astype(xn_ref.dtype)


def conv_ffn_block(x2, seq, w_up, conv_w, conv_b, w_down, nw_in, nw_out, layer, nw_next=None,
                   tm=1024, fc=256):
    T, D = x2.shape
    d_ff = w_down.shape[1]
    halo = BF16_ROWS
    tps = seq // tm
    hb = tm // halo
    n_hb = T // halo
    emit_next = nw_next is not None
    kern = functools.partial(_ffn_kernel, tm=tm, tiles_per_seq=tps, d_ff=d_ff, fc=fc,
                             emit_next=emit_next)
    vec = pl.BlockSpec((1, D), lambda i: (0, 0))
    tile = pl.BlockSpec((tm, D), lambda i: (i, 0))
    in_specs = [tile,
                pl.BlockSpec((halo, D), lambda i: (jnp.maximum(i * hb - 1, 0), 0)),
                pl.BlockSpec((halo, D), lambda i: (jnp.minimum((i + 1) * hb, n_hb - 1), 0)),
                vec,
                _resident((None, D, 2 * d_ff), lambda i: (layer, 0, 0)),
                pl.BlockSpec((None, CONV_W, 2 * d_ff), lambda i: (layer, 0, 0)),
                pl.BlockSpec((None, 1, 2 * d_ff), lambda i: (layer, 0, 0)),
                _resident((None, d_ff, D), lambda i: (layer, 0, 0)),
                vec]
    args = [x2, x2, x2, nw_in.reshape(1, D), w_up, conv_w, conv_b, w_down, nw_out.reshape(1, D)]
    out_shape = jax.ShapeDtypeStruct((T, D), F32)
    out_specs = tile
    if emit_next:
        in_specs.append(vec)
        args.append(nw_next.reshape(1, D))
        out_shape = (out_shape, jax.ShapeDtypeStruct((T, D), BF16))
        out_specs = (tile, tile)
    return pl.pallas_call(
        kern,
        out_shape=out_shape,
        grid=(T // tm,),
        in_specs=in_specs,
        out_specs=out_specs,
        scratch_shapes=[pltpu.VMEM((tm + 2 * halo, D), BF16),
                        pltpu.VMEM((tm, D), F32)],
        compiler_params=_params("parallel"),
        name="conv_ffn",
    )(*args)


def kernel(x, positions, norm_w, w_in, hgrn_lb_logits, hgrn_norm_w, w_ret_o, w_hgrn_o,
           w_fnet, w_out, w_up, conv_w, conv_b, w_down):
    B, S, D = x.shape
    depth = w_in.shape[0]
    T = B * S

    hgrn_off = 2 * D + 2 * 2 * D
    fu_off = hgrn_off + 5 * D
    ga_off = fu_off + D

    w_in_b = w_in.astype(BF16)
    w_ret_o_b = w_ret_o.astype(BF16)
    w_hgrn_o_b = w_hgrn_o.astype(BF16)
    w_fnet_b = w_fnet.astype(BF16)
    w_out_b = w_out.astype(BF16)
    w_up_b = w_up.astype(BF16)
    w_down_b = w_down.astype(BF16)
    conv_b3 = conv_b.reshape(depth, 1, -1)

    log_gamma = jnp.log(1.0 - 2.0 ** (-5.0 - jnp.arange(RET_HEADS, dtype=F32)))
    p = jax.nn.softmax(hgrn_lb_logits.astype(F32), axis=1)
    lower_bounds = jnp.cumsum(p, axis=1) - p[:, :1]

    cos, sin = rope_tables(positions, D // RET_HEADS // 2)

    x2 = x.reshape(T, D)
    xn2 = rms_norm_bf16(x2, norm_w[0, 0])
    for l in range(depth):
        xn3 = xn2.reshape(B, S, D)
        ro = retention_branch(xn3, w_in_b, l, cos, sin, log_gamma)
        ho = hgrn_branch(xn3, w_in_b, l, hgrn_off, lower_bounds[0, l], lower_bounds[1, l],
                         hgrn_norm_w[l])
        fo = fourier_branch(xn3, w_in_b, l, fu_off // D)
        x2 = merge_branches(x2, xn2, ro.reshape(T, -1), ho.reshape(T, D), fo.reshape(T, D),
                            w_in_b, l, ga_off // (N_BRANCH * D),
                            w_ret_o_b, w_hgrn_o_b, w_fnet_b, w_out_b, norm_w[l, 1])
        if l + 1 < depth:
            x2, xn2 = conv_ffn_block(x2, S, w_up_b, conv_w, conv_b3, w_down_b,
                                     norm_w[l, 2], norm_w[l, 3], l, nw_next=norm_w[l + 1, 0])
        else:
            x2 = conv_ffn_block(x2, S, w_up_b, conv_w, conv_b3, w_down_b,
                                norm_w[l, 2], norm_w[l, 3], l)
    return x2.reshape(B, S, D)
```

```python
import functools
import math

import numpy as np
import jax
import jax.numpy as jnp
from jax import lax
from jax.experimental import pallas as pl
from jax.experimental.pallas import tpu as pltpu

F32 = jnp.float32
BF16 = jnp.bfloat16

RET_HEADS = 4
HGRN_HEADS = 8
FNET_GROUPS = 4
N_BRANCH = 3
CONV_W = 3
ROPE_BASE = 10000.0
LB_FLOOR = 1e-30
EPS = 1e-6
GELU_C0 = math.sqrt(2.0 / math.pi)
GELU_C1 = GELU_C0 * 0.044715

V7X_VMEM_LIMIT_BYTES = 56 * 1024 * 1024
SUBLANES = 8
BF16_ROWS = 16

RET_CHUNK = 256
HGRN_CHUNK = 128
HGRN_HEADS_PER_STEP = 2
ROW_TILE = 512


def _dot(a, b):
    return jnp.dot(a, b, preferred_element_type=F32)


def _dot_nt(a, b):
    return lax.dot_general(a, b, (((1,), (1,)), ((), ())), preferred_element_type=F32)


def _dot_tn(a, b):
    return lax.dot_general(a, b, (((0,), (0,)), ((), ())), preferred_element_type=F32)


def _rms(x):
    return x * lax.rsqrt(jnp.mean(x * x, axis=-1, keepdims=True) + EPS)


def _params(*sem):
    return pltpu.CompilerParams(dimension_semantics=sem,
                                vmem_limit_bytes=V7X_VMEM_LIMIT_BYTES)


def _resident(shape, index_map):
    return pl.BlockSpec(shape, index_map, pipeline_mode=pl.Buffered(1))


def _rope_kernel(pos_ref, invf_ref, cos_ref, sin_ref):
    ang = pos_ref[...] * invf_ref[...]
    cos_ref[...] = jnp.cos(ang)
    sin_ref[...] = jnp.sin(ang)


def rope_tables(positions, half):
    B, S = positions.shape
    pos = positions.astype(F32).reshape(B, S, 1)
    inv_freq = (ROPE_BASE ** (-jnp.arange(half, dtype=F32) / half)).reshape(1, half)
    out = jax.ShapeDtypeStruct((B, S, half), F32)
    return pl.pallas_call(
        _rope_kernel,
        out_shape=(out, out),
        grid=(B,),
        in_specs=[pl.BlockSpec((None, S, 1), lambda b: (b, 0, 0)),
                  pl.BlockSpec((1, half), lambda b: (0, 0))],
        out_specs=(pl.BlockSpec((None, S, half), lambda b: (b, 0, 0)),
                   pl.BlockSpec((None, S, half), lambda b: (b, 0, 0))),
        compiler_params=_params("parallel"),
        name="rope_tables",
    )(pos, inv_freq)


def _norm_kernel(x_ref, w_ref, o_ref):
    o_ref[...] = (_rms(x_ref[...]) * w_ref[...]).astype(o_ref.dtype)


def rms_norm_bf16(x2, w, tm=1024):
    T, D = x2.shape
    return pl.pallas_call(
        _norm_kernel,
        out_shape=jax.ShapeDtypeStruct((T, D), BF16),
        grid=(T // tm,),
        in_specs=[pl.BlockSpec((tm, D), lambda i: (i, 0)),
                  pl.BlockSpec((1, D), lambda i: (0, 0))],
        out_specs=pl.BlockSpec((tm, D), lambda i: (i, 0)),
        compiler_params=_params("parallel"),
        name="rms_norm",
    )(x2, w.reshape(1, D))


def _ret_kernel(lg_ref, xn_ref, wq_ref, wk_ref, wv_ref, wg_ref, cos_ref, sin_ref,
                o_ref, qi_s, qd_s, ki_s, v_s, g_s, st_s, kvb_s, run_s, *, seq, dk, dv):
    C = RET_CHUNK
    R = seq // C
    half = dk // 2
    lg = lg_ref[pl.program_id(1)]
    ret_scale = dk ** -0.5

    def rows_of(n):
        return pl.ds(pl.multiple_of(n * C, C), C)

    pos = lax.broadcasted_iota(jnp.int32, (C, 1), 0).astype(F32)
    qdec_f = jnp.exp(lg * (pos + 1.0))
    qdec_b = jnp.exp(lg * (C - pos))
    kdec_f = jnp.exp(lg * (C - 1.0 - pos))
    kdec_b = jnp.exp(lg * pos)
    chunk_dec = jnp.exp(lg * C)
    ii = lax.broadcasted_iota(jnp.int32, (C, C), 0)
    jj = lax.broadcasted_iota(jnp.int32, (C, C), 1)
    decay = jnp.exp(lg * jnp.abs(ii - jj).astype(F32))

    run_s[...] = jnp.zeros_like(run_s)

    def proj(t, carry):
        rows = pl.ds(pl.multiple_of(t * ROW_TILE, ROW_TILE), ROW_TILE)
        xc = xn_ref[rows, :]
        cos = cos_ref[rows, :]
        sin = sin_ref[rows, :]
        q = _dot(xc, wq_ref[...])
        q1, q2 = q[:, :half], q[:, half:]
        q = jnp.concatenate([q1 * cos - q2 * sin, q1 * sin + q2 * cos], axis=-1)
        k = _dot(xc, wk_ref[...]) * ret_scale
        k1, k2 = k[:, :half], k[:, half:]
        k = jnp.concatenate([k1 * cos - k2 * sin, k1 * sin + k2 * cos], axis=-1)
        v = _dot(xc, wv_ref[...]).astype(BF16)
        g = _dot(xc, wg_ref[...])
        qi_s[rows, :] = q.astype(BF16)
        ki_s[rows, :] = k.astype(BF16)
        v_s[rows, :] = v
        g_s[rows, :] = (g * jax.nn.sigmoid(g)).astype(BF16)
        for j in range(ROW_TILE // C):
            n = t * (ROW_TILE // C) + j
            sl = slice(j * C, (j + 1) * C)
            qd_s[rows_of(n), :] = jnp.concatenate([q[sl] * qdec_f, q[sl] * qdec_b],
                                                  axis=-1).astype(BF16)
            st_s[n, pl.ds(0, dk), :] = run_s[...].astype(BF16)
            run_s[...] = run_s[...] * chunk_dec + _dot_tn((k[sl] * kdec_f).astype(BF16), v[sl])
            kvb_s[n] = _dot_tn((k[sl] * kdec_b).astype(BF16), v[sl])
        return carry

    lax.fori_loop(0, seq // ROW_TILE, proj, 0)

    run_s[...] = jnp.zeros_like(run_s)

    def bwd(t, carry):
        n = R - 1 - t
        st_s[n, pl.ds(dk, dk), :] = run_s[...].astype(BF16)
        run_s[...] = run_s[...] * chunk_dec + kvb_s[n]
        return carry

    lax.fori_loop(0, R, bwd, 0)

    def out(n, carry):
        rows = rows_of(n)
        s = _dot_nt(qi_s[rows, :], ki_s[rows, :]) * decay
        o = _dot(s.astype(BF16), v_s[rows, :]) + _dot(qd_s[rows, :], st_s[n])
        o_ref[rows, :] = (_rms(o) * g_s[rows, :].astype(F32)).astype(o_ref.dtype)
        return carry

    lax.fori_loop(0, R, out, 0, unroll=4)


def retention_branch(xn3, w_in_b, layer, cos, sin, log_gamma):
    B, S, D = xn3.shape
    dk = D // RET_HEADS
    dv = 2 * dk
    H = RET_HEADS
    qk_blocks = D // dk
    v_off = 2 * D // dv
    g_off = v_off + H
    kern = functools.partial(_ret_kernel, seq=S, dk=dk, dv=dv)
    return pl.pallas_call(
        kern,
        out_shape=jax.ShapeDtypeStruct((B, S, H * dv), BF16),
        grid=(B, H),
        in_specs=[
            pl.BlockSpec(memory_space=pltpu.SMEM),
            pl.BlockSpec((None, S, D), lambda b, h: (b, 0, 0)),
            pl.BlockSpec((None, D, dk), lambda b, h: (layer, 0, h)),
            pl.BlockSpec((None, D, dk), lambda b, h: (layer, 0, qk_blocks + h)),
            pl.BlockSpec((None, D, dv), lambda b, h: (layer, 0, v_off + h)),
            pl.BlockSpec((None, D, dv), lambda b, h: (layer, 0, g_off + h)),
            pl.BlockSpec((None, S, dk // 2), lambda b, h: (b, 0, 0)),
            pl.BlockSpec((None, S, dk // 2), lambda b, h: (b, 0, 0)),
        ],
        out_specs=pl.BlockSpec((None, S, dv), lambda b, h: (b, 0, h)),
        scratch_shapes=[
            pltpu.VMEM((S, dk), BF16),
            pltpu.VMEM((S, 2 * dk), BF16),
            pltpu.VMEM((S, dk), BF16),
            pltpu.VMEM((S, dv), BF16),
            pltpu.VMEM((S, dv), BF16),
            pltpu.VMEM((S // RET_CHUNK, 2 * dk, dv), BF16),
            pltpu.VMEM((S // RET_CHUNK, dk, dv), F32),
            pltpu.VMEM((dk, dv), F32),
        ],
        compiler_params=_params("parallel", "arbitrary"),
        name="retention",
    )(log_gamma, xn3, w_in_b, w_in_b, w_in_b, w_in_b, cos, sin)


def _hgrn_gate(z, lb):
    e = jnp.exp(-jnp.abs(z))
    pos = z >= 0.0
    sig_neg_num = jnp.where(pos, e, 1.0)
    num = jnp.where(pos, 1.0, e) + jnp.maximum(lb, LB_FLOOR) * sig_neg_num
    log2_f = jnp.log2(num) - jnp.log2(1.0 + e)
    return log2_f, (1.0 - lb) * sig_neg_num / (1.0 + e)


def _boundary_rows(cum_ref, base, m, reverse, row_in_group):
    C = HGRN_CHUNK
    d = cum_ref.shape[1]
    blk = 2 * m
    off = m if reverse else m - 1
    pieces = []
    if blk >= SUBLANES:
        for b in range(C // blk):
            pieces.append(jnp.broadcast_to(cum_ref[pl.ds(base + (b * blk + off), 1), :], (blk, d)))
    else:
        for g in range(C // SUBLANES):
            val = None
            for u in range(SUBLANES // blk):
                row = g * SUBLANES + u * blk + off
                piece = jnp.broadcast_to(cum_ref[pl.ds(base + row, 1), :], (SUBLANES, d))
                val = piece if val is None else jnp.where(row_in_group >= u * blk, piece, val)
            pieces.append(val)
    return jnp.concatenate(pieces, axis=0) if len(pieces) > 1 else pieces[0]


def _level_operands(level, q, k, cum, cum_ref, base, consts, reverse):
    C = HGRN_CHUNK
    _, row_in_group, signs = consts
    m = 2 ** level
    if 2 * m <= SUBLANES:
        sign = signs[level]
        ref_pt = _boundary_rows(cum_ref, base, m, reverse, row_in_group)
        x = (jnp.where(sign > 0.0, q, k) * jnp.exp2((cum - ref_pt) * sign)).astype(BF16)
        return x, x, list(range(C // SUBLANES))
    xq, xall, q_groups = [], [], []
    for b in range(C // (2 * m)):
        first = slice(b * 2 * m, b * 2 * m + m)
        second = slice(b * 2 * m + m, (b + 1) * 2 * m)
        q_rows, k_rows = (first, second) if reverse else (second, first)
        edge = k_rows.start if reverse else k_rows.stop - 1
        ref_pt = cum_ref[pl.ds(base + edge, 1), :]
        xq_b = q[q_rows] * jnp.exp2(cum[q_rows] - ref_pt)
        xk_b = k[k_rows] * jnp.exp2(ref_pt - cum[k_rows])
        xq.append(xq_b)
        xall.extend([xq_b, xk_b] if reverse else [xk_b, xq_b])
        q_groups.extend(range(q_rows.start // SUBLANES, q_rows.stop // SUBLANES))
    return (jnp.concatenate(xq, axis=0).astype(BF16), jnp.concatenate(xall, axis=0).astype(BF16),
            q_groups)


def _paired_dot_nt(lhs_a, rhs_a, lhs_b, rhs_b):
    rhs = jnp.concatenate([rhs_a, rhs_b], axis=1)
    lhs = jnp.concatenate(
        [jnp.concatenate([lhs_a, jnp.zeros_like(lhs_a)], axis=1),
         jnp.concatenate([jnp.zeros_like(lhs_b), lhs_b], axis=1)], axis=0)
    s = _dot_nt(lhs, rhs)
    return s[:lhs_a.shape[0]], s[lhs_a.shape[0]:]


def _assemble_scores(scores, level_id):
    C = HGRN_CHUNK
    rows = [jnp.zeros((SUBLANES, C), F32) for _ in range(C // SUBLANES)]
    for level, entry in enumerate(scores):
        if entry is None:
            continue
        s, q_groups = entry
        for i, g in enumerate(q_groups):
            lid = level_id[g * SUBLANES:(g + 1) * SUBLANES]
            rows[g] = jnp.where(lid == level, s[i * SUBLANES:(i + 1) * SUBLANES], rows[g])
    return jnp.concatenate(rows, axis=0)


def _hgrn_intra_pair(fwd, bwd, consts):
    C = HGRN_CHUNK
    args = ((fwd, consts[0], False), (bwd, consts[1], True))
    scores = ([None], [None])
    for level in range(1, C.bit_length() - 1):
        ops = [_level_operands(level, q, k, cum, cum_ref, base, cst, rev)
               for (q, k, _, cum, cum_ref, base), cst, rev in args]
        s_f, s_b = _paired_dot_nt(ops[0][0], ops[0][1], ops[1][0], ops[1][1])
        scores[0].append((s_f, ops[0][2]))
        scores[1].append((s_b, ops[1][2]))
    outs = []
    for idx, ((q, k, v, cum, _, _), cst, rev) in enumerate(args):
        attn = _assemble_scores(scores[idx], cst[0])
        o = _dot(attn.astype(BF16), v.astype(BF16))
        o += jnp.sum(q * k, axis=-1, keepdims=True) * v
        outs.append(o + _adjacent_pairs(q, k, v, cum, rev))
    return outs


def _adjacent_pairs(q, k, v, cum, reverse):
    C, d = q.shape
    shape3 = (C // SUBLANES, SUBLANES, d)
    shift = (SUBLANES - 1) if reverse else 1

    def beside(x):
        return pltpu.roll(x.reshape(shape3), shift, 1).reshape(C, d)

    row = lax.broadcasted_iota(jnp.int32, (C, d), 0)
    is_query = ((row & 1) == 0) if reverse else ((row & 1) == 1)
    w = jnp.exp2(jnp.where(is_query, cum - beside(cum), 0.0))
    score = jnp.sum(q * beside(k) * w, axis=-1, keepdims=True)
    return jnp.where(is_query, score * beside(v), 0.0)


def _hgrn_consts(reverse, d):
    C = HGRN_CHUNK
    ii = lax.broadcasted_iota(jnp.int32, (C, C), 0)
    jj = lax.broadcasted_iota(jnp.int32, (C, C), 1)
    diff = ii ^ jj
    level_id = jnp.full((C, C), -1, jnp.int32)
    n_levels = C.bit_length() - 1
    for level in range(n_levels):
        level_id = jnp.where((diff >> level) == 1, level, level_id)
    level_id = jnp.where((ii < jj) if reverse else (ii > jj), level_id, -1)
    rows = lax.broadcasted_iota(jnp.int32, (C, d), 0)
    row_in_group = lax.broadcasted_iota(jnp.int32, (SUBLANES, d), 0)
    signs = []
    for level in range(SUBLANES.bit_length() - 1):
        second = ((rows >> level) & 1) == 1
        is_query = jnp.logical_not(second) if reverse else second
        signs.append(jnp.where(is_query, 1.0, -1.0))
    return level_id, row_in_group, signs


def _chunk_cumsum(x, reverse, row_in_group):
    rows, d = x.shape
    groups = rows // SUBLANES
    per_chunk = HGRN_CHUNK // SUBLANES
    y = x.reshape(groups, SUBLANES, d)
    step = 1
    while step < SUBLANES:
        rolled = pltpu.roll(y, (SUBLANES - step) if reverse else step, 1)
        valid = (row_in_group < SUBLANES - step) if reverse else (row_in_group >= step)
        y = y + jnp.where(valid, rolled, 0.0)
        step *= 2
    out = [None] * groups
    for c in range(rows // HGRN_CHUNK):
        order = range(c * per_chunk, (c + 1) * per_chunk)
        carry = None
        for g in (reversed(order) if reverse else order):
            yg = y[g] if carry is None else y[g] + carry
            out[g] = yg
            edge = 0 if reverse else SUBLANES - 1
            carry = jnp.broadcast_to(yg[edge:edge + 1, :], (SUBLANES, d))
    return jnp.concatenate(out, axis=0)


def _hgrn_kernel(xn_ref, wq_ref, wzf_ref, wzb_ref, wi_ref, wg_ref, lbf_ref, lbb_ref, nw_ref, o_ref,
                 q_s, v_s, g_s, acc_s, kf_s, kb_s, cumf_s, cumb_s,
                 qef_s, qeb_s, ktf_s, ktb_s, decf_s, decb_s, *, seq, dk, heads):
    C = HGRN_CHUNK
    R = seq // C
    per_tile = ROW_TILE // C
    scale = dk ** -0.5

    def tile_rows(n):
        return pl.ds(pl.multiple_of(n * ROW_TILE, ROW_TILE), ROW_TILE)

    def head_cols(h):
        return slice(h * dk, (h + 1) * dk)

    dirs = [((False, lbf_ref, kf_s.at[h], cumf_s.at[h], qef_s.at[h], ktf_s.at[h], decf_s.at[h]),
             (True, lbb_ref, kb_s.at[h], cumb_s.at[h], qeb_s.at[h], ktb_s.at[h], decb_s.at[h]))
            for h in range(heads)]
    row_in_group = lax.broadcasted_iota(jnp.int32, (1, SUBLANES, dk), 1)

    def proj(n, carry):
        rows = tile_rows(n)
        xc = xn_ref[rows, :]
        hq_all = _dot(xc, wq_ref[...])
        z_all = (_dot(xc, wzf_ref[...]), _dot(xc, wzb_ref[...]))
        v_all = _dot(xc, wi_ref[...])
        hg_all = _dot(xc, wg_ref[...])
        for h in range(heads):
            cols = head_cols(h)
            hq = hq_all[:, cols]
            q = hq * jax.nn.sigmoid(hq) * scale
            q_s[h, rows, :] = q
            for idx, (reverse, lb_ref, k_s, cum_s, qe_s, kt_s, dec_s) in enumerate(dirs[h]):
                lf, kk = _hgrn_gate(z_all[idx][:, cols], lb_ref[:, cols])
                k_s[rows, :] = kk
                cum = _chunk_cumsum(lf, reverse, row_in_group)
                cum_s[rows, :] = cum
                for j in range(per_tile):
                    sl = slice(j * C, (j + 1) * C)
                    edge = j * C if reverse else (j + 1) * C - 1
                    total = cum[edge:edge + 1, :]
                    r0 = pl.multiple_of(n * ROW_TILE + j * C, C)
                    qe_s[pl.ds(r0, C), :] = (q[sl] * jnp.exp2(cum[sl])).astype(BF16)
                    kt_s[pl.ds(r0, C), :] = (kk[sl] * jnp.exp2(total - cum[sl])).astype(BF16)
                    dec_s[n * per_tile + j] = jnp.broadcast_to(jnp.exp2(total), (SUBLANES, dk))
            v_s[h, rows, :] = v_all[:, cols]
            hg = hg_all[:, cols]
            g_s[h, rows, :] = hg * jax.nn.sigmoid(hg)
            acc_s[h, rows, :] = jnp.zeros((ROW_TILE, dk), F32)
        return carry

    lax.fori_loop(0, seq // ROW_TILE, proj, 0, unroll=2)

    consts = (_hgrn_consts(False, dk), _hgrn_consts(True, dk))

    def chunk_pair(h, cf, cb, states):
        data, rows = [], []
        for c, (_, _, k_s, cum_s, _, _, _) in zip((cf, cb), dirs[h]):
            base = pl.multiple_of(c * C, C)
            r = pl.ds(base, C)
            data.append((q_s[h, r, :], k_s[r, :], v_s[h, r, :], cum_s[r, :], cum_s, base))
            rows.append(r)
        o_f, o_b = _hgrn_intra_pair(data[0], data[1], consts)
        i_f, i_b = _paired_dot_nt(dirs[h][0][4][rows[0], :], states[0].astype(BF16),
                                  dirs[h][1][4][rows[1], :], states[1].astype(BF16))
        acc_s[h, rows[0], :] += o_f + i_f
        acc_s[h, rows[1], :] += o_b + i_b
        new_states = []
        for c, r, st, (_, _, v, _, _, _), (_, _, _, _, _, kt_s, dec_s) in zip(
                (cf, cb), rows, states, data, dirs[h]):
            dec = jnp.tile(dec_s[c], (dk // SUBLANES, 1))
            new_states.append(st * dec + _dot_tn(v.astype(BF16), kt_s[r, :]))
        return tuple(new_states)

    def step(i, states):
        return tuple(chunk_pair(h, i, R - 1 - i, states[h]) for h in range(heads))

    zero = jnp.zeros((dk, dk), F32)
    lax.fori_loop(0, R, step, tuple((zero, zero) for _ in range(heads)))

    def finish(n, carry):
        rows = tile_rows(n)
        for h in range(heads):
            o_ref[rows, head_cols(h)] = (_rms(acc_s[h, rows, :]) * nw_ref[...]
                                         * g_s[h, rows, :]).astype(o_ref.dtype)
        return carry

    lax.fori_loop(0, seq // ROW_TILE, finish, 0)


def hgrn_branch(xn3, w_in_b, layer, hgrn_off, lb_f, lb_b, norm_w):
    B, S, D = xn3.shape
    dk = D // HGRN_HEADS
    hp = HGRN_HEADS_PER_STEP
    wide = hp * dk
    kern = functools.partial(_hgrn_kernel, seq=S, dk=dk, heads=hp)
    vec = lambda: pltpu.VMEM((hp, S, dk), F32)
    half = lambda: pltpu.VMEM((hp, S, dk), BF16)
    dec = lambda: pltpu.VMEM((hp, S // HGRN_CHUNK, SUBLANES, dk), F32)

    def w_spec(group):
        first = (hgrn_off + group * D) // wide
        return pl.BlockSpec((None, D, wide), lambda b, j: (layer, 0, first + j))

    lb_spec = pl.BlockSpec((1, wide), lambda b, j: (0, j))
    return pl.pallas_call(
        kern,
        out_shape=jax.ShapeDtypeStruct((B, S, D), BF16),
        grid=(B, HGRN_HEADS // hp),
        in_specs=[pl.BlockSpec((None, S, D), lambda b, j: (b, 0, 0)),
                  w_spec(0), w_spec(1), w_spec(2), w_spec(3), w_spec(4),
                  lb_spec, lb_spec,
                  pl.BlockSpec((1, dk), lambda b, j: (0, 0))],
        out_specs=pl.BlockSpec((None, S, wide), lambda b, j: (b, 0, j)),
        scratch_shapes=[vec(), vec(), vec(), vec(), vec(), vec(), vec(), vec(),
                        half(), half(), half(), half(), dec(), dec()],
        compiler_params=_params("parallel", "arbitrary"),
        name="hgrn2",
    )(xn3, w_in_b, w_in_b, w_in_b, w_in_b, w_in_b,
      lb_f.reshape(1, D), lb_b.reshape(1, D), norm_w.reshape(1, dk))


def _fnet_proj_kernel(xn_ref, w_ref, cs_ref, o_ref, *, gdim):
    fu = _dot(xn_ref[...], w_ref[...]).astype(BF16)
    for g in range(FNET_GROUPS):
        t = _dot(fu[:, g * gdim:(g + 1) * gdim], cs_ref[...])
        o_ref[0, :, g * gdim:(g + 1) * gdim] = t[:, :gdim].astype(o_ref.dtype)
        o_ref[1, :, g * gdim:(g + 1) * gdim] = t[:, gdim:].astype(o_ref.dtype)


def _seq_dft_kernel(dft_ref, rhs_ref, o_ref):
    o_ref[...] = _dot(dft_ref[...], rhs_ref[...]).astype(o_ref.dtype)


def _dft_tables(n):
    idx = np.arange(n, dtype=np.int64)
    ang = 2.0 * np.pi * ((idx[:, None] * idx[None, :]) % n).astype(np.float64) / n
    s = 1.0 / math.sqrt(n)
    return np.cos(ang) * s, np.sin(ang) * s


def fourier_branch(xn3, w_in_b, layer, fu_off_blocks, tm=512):
    B, S, D = xn3.shape
    W = D
    gdim = W // FNET_GROUPS
    c_small, s_small = _dft_tables(gdim)
    cs_small = jnp.asarray(np.concatenate([c_small, s_small], axis=1), dtype=BF16)
    c_seq, s_seq = _dft_tables(S)
    dft_seq = jnp.asarray(np.concatenate([c_seq, -s_seq], axis=1), dtype=BF16)
    tiles = S // tm
    rhs = pl.pallas_call(
        functools.partial(_fnet_proj_kernel, gdim=gdim),
        out_shape=jax.ShapeDtypeStruct((B, 2, S, W), BF16),
        grid=(B, tiles),
        in_specs=[pl.BlockSpec((None, tm, D), lambda b, r: (b, r, 0)),
                  pl.BlockSpec((None, D, W), lambda b, r: (layer, 0, fu_off_blocks)),
                  pl.BlockSpec((gdim, 2 * gdim), lambda b, r: (0, 0))],
        out_specs=pl.BlockSpec((None, 2, tm, W), lambda b, r: (b, 0, r, 0)),
        compiler_params=_params("parallel", "parallel"),
        name="fnet_proj",
    )(xn3, w_in_b, cs_small)
    rhs = rhs.reshape(B, 2 * S, W)
    return pl.pallas_call(
        _seq_dft_kernel,
        out_shape=jax.ShapeDtypeStruct((B, S, W), BF16),
        grid=(B, tiles),
        in_specs=[pl.BlockSpec((tm, 2 * S), lambda b, r: (r, 0)),
                  pl.BlockSpec((None, 2 * S, W), lambda b, r: (b, 0, 0))],
        out_specs=pl.BlockSpec((None, tm, W), lambda b, r: (b, r, 0)),
        compiler_params=_params("parallel", "arbitrary"),
        name="fnet_seq_dft",
    )(dft_seq, rhs)


def _merge_kernel(x_ref, xn_ref, ro_ref, ho_ref, fo_ref, wga_ref, wro_ref, who_ref,
                  wf_ref, wout_ref, nw_ref, o_ref, *, d):
    xn = xn_ref[...]

    def gate(i):
        return jax.nn.sigmoid(_dot(xn, wga_ref[:, i * d:(i + 1) * d]))

    mix = gate(0) * _dot(ro_ref[...], wro_ref[...])
    mix += gate(1) * _dot(ho_ref[...], who_ref[...])
    mix += gate(2) * _dot(fo_ref[...], wf_ref[...])
    y = _dot(mix.astype(BF16), wout_ref[...])
    o_ref[...] = x_ref[...] + _rms(y) * nw_ref[...]


def merge_branches(x2, xn2, ro2, ho2, fo2, w_in_b, layer, ga_off_blocks,
                   w_ret_o, w_hgrn_o, w_fnet, w_out, norm_w, tm=512):
    T, D = x2.shape
    RV = ro2.shape[1]
    tile = lambda w: pl.BlockSpec((tm, w), lambda i: (i, 0))
    return pl.pallas_call(
        functools.partial(_merge_kernel, d=D),
        out_shape=jax.ShapeDtypeStruct((T, D), F32),
        grid=(T // tm,),
        in_specs=[tile(D), tile(D), tile(RV), tile(D), tile(D),
                  _resident((None, D, N_BRANCH * D), lambda i: (layer, 0, ga_off_blocks)),
                  _resident((None, RV, D), lambda i: (layer, 0, 0)),
                  _resident((None, D, D), lambda i: (layer, 0, 0)),
                  _resident((None, D, D), lambda i: (layer, 0, 0)),
                  _resident((None, D, D), lambda i: (layer, 0, 0)),
                  pl.BlockSpec((1, D), lambda i: (0, 0))],
        out_specs=tile(D),
        compiler_params=_params("parallel"),
        name="merge",
    )(x2, xn2, ro2, ho2, fo2, w_in_b, w_ret_o, w_hgrn_o, w_fnet, w_out, norm_w.reshape(1, D))


def _ffn_kernel(x_ref, xp_ref, xnx_ref, nw_in_ref, wup_ref, cw_ref, cb_ref, wdn_ref, nw_out_ref,
                *rest, tm, tiles_per_seq, d_ff, fc, emit_next):
    if emit_next:
        nw_next_ref, o_ref, xn_ref, hn_s, acc_s = rest
    else:
        (o_ref, hn_s, acc_s), nw_next_ref, xn_ref = rest, None, None
    i = pl.program_id(0)
    r = i % tiles_per_seq
    halo = BF16_ROWS
    x = x_ref[...]
    nw = nw_in_ref[...]
    hp = jnp.where(r == 0, 0.0, _rms(xp_ref[...]) * nw)
    hx = jnp.where(r == tiles_per_seq - 1, 0.0, _rms(xnx_ref[...]) * nw)
    hn = jnp.concatenate([hp, _rms(x) * nw, hx], axis=0).astype(BF16)
    n_ext = tm + 2 * halo
    hn_s[...] = hn
    acc_s[...] = jnp.zeros_like(acc_s)

    def conv(col, scale):
        cols = pl.ds(pl.multiple_of(col, fc), fc)
        h = _dot(hn_s[...], wup_ref[:, cols])
        cw = cw_ref[:, cols] * scale
        prev = pltpu.roll(h, 1, 0)[halo:halo + tm]
        nxt = pltpu.roll(h, n_ext - 1, 0)[halo:halo + tm]
        return (cb_ref[:, cols] * scale + prev * cw[0:1] + h[halo:halo + tm] * cw[1:2]
                + nxt * cw[2:3])

    def chunk(c, carry):
        gate = conv(c * fc, 1.0)
        half_up = conv(d_ff + c * fc, 0.5)
        inner = gate * (GELU_C0 + GELU_C1 * (gate * gate))
        act = (gate * (1.0 + jnp.tanh(inner)) * half_up).astype(BF16)
        acc_s[...] += _dot(act, wdn_ref[pl.ds(pl.multiple_of(c * fc, fc), fc), :])
        return carry

    lax.fori_loop(0, d_ff // fc, chunk, 0, unroll=2)
    y = x + _rms(acc_s[...]) * nw_out_ref[...]
    o_ref[...] = y
    if xn_ref is not None:
        xn_ref[...] = (_rms(y) * nw_next_ref[...]).astype(xn_ref.dtype)


def conv_ffn_block(x2, seq, w_up, conv_w, conv_b, w_down, nw_in, nw_out, layer, nw_next=None,
                   tm=1024, fc=256):
    T, D = x2.shape
    d_ff = w_down.shape[1]
    halo = BF16_ROWS
    tps = seq // tm
    hb = tm // halo
    n_hb = T // halo
    emit_next = nw_next is not None
    kern = functools.partial(_ffn_kernel, tm=tm, tiles_per_seq=tps, d_ff=d_ff, fc=fc,
                             emit_next=emit_next)
    vec = pl.BlockSpec((1, D), lambda i: (0, 0))
    tile = pl.BlockSpec((tm, D), lambda i: (i, 0))
    in_specs = [tile,
                pl.BlockSpec((halo, D), lambda i: (jnp.maximum(i * hb - 1, 0), 0)),
                pl.BlockSpec((halo, D), lambda i: (jnp.minimum((i + 1) * hb, n_hb - 1), 0)),
                vec,
                _resident((None, D, 2 * d_ff), lambda i: (layer, 0, 0)),
                pl.BlockSpec((None, CONV_W, 2 * d_ff), lambda i: (layer, 0, 0)),
                pl.BlockSpec((None, 1, 2 * d_ff), lambda i: (layer, 0, 0)),
                _resident((None, d_ff, D), lambda i: (layer, 0, 0)),
                vec]
    args = [x2, x2, x2, nw_in.reshape(1, D), w_up, conv_w, conv_b, w_down, nw_out.reshape(1, D)]
    out_shape = jax.ShapeDtypeStruct((T, D), F32)
    out_specs = tile
    if emit_next:
        in_specs.append(vec)
        args.append(nw_next.reshape(1, D))
        out_shape = (out_shape, jax.ShapeDtypeStruct((T, D), BF16))
        out_specs = (tile, tile)
    return pl.pallas_call(
        kern,
        out_shape=out_shape,
        grid=(T // tm,),
        in_specs=in_specs,
        out_specs=out_specs,
        scratch_shapes=[pltpu.VMEM((tm + 2 * halo, D), BF16),
                        pltpu.VMEM((tm, D), F32)],
        compiler_params=_params("parallel"),
        name="conv_ffn",
    )(*args)


def kernel(x, positions, norm_w, w_in, hgrn_lb_logits, hgrn_norm_w, w_ret_o, w_hgrn_o,
           w_fnet, w_out, w_up, conv_w, conv_b, w_down):
    B, S, D = x.shape
    depth = w_in.shape[0]
    T = B * S

    hgrn_off = 2 * D + 2 * 2 * D
    fu_off = hgrn_off + 5 * D
    ga_off = fu_off + D

    w_in_b = w_in.astype(BF16)
    w_ret_o_b = w_ret_o.astype(BF16)
    w_hgrn_o_b = w_hgrn_o.astype(BF16)
    w_fnet_b = w_fnet.astype(BF16)
    w_out_b = w_out.astype(BF16)
    w_up_b = w_up.astype(BF16)
    w_down_b = w_down.astype(BF16)
    conv_b3 = conv_b.reshape(depth, 1, -1)

    log_gamma = jnp.log(1.0 - 2.0 ** (-5.0 - jnp.arange(RET_HEADS, dtype=F32)))
    p = jax.nn.softmax(hgrn_lb_logits.astype(F32), axis=1)
    lower_bounds = jnp.cumsum(p, axis=1) - p[:, :1]

    cos, sin = rope_tables(positions, D // RET_HEADS // 2)

    x2 = x.reshape(T, D)
    xn2 = rms_norm_bf16(x2, norm_w[0, 0])
    for l in range(depth):
        xn3 = xn2.reshape(B, S, D)
        ro = retention_branch(xn3, w_in_b, l, cos, sin, log_gamma)
        ho = hgrn_branch(xn3, w_in_b, l, hgrn_off, lower_bounds[0, l], lower_bounds[1, l],
                         hgrn_norm_w[l])
        fo = fourier_branch(xn3, w_in_b, l, fu_off // D)
        x2 = merge_branches(x2, xn2, ro.reshape(T, -1), ho.reshape(T, D), fo.reshape(T, D),
                            w_in_b, l, ga_off // (N_BRANCH * D),
                            w_ret_o_b, w_hgrn_o_b, w_fnet_b, w_out_b, norm_w[l, 1])
        if l + 1 < depth:
            x2, xn2 = conv_ffn_block(x2, S, w_up_b, conv_w, conv_b3, w_down_b,
                                     norm_w[l, 2], norm_w[l, 3], l, nw_next=norm_w[l + 1, 0])
        else:
            x2 = conv_ffn_block(x2, S, w_up_b, conv_w, conv_b3, w_down_b,
                                norm_w[l, 2], norm_w[l, 3], l)
    return x2.reshape(B, S, D)
```

```python
import functools
import math

import numpy as np
import jax
import jax.numpy as jnp
from jax import lax
from jax.experimental import pallas as pl
from jax.experimental.pallas import tpu as pltpu

F32 = jnp.float32
BF16 = jnp.bfloat16

RET_HEADS = 4
HGRN_HEADS = 8
FNET_GROUPS = 4
N_BRANCH = 3
CONV_W = 3
ROPE_BASE = 10000.0
LB_FLOOR = 1e-30
EPS = 1e-6
GELU_C0 = math.sqrt(2.0 / math.pi)
GELU_C1 = GELU_C0 * 0.044715

V7X_VMEM_LIMIT_BYTES = 56 * 1024 * 1024
SUBLANES = 8
BF16_ROWS = 16

RET_CHUNK = 256
HGRN_CHUNK = 128
HGRN_HEADS_PER_STEP = 2
ROW_TILE = 512


def _dot(a, b):
    return jnp.dot(a, b, preferred_element_type=F32)


def _dot_nt(a, b):
    return lax.dot_general(a, b, (((1,), (1,)), ((), ())), preferred_element_type=F32)


def _dot_tn(a, b):
    return lax.dot_general(a, b, (((0,), (0,)), ((), ())), preferred_element_type=F32)


def _rms(x):
    return x * lax.rsqrt(jnp.mean(x * x, axis=-1, keepdims=True) + EPS)


def _params(*sem):
    return pltpu.CompilerParams(dimension_semantics=sem,
                                vmem_limit_bytes=V7X_VMEM_LIMIT_BYTES)


def _resident(shape, index_map):
    return pl.BlockSpec(shape, index_map, pipeline_mode=pl.Buffered(1))


def _rope_kernel(pos_ref, invf_ref, cos_ref, sin_ref):
    ang = pos_ref[...] * invf_ref[...]
    cos_ref[...] = jnp.cos(ang)
    sin_ref[...] = jnp.sin(ang)


def rope_tables(positions, half):
    B, S = positions.shape
    pos = positions.astype(F32).reshape(B, S, 1)
    inv_freq = (ROPE_BASE ** (-jnp.arange(half, dtype=F32) / half)).reshape(1, half)
    out = jax.ShapeDtypeStruct((B, S, half), F32)
    return pl.pallas_call(
        _rope_kernel,
        out_shape=(out, out),
        grid=(B,),
        in_specs=[pl.BlockSpec((None, S, 1), lambda b: (b, 0, 0)),
                  pl.BlockSpec((1, half), lambda b: (0, 0))],
        out_specs=(pl.BlockSpec((None, S, half), lambda b: (b, 0, 0)),
                   pl.BlockSpec((None, S, half), lambda b: (b, 0, 0))),
        compiler_params=_params("parallel"),
        name="rope_tables",
    )(pos, inv_freq)


def _norm_kernel(x_ref, w_ref, o_ref):
    o_ref[...] = (_rms(x_ref[...]) * w_ref[...]).astype(o_ref.dtype)


def rms_norm_bf16(x2, w, tm=1024):
    T, D = x2.shape
    return pl.pallas_call(
        _norm_kernel,
        out_shape=jax.ShapeDtypeStruct((T, D), BF16),
        grid=(T // tm,),
        in_specs=[pl.BlockSpec((tm, D), lambda i: (i, 0)),
                  pl.BlockSpec((1, D), lambda i: (0, 0))],
        out_specs=pl.BlockSpec((tm, D), lambda i: (i, 0)),
        compiler_params=_params("parallel"),
        name="rms_norm",
    )(x2, w.reshape(1, D))


def _ret_kernel(lg_ref, xn_ref, wq_ref, wk_ref, wv_ref, wg_ref, cos_ref, sin_ref,
                o_ref, qi_s, qd_s, ki_s, v_s, g_s, st_s, kvb_s, run_s, *, seq, dk, dv):
    C = RET_CHUNK
    R = seq // C
    half = dk // 2
    lg = lg_ref[pl.program_id(1)]
    ret_scale = dk ** -0.5

    def rows_of(n):
        return pl.ds(pl.multiple_of(n * C, C), C)

    pos = lax.broadcasted_iota(jnp.int32, (C, 1), 0).astype(F32)
    qdec_f = jnp.exp(lg * (pos + 1.0))
    qdec_b = jnp.exp(lg * (C - pos))
    kdec_f = jnp.exp(lg * (C - 1.0 - pos))
    kdec_b = jnp.exp(lg * pos)
    chunk_dec = jnp.exp(lg * C)
    ii = lax.broadcasted_iota(jnp.int32, (C, C), 0)
    jj = lax.broadcasted_iota(jnp.int32, (C, C), 1)
    decay = jnp.exp(lg * jnp.abs(ii - jj).astype(F32))

    run_s[...] = jnp.zeros_like(run_s)

    def proj(t, carry):
        rows = pl.ds(pl.multiple_of(t * ROW_TILE, ROW_TILE), ROW_TILE)
        xc = xn_ref[rows, :]
        cos = cos_ref[rows, :]
        sin = sin_ref[rows, :]
        q = _dot(xc, wq_ref[...])
        q1, q2 = q[:, :half], q[:, half:]
        q = jnp.concatenate([q1 * cos - q2 * sin, q1 * sin + q2 * cos], axis=-1)
        k = _dot(xc, wk_ref[...]) * ret_scale
        k1, k2 = k[:, :half], k[:, half:]
        k = jnp.concatenate([k1 * cos - k2 * sin, k1 * sin + k2 * cos], axis=-1)
        v = _dot(xc, wv_ref[...]).astype(BF16)
        g = _dot(xc, wg_ref[...])
        qi_s[rows, :] = q.astype(BF16)
        ki_s[rows, :] = k.astype(BF16)
        v_s[rows, :] = v
        g_s[rows, :] = (g * jax.nn.sigmoid(g)).astype(BF16)
        for j in range(ROW_TILE // C):
            n = t * (ROW_TILE // C) + j
            sl = slice(j * C, (j + 1) * C)
            qd_s[rows_of(n), :] = jnp.concatenate([q[sl] * qdec_f, q[sl] * qdec_b],
                                                  axis=-1).astype(BF16)
            st_s[n, pl.ds(0, dk), :] = run_s[...].astype(BF16)
            run_s[...] = run_s[...] * chunk_dec + _dot_tn((k[sl] * kdec_f).astype(BF16), v[sl])
            kvb_s[n] = _dot_tn((k[sl] * kdec_b).astype(BF16), v[sl])
        return carry

    lax.fori_loop(0, seq // ROW_TILE, proj, 0)

    run_s[...] = jnp.zeros_like(run_s)

    def bwd(t, carry):
        n = R - 1 - t
        st_s[n, pl.ds(dk, dk), :] = run_s[...].astype(BF16)
        run_s[...] = run_s[...] * chunk_dec + kvb_s[n]
        return carry

    lax.fori_loop(0, R, bwd, 0)

    def out(n, carry):
        rows = rows_of(n)
        s = _dot_nt(qi_s[rows, :], ki_s[rows, :]) * decay
        o = _dot(s.astype(BF16), v_s[rows, :]) + _dot(qd_s[rows, :], st_s[n])
        o_ref[rows, :] = (_rms(o) * g_s[rows, :].astype(F32)).astype(o_ref.dtype)
        return carry

    lax.fori_loop(0, R, out, 0, unroll=4)


def retention_branch(xn3, w_in_b, layer, cos, sin, log_gamma):
    B, S, D = xn3.shape
    dk = D // RET_HEADS
    dv = 2 * dk
    H = RET_HEADS
    qk_blocks = D // dk
    v_off = 2 * D // dv
    g_off = v_off + H
    kern = functools.partial(_ret_kernel, seq=S, dk=dk, dv=dv)
    return pl.pallas_call(
        kern,
        out_shape=jax.ShapeDtypeStruct((B, S, H * dv), BF16),
        grid=(B, H),
        in_specs=[
            pl.BlockSpec(memory_space=pltpu.SMEM),
            pl.BlockSpec((None, S, D), lambda b, h: (b, 0, 0)),
            pl.BlockSpec((None, D, dk), lambda b, h: (layer, 0, h)),
            pl.BlockSpec((None, D, dk), lambda b, h: (layer, 0, qk_blocks + h)),
            pl.BlockSpec((None, D, dv), lambda b, h: (layer, 0, v_off + h)),
            pl.BlockSpec((None, D, dv), lambda b, h: (layer, 0, g_off + h)),
            pl.BlockSpec((None, S, dk // 2), lambda b, h: (b, 0, 0)),
            pl.BlockSpec((None, S, dk // 2), lambda b, h: (b, 0, 0)),
        ],
        out_specs=pl.BlockSpec((None, S, dv), lambda b, h: (b, 0, h)),
        scratch_shapes=[
            pltpu.VMEM((S, dk), BF16),
            pltpu.VMEM((S, 2 * dk), BF16),
            pltpu.VMEM((S, dk), BF16),
            pltpu.VMEM((S, dv), BF16),
            pltpu.VMEM((S, dv), BF16),
            pltpu.VMEM((S // RET_CHUNK, 2 * dk, dv), BF16),
            pltpu.VMEM((S // RET_CHUNK, dk, dv), F32),
            pltpu.VMEM((dk, dv), F32),
        ],
        compiler_params=_params("parallel", "arbitrary"),
        name="retention",
    )(log_gamma, xn3, w_in_b, w_in_b, w_in_b, w_in_b, cos, sin)


def _hgrn_gate(z, lb):
    e = jnp.exp(-jnp.abs(z))
    pos = z >= 0.0
    sig_neg_num = jnp.where(pos, e, 1.0)
    num = jnp.where(pos, 1.0, e) + jnp.maximum(lb, LB_FLOOR) * sig_neg_num
    log2_f = jnp.log2(num) - jnp.log2(1.0 + e)
    return log2_f, (1.0 - lb) * sig_neg_num / (1.0 + e)


def _boundary_rows(cum_ref, base, m, reverse, row_in_group):
    C = HGRN_CHUNK
    d = cum_ref.shape[1]
    blk = 2 * m
    off = m if reverse else m - 1
    pieces = []
    if blk >= SUBLANES:
        for b in range(C // blk):
            pieces.append(jnp.broadcast_to(cum_ref[pl.ds(base + (b * blk + off), 1), :], (blk, d)))
    else:
        for g in range(C // SUBLANES):
            val = None
            for u in range(SUBLANES // blk):
                row = g * SUBLANES + u * blk + off
                piece = jnp.broadcast_to(cum_ref[pl.ds(base + row, 1), :], (SUBLANES, d))
                val = piece if val is None else jnp.where(row_in_group >= u * blk, piece, val)
            pieces.append(val)
    return jnp.concatenate(pieces, axis=0) if len(pieces) > 1 else pieces[0]


def _level_operands(level, q, k, cum, cum_ref, base, consts, reverse):
    C = HGRN_CHUNK
    _, row_in_group, signs = consts
    m = 2 ** level
    if 2 * m <= SUBLANES:
        sign = signs[level]
        ref_pt = _boundary_rows(cum_ref, base, m, reverse, row_in_group)
        x = (jnp.where(sign > 0.0, q, k) * jnp.exp2((cum - ref_pt) * sign)).astype(BF16)
        return x, x, list(range(C // SUBLANES))
    xq, xall, q_groups = [], [], []
    for b in range(C // (2 * m)):
        first = slice(b * 2 * m, b * 2 * m + m)
        second = slice(b * 2 * m + m, (b + 1) * 2 * m)
        q_rows, k_rows = (first, second) if reverse else (second, first)
        edge = k_rows.start if reverse else k_rows.stop - 1
        ref_pt = cum_ref[pl.ds(base + edge, 1), :]
        xq_b = q[q_rows] * jnp.exp2(cum[q_rows] - ref_pt)
        xk_b = k[k_rows] * jnp.exp2(ref_pt - cum[k_rows])
        xq.append(xq_b)
        xall.extend([xq_b, xk_b] if reverse else [xk_b, xq_b])
        q_groups.extend(range(q_rows.start // SUBLANES, q_rows.stop // SUBLANES))
    return (jnp.concatenate(xq, axis=0).astype(BF16), jnp.concatenate(xall, axis=0).astype(BF16),
            q_groups)


def _paired_dot_nt(lhs_a, rhs_a, lhs_b, rhs_b):
    rhs = jnp.concatenate([rhs_a, rhs_b], axis=1)
    lhs = jnp.concatenate(
        [jnp.concatenate([lhs_a, jnp.zeros_like(lhs_a)], axis=1),
         jnp.concatenate([jnp.zeros_like(lhs_b), lhs_b], axis=1)], axis=0)
    s = _dot_nt(lhs, rhs)
    return s[:lhs_a.shape[0]], s[lhs_a.shape[0]:]


def _assemble_scores(scores, level_id):
    C = HGRN_CHUNK
    rows = [jnp.zeros((SUBLANES, C), F32) for _ in range(C // SUBLANES)]
    for level, entry in enumerate(scores):
        if entry is None:
            continue
        s, q_groups = entry
        for i, g in enumerate(q_groups):
            lid = level_id[g * SUBLANES:(g + 1) * SUBLANES]
            rows[g] = jnp.where(lid == level, s[i * SUBLANES:(i + 1) * SUBLANES], rows[g])
    return jnp.concatenate(rows, axis=0)


def _hgrn_intra_pair(fwd, bwd, consts):
    C = HGRN_CHUNK
    args = ((fwd, consts[0], False), (bwd, consts[1], True))
    scores = ([None], [None])
    for level in range(1, C.bit_length() - 1):
        ops = [_level_operands(level, q, k, cum, cum_ref, base, cst, rev)
               for (q, k, _, cum, cum_ref, base), cst, rev in args]
        s_f, s_b = _paired_dot_nt(ops[0][0], ops[0][1], ops[1][0], ops[1][1])
        scores[0].append((s_f, ops[0][2]))
        scores[1].append((s_b, ops[1][2]))
    outs = []
    for idx, ((q, k, v, cum, _, _), cst, rev) in enumerate(args):
        attn = _assemble_scores(scores[idx], cst[0])
        o = _dot(attn.astype(BF16), v.astype(BF16))
        o += jnp.sum(q * k, axis=-1, keepdims=True) * v
        outs.append(o + _adjacent_pairs(q, k, v, cum, rev))
    return outs


def _adjacent_pairs(q, k, v, cum, reverse):
    C, d = q.shape
    shape3 = (C // SUBLANES, SUBLANES, d)
    shift = (SUBLANES - 1) if reverse else 1

    def beside(x):
        return pltpu.roll(x.reshape(shape3), shift, 1).reshape(C, d)

    row = lax.broadcasted_iota(jnp.int32, (C, d), 0)
    is_query = ((row & 1) == 0) if reverse else ((row & 1) == 1)
    w = jnp.exp2(jnp.where(is_query, cum - beside(cum), 0.0))
    score = jnp.sum(q * beside(k) * w, axis=-1, keepdims=True)
    return jnp.where(is_query, score * beside(v), 0.0)


def _hgrn_consts(reverse, d):
    C = HGRN_CHUNK
    ii = lax.broadcasted_iota(jnp.int32, (C, C), 0)
    jj = lax.broadcasted_iota(jnp.int32, (C, C), 1)
    diff = ii ^ jj
    level_id = jnp.full((C, C), -1, jnp.int32)
    n_levels = C.bit_length() - 1
    for level in range(n_levels):
        level_id = jnp.where((diff >> level) == 1, level, level_id)
    level_id = jnp.where((ii < jj) if reverse else (ii > jj), level_id, -1)
    rows = lax.broadcasted_iota(jnp.int32, (C, d), 0)
    row_in_group = lax.broadcasted_iota(jnp.int32, (SUBLANES, d), 0)
    signs = []
    for level in range(SUBLANES.bit_length() - 1):
        second = ((rows >> level) & 1) == 1
        is_query = jnp.logical_not(second) if reverse else second
        signs.append(jnp.where(is_query, 1.0, -1.0))
    return level_id, row_in_group, signs


def _chunk_cumsum(x, reverse, row_in_group):
    rows, d = x.shape
    groups = rows // SUBLANES
    per_chunk = HGRN_CHUNK // SUBLANES
    y = x.reshape(groups, SUBLANES, d)
    step = 1
    while step < SUBLANES:
        rolled = pltpu.roll(y, (SUBLANES - step) if reverse else step, 1)
        valid = (row_in_group < SUBLANES - step) if reverse else (row_in_group >= step)
        y = y + jnp.where(valid, rolled, 0.0)
        step *= 2
    out = [None] * groups
    for c in range(rows // HGRN_CHUNK):
        order = range(c * per_chunk, (c + 1) * per_chunk)
        carry = None
        for g in (reversed(order) if reverse else order):
            yg = y[g] if carry is None else y[g] + carry
            out[g] = yg
            edge = 0 if reverse else SUBLANES - 1
            carry = jnp.broadcast_to(yg[edge:edge + 1, :], (SUBLANES, d))
    return jnp.concatenate(out, axis=0)


def _hgrn_kernel(xn_ref, wq_ref, wzf_ref, wzb_ref, wi_ref, wg_ref, lbf_ref, lbb_ref, nw_ref, o_ref,
                 q_s, v_s, g_s, acc_s, kf_s, kb_s, cumf_s, cumb_s,
                 qef_s, qeb_s, ktf_s, ktb_s, decf_s, decb_s, *, seq, dk, heads):
    C = HGRN_CHUNK
    R = seq // C
    per_tile = ROW_TILE // C
    scale = dk ** -0.5

    def tile_rows(n):
        return pl.ds(pl.multiple_of(n * ROW_TILE, ROW_TILE), ROW_TILE)

    def head_cols(h):
        return slice(h * dk, (h + 1) * dk)

    dirs = [((False, lbf_ref, kf_s.at[h], cumf_s.at[h], qef_s.at[h], ktf_s.at[h], decf_s.at[h]),
             (True, lbb_ref, kb_s.at[h], cumb_s.at[h], qeb_s.at[h], ktb_s.at[h], decb_s.at[h]))
            for h in range(heads)]
    row_in_group = lax.broadcasted_iota(jnp.int32, (1, SUBLANES, dk), 1)

    def proj(n, carry):
        rows = tile_rows(n)
        xc = xn_ref[rows, :]
        hq_all = _dot(xc, wq_ref[...])
        z_all = (_dot(xc, wzf_ref[...]), _dot(xc, wzb_ref[...]))
        v_all = _dot(xc, wi_ref[...])
        hg_all = _dot(xc, wg_ref[...])
        for h in range(heads):
            cols = head_cols(h)
            hq = hq_all[:, cols]
            q = hq * jax.nn.sigmoid(hq) * scale
            q_s[h, rows, :] = q
            for idx, (reverse, lb_ref, k_s, cum_s, qe_s, kt_s, dec_s) in enumerate(dirs[h]):
                lf, kk = _hgrn_gate(z_all[idx][:, cols], lb_ref[:, cols])
                k_s[rows, :] = kk
                cum = _chunk_cumsum(lf, reverse, row_in_group)
                cum_s[rows, :] = cum
                for j in range(per_tile):
                    sl = slice(j * C, (j + 1) * C)
                    edge = j * C if reverse else (j + 1) * C - 1
                    total = cum[edge:edge + 1, :]
                    r0 = pl.multiple_of(n * ROW_TILE + j * C, C)
                    qe_s[pl.ds(r0, C), :] = (q[sl] * jnp.exp2(cum[sl])).astype(BF16)
                    kt_s[pl.ds(r0, C), :] = (kk[sl] * jnp.exp2(total - cum[sl])).astype(BF16)
                    dec_s[n * per_tile + j] = jnp.broadcast_to(jnp.exp2(total), (SUBLANES, dk))
            v_s[h, rows, :] = v_all[:, cols]
            hg = hg_all[:, cols]
            g_s[h, rows, :] = hg * jax.nn.sigmoid(hg)
            acc_s[h, rows, :] = jnp.zeros((ROW_TILE, dk), F32)
        return carry

    lax.fori_loop(0, seq // ROW_TILE, proj, 0, unroll=2)

    consts = (_hgrn_consts(False, dk), _hgrn_consts(True, dk))

    def chunk_pair(h, cf, cb, states):
        data, rows = [], []
        for c, (_, _, k_s, cum_s, _, _, _) in zip((cf, cb), dirs[h]):
            base = pl.multiple_of(c * C, C)
            r = pl.ds(base, C)
            data.append((q_s[h, r, :], k_s[r, :], v_s[h, r, :], cum_s[r, :], cum_s, base))
            rows.append(r)
        o_f, o_b = _hgrn_intra_pair(data[0], data[1], consts)
        i_f, i_b = _paired_dot_nt(dirs[h][0][4][rows[0], :], states[0].astype(BF16),
                                  dirs[h][1][4][rows[1], :], states[1].astype(BF16))
        acc_s[h, rows[0], :] += o_f + i_f
        acc_s[h, rows[1], :] += o_b + i_b
        new_states = []
        for c, r, st, (_, _, v, _, _, _), (_, _, _, _, _, kt_s, dec_s) in zip(
                (cf, cb), rows, states, data, dirs[h]):
            dec = jnp.tile(dec_s[c], (dk // SUBLANES, 1))
            new_states.append(st * dec + _dot_tn(v.astype(BF16), kt_s[r, :]))
        return tuple(new_states)

    def step(i, states):
        return tuple(chunk_pair(h, i, R - 1 - i, states[h]) for h in range(heads))

    zero = jnp.zeros((dk, dk), F32)
    lax.fori_loop(0, R, step, tuple((zero, zero) for _ in range(heads)))

    def finish(n, carry):
        rows = tile_rows(n)
        for h in range(heads):
            o_ref[rows, head_cols(h)] = (_rms(acc_s[h, rows, :]) * nw_ref[...]
                                         * g_s[h, rows, :]).astype(o_ref.dtype)
        return carry

    lax.fori_loop(0, seq // ROW_TILE, finish, 0)


def hgrn_branch(xn3, w_in_b, layer, hgrn_off, lb_f, lb_b, norm_w):
    B, S, D = xn3.shape
    dk = D // HGRN_HEADS
    hp = HGRN_HEADS_PER_STEP
    wide = hp * dk
    kern = functools.partial(_hgrn_kernel, seq=S, dk=dk, heads=hp)
    vec = lambda: pltpu.VMEM((hp, S, dk), F32)
    half = lambda: pltpu.VMEM((hp, S, dk), BF16)
    dec = lambda: pltpu.VMEM((hp, S // HGRN_CHUNK, SUBLANES, dk), F32)

    def w_spec(group):
        first = (hgrn_off + group * D) // wide
        return pl.BlockSpec((None, D, wide), lambda b, j: (layer, 0, first + j))

    lb_spec = pl.BlockSpec((1, wide), lambda b, j: (0, j))
    return pl.pallas_call(
        kern,
        out_shape=jax.ShapeDtypeStruct((B, S, D), BF16),
        grid=(B, HGRN_HEADS // hp),
        in_specs=[pl.BlockSpec((None, S, D), lambda b, j: (b, 0, 0)),
                  w_spec(0), w_spec(1), w_spec(2), w_spec(3), w_spec(4),
                  lb_spec, lb_spec,
                  pl.BlockSpec((1, dk), lambda b, j: (0, 0))],
        out_specs=pl.BlockSpec((None, S, wide), lambda b, j: (b, 0, j)),
        scratch_shapes=[vec(), vec(), vec(), vec(), vec(), vec(), vec(), vec(),
                        half(), half(), half(), half(), dec(), dec()],
        compiler_params=_params("parallel", "arbitrary"),
        name="hgrn2",
    )(xn3, w_in_b, w_in_b, w_in_b, w_in_b, w_in_b,
      lb_f.reshape(1, D), lb_b.reshape(1, D), norm_w.reshape(1, dk))


def _fnet_proj_kernel(xn_ref, w_ref, cs_ref, o_ref, *, gdim):
    fu = _dot(xn_ref[...], w_ref[...]).astype(BF16)
    for g in range(FNET_GROUPS):
        t = _dot(fu[:, g * gdim:(g + 1) * gdim], cs_ref[...])
        o_ref[0, :, g * gdim:(g + 1) * gdim] = t[:, :gdim].astype(o_ref.dtype)
        o_ref[1, :, g * gdim:(g + 1) * gdim] = t[:, gdim:].astype(o_ref.dtype)


def _seq_dft_kernel(dft_ref, perm_ref, rhs_ref, o_ref, fold_s, *, seq):
    n = seq
    half = n // 2
    blk = perm_ref.shape[0]

    @pl.when(pl.program_id(1) == 0)
    def _fold():
        for part, sign in ((0, 1.0), (1, -1.0)):
            base = part * n
            for j in range(half // blk):
                own = rhs_ref[pl.ds(base + j * blk, blk), :].astype(F32)
                if j == 0:
                    mirror = _dot(perm_ref[:, :blk], rhs_ref[pl.ds(base + n - blk, blk), :])
                else:
                    mirror = _dot(perm_ref[...], rhs_ref[pl.ds(base + n - (j + 1) * blk, 2 * blk), :])
                folded = own + sign * mirror
                if part == 1 and j == 0:
                    mid = rhs_ref[pl.ds(half, BF16_ROWS), :].astype(F32)[0:1]
                    row = lax.broadcasted_iota(jnp.int32, folded.shape, 0)
                    folded = jnp.where(row == 0, mid, folded)
                fold_s[pl.ds(part * half + j * blk, blk), :] = folded.astype(fold_s.dtype)

    o_ref[...] = _dot(dft_ref[...], fold_s[...]).astype(o_ref.dtype)


DFT_FOLD_BLOCK = 128


def _folded_dft_table(n):
    c, s = _dft_tables(n)
    half = n // 2
    return np.concatenate([c[:, :half + 1], -s[:, 1:half]], axis=1)


def _mirror_permutation(blk):
    p = np.zeros((blk, 2 * blk), np.float32)
    i = np.arange(blk)
    p[i, blk - i] = 1.0
    return p


def _dft_tables(n):
    idx = np.arange(n, dtype=np.int64)
    ang = 2.0 * np.pi * ((idx[:, None] * idx[None, :]) % n).astype(np.float64) / n
    s = 1.0 / math.sqrt(n)
    return np.cos(ang) * s, np.sin(ang) * s


def fourier_branch(xn3, w_in_b, layer, fu_off_blocks, tm=512):
    B, S, D = xn3.shape
    W = D
    gdim = W // FNET_GROUPS
    c_small, s_small = _dft_tables(gdim)
    cs_small = jnp.asarray(np.concatenate([c_small, s_small], axis=1), dtype=BF16)
    dft_seq = jnp.asarray(_folded_dft_table(S), dtype=BF16)
    perm = jnp.asarray(_mirror_permutation(DFT_FOLD_BLOCK), dtype=BF16)
    tiles = S // tm
    rhs = pl.pallas_call(
        functools.partial(_fnet_proj_kernel, gdim=gdim),
        out_shape=jax.ShapeDtypeStruct((B, 2, S, W), BF16),
        grid=(B, tiles),
        in_specs=[pl.BlockSpec((None, tm, D), lambda b, r: (b, r, 0)),
                  pl.BlockSpec((None, D, W), lambda b, r: (layer, 0, fu_off_blocks)),
                  pl.BlockSpec((gdim, 2 * gdim), lambda b, r: (0, 0))],
        out_specs=pl.BlockSpec((None, 2, tm, W), lambda b, r: (b, 0, r, 0)),
        compiler_params=_params("parallel", "parallel"),
        name="fnet_proj",
    )(xn3, w_in_b, cs_small)
    rhs = rhs.reshape(B, 2 * S, W)
    return pl.pallas_call(
        functools.partial(_seq_dft_kernel, seq=S),
        out_shape=jax.ShapeDtypeStruct((B, S, W), BF16),
        grid=(B, tiles),
        in_specs=[pl.BlockSpec((tm, S), lambda b, r: (r, 0)),
                  pl.BlockSpec((DFT_FOLD_BLOCK, 2 * DFT_FOLD_BLOCK), lambda b, r: (0, 0)),
                  pl.BlockSpec((None, 2 * S, W), lambda b, r: (b, 0, 0))],
        out_specs=pl.BlockSpec((None, tm, W), lambda b, r: (b, r, 0)),
        scratch_shapes=[pltpu.VMEM((S, W), BF16)],
        compiler_params=_params("parallel", "arbitrary"),
        name="fnet_seq_dft",
    )(dft_seq, perm, rhs)


def _merge_kernel(x_ref, xn_ref, ro_ref, ho_ref, fo_ref, wga_ref, wro_ref, who_ref,
                  wf_ref, wout_ref, nw_ref, o_ref, *, d):
    xn = xn_ref[...]

    def gate(i):
        return jax.nn.sigmoid(_dot(xn, wga_ref[:, i * d:(i + 1) * d]))

    mix = gate(0) * _dot(ro_ref[...], wro_ref[...])
    mix += gate(1) * _dot(ho_ref[...], who_ref[...])
    mix += gate(2) * _dot(fo_ref[...], wf_ref[...])
    y = _dot(mix.astype(BF16), wout_ref[...])
    o_ref[...] = x_ref[...] + _rms(y) * nw_ref[...]


def merge_branches(x2, xn2, ro2, ho2, fo2, w_in_b, layer, ga_off_blocks,
                   w_ret_o, w_hgrn_o, w_fnet, w_out, norm_w, tm=512):
    T, D = x2.shape
    RV = ro2.shape[1]
    tile = lambda w: pl.BlockSpec((tm, w), lambda i: (i, 0))
    return pl.pallas_call(
        functools.partial(_merge_kernel, d=D),
        out_shape=jax.ShapeDtypeStruct((T, D), F32),
        grid=(T // tm,),
        in_specs=[tile(D), tile(D), tile(RV), tile(D), tile(D),
                  _resident((None, D, N_BRANCH * D), lambda i: (layer, 0, ga_off_blocks)),
                  _resident((None, RV, D), lambda i: (layer, 0, 0)),
                  _resident((None, D, D), lambda i: (layer, 0, 0)),
                  _resident((None, D, D), lambda i: (layer, 0, 0)),
                  _resident((None, D, D), lambda i: (layer, 0, 0)),
                  pl.BlockSpec((1, D), lambda i: (0, 0))],
        out_specs=tile(D),
        compiler_params=_params("parallel"),
        name="merge",
    )(x2, xn2, ro2, ho2, fo2, w_in_b, w_ret_o, w_hgrn_o, w_fnet, w_out, norm_w.reshape(1, D))


def _ffn_kernel(x_ref, xp_ref, xnx_ref, nw_in_ref, wup_ref, cw_ref, cb_ref, wdn_ref, nw_out_ref,
                *rest, tm, tiles_per_seq, d_ff, fc, emit_next):
    if emit_next:
        nw_next_ref, o_ref, xn_ref, hn_s, acc_s = rest
    else:
        (o_ref, hn_s, acc_s), nw_next_ref, xn_ref = rest, None, None
    i = pl.program_id(0)
    r = i % tiles_per_seq
    halo = BF16_ROWS
    x = x_ref[...]
    nw = nw_in_ref[...]
    hp = jnp.where(r == 0, 0.0, _rms(xp_ref[...]) * nw)
    hx = jnp.where(r == tiles_per_seq - 1, 0.0, _rms(xnx_ref[...]) * nw)
    hn = jnp.concatenate([hp, _rms(x) * nw, hx], axis=0).astype(BF16)
    n_ext = tm + 2 * halo
    hn_s[...] = hn
    acc_s[...] = jnp.zeros_like(acc_s)

    def conv(col, scale):
        cols = pl.ds(pl.multiple_of(col, fc), fc)
        h = _dot(hn_s[...], wup_ref[:, cols])
        cw = cw_ref[:, cols] * scale
        prev = pltpu.roll(h, 1, 0)[halo:halo + tm]
        nxt = pltpu.roll(h, n_ext - 1, 0)[halo:halo + tm]
        return (cb_ref[:, cols] * scale + prev * cw[0:1] + h[halo:halo + tm] * cw[1:2]
                + nxt * cw[2:3])

    def chunk(c, carry):
        gate = conv(c * fc, 1.0)
        half_up = conv(d_ff + c * fc, 0.5)
        inner = gate * (GELU_C0 + GELU_C1 * (gate * gate))
        act = (gate * (1.0 + jnp.tanh(inner)) * half_up).astype(BF16)
        acc_s[...] += _dot(act, wdn_ref[pl.ds(pl.multiple_of(c * fc, fc), fc), :])
        return carry

    lax.fori_loop(0, d_ff // fc, chunk, 0, unroll=2)
    y = x + _rms(acc_s[...]) * nw_out_ref[...]
    o_ref[...] = y
    if xn_ref is not None:
        xn_ref[...] = (_rms(y) * nw_next_ref[...]).astype(xn_ref.dtype)


def conv_ffn_block(x2, seq, w_up, conv_w, conv_b, w_down, nw_in, nw_out, layer, nw_next=None,
                   tm=1024, fc=256):
    T, D = x2.shape
    d_ff = w_down.shape[1]
    halo = BF16_ROWS
    tps = seq // tm
    hb = tm // halo
    n_hb = T // halo
    emit_next = nw_next is not None
    kern = functools.partial(_ffn_kernel, tm=tm, tiles_per_seq=tps, d_ff=d_ff, fc=fc,
                             emit_next=emit_next)
    vec = pl.BlockSpec((1, D), lambda i: (0, 0))
    tile = pl.BlockSpec((tm, D), lambda i: (i, 0))
    in_specs = [tile,
                pl.BlockSpec((halo, D), lambda i: (jnp.maximum(i * hb - 1, 0), 0)),
                pl.BlockSpec((halo, D), lambda i: (jnp.minimum((i + 1) * hb, n_hb - 1), 0)),
                vec,
                _resident((None, D, 2 * d_ff), lambda i: (layer, 0, 0)),
                pl.BlockSpec((None, CONV_W, 2 * d_ff), lambda i: (layer, 0, 0)),
                pl.BlockSpec((None, 1, 2 * d_ff), lambda i: (layer, 0, 0)),
                _resident((None, d_ff, D), lambda i: (layer, 0, 0)),
                vec]
    args = [x2, x2, x2, nw_in.reshape(1, D), w_up, conv_w, conv_b, w_down, nw_out.reshape(1, D)]
    out_shape = jax.ShapeDtypeStruct((T, D), F32)
    out_specs = tile
    if emit_next:
        in_specs.append(vec)
        args.append(nw_next.reshape(1, D))
        out_shape = (out_shape, jax.ShapeDtypeStruct((T, D), BF16))
        out_specs = (tile, tile)
    return pl.pallas_call(
        kern,
        out_shape=out_shape,
        grid=(T // tm,),
        in_specs=in_specs,
        out_specs=out_specs,
        scratch_shapes=[pltpu.VMEM((tm + 2 * halo, D), BF16),
                        pltpu.VMEM((tm, D), F32)],
        compiler_params=_params("parallel"),
        name="conv_ffn",
    )(*args)


def kernel(x, positions, norm_w, w_in, hgrn_lb_logits, hgrn_norm_w, w_ret_o, w_hgrn_o,
           w_fnet, w_out, w_up, conv_w, conv_b, w_down):
    B, S, D = x.shape
    depth = w_in.shape[0]
    T = B * S

    hgrn_off = 2 * D + 2 * 2 * D
    fu_off = hgrn_off + 5 * D
    ga_off = fu_off + D

    w_in_b = w_in.astype(BF16)
    w_ret_o_b = w_ret_o.astype(BF16)
    w_hgrn_o_b = w_hgrn_o.astype(BF16)
    w_fnet_b = w_fnet.astype(BF16)
    w_out_b = w_out.astype(BF16)
    w_up_b = w_up.astype(BF16)
    w_down_b = w_down.astype(BF16)
    conv_b3 = conv_b.reshape(depth, 1, -1)

    log_gamma = jnp.log(1.0 - 2.0 ** (-5.0 - jnp.arange(RET_HEADS, dtype=F32)))
    p = jax.nn.softmax(hgrn_lb_logits.astype(F32), axis=1)
    lower_bounds = jnp.cumsum(p, axis=1) - p[:, :1]

    cos, sin = rope_tables(positions, D // RET_HEADS // 2)

    x2 = x.reshape(T, D)
    xn2 = rms_norm_bf16(x2, norm_w[0, 0])
    for l in range(depth):
        xn3 = xn2.reshape(B, S, D)
        ro = retention_branch(xn3, w_in_b, l, cos, sin, log_gamma)
        ho = hgrn_branch(xn3, w_in_b, l, hgrn_off, lower_bounds[0, l], lower_bounds[1, l],
                         hgrn_norm_w[l])
        fo = fourier_branch(xn3, w_in_b, l, fu_off // D)
        x2 = merge_branches(x2, xn2, ro.reshape(T, -1), ho.reshape(T, D), fo.reshape(T, D),
                            w_in_b, l, ga_off // (N_BRANCH * D),
                            w_ret_o_b, w_hgrn_o_b, w_fnet_b, w_out_b, norm_w[l, 1])
        if l + 1 < depth:
            x2, xn2 = conv_ffn_block(x2, S, w_up_b, conv_w, conv_b3, w_down_b,
                                     norm_w[l, 2], norm_w[l, 3], l, nw_next=norm_w[l + 1, 0])
        else:
            x2 = conv_ffn_block(x2, S, w_up_b, conv_w, conv_b3, w_down_b,
                                norm_w[l, 2], norm_w[l, 3], l)
    return x2.reshape(B, S, D)
```

```python
import functools
import math

import numpy as np
import jax
import jax.numpy as jnp
from jax import lax
from jax.experimental import pallas as pl
from jax.experimental.pallas import tpu as pltpu

F32 = jnp.float32
BF16 = jnp.bfloat16

RET_HEADS = 4
HGRN_HEADS = 8
FNET_GROUPS = 4
N_BRANCH = 3
CONV_W = 3
ROPE_BASE = 10000.0
LB_FLOOR = 1e-30
EPS = 1e-6
LOG2_E = 1.4426950408889634
GELU_C0 = math.sqrt(2.0 / math.pi)
GELU_C1 = GELU_C0 * 0.044715

V7X_VMEM_LIMIT_BYTES = 56 * 1024 * 1024
SUBLANES = 8
BF16_ROWS = 16

RET_CHUNK = 256
HGRN_CHUNK = 128
HGRN_HEADS_PER_STEP = 2
HGRN_VPU_LEVELS = (0,)
ROW_TILE = 512
HGRN_ROW_TILE = 512


def _dot(a, b):
    return jnp.dot(a, b, preferred_element_type=F32)


def _dot_nt(a, b):
    return lax.dot_general(a, b, (((1,), (1,)), ((), ())), preferred_element_type=F32)


def _dot_tn(a, b):
    return lax.dot_general(a, b, (((0,), (0,)), ((), ())), preferred_element_type=F32)


def _silu(x, scale=1.0):
    return (x * scale if scale != 1.0 else x) / (1.0 + jnp.exp2(x * (-LOG2_E)))


def _rms(x):
    return x * lax.rsqrt(jnp.mean(x * x, axis=-1, keepdims=True) + EPS)


def _params(*sem):
    return pltpu.CompilerParams(dimension_semantics=sem,
                                vmem_limit_bytes=V7X_VMEM_LIMIT_BYTES)


def _resident(shape, index_map):
    return pl.BlockSpec(shape, index_map, pipeline_mode=pl.Buffered(1))


def _rope_kernel(pos_ref, invf_ref, cos_ref, sin_ref):
    ang = pos_ref[...] * invf_ref[...]
    cos_ref[...] = jnp.cos(ang)
    sin_ref[...] = jnp.sin(ang)


def rope_tables(positions, half):
    B, S = positions.shape
    pos = positions.astype(F32).reshape(B, S, 1)
    inv_freq = (ROPE_BASE ** (-jnp.arange(half, dtype=F32) / half)).reshape(1, half)
    out = jax.ShapeDtypeStruct((B, S, half), F32)
    return pl.pallas_call(
        _rope_kernel,
        out_shape=(out, out),
        grid=(B,),
        in_specs=[pl.BlockSpec((None, S, 1), lambda b: (b, 0, 0)),
                  pl.BlockSpec((1, half), lambda b: (0, 0))],
        out_specs=(pl.BlockSpec((None, S, half), lambda b: (b, 0, 0)),
                   pl.BlockSpec((None, S, half), lambda b: (b, 0, 0))),
        compiler_params=_params("parallel"),
        name="rope_tables",
    )(pos, inv_freq)


def _norm_kernel(x_ref, w_ref, o_ref):
    o_ref[...] = (_rms(x_ref[...]) * w_ref[...]).astype(o_ref.dtype)


def rms_norm_bf16(x2, w, tm=1024):
    T, D = x2.shape
    return pl.pallas_call(
        _norm_kernel,
        out_shape=jax.ShapeDtypeStruct((T, D), BF16),
        grid=(T // tm,),
        in_specs=[pl.BlockSpec((tm, D), lambda i: (i, 0)),
                  pl.BlockSpec((1, D), lambda i: (0, 0))],
        out_specs=pl.BlockSpec((tm, D), lambda i: (i, 0)),
        compiler_params=_params("parallel"),
        name="rms_norm",
    )(x2, w.reshape(1, D))


def _ret_kernel(lg_ref, xn_ref, wq_ref, wk_ref, wv_ref, wg_ref, cos_ref, sin_ref,
                o_ref, qi_s, qd_s, ki_s, v_s, g_s, st_s, kvb_s, run_s, *, seq, dk, dv):
    C = RET_CHUNK
    R = seq // C
    half = dk // 2
    lg = lg_ref[pl.program_id(1)]
    ret_scale = dk ** -0.5

    def rows_of(n):
        return pl.ds(pl.multiple_of(n * C, C), C)

    pos = lax.broadcasted_iota(jnp.int32, (C, 1), 0).astype(F32)
    qdec_f = jnp.exp(lg * (pos + 1.0))
    qdec_b = jnp.exp(lg * (C - pos))
    kdec_f = jnp.exp(lg * (C - 1.0 - pos))
    kdec_b = jnp.exp(lg * pos)
    chunk_dec = jnp.exp(lg * C)
    ii = lax.broadcasted_iota(jnp.int32, (C, C), 0)
    jj = lax.broadcasted_iota(jnp.int32, (C, C), 1)
    decay = jnp.exp(lg * jnp.abs(ii - jj).astype(F32))

    run_s[...] = jnp.zeros_like(run_s)

    def proj(t, carry):
        rows = pl.ds(pl.multiple_of(t * ROW_TILE, ROW_TILE), ROW_TILE)
        xc = xn_ref[rows, :]
        cos = cos_ref[rows, :]
        sin = sin_ref[rows, :]
        q = _dot(xc, wq_ref[...])
        q1, q2 = q[:, :half], q[:, half:]
        q = jnp.concatenate([q1 * cos - q2 * sin, q1 * sin + q2 * cos], axis=-1)
        k = _dot(xc, wk_ref[...]) * ret_scale
        k1, k2 = k[:, :half], k[:, half:]
        k = jnp.concatenate([k1 * cos - k2 * sin, k1 * sin + k2 * cos], axis=-1)
        v = _dot(xc, wv_ref[...]).astype(BF16)
        g = _dot(xc, wg_ref[...])
        qi_s[rows, :] = q.astype(BF16)
        ki_s[rows, :] = k.astype(BF16)
        v_s[rows, :] = v
        g_s[rows, :] = _silu(g).astype(BF16)
        for j in range(ROW_TILE // C):
            n = t * (ROW_TILE // C) + j
            sl = slice(j * C, (j + 1) * C)
            qd_s[rows_of(n), :] = jnp.concatenate([q[sl] * qdec_f, q[sl] * qdec_b],
                                                  axis=-1).astype(BF16)
            st_s[n, pl.ds(0, dk), :] = run_s[...].astype(BF16)
            run_s[...] = run_s[...] * chunk_dec + _dot_tn((k[sl] * kdec_f).astype(BF16), v[sl])
            kvb_s[n] = _dot_tn((k[sl] * kdec_b).astype(BF16), v[sl])
        return carry

    lax.fori_loop(0, seq // ROW_TILE, proj, 0, unroll=2)

    run_s[...] = jnp.zeros_like(run_s)

    def bwd(t, carry):
        n = R - 1 - t
        st_s[n, pl.ds(dk, dk), :] = run_s[...].astype(BF16)
        run_s[...] = run_s[...] * chunk_dec + kvb_s[n]
        return carry

    lax.fori_loop(0, R, bwd, 0)

    def out(n, carry):
        rows = rows_of(n)
        s = _dot_nt(qi_s[rows, :], ki_s[rows, :]) * decay
        o = _dot(s.astype(BF16), v_s[rows, :]) + _dot(qd_s[rows, :], st_s[n])
        o_ref[rows, :] = (_rms(o) * g_s[rows, :].astype(F32)).astype(o_ref.dtype)
        return carry

    lax.fori_loop(0, R, out, 0, unroll=4)


def retention_branch(xn3, w_in_b, layer, cos, sin, log_gamma):
    B, S, D = xn3.shape
    dk = D // RET_HEADS
    dv = 2 * dk
    H = RET_HEADS
    qk_blocks = D // dk
    v_off = 2 * D // dv
    g_off = v_off + H
    kern = functools.partial(_ret_kernel, seq=S, dk=dk, dv=dv)
    return pl.pallas_call(
        kern,
        out_shape=jax.ShapeDtypeStruct((B, S, H * dv), BF16),
        grid=(B, H),
        in_specs=[
            pl.BlockSpec(memory_space=pltpu.SMEM),
            pl.BlockSpec((None, S, D), lambda b, h: (b, 0, 0)),
            pl.BlockSpec((None, D, dk), lambda b, h: (layer, 0, h)),
            pl.BlockSpec((None, D, dk), lambda b, h: (layer, 0, qk_blocks + h)),
            pl.BlockSpec((None, D, dv), lambda b, h: (layer, 0, v_off + h)),
            pl.BlockSpec((None, D, dv), lambda b, h: (layer, 0, g_off + h)),
            pl.BlockSpec((None, S, dk // 2), lambda b, h: (b, 0, 0)),
            pl.BlockSpec((None, S, dk // 2), lambda b, h: (b, 0, 0)),
        ],
        out_specs=pl.BlockSpec((None, S, dv), lambda b, h: (b, 0, h)),
        scratch_shapes=[
            pltpu.VMEM((S, dk), BF16),
            pltpu.VMEM((S, 2 * dk), BF16),
            pltpu.VMEM((S, dk), BF16),
            pltpu.VMEM((S, dv), BF16),
            pltpu.VMEM((S, dv), BF16),
            pltpu.VMEM((S // RET_CHUNK, 2 * dk, dv), BF16),
            pltpu.VMEM((S // RET_CHUNK, dk, dv), F32),
            pltpu.VMEM((dk, dv), F32),
        ],
        compiler_params=_params("parallel", "arbitrary"),
        name="retention",
    )(log_gamma, xn3, w_in_b, w_in_b, w_in_b, w_in_b, cos, sin)


def _hgrn_gate(z, lb):
    e = jnp.exp2(jnp.abs(z) * (-LOG2_E))
    pos = z >= 0.0
    sig_neg_num = jnp.where(pos, e, 1.0)
    num = jnp.where(pos, 1.0, e) + jnp.maximum(lb, LB_FLOOR) * sig_neg_num
    inv = 1.0 / (1.0 + e)
    log2_f = jnp.log2(num * inv)
    return log2_f, (1.0 - lb) * sig_neg_num * inv


def _boundary_rows(cum_ref, base, m, reverse, row_in_group):
    C = HGRN_CHUNK
    d = cum_ref.shape[1]
    blk = 2 * m
    off = m if reverse else m - 1
    pieces = []
    if blk >= SUBLANES:
        for b in range(C // blk):
            pieces.append(jnp.broadcast_to(cum_ref[pl.ds(base + (b * blk + off), 1), :], (blk, d)))
    else:
        for g in range(C // SUBLANES):
            val = None
            for u in range(SUBLANES // blk):
                row = g * SUBLANES + u * blk + off
                piece = jnp.broadcast_to(cum_ref[pl.ds(base + row, 1), :], (SUBLANES, d))
                val = piece if val is None else jnp.where(row_in_group >= u * blk, piece, val)
            pieces.append(val)
    return jnp.concatenate(pieces, axis=0) if len(pieces) > 1 else pieces[0]


def _level_operands(level, q, k, cum, cum_ref, base, consts, reverse):
    C = HGRN_CHUNK
    _, row_in_group, signs, _ = consts
    m = 2 ** level
    if 2 * m <= SUBLANES:
        sign = signs[level]
        ref_pt = _boundary_rows(cum_ref, base, m, reverse, row_in_group)
        x = (jnp.where(sign > 0.0, q, k) * jnp.exp2((cum - ref_pt) * sign)).astype(BF16)
        return x, x, list(range(C // SUBLANES))
    xq, xall, q_groups = [], [], []
    for b in range(C // (2 * m)):
        first = slice(b * 2 * m, b * 2 * m + m)
        second = slice(b * 2 * m + m, (b + 1) * 2 * m)
        q_rows, k_rows = (first, second) if reverse else (second, first)
        edge = k_rows.start if reverse else k_rows.stop - 1
        ref_pt = cum_ref[pl.ds(base + edge, 1), :]
        xq_b = q[q_rows] * jnp.exp2(cum[q_rows] - ref_pt)
        xk_b = k[k_rows] * jnp.exp2(ref_pt - cum[k_rows])
        xq.append(xq_b)
        xall.extend([xq_b, xk_b] if reverse else [xk_b, xq_b])
        q_groups.extend(range(q_rows.start // SUBLANES, q_rows.stop // SUBLANES))
    return (jnp.concatenate(xq, axis=0).astype(BF16), jnp.concatenate(xall, axis=0).astype(BF16),
            q_groups)


def _paired_dot_nt(lhs_a, rhs_a, lhs_b, rhs_b):
    rhs = jnp.concatenate([rhs_a, rhs_b], axis=1)
    lhs = jnp.concatenate(
        [jnp.concatenate([lhs_a, jnp.zeros_like(lhs_a)], axis=1),
         jnp.concatenate([jnp.zeros_like(lhs_b), lhs_b], axis=1)], axis=0)
    s = _dot_nt(lhs, rhs)
    return s[:lhs_a.shape[0]], s[lhs_a.shape[0]:]


def _assemble_scores(scores, level_id):
    C = HGRN_CHUNK
    rows = [jnp.zeros((SUBLANES, C), F32) for _ in range(C // SUBLANES)]
    for level, entry in enumerate(scores):
        if entry is None:
            continue
        s, q_groups = entry
        for i, g in enumerate(q_groups):
            lid = level_id[g * SUBLANES:(g + 1) * SUBLANES]
            rows[g] = jnp.where(lid == level, s[i * SUBLANES:(i + 1) * SUBLANES], rows[g])
    return jnp.concatenate(rows, axis=0)


def _hgrn_intra_pair(fwd, bwd, consts):
    C = HGRN_CHUNK
    args = ((fwd, consts[0], False), (bwd, consts[1], True))
    n_vpu = len(HGRN_VPU_LEVELS)
    scores = ([None] * n_vpu, [None] * n_vpu)
    for level in range(n_vpu, C.bit_length() - 1):
        ops = [_level_operands(level, q, k, cum, cum_ref, base, cst, rev)
               for (q, k, _, cum, cum_ref, base), cst, rev in args]
        s_f, s_b = _paired_dot_nt(ops[0][0], ops[0][1], ops[1][0], ops[1][1])
        scores[0].append((s_f, ops[0][2]))
        scores[1].append((s_b, ops[1][2]))
    outs = []
    for idx, ((q, k, v, cum, _, _), cst, rev) in enumerate(args):
        attn = _assemble_scores(scores[idx], cst[0])
        o = _dot(attn.astype(BF16), v.astype(BF16))
        o += jnp.sum(q * k, axis=-1, keepdims=True) * v
        outs.append(o + _near_pairs(q, k, v, cum, rev, cst[3]))
    return outs


def _near_pairs(q, k, v, cum, reverse, masks):
    C, d = q.shape
    shape3 = (C // SUBLANES, SUBLANES, d)
    out = jnp.zeros((C, d), F32)
    for offset, mask in enumerate(masks, start=1):
        valid = mask != 0
        shift = (SUBLANES - offset) if reverse else offset

        def key_row(x):
            return pltpu.roll(x.reshape(shape3), shift, 1).reshape(C, d)

        w = jnp.exp2(jnp.where(valid, cum - key_row(cum), 0.0))
        score = jnp.sum(q * key_row(k) * w, axis=-1, keepdims=True)
        out += jnp.where(valid, score * key_row(v), 0.0)
    return out


def _near_pair_masks(C, d, reverse, levels):
    pos = lax.broadcasted_iota(jnp.int32, (C, d), 0) % SUBLANES
    masks = []
    for offset in range(1, 2 ** (max(levels) + 1)):
        valid = jnp.zeros((C, d), jnp.int32)
        for level in levels:
            m, blk = 2 ** level, 2 ** (level + 1)
            r = pos % blk
            key = (r + offset) if reverse else (r - offset)
            if reverse:
                ok = (r < m) & (key >= m) & (key < blk)
            else:
                ok = (r >= m) & (key >= 0) & (key < m)
            valid = jnp.where(ok, 1, valid)
        masks.append(valid)
    return masks


def _hgrn_consts(reverse, d):
    C = HGRN_CHUNK
    ii = lax.broadcasted_iota(jnp.int32, (C, C), 0)
    jj = lax.broadcasted_iota(jnp.int32, (C, C), 1)
    diff = ii ^ jj
    level_id = jnp.full((C, C), -1, jnp.int32)
    n_levels = C.bit_length() - 1
    for level in range(n_levels):
        level_id = jnp.where((diff >> level) == 1, level, level_id)
    level_id = jnp.where((ii < jj) if reverse else (ii > jj), level_id, -1)
    rows = lax.broadcasted_iota(jnp.int32, (C, d), 0)
    row_in_group = lax.broadcasted_iota(jnp.int32, (SUBLANES, d), 0)
    signs = []
    for level in range(SUBLANES.bit_length() - 1):
        second = ((rows >> level) & 1) == 1
        is_query = jnp.logical_not(second) if reverse else second
        signs.append(jnp.where(is_query, 1.0, -1.0))
    return level_id, row_in_group, signs, _near_pair_masks(C, d, reverse, HGRN_VPU_LEVELS)


def _chunk_cumsum(x, reverse, row_in_group):
    rows, d = x.shape
    groups = rows // SUBLANES
    per_chunk = HGRN_CHUNK // SUBLANES
    y = x.reshape(groups, SUBLANES, d)
    step = 1
    while step < SUBLANES:
        rolled = pltpu.roll(y, (SUBLANES - step) if reverse else step, 1)
        valid = (row_in_group < SUBLANES - step) if reverse else (row_in_group >= step)
        y = y + jnp.where(valid, rolled, 0.0)
        step *= 2
    out = [None] * groups
    for c in range(rows // HGRN_CHUNK):
        order = range(c * per_chunk, (c + 1) * per_chunk)
        carry = None
        for g in (reversed(order) if reverse else order):
            yg = y[g] if carry is None else y[g] + carry
            out[g] = yg
            edge = 0 if reverse else SUBLANES - 1
            carry = jnp.broadcast_to(yg[edge:edge + 1, :], (SUBLANES, d))
    return jnp.concatenate(out, axis=0)


def _hgrn_kernel(xn_ref, wq_ref, wzf_ref, wzb_ref, wi_ref, wg_ref, lbf_ref, lbb_ref, nw_ref, o_ref,
                 q_s, v_s, g_s, acc_s, kf_s, kb_s, cumf_s, cumb_s,
                 qef_s, qeb_s, ktf_s, ktb_s, decf_s, decb_s, *, seq, dk, heads):
    C = HGRN_CHUNK
    R = seq // C
    row_tile = HGRN_ROW_TILE
    per_tile = row_tile // C
    scale = dk ** -0.5

    def tile_rows(n):
        return pl.ds(pl.multiple_of(n * row_tile, row_tile), row_tile)

    def head_cols(h):
        return slice(h * dk, (h + 1) * dk)

    dirs = [((False, lbf_ref, kf_s.at[h], cumf_s.at[h], qef_s.at[h], ktf_s.at[h], decf_s.at[h]),
             (True, lbb_ref, kb_s.at[h], cumb_s.at[h], qeb_s.at[h], ktb_s.at[h], decb_s.at[h]))
            for h in range(heads)]
    row_in_group = lax.broadcasted_iota(jnp.int32, (1, SUBLANES, dk), 1)

    def proj(n, carry):
        rows = tile_rows(n)
        xc = xn_ref[rows, :]
        hq_all = _dot(xc, wq_ref[...])
        z_all = (_dot(xc, wzf_ref[...]), _dot(xc, wzb_ref[...]))
        v_all = _dot(xc, wi_ref[...])
        hg_all = _dot(xc, wg_ref[...])
        for h in range(heads):
            cols = head_cols(h)
            hq = hq_all[:, cols]
            q = _silu(hq, scale)
            q_s[h, rows, :] = q
            for idx, (reverse, lb_ref, k_s, cum_s, qe_s, kt_s, dec_s) in enumerate(dirs[h]):
                lf, kk = _hgrn_gate(z_all[idx][:, cols], lb_ref[:, cols])
                k_s[rows, :] = kk
                cum = _chunk_cumsum(lf, reverse, row_in_group)
                cum_s[rows, :] = cum
                for j in range(per_tile):
                    sl = slice(j * C, (j + 1) * C)
                    edge = j * C if reverse else (j + 1) * C - 1
                    total = cum[edge:edge + 1, :]
                    r0 = pl.multiple_of(n * row_tile + j * C, C)
                    qe_s[pl.ds(r0, C), :] = (q[sl] * jnp.exp2(cum[sl])).astype(BF16)
                    kt_s[pl.ds(r0, C), :] = (kk[sl] * jnp.exp2(total - cum[sl])).astype(BF16)
                    dec_s[n * per_tile + j] = jnp.broadcast_to(jnp.exp2(total), (SUBLANES, dk))
            v_s[h, rows, :] = v_all[:, cols]
            hg = hg_all[:, cols]
            g_s[h, rows, :] = _silu(hg)
            acc_s[h, rows, :] = jnp.zeros((row_tile, dk), F32)
        return carry

    lax.fori_loop(0, seq // row_tile, proj, 0, unroll=2)

    consts = (_hgrn_consts(False, dk), _hgrn_consts(True, dk))

    def chunk_pair(h, cf, cb, states):
        data, rows = [], []
        for c, (_, _, k_s, cum_s, _, _, _) in zip((cf, cb), dirs[h]):
            base = pl.multiple_of(c * C, C)
            r = pl.ds(base, C)
            data.append((q_s[h, r, :], k_s[r, :], v_s[h, r, :], cum_s[r, :], cum_s, base))
            rows.append(r)
        o_f, o_b = _hgrn_intra_pair(data[0], data[1], consts)
        i_f, i_b = _paired_dot_nt(dirs[h][0][4][rows[0], :], states[0].astype(BF16),
                                  dirs[h][1][4][rows[1], :], states[1].astype(BF16))
        acc_s[h, rows[0], :] += o_f + i_f
        acc_s[h, rows[1], :] += o_b + i_b
        new_states = []
        for c, r, st, (_, _, v, _, _, _), (_, _, _, _, _, kt_s, dec_s) in zip(
                (cf, cb), rows, states, data, dirs[h]):
            dec = jnp.tile(dec_s[c], (dk // SUBLANES, 1))
            new_states.append(st * dec + _dot_tn(v.astype(BF16), kt_s[r, :]))
        return tuple(new_states)

    def step(i, states):
        return tuple(chunk_pair(h, i, R - 1 - i, states[h]) for h in range(heads))

    zero = jnp.zeros((dk, dk), F32)
    lax.fori_loop(0, R, step, tuple((zero, zero) for _ in range(heads)))

    def finish(n, carry):
        rows = tile_rows(n)
        for h in range(heads):
            o_ref[rows, head_cols(h)] = (_rms(acc_s[h, rows, :]) * nw_ref[...]
                                         * g_s[h, rows, :]).astype(o_ref.dtype)
        return carry

    lax.fori_loop(0, seq // row_tile, finish, 0)


def hgrn_branch(xn3, w_in_b, layer, hgrn_off, lb_f, lb_b, norm_w):
    B, S, D = xn3.shape
    dk = D // HGRN_HEADS
    hp = HGRN_HEADS_PER_STEP
    wide = hp * dk
    kern = functools.partial(_hgrn_kernel, seq=S, dk=dk, heads=hp)
    vec = lambda: pltpu.VMEM((hp, S, dk), F32)
    half = lambda: pltpu.VMEM((hp, S, dk), BF16)
    dec = lambda: pltpu.VMEM((hp, S // HGRN_CHUNK, SUBLANES, dk), F32)

    def w_spec(group):
        first = (hgrn_off + group * D) // wide
        return pl.BlockSpec((None, D, wide), lambda b, j: (layer, 0, first + j))

    lb_spec = pl.BlockSpec((1, wide), lambda b, j: (0, j))
    return pl.pallas_call(
        kern,
        out_shape=jax.ShapeDtypeStruct((B, S, D), BF16),
        grid=(B, HGRN_HEADS // hp),
        in_specs=[pl.BlockSpec((None, S, D), lambda b, j: (b, 0, 0)),
                  w_spec(0), w_spec(1), w_spec(2), w_spec(3), w_spec(4),
                  lb_spec, lb_spec,
                  pl.BlockSpec((1, dk), lambda b, j: (0, 0))],
        out_specs=pl.BlockSpec((None, S, wide), lambda b, j: (b, 0, j)),
        scratch_shapes=[vec(), vec(), vec(), vec(), vec(), vec(), vec(), vec(),
                        half(), half(), half(), half(), dec(), dec()],
        compiler_params=_params("parallel", "arbitrary"),
        name="hgrn2",
    )(xn3, w_in_b, w_in_b, w_in_b, w_in_b, w_in_b,
      lb_f.reshape(1, D), lb_b.reshape(1, D), norm_w.reshape(1, dk))


def _fnet_proj_kernel(xn_ref, w_ref, cs_ref, o_ref, *, gdim):
    fu = _dot(xn_ref[...], w_ref[...]).astype(BF16)
    for g in range(FNET_GROUPS):
        t = _dot(fu[:, g * gdim:(g + 1) * gdim], cs_ref[...])
        o_ref[0, :, g * gdim:(g + 1) * gdim] = t[:, :gdim].astype(o_ref.dtype)
        o_ref[1, :, g * gdim:(g + 1) * gdim] = t[:, gdim:].astype(o_ref.dtype)


def _seq_dft_kernel(dft_ref, perm_ref, rhs_ref, o_ref, fold_s, *, seq):
    n = seq
    half = n // 2
    blk = perm_ref.shape[0]

    @pl.when(pl.program_id(1) == 0)
    def _fold():
        for part, sign in ((0, 1.0), (1, -1.0)):
            base = part * n
            for j in range(half // blk):
                own = rhs_ref[pl.ds(base + j * blk, blk), :].astype(F32)
                if j == 0:
                    mirror = _dot(perm_ref[:, :blk], rhs_ref[pl.ds(base + n - blk, blk), :])
                else:
                    mirror = _dot(perm_ref[...], rhs_ref[pl.ds(base + n - (j + 1) * blk, 2 * blk), :])
                folded = own + sign * mirror
                if part == 1 and j == 0:
                    mid = rhs_ref[pl.ds(half, BF16_ROWS), :].astype(F32)[0:1]
                    row = lax.broadcasted_iota(jnp.int32, folded.shape, 0)
                    folded = jnp.where(row == 0, mid, folded)
                fold_s[pl.ds(part * half + j * blk, blk), :] = folded.astype(fold_s.dtype)

    o_ref[...] = _dot(dft_ref[...], fold_s[...]).astype(o_ref.dtype)


DFT_FOLD_BLOCK = 128


def _folded_dft_table(n):
    c, s = _dft_tables(n)
    half = n // 2
    return np.concatenate([c[:, :half + 1], -s[:, 1:half]], axis=1)


def _mirror_permutation(blk):
    p = np.zeros((blk, 2 * blk), np.float32)
    i = np.arange(blk)
    p[i, blk - i] = 1.0
    return p


def _dft_tables(n):
    idx = np.arange(n, dtype=np.int64)
    ang = 2.0 * np.pi * ((idx[:, None] * idx[None, :]) % n).astype(np.float64) / n
    s = 1.0 / math.sqrt(n)
    return np.cos(ang) * s, np.sin(ang) * s


def fourier_branch(xn3, w_in_b, layer, fu_off_blocks, tm=512):
    B, S, D = xn3.shape
    W = D
    gdim = W // FNET_GROUPS
    c_small, s_small = _dft_tables(gdim)
    cs_small = jnp.asarray(np.concatenate([c_small, s_small], axis=1), dtype=BF16)
    dft_seq = jnp.asarray(_folded_dft_table(S), dtype=BF16)
    perm = jnp.asarray(_mirror_permutation(DFT_FOLD_BLOCK), dtype=BF16)
    tiles = S // tm
    rhs = pl.pallas_call(
        functools.partial(_fnet_proj_kernel, gdim=gdim),
        out_shape=jax.ShapeDtypeStruct((B, 2, S, W), BF16),
        grid=(B, tiles),
        in_specs=[pl.BlockSpec((None, tm, D), lambda b, r: (b, r, 0)),
                  pl.BlockSpec((None, D, W), lambda b, r: (layer, 0, fu_off_blocks)),
                  pl.BlockSpec((gdim, 2 * gdim), lambda b, r: (0, 0))],
        out_specs=pl.BlockSpec((None, 2, tm, W), lambda b, r: (b, 0, r, 0)),
        compiler_params=_params("parallel", "parallel"),
        name="fnet_proj",
    )(xn3, w_in_b, cs_small)
    rhs = rhs.reshape(B, 2 * S, W)
    return pl.pallas_call(
        functools.partial(_seq_dft_kernel, seq=S),
        out_shape=jax.ShapeDtypeStruct((B, S, W), BF16),
        grid=(B, tiles),
        in_specs=[pl.BlockSpec((tm, S), lambda b, r: (r, 0)),
                  pl.BlockSpec((DFT_FOLD_BLOCK, 2 * DFT_FOLD_BLOCK), lambda b, r: (0, 0)),
                  pl.BlockSpec((None, 2 * S, W), lambda b, r: (b, 0, 0))],
        out_specs=pl.BlockSpec((None, tm, W), lambda b, r: (b, r, 0)),
        scratch_shapes=[pltpu.VMEM((S, W), BF16)],
        compiler_params=_params("parallel", "arbitrary"),
        name="fnet_seq_dft",
    )(dft_seq, perm, rhs)


def _merge_kernel(x_ref, xn_ref, ro_ref, ho_ref, fo_ref, wga_ref, wro_ref, who_ref,
                  wf_ref, wout_ref, nw_ref, o_ref, *, d):
    xn = xn_ref[...]

    def gate(i):
        return jax.nn.sigmoid(_dot(xn, wga_ref[:, i * d:(i + 1) * d]))

    mix = gate(0) * _dot(ro_ref[...], wro_ref[...])
    mix += gate(1) * _dot(ho_ref[...], who_ref[...])
    mix += gate(2) * _dot(fo_ref[...], wf_ref[...])
    y = _dot(mix.astype(BF16), wout_ref[...])
    o_ref[...] = x_ref[...] + _rms(y) * nw_ref[...]


def merge_branches(x2, xn2, ro2, ho2, fo2, w_in_b, layer, ga_off_blocks,
                   w_ret_o, w_hgrn_o, w_fnet, w_out, norm_w, tm=512):
    T, D = x2.shape
    RV = ro2.shape[1]
    tile = lambda w: pl.BlockSpec((tm, w), lambda i: (i, 0))
    return pl.pallas_call(
        functools.partial(_merge_kernel, d=D),
        out_shape=jax.ShapeDtypeStruct((T, D), F32),
        grid=(T // tm,),
        in_specs=[tile(D), tile(D), tile(RV), tile(D), tile(D),
                  _resident((None, D, N_BRANCH * D), lambda i: (layer, 0, ga_off_blocks)),
                  _resident((None, RV, D), lambda i: (layer, 0, 0)),
                  _resident((None, D, D), lambda i: (layer, 0, 0)),
                  _resident((None, D, D), lambda i: (layer, 0, 0)),
                  _resident((None, D, D), lambda i: (layer, 0, 0)),
                  pl.BlockSpec((1, D), lambda i: (0, 0))],
        out_specs=tile(D),
        compiler_params=_params("parallel"),
        name="merge",
    )(x2, xn2, ro2, ho2, fo2, w_in_b, w_ret_o, w_hgrn_o, w_fnet, w_out, norm_w.reshape(1, D))


def _ffn_kernel(x_ref, xp_ref, xnx_ref, nw_in_ref, wup_ref, cw_ref, cb_ref, wdn_ref, nw_out_ref,
                *rest, tm, tiles_per_seq, d_ff, fc, emit_next):
    if emit_next:
        nw_next_ref, o_ref, xn_ref, hn_s, acc_s = rest
    else:
        (o_ref, hn_s, acc_s), nw_next_ref, xn_ref = rest, None, None
    i = pl.program_id(0)
    r = i % tiles_per_seq
    halo = BF16_ROWS
    x = x_ref[...]
    nw = nw_in_ref[...]
    hp = jnp.where(r == 0, 0.0, _rms(xp_ref[...]) * nw)
    hx = jnp.where(r == tiles_per_seq - 1, 0.0, _rms(xnx_ref[...]) * nw)
    hn = jnp.concatenate([hp, _rms(x) * nw, hx], axis=0).astype(BF16)
    n_ext = tm + 2 * halo
    hn_s[...] = hn
    acc_s[...] = jnp.zeros_like(acc_s)

    def conv(col, scale):
        cols = pl.ds(pl.multiple_of(col, fc), fc)
        h = _dot(hn_s[...], wup_ref[:, cols])
        cw = cw_ref[:, cols] * scale
        prev = pltpu.roll(h, 1, 0)[halo:halo + tm]
        nxt = pltpu.roll(h, n_ext - 1, 0)[halo:halo + tm]
        return (cb_ref[:, cols] * scale + prev * cw[0:1] + h[halo:halo + tm] * cw[1:2]
                + nxt * cw[2:3])

    def chunk(c, carry):
        gate = conv(c * fc, 1.0)
        half_up = conv(d_ff + c * fc, 0.5)
        inner = gate * (GELU_C0 + GELU_C1 * (gate * gate))
        act = (gate * (1.0 + jnp.tanh(inner)) * half_up).astype(BF16)
        acc_s[...] += _dot(act, wdn_ref[pl.ds(pl.multiple_of(c * fc, fc), fc), :])
        return carry

    lax.fori_loop(0, d_ff // fc, chunk, 0, unroll=2)
    y = x + _rms(acc_s[...]) * nw_out_ref[...]
    o_ref[...] = y
    if xn_ref is not None:
        xn_ref[...] = (_rms(y) * nw_next_ref[...]).astype(xn_ref.dtype)


def conv_ffn_block(x2, seq, w_up, conv_w, conv_b, w_down, nw_in, nw_out, layer, nw_next=None,
                   tm=1024, fc=256):
    T, D = x2.shape
    d_ff = w_down.shape[1]
    halo = BF16_ROWS
    tps = seq // tm
    hb = tm // halo
    n_hb = T // halo
    emit_next = nw_next is not None
    kern = functools.partial(_ffn_kernel, tm=tm, tiles_per_seq=tps, d_ff=d_ff, fc=fc,
                             emit_next=emit_next)
    vec = pl.BlockSpec((1, D), lambda i: (0, 0))
    tile = pl.BlockSpec((tm, D), lambda i: (i, 0))
    in_specs = [tile,
                pl.BlockSpec((halo, D), lambda i: (jnp.maximum(i * hb - 1, 0), 0)),
                pl.BlockSpec((halo, D), lambda i: (jnp.minimum((i + 1) * hb, n_hb - 1), 0)),
                vec,
                _resident((None, D, 2 * d_ff), lambda i: (layer, 0, 0)),
                pl.BlockSpec((None, CONV_W, 2 * d_ff), lambda i: (layer, 0, 0)),
                pl.BlockSpec((None, 1, 2 * d_ff), lambda i: (layer, 0, 0)),
                _resident((None, d_ff, D), lambda i: (layer, 0, 0)),
                vec]
    args = [x2, x2, x2, nw_in.reshape(1, D), w_up, conv_w, conv_b, w_down, nw_out.reshape(1, D)]
    out_shape = jax.ShapeDtypeStruct((T, D), F32)
    out_specs = tile
    if emit_next:
        in_specs.append(vec)
        args.append(nw_next.reshape(1, D))
        out_shape = (out_shape, jax.ShapeDtypeStruct((T, D), BF16))
        out_specs = (tile, tile)
    return pl.pallas_call(
        kern,
        out_shape=out_shape,
        grid=(T // tm,),
        in_specs=in_specs,
        out_specs=out_specs,
        scratch_shapes=[pltpu.VMEM((tm + 2 * halo, D), BF16),
                        pltpu.VMEM((tm, D), F32)],
        compiler_params=_params("parallel"),
        name="conv_ffn",
    )(*args)


def kernel(x, positions, norm_w, w_in, hgrn_lb_logits, hgrn_norm_w, w_ret_o, w_hgrn_o,
           w_fnet, w_out, w_up, conv_w, conv_b, w_down):
    B, S, D = x.shape
    depth = w_in.shape[0]
    T = B * S

    hgrn_off = 2 * D + 2 * 2 * D
    fu_off = hgrn_off + 5 * D
    ga_off = fu_off + D

    w_in_b = w_in.astype(BF16)
    w_ret_o_b = w_ret_o.astype(BF16)
    w_hgrn_o_b = w_hgrn_o.astype(BF16)
    w_fnet_b = w_fnet.astype(BF16)
    w_out_b = w_out.astype(BF16)
    w_up_b = w_up.astype(BF16)
    w_down_b = w_down.astype(BF16)
    conv_b3 = conv_b.reshape(depth, 1, -1)

    log_gamma = jnp.log(1.0 - 2.0 ** (-5.0 - jnp.arange(RET_HEADS, dtype=F32)))
    p = jax.nn.softmax(hgrn_lb_logits.astype(F32), axis=1)
    lower_bounds = jnp.cumsum(p, axis=1) - p[:, :1]

    cos, sin = rope_tables(positions, D // RET_HEADS // 2)

    x2 = x.reshape(T, D)
    xn2 = rms_norm_bf16(x2, norm_w[0, 0])
    for l in range(depth):
        xn3 = xn2.reshape(B, S, D)
        ro = retention_branch(xn3, w_in_b, l, cos, sin, log_gamma)
        ho = hgrn_branch(xn3, w_in_b, l, hgrn_off, lower_bounds[0, l], lower_bounds[1, l],
                         hgrn_norm_w[l])
        fo = fourier_branch(xn3, w_in_b, l, fu_off // D)
        x2 = merge_branches(x2, xn2, ro.reshape(T, -1), ho.reshape(T, D), fo.reshape(T, D),
                            w_in_b, l, ga_off // (N_BRANCH * D),
                            w_ret_o_b, w_hgrn_o_b, w_fnet_b, w_out_b, norm_w[l, 1])
        if l + 1 < depth:
            x2, xn2 = conv_ffn_block(x2, S, w_up_b, conv_w, conv_b3, w_down_b,
                                     norm_w[l, 2], norm_w[l, 3], l, nw_next=norm_w[l + 1, 0])
        else:
            x2 = conv_ffn_block(x2, S, w_up_b, conv_w, conv_b3, w_down_b,
                                norm_w[l, 2], norm_w[l, 3], l)
    return x2.reshape(B, S, D)
```

```python
import functools
import math

import numpy as np
import jax
import jax.numpy as jnp
from jax import lax
from jax.experimental import pallas as pl
from jax.experimental.pallas import tpu as pltpu

F32 = jnp.float32
BF16 = jnp.bfloat16

RET_HEADS = 4
HGRN_HEADS = 8
FNET_GROUPS = 4
N_BRANCH = 3
CONV_W = 3
ROPE_BASE = 10000.0
LB_FLOOR = 1e-30
EPS = 1e-6
LOG2_E = 1.4426950408889634
GELU_C0 = math.sqrt(2.0 / math.pi)
GELU_C1 = GELU_C0 * 0.044715

V7X_VMEM_LIMIT_BYTES = 56 * 1024 * 1024
SUBLANES = 8
BF16_ROWS = 16

RET_CHUNK = 256
HGRN_CHUNK = 128
HGRN_HEADS_PER_STEP = 2
HGRN_VPU_LEVELS = (0,)
ROW_TILE = 512
HGRN_ROW_TILE = 512


def _dot(a, b):
    return jnp.dot(a, b, preferred_element_type=F32)


def _dot_nt(a, b):
    return lax.dot_general(a, b, (((1,), (1,)), ((), ())), preferred_element_type=F32)


def _dot_tn(a, b):
    return lax.dot_general(a, b, (((0,), (0,)), ((), ())), preferred_element_type=F32)


def _silu(x, scale=1.0):
    return (x * scale if scale != 1.0 else x) / (1.0 + jnp.exp2(x * (-LOG2_E)))


def _rms(x):
    return x * lax.rsqrt(jnp.mean(x * x, axis=-1, keepdims=True) + EPS)


def _params(*sem):
    return pltpu.CompilerParams(dimension_semantics=sem,
                                vmem_limit_bytes=V7X_VMEM_LIMIT_BYTES)


def _resident(shape, index_map):
    return pl.BlockSpec(shape, index_map, pipeline_mode=pl.Buffered(1))


def _rope_kernel(pos_ref, invf_ref, cos_ref, sin_ref):
    ang = pos_ref[...] * invf_ref[...]
    cos_ref[...] = jnp.cos(ang)
    sin_ref[...] = jnp.sin(ang)


def rope_tables(positions, half):
    B, S = positions.shape
    pos = positions.astype(F32).reshape(B, S, 1)
    inv_freq = (ROPE_BASE ** (-jnp.arange(half, dtype=F32) / half)).reshape(1, half)
    out = jax.ShapeDtypeStruct((B, S, half), F32)
    return pl.pallas_call(
        _rope_kernel,
        out_shape=(out, out),
        grid=(B,),
        in_specs=[pl.BlockSpec((None, S, 1), lambda b: (b, 0, 0)),
                  pl.BlockSpec((1, half), lambda b: (0, 0))],
        out_specs=(pl.BlockSpec((None, S, half), lambda b: (b, 0, 0)),
                   pl.BlockSpec((None, S, half), lambda b: (b, 0, 0))),
        compiler_params=_params("parallel"),
        name="rope_tables",
    )(pos, inv_freq)


def _norm_kernel(x_ref, w_ref, o_ref):
    o_ref[...] = (_rms(x_ref[...]) * w_ref[...]).astype(o_ref.dtype)


def rms_norm_bf16(x2, w, tm=1024):
    T, D = x2.shape
    return pl.pallas_call(
        _norm_kernel,
        out_shape=jax.ShapeDtypeStruct((T, D), BF16),
        grid=(T // tm,),
        in_specs=[pl.BlockSpec((tm, D), lambda i: (i, 0)),
                  pl.BlockSpec((1, D), lambda i: (0, 0))],
        out_specs=pl.BlockSpec((tm, D), lambda i: (i, 0)),
        compiler_params=_params("parallel"),
        name="rms_norm",
    )(x2, w.reshape(1, D))


def _ret_kernel(lg_ref, xn_ref, wq_ref, wk_ref, wv_ref, wg_ref, cos_ref, sin_ref,
                o_ref, qi_s, qd_s, ki_s, v_s, g_s, st_s, kvb_s, run_s, *, seq, dk, dv):
    C = RET_CHUNK
    R = seq // C
    half = dk // 2
    lg = lg_ref[pl.program_id(1)]
    ret_scale = dk ** -0.5

    def rows_of(n):
        return pl.ds(pl.multiple_of(n * C, C), C)

    pos = lax.broadcasted_iota(jnp.int32, (C, 1), 0).astype(F32)
    qdec_f = jnp.exp(lg * (pos + 1.0))
    qdec_b = jnp.exp(lg * (C - pos))
    kdec_f = jnp.exp(lg * (C - 1.0 - pos))
    kdec_b = jnp.exp(lg * pos)
    chunk_dec = jnp.exp(lg * C)
    ii = lax.broadcasted_iota(jnp.int32, (C, C), 0)
    jj = lax.broadcasted_iota(jnp.int32, (C, C), 1)
    decay = jnp.exp(lg * jnp.abs(ii - jj).astype(F32))

    run_s[...] = jnp.zeros_like(run_s)

    def proj(t, carry):
        rows = pl.ds(pl.multiple_of(t * ROW_TILE, ROW_TILE), ROW_TILE)
        xc = xn_ref[rows, :]
        cos = cos_ref[rows, :]
        sin = sin_ref[rows, :]
        q = _dot(xc, wq_ref[...])
        q1, q2 = q[:, :half], q[:, half:]
        q = jnp.concatenate([q1 * cos - q2 * sin, q1 * sin + q2 * cos], axis=-1)
        k = _dot(xc, wk_ref[...]) * ret_scale
        k1, k2 = k[:, :half], k[:, half:]
        k = jnp.concatenate([k1 * cos - k2 * sin, k1 * sin + k2 * cos], axis=-1)
        v = _dot(xc, wv_ref[...]).astype(BF16)
        g = _dot(xc, wg_ref[...])
        qi_s[rows, :] = q.astype(BF16)
        ki_s[rows, :] = k.astype(BF16)
        v_s[rows, :] = v
        g_s[rows, :] = _silu(g).astype(BF16)
        for j in range(ROW_TILE // C):
            n = t * (ROW_TILE // C) + j
            sl = slice(j * C, (j + 1) * C)
            qd_s[rows_of(n), :] = jnp.concatenate([q[sl] * qdec_f, q[sl] * qdec_b],
                                                  axis=-1).astype(BF16)
            st_s[n, pl.ds(0, dk), :] = run_s[...].astype(BF16)
            run_s[...] = run_s[...] * chunk_dec + _dot_tn((k[sl] * kdec_f).astype(BF16), v[sl])
            kvb_s[n] = _dot_tn((k[sl] * kdec_b).astype(BF16), v[sl])
        return carry

    lax.fori_loop(0, seq // ROW_TILE, proj, 0, unroll=2)

    run_s[...] = jnp.zeros_like(run_s)

    def bwd(t, carry):
        n = R - 1 - t
        st_s[n, pl.ds(dk, dk), :] = run_s[...].astype(BF16)
        run_s[...] = run_s[...] * chunk_dec + kvb_s[n]
        return carry

    lax.fori_loop(0, R, bwd, 0)

    def out(n, carry):
        rows = rows_of(n)
        s = _dot_nt(qi_s[rows, :], ki_s[rows, :]) * decay
        o = _dot(s.astype(BF16), v_s[rows, :]) + _dot(qd_s[rows, :], st_s[n])
        o_ref[rows, :] = (_rms(o) * g_s[rows, :].astype(F32)).astype(o_ref.dtype)
        return carry

    lax.fori_loop(0, R, out, 0, unroll=4)


def retention_branch(xn3, w_in_b, layer, cos, sin, log_gamma):
    B, S, D = xn3.shape
    dk = D // RET_HEADS
    dv = 2 * dk
    H = RET_HEADS
    qk_blocks = D // dk
    v_off = 2 * D // dv
    g_off = v_off + H
    kern = functools.partial(_ret_kernel, seq=S, dk=dk, dv=dv)
    return pl.pallas_call(
        kern,
        out_shape=jax.ShapeDtypeStruct((B, S, H * dv), BF16),
        grid=(B, H),
        in_specs=[
            pl.BlockSpec(memory_space=pltpu.SMEM),
            pl.BlockSpec((None, S, D), lambda b, h: (b, 0, 0)),
            pl.BlockSpec((None, D, dk), lambda b, h: (layer, 0, h)),
            pl.BlockSpec((None, D, dk), lambda b, h: (layer, 0, qk_blocks + h)),
            pl.BlockSpec((None, D, dv), lambda b, h: (layer, 0, v_off + h)),
            pl.BlockSpec((None, D, dv), lambda b, h: (layer, 0, g_off + h)),
            pl.BlockSpec((None, S, dk // 2), lambda b, h: (b, 0, 0)),
            pl.BlockSpec((None, S, dk // 2), lambda b, h: (b, 0, 0)),
        ],
        out_specs=pl.BlockSpec((None, S, dv), lambda b, h: (b, 0, h)),
        scratch_shapes=[
            pltpu.VMEM((S, dk), BF16),
            pltpu.VMEM((S, 2 * dk), BF16),
            pltpu.VMEM((S, dk), BF16),
            pltpu.VMEM((S, dv), BF16),
            pltpu.VMEM((S, dv), BF16),
            pltpu.VMEM((S // RET_CHUNK, 2 * dk, dv), BF16),
            pltpu.VMEM((S // RET_CHUNK, dk, dv), F32),
            pltpu.VMEM((dk, dv), F32),
        ],
        compiler_params=_params("parallel", "arbitrary"),
        name="retention",
    )(log_gamma, xn3, w_in_b, w_in_b, w_in_b, w_in_b, cos, sin)


def _hgrn_gate(z, lb):
    e = jnp.exp2(jnp.abs(z) * (-LOG2_E))
    pos = z >= 0.0
    sig_neg_num = jnp.where(pos, e, 1.0)
    num = jnp.where(pos, 1.0, e) + jnp.maximum(lb, LB_FLOOR) * sig_neg_num
    inv = 1.0 / (1.0 + e)
    log2_f = jnp.log2(num * inv)
    return log2_f, (1.0 - lb) * sig_neg_num * inv


def _boundary_rows(cum_ref, base, m, reverse, row_in_group):
    C = HGRN_CHUNK
    d = cum_ref.shape[1]
    blk = 2 * m
    off = m if reverse else m - 1
    pieces = []
    if blk >= SUBLANES:
        for b in range(C // blk):
            pieces.append(jnp.broadcast_to(cum_ref[pl.ds(base + (b * blk + off), 1), :], (blk, d)))
    else:
        for g in range(C // SUBLANES):
            val = None
            for u in range(SUBLANES // blk):
                row = g * SUBLANES + u * blk + off
                piece = jnp.broadcast_to(cum_ref[pl.ds(base + row, 1), :], (SUBLANES, d))
                val = piece if val is None else jnp.where(row_in_group >= u * blk, piece, val)
            pieces.append(val)
    return jnp.concatenate(pieces, axis=0) if len(pieces) > 1 else pieces[0]


def _level_operands(level, q, k, cum, cum_ref, base, consts, reverse):
    C = HGRN_CHUNK
    _, row_in_group, signs, _ = consts
    m = 2 ** level
    if 2 * m <= SUBLANES:
        sign = signs[level]
        ref_pt = _boundary_rows(cum_ref, base, m, reverse, row_in_group)
        x = (jnp.where(sign > 0.0, q, k) * jnp.exp2((cum - ref_pt) * sign)).astype(BF16)
        return x, x, list(range(C // SUBLANES))
    xq, xall, q_groups = [], [], []
    for b in range(C // (2 * m)):
        first = slice(b * 2 * m, b * 2 * m + m)
        second = slice(b * 2 * m + m, (b + 1) * 2 * m)
        q_rows, k_rows = (first, second) if reverse else (second, first)
        edge = k_rows.start if reverse else k_rows.stop - 1
        ref_pt = cum_ref[pl.ds(base + edge, 1), :]
        xq_b = q[q_rows] * jnp.exp2(cum[q_rows] - ref_pt)
        xk_b = k[k_rows] * jnp.exp2(ref_pt - cum[k_rows])
        xq.append(xq_b)
        xall.extend([xq_b, xk_b] if reverse else [xk_b, xq_b])
        q_groups.extend(range(q_rows.start // SUBLANES, q_rows.stop // SUBLANES))
    return (jnp.concatenate(xq, axis=0).astype(BF16), jnp.concatenate(xall, axis=0).astype(BF16),
            q_groups)


def _paired_dot_nt(lhs_a, rhs_a, lhs_b, rhs_b):
    rhs = jnp.concatenate([rhs_a, rhs_b], axis=1)
    lhs = jnp.concatenate(
        [jnp.concatenate([lhs_a, jnp.zeros_like(lhs_a)], axis=1),
         jnp.concatenate([jnp.zeros_like(lhs_b), lhs_b], axis=1)], axis=0)
    s = _dot_nt(lhs, rhs)
    return s[:lhs_a.shape[0]], s[lhs_a.shape[0]:]


def _assemble_scores(scores, level_id):
    C = HGRN_CHUNK
    rows = [jnp.zeros((SUBLANES, C), F32) for _ in range(C // SUBLANES)]
    for level, entry in enumerate(scores):
        if entry is None:
            continue
        s, q_groups = entry
        for i, g in enumerate(q_groups):
            lid = level_id[g * SUBLANES:(g + 1) * SUBLANES]
            rows[g] = jnp.where(lid == level, s[i * SUBLANES:(i + 1) * SUBLANES], rows[g])
    return jnp.concatenate(rows, axis=0)


def _hgrn_intra_pair(fwd, bwd, consts):
    C = HGRN_CHUNK
    args = ((fwd, consts[0], False), (bwd, consts[1], True))
    n_vpu = len(HGRN_VPU_LEVELS)
    scores = ([None] * n_vpu, [None] * n_vpu)
    for level in range(n_vpu, C.bit_length() - 1):
        ops = [_level_operands(level, q, k, cum, cum_ref, base, cst, rev)
               for (q, k, _, cum, cum_ref, base), cst, rev in args]
        s_f, s_b = _paired_dot_nt(ops[0][0], ops[0][1], ops[1][0], ops[1][1])
        scores[0].append((s_f, ops[0][2]))
        scores[1].append((s_b, ops[1][2]))
    outs = []
    for idx, ((q, k, v, cum, _, _), cst, rev) in enumerate(args):
        attn = _assemble_scores(scores[idx], cst[0])
        o = _dot(attn.astype(BF16), v.astype(BF16))
        o += jnp.sum(q * k, axis=-1, keepdims=True) * v
        outs.append(o + _near_pairs(q, k, v, cum, rev, cst[3]))
    return outs


def _near_pairs(q, k, v, cum, reverse, masks):
    C, d = q.shape
    shape3 = (C // SUBLANES, SUBLANES, d)
    out = jnp.zeros((C, d), F32)
    for offset, mask in enumerate(masks, start=1):
        valid = mask != 0
        shift = (SUBLANES - offset) if reverse else offset

        def key_row(x):
            return pltpu.roll(x.reshape(shape3), shift, 1).reshape(C, d)

        w = jnp.exp2(jnp.where(valid, cum - key_row(cum), 0.0))
        score = jnp.sum(q * key_row(k) * w, axis=-1, keepdims=True)
        out += jnp.where(valid, score * key_row(v), 0.0)
    return out


def _near_pair_masks(C, d, reverse, levels):
    pos = lax.broadcasted_iota(jnp.int32, (C, d), 0) % SUBLANES
    masks = []
    for offset in range(1, 2 ** (max(levels) + 1)):
        valid = jnp.zeros((C, d), jnp.int32)
        for level in levels:
            m, blk = 2 ** level, 2 ** (level + 1)
            r = pos % blk
            key = (r + offset) if reverse else (r - offset)
            if reverse:
                ok = (r < m) & (key >= m) & (key < blk)
            else:
                ok = (r >= m) & (key >= 0) & (key < m)
            valid = jnp.where(ok, 1, valid)
        masks.append(valid)
    return masks


def _hgrn_consts(reverse, d):
    C = HGRN_CHUNK
    ii = lax.broadcasted_iota(jnp.int32, (C, C), 0)
    jj = lax.broadcasted_iota(jnp.int32, (C, C), 1)
    diff = ii ^ jj
    level_id = jnp.full((C, C), -1, jnp.int32)
    n_levels = C.bit_length() - 1
    for level in range(n_levels):
        level_id = jnp.where((diff >> level) == 1, level, level_id)
    level_id = jnp.where((ii < jj) if reverse else (ii > jj), level_id, -1)
    rows = lax.broadcasted_iota(jnp.int32, (C, d), 0)
    row_in_group = lax.broadcasted_iota(jnp.int32, (SUBLANES, d), 0)
    signs = []
    for level in range(SUBLANES.bit_length() - 1):
        second = ((rows >> level) & 1) == 1
        is_query = jnp.logical_not(second) if reverse else second
        signs.append(jnp.where(is_query, 1.0, -1.0))
    return level_id, row_in_group, signs, _near_pair_masks(C, d, reverse, HGRN_VPU_LEVELS)


def _chunk_cumsum(x, reverse, row_in_group):
    rows, d = x.shape
    groups = rows // SUBLANES
    per_chunk = HGRN_CHUNK // SUBLANES
    y = x.reshape(groups, SUBLANES, d)
    step = 1
    while step < SUBLANES:
        rolled = pltpu.roll(y, (SUBLANES - step) if reverse else step, 1)
        valid = (row_in_group < SUBLANES - step) if reverse else (row_in_group >= step)
        y = y + jnp.where(valid, rolled, 0.0)
        step *= 2
    out = [None] * groups
    for c in range(rows // HGRN_CHUNK):
        order = range(c * per_chunk, (c + 1) * per_chunk)
        carry = None
        for g in (reversed(order) if reverse else order):
            yg = y[g] if carry is None else y[g] + carry
            out[g] = yg
            edge = 0 if reverse else SUBLANES - 1
            carry = jnp.broadcast_to(yg[edge:edge + 1, :], (SUBLANES, d))
    return jnp.concatenate(out, axis=0)


def _hgrn_kernel(xn_ref, wq_ref, wzf_ref, wzb_ref, wi_ref, wg_ref, lbf_ref, lbb_ref, nw_ref, o_ref,
                 q_s, v_s, g_s, acc_s, kf_s, kb_s, cumf_s, cumb_s,
                 qef_s, qeb_s, ktf_s, ktb_s, decf_s, decb_s, *, seq, dk, heads):
    C = HGRN_CHUNK
    R = seq // C
    row_tile = HGRN_ROW_TILE
    per_tile = row_tile // C
    scale = dk ** -0.5

    def tile_rows(n):
        return pl.ds(pl.multiple_of(n * row_tile, row_tile), row_tile)

    def head_cols(h):
        return slice(h * dk, (h + 1) * dk)

    dirs = [((False, lbf_ref, kf_s.at[h], cumf_s.at[h], qef_s.at[h], ktf_s.at[h], decf_s.at[h]),
             (True, lbb_ref, kb_s.at[h], cumb_s.at[h], qeb_s.at[h], ktb_s.at[h], decb_s.at[h]))
            for h in range(heads)]
    row_in_group = lax.broadcasted_iota(jnp.int32, (1, SUBLANES, dk), 1)

    def proj(n, carry):
        rows = tile_rows(n)
        xc = xn_ref[rows, :]
        hq_all = _dot(xc, wq_ref[...])
        z_all = (_dot(xc, wzf_ref[...]), _dot(xc, wzb_ref[...]))
        v_all = _dot(xc, wi_ref[...])
        hg_all = _dot(xc, wg_ref[...])
        for h in range(heads):
            cols = head_cols(h)
            hq = hq_all[:, cols]
            q = _silu(hq, scale)
            q_s[h, rows, :] = q
            for idx, (reverse, lb_ref, k_s, cum_s, qe_s, kt_s, dec_s) in enumerate(dirs[h]):
                lf, kk = _hgrn_gate(z_all[idx][:, cols], lb_ref[:, cols])
                k_s[rows, :] = kk
                cum = _chunk_cumsum(lf, reverse, row_in_group)
                cum_s[rows, :] = cum
                for j in range(per_tile):
                    sl = slice(j * C, (j + 1) * C)
                    edge = j * C if reverse else (j + 1) * C - 1
                    total = cum[edge:edge + 1, :]
                    r0 = pl.multiple_of(n * row_tile + j * C, C)
                    qe_s[pl.ds(r0, C), :] = (q[sl] * jnp.exp2(cum[sl])).astype(BF16)
                    kt_s[pl.ds(r0, C), :] = (kk[sl] * jnp.exp2(total - cum[sl])).astype(BF16)
                    dec_s[n * per_tile + j] = jnp.broadcast_to(jnp.exp2(total), (SUBLANES, dk))
            v_s[h, rows, :] = v_all[:, cols]
            hg = hg_all[:, cols]
            g_s[h, rows, :] = _silu(hg)
            acc_s[h, rows, :] = jnp.zeros((row_tile, dk), F32)
        return carry

    lax.fori_loop(0, seq // row_tile, proj, 0, unroll=2)

    consts = (_hgrn_consts(False, dk), _hgrn_consts(True, dk))

    def chunk_pair(h, cf, cb, states):
        data, rows = [], []
        for c, (_, _, k_s, cum_s, _, _, _) in zip((cf, cb), dirs[h]):
            base = pl.multiple_of(c * C, C)
            r = pl.ds(base, C)
            data.append((q_s[h, r, :], k_s[r, :], v_s[h, r, :], cum_s[r, :], cum_s, base))
            rows.append(r)
        o_f, o_b = _hgrn_intra_pair(data[0], data[1], consts)
        i_f, i_b = _paired_dot_nt(dirs[h][0][4][rows[0], :], states[0].astype(BF16),
                                  dirs[h][1][4][rows[1], :], states[1].astype(BF16))
        acc_s[h, rows[0], :] += o_f + i_f
        acc_s[h, rows[1], :] += o_b + i_b
        new_states = []
        for c, r, st, (_, _, v, _, _, _), (_, _, _, _, _, kt_s, dec_s) in zip(
                (cf, cb), rows, states, data, dirs[h]):
            dec = jnp.tile(dec_s[c], (dk // SUBLANES, 1))
            new_states.append(st * dec + _dot_tn(v.astype(BF16), kt_s[r, :]))
        return tuple(new_states)

    def step(i, states):
        return tuple(chunk_pair(h, i, R - 1 - i, states[h]) for h in range(heads))

    zero = jnp.zeros((dk, dk), F32)
    lax.fori_loop(0, R, step, tuple((zero, zero) for _ in range(heads)))

    def finish(n, carry):
        rows = tile_rows(n)
        for h in range(heads):
            o_ref[rows, head_cols(h)] = (_rms(acc_s[h, rows, :]) * nw_ref[...]
                                         * g_s[h, rows, :]).astype(o_ref.dtype)
        return carry

    lax.fori_loop(0, seq // row_tile, finish, 0)


def hgrn_branch(xn3, w_in_b, layer, hgrn_off, lb_f, lb_b, norm_w):
    B, S, D = xn3.shape
    dk = D // HGRN_HEADS
    hp = HGRN_HEADS_PER_STEP
    wide = hp * dk
    kern = functools.partial(_hgrn_kernel, seq=S, dk=dk, heads=hp)
    vec = lambda: pltpu.VMEM((hp, S, dk), F32)
    half = lambda: pltpu.VMEM((hp, S, dk), BF16)
    dec = lambda: pltpu.VMEM((hp, S // HGRN_CHUNK, SUBLANES, dk), F32)

    def w_spec(group):
        first = (hgrn_off + group * D) // wide
        return pl.BlockSpec((None, D, wide), lambda b, j: (layer, 0, first + j))

    lb_spec = pl.BlockSpec((1, wide), lambda b, j: (0, j))
    return pl.pallas_call(
        kern,
        out_shape=jax.ShapeDtypeStruct((B, S, D), BF16),
        grid=(B, HGRN_HEADS // hp),
        in_specs=[pl.BlockSpec((None, S, D), lambda b, j: (b, 0, 0)),
                  w_spec(0), w_spec(1), w_spec(2), w_spec(3), w_spec(4),
                  lb_spec, lb_spec,
                  pl.BlockSpec((1, dk), lambda b, j: (0, 0))],
        out_specs=pl.BlockSpec((None, S, wide), lambda b, j: (b, 0, j)),
        scratch_shapes=[vec(), vec(), vec(), vec(), vec(), vec(), vec(), vec(),
                        half(), half(), half(), half(), dec(), dec()],
        compiler_params=_params("parallel", "arbitrary"),
        name="hgrn2",
    )(xn3, w_in_b, w_in_b, w_in_b, w_in_b, w_in_b,
      lb_f.reshape(1, D), lb_b.reshape(1, D), norm_w.reshape(1, dk))


def _fnet_proj_kernel(xn_ref, w_ref, cs_ref, o_ref, *, gdim):
    fu = _dot(xn_ref[...], w_ref[...]).astype(BF16)
    for g in range(FNET_GROUPS):
        t = _dot(fu[:, g * gdim:(g + 1) * gdim], cs_ref[...])
        o_ref[0, :, g * gdim:(g + 1) * gdim] = t[:, :gdim].astype(o_ref.dtype)
        o_ref[1, :, g * gdim:(g + 1) * gdim] = t[:, gdim:].astype(o_ref.dtype)


def _seq_dft_kernel(dft_ref, perm_ref, rhs_ref, o_ref, fold_s, *, seq):
    n = seq
    half = n // 2
    blk = perm_ref.shape[0]

    @pl.when(pl.program_id(1) == 0)
    def _fold():
        for part, sign in ((0, 1.0), (1, -1.0)):
            base = part * n
            for j in range(half // blk):
                own = rhs_ref[pl.ds(base + j * blk, blk), :].astype(F32)
                if j == 0:
                    mirror = _dot(perm_ref[:, :blk], rhs_ref[pl.ds(base + n - blk, blk), :])
                else:
                    mirror = _dot(perm_ref[...], rhs_ref[pl.ds(base + n - (j + 1) * blk, 2 * blk), :])
                folded = own + sign * mirror
                if part == 1 and j == 0:
                    mid = rhs_ref[pl.ds(half, BF16_ROWS), :].astype(F32)[0:1]
                    row = lax.broadcasted_iota(jnp.int32, folded.shape, 0)
                    folded = jnp.where(row == 0, mid, folded)
                fold_s[pl.ds(part * half + j * blk, blk), :] = folded.astype(fold_s.dtype)

    o_ref[...] = _dot(dft_ref[...], fold_s[...]).astype(o_ref.dtype)


DFT_FOLD_BLOCK = 128


def _folded_dft_table(n):
    c, s = _dft_tables(n)
    half = n // 2
    return np.concatenate([c[:, :half + 1], -s[:, 1:half]], axis=1)


def _mirror_permutation(blk):
    p = np.zeros((blk, 2 * blk), np.float32)
    i = np.arange(blk)
    p[i, blk - i] = 1.0
    return p


def _dft_tables(n):
    idx = np.arange(n, dtype=np.int64)
    ang = 2.0 * np.pi * ((idx[:, None] * idx[None, :]) % n).astype(np.float64) / n
    s = 1.0 / math.sqrt(n)
    return np.cos(ang) * s, np.sin(ang) * s


def fourier_branch(xn3, w_in_b, layer, fu_off_blocks, tm=512):
    B, S, D = xn3.shape
    W = D
    gdim = W // FNET_GROUPS
    c_small, s_small = _dft_tables(gdim)
    cs_small = jnp.asarray(np.concatenate([c_small, s_small], axis=1), dtype=BF16)
    dft_seq = jnp.asarray(_folded_dft_table(S), dtype=BF16)
    perm = jnp.asarray(_mirror_permutation(DFT_FOLD_BLOCK), dtype=BF16)
    tiles = S // tm
    rhs = pl.pallas_call(
        functools.partial(_fnet_proj_kernel, gdim=gdim),
        out_shape=jax.ShapeDtypeStruct((B, 2, S, W), BF16),
        grid=(B, tiles),
        in_specs=[pl.BlockSpec((None, tm, D), lambda b, r: (b, r, 0)),
                  pl.BlockSpec((None, D, W), lambda b, r: (layer, 0, fu_off_blocks)),
                  pl.BlockSpec((gdim, 2 * gdim), lambda b, r: (0, 0))],
        out_specs=pl.BlockSpec((None, 2, tm, W), lambda b, r: (b, 0, r, 0)),
        compiler_params=_params("parallel", "parallel"),
        name="fnet_proj",
    )(xn3, w_in_b, cs_small)
    rhs = rhs.reshape(B, 2 * S, W)
    return pl.pallas_call(
        functools.partial(_seq_dft_kernel, seq=S),
        out_shape=jax.ShapeDtypeStruct((B, S, W), BF16),
        grid=(B, tiles),
        in_specs=[pl.BlockSpec((tm, S), lambda b, r: (r, 0)),
                  pl.BlockSpec((DFT_FOLD_BLOCK, 2 * DFT_FOLD_BLOCK), lambda b, r: (0, 0)),
                  pl.BlockSpec((None, 2 * S, W), lambda b, r: (b, 0, 0))],
        out_specs=pl.BlockSpec((None, tm, W), lambda b, r: (b, r, 0)),
        scratch_shapes=[pltpu.VMEM((S, W), BF16)],
        compiler_params=_params("parallel", "arbitrary"),
        name="fnet_seq_dft",
    )(dft_seq, perm, rhs)


def _merge_kernel(x_ref, xn_ref, ro_ref, ho_ref, fo_ref, wga_ref, wro_ref, who_ref,
                  wf_ref, wout_ref, nw_ref, o_ref, *, d):
    xn = xn_ref[...]

    def gate(i):
        return jax.nn.sigmoid(_dot(xn, wga_ref[:, i * d:(i + 1) * d]))

    mix = gate(0) * _dot(ro_ref[...], wro_ref[...])
    mix += gate(1) * _dot(ho_ref[...], who_ref[...])
    mix += gate(2) * _dot(fo_ref[...], wf_ref[...])
    y = _dot(mix.astype(BF16), wout_ref[...])
    o_ref[...] = x_ref[...] + _rms(y) * nw_ref[...]


def merge_branches(x2, xn2, ro2, ho2, fo2, w_in_b, layer, ga_off_blocks,
                   w_ret_o, w_hgrn_o, w_fnet, w_out, norm_w, tm=512):
    T, D = x2.shape
    RV = ro2.shape[1]
    tile = lambda w: pl.BlockSpec((tm, w), lambda i: (i, 0))
    return pl.pallas_call(
        functools.partial(_merge_kernel, d=D),
        out_shape=jax.ShapeDtypeStruct((T, D), F32),
        grid=(T // tm,),
        in_specs=[tile(D), tile(D), tile(RV), tile(D), tile(D),
                  _resident((None, D, N_BRANCH * D), lambda i: (layer, 0, ga_off_blocks)),
                  _resident((None, RV, D), lambda i: (layer, 0, 0)),
                  _resident((None, D, D), lambda i: (layer, 0, 0)),
                  _resident((None, D, D), lambda i: (layer, 0, 0)),
                  _resident((None, D, D), lambda i: (layer, 0, 0)),
                  pl.BlockSpec((1, D), lambda i: (0, 0))],
        out_specs=tile(D),
        compiler_params=_params("parallel"),
        name="merge",
    )(x2, xn2, ro2, ho2, fo2, w_in_b, w_ret_o, w_hgrn_o, w_fnet, w_out, norm_w.reshape(1, D))


def _ffn_kernel(x_ref, xp_ref, xnx_ref, nw_in_ref, wup_ref, cw_ref, cb_ref, wdn_ref, nw_out_ref,
                *rest, tm, tiles_per_seq, d_ff, fc, emit_next):
    if emit_next:
        nw_next_ref, o_ref, xn_ref, hn_s, acc_s = rest
    else:
        (o_ref, hn_s, acc_s), nw_next_ref, xn_ref = rest, None, None
    i = pl.program_id(0)
    r = i % tiles_per_seq
    halo = BF16_ROWS
    x = x_ref[...]
    nw = nw_in_ref[...]
    hp = jnp.where(r == 0, 0.0, _rms(xp_ref[...]) * nw)
    hx = jnp.where(r == tiles_per_seq - 1, 0.0, _rms(xnx_ref[...]) * nw)
    hn = jnp.concatenate([hp, _rms(x) * nw, hx], axis=0).astype(BF16)
    n_ext = tm + 2 * halo
    hn_s[...] = hn
    acc_s[...] = jnp.zeros_like(acc_s)

    def conv(col, scale):
        cols = pl.ds(pl.multiple_of(col, fc), fc)
        h = _dot(hn_s[...], wup_ref[:, cols])
        cw = cw_ref[:, cols] * scale
        prev = pltpu.roll(h, 1, 0)[halo:halo + tm]
        nxt = pltpu.roll(h, n_ext - 1, 0)[halo:halo + tm]
        return (cb_ref[:, cols] * scale + prev * cw[0:1] + h[halo:halo + tm] * cw[1:2]
                + nxt * cw[2:3])

    def chunk(c, carry):
        gate = conv(c * fc, 1.0)
        half_up = conv(d_ff + c * fc, 0.5)
        inner = gate * (GELU_C0 + GELU_C1 * (gate * gate))
        act = (gate * (1.0 + jnp.tanh(inner)) * half_up).astype(BF16)
        acc_s[...] += _dot(act, wdn_ref[pl.ds(pl.multiple_of(c * fc, fc), fc), :])
        return carry

    for c in range(d_ff // fc):
        chunk(c, 0)
    y = x + _rms(acc_s[...]) * nw_out_ref[...]
    o_ref[...] = y
    if xn_ref is not None:
        xn_ref[...] = (_rms(y) * nw_next_ref[...]).astype(xn_ref.dtype)


def conv_ffn_block(x2, seq, w_up, conv_w, conv_b, w_down, nw_in, nw_out, layer, nw_next=None,
                   tm=1024, fc=256):
    T, D = x2.shape
    d_ff = w_down.shape[1]
    halo = BF16_ROWS
    tps = seq // tm
    hb = tm // halo
    n_hb = T // halo
    assert seq % tm == 0, "a row tile must not straddle two sequences (the conv zero-pads each)"
    emit_next = nw_next is not None
    kern = functools.partial(_ffn_kernel, tm=tm, tiles_per_seq=tps, d_ff=d_ff, fc=fc,
                             emit_next=emit_next)
    vec = pl.BlockSpec((1, D), lambda i: (0, 0))
    tile = pl.BlockSpec((tm, D), lambda i: (i, 0))
    in_specs = [tile,
                pl.BlockSpec((halo, D), lambda i: (jnp.maximum(i * hb - 1, 0), 0)),
                pl.BlockSpec((halo, D), lambda i: (jnp.minimum((i + 1) * hb, n_hb - 1), 0)),
                vec,
                _resident((None, D, 2 * d_ff), lambda i: (layer, 0, 0)),
                pl.BlockSpec((None, CONV_W, 2 * d_ff), lambda i: (layer, 0, 0)),
                pl.BlockSpec((None, 1, 2 * d_ff), lambda i: (layer, 0, 0)),
                _resident((None, d_ff, D), lambda i: (layer, 0, 0)),
                vec]
    args = [x2, x2, x2, nw_in.reshape(1, D), w_up, conv_w, conv_b, w_down, nw_out.reshape(1, D)]
    out_shape = jax.ShapeDtypeStruct((T, D), F32)
    out_specs = tile
    if emit_next:
        in_specs.append(vec)
        args.append(nw_next.reshape(1, D))
        out_shape = (out_shape, jax.ShapeDtypeStruct((T, D), BF16))
        out_specs = (tile, tile)
    return pl.pallas_call(
        kern,
        out_shape=out_shape,
        grid=(T // tm,),
        in_specs=in_specs,
        out_specs=out_specs,
        scratch_shapes=[pltpu.VMEM((tm + 2 * halo, D), BF16),
                        pltpu.VMEM((tm, D), F32)],
        compiler_params=_params("parallel"),
        name="conv_ffn",
    )(*args)


def kernel(x, positions, norm_w, w_in, hgrn_lb_logits, hgrn_norm_w, w_ret_o, w_hgrn_o,
           w_fnet, w_out, w_up, conv_w, conv_b, w_down):
    B, S, D = x.shape
    depth = w_in.shape[0]
    T = B * S

    hgrn_off = 2 * D + 2 * 2 * D
    fu_off = hgrn_off + 5 * D
    ga_off = fu_off + D

    w_in_b = w_in.astype(BF16)
    w_ret_o_b = w_ret_o.astype(BF16)
    w_hgrn_o_b = w_hgrn_o.astype(BF16)
    w_fnet_b = w_fnet.astype(BF16)
    w_out_b = w_out.astype(BF16)
    w_up_b = w_up.astype(BF16)
    w_down_b = w_down.astype(BF16)
    conv_b3 = conv_b.reshape(depth, 1, -1)

    log_gamma = jnp.log(1.0 - 2.0 ** (-5.0 - jnp.arange(RET_HEADS, dtype=F32)))
    p = jax.nn.softmax(hgrn_lb_logits.astype(F32), axis=1)
    lower_bounds = jnp.cumsum(p, axis=1) - p[:, :1]

    cos, sin = rope_tables(positions, D // RET_HEADS // 2)

    x2 = x.reshape(T, D)
    xn2 = rms_norm_bf16(x2, norm_w[0, 0])
    for l in range(depth):
        xn3 = xn2.reshape(B, S, D)
        ro = retention_branch(xn3, w_in_b, l, cos, sin, log_gamma)
        ho = hgrn_branch(xn3, w_in_b, l, hgrn_off, lower_bounds[0, l], lower_bounds[1, l],
                         hgrn_norm_w[l])
        fo = fourier_branch(xn3, w_in_b, l, fu_off // D)
        x2 = merge_branches(x2, xn2, ro.reshape(T, -1), ho.reshape(T, D), fo.reshape(T, D),
                            w_in_b, l, ga_off // (N_BRANCH * D),
                            w_ret_o_b, w_hgrn_o_b, w_fnet_b, w_out_b, norm_w[l, 1])
        if l + 1 < depth:
            x2, xn2 = conv_ffn_block(x2, S, w_up_b, conv_w, conv_b3, w_down_b,
                                     norm_w[l, 2], norm_w[l, 3], l, nw_next=norm_w[l + 1, 0])
        else:
            x2 = conv_ffn_block(x2, S, w_up_b, conv_w, conv_b3, w_down_b,
                                norm_w[l, 2], norm_w[l, 3], l)
    return x2.reshape(B, S, D)
```

```python
import functools
import math

import numpy as np
import jax
import jax.numpy as jnp
from jax import lax
from jax.experimental import pallas as pl
from jax.experimental.pallas import tpu as pltpu

F32 = jnp.float32
BF16 = jnp.bfloat16

RET_HEADS = 4
HGRN_HEADS = 8
FNET_GROUPS = 4
N_BRANCH = 3
CONV_W = 3
ROPE_BASE = 10000.0
LB_FLOOR = 1e-30
EPS = 1e-6
LOG2_E = 1.4426950408889634
GELU_C0 = math.sqrt(2.0 / math.pi)
GELU_C1 = GELU_C0 * 0.044715

V7X_VMEM_LIMIT_BYTES = 56 * 1024 * 1024
SUBLANES = 8
BF16_ROWS = 16

RET_CHUNK = 256
HGRN_CHUNK = 128
HGRN_HEADS_PER_STEP = 2
HGRN_VPU_LEVELS = (0,)
ROW_TILE = 512
HGRN_ROW_TILE = 512


def _dot(a, b):
    return jnp.dot(a, b, preferred_element_type=F32)


def _dot_nt(a, b):
    return lax.dot_general(a, b, (((1,), (1,)), ((), ())), preferred_element_type=F32)


def _dot_tn(a, b):
    return lax.dot_general(a, b, (((0,), (0,)), ((), ())), preferred_element_type=F32)


def _silu(x, scale=1.0):
    return (x * scale if scale != 1.0 else x) / (1.0 + jnp.exp2(x * (-LOG2_E)))


def _rms(x):
    return x * lax.rsqrt(jnp.mean(x * x, axis=-1, keepdims=True) + EPS)


def _params(*sem):
    return pltpu.CompilerParams(dimension_semantics=sem,
                                vmem_limit_bytes=V7X_VMEM_LIMIT_BYTES)


def _resident(shape, index_map):
    return pl.BlockSpec(shape, index_map, pipeline_mode=pl.Buffered(1))


def _rope_kernel(pos_ref, invf_ref, cos_ref, sin_ref):
    ang = pos_ref[...] * invf_ref[...]
    cos_ref[...] = jnp.cos(ang)
    sin_ref[...] = jnp.sin(ang)


def rope_tables(positions, half):
    B, S = positions.shape
    pos = positions.astype(F32).reshape(B, S, 1)
    inv_freq = (ROPE_BASE ** (-jnp.arange(half, dtype=F32) / half)).reshape(1, half)
    out = jax.ShapeDtypeStruct((B, S, half), F32)
    return pl.pallas_call(
        _rope_kernel,
        out_shape=(out, out),
        grid=(B,),
        in_specs=[pl.BlockSpec((None, S, 1), lambda b: (b, 0, 0)),
                  pl.BlockSpec((1, half), lambda b: (0, 0))],
        out_specs=(pl.BlockSpec((None, S, half), lambda b: (b, 0, 0)),
                   pl.BlockSpec((None, S, half), lambda b: (b, 0, 0))),
        compiler_params=_params("parallel"),
        name="rope_tables",
    )(pos, inv_freq)


def _norm_kernel(x_ref, w_ref, o_ref):
    o_ref[...] = (_rms(x_ref[...]) * w_ref[...]).astype(o_ref.dtype)


def rms_norm_bf16(x2, w, tm=1024):
    T, D = x2.shape
    return pl.pallas_call(
        _norm_kernel,
        out_shape=jax.ShapeDtypeStruct((T, D), BF16),
        grid=(T // tm,),
        in_specs=[pl.BlockSpec((tm, D), lambda i: (i, 0)),
                  pl.BlockSpec((1, D), lambda i: (0, 0))],
        out_specs=pl.BlockSpec((tm, D), lambda i: (i, 0)),
        compiler_params=_params("parallel"),
        name="rms_norm",
    )(x2, w.reshape(1, D))


def _ret_kernel(lg_ref, xn_ref, wq_ref, wk_ref, wv_ref, wg_ref, cos_ref, sin_ref,
                o_ref, qi_s, qd_s, ki_s, v_s, g_s, st_s, kvb_s, run_s, *, seq, dk, dv):
    C = RET_CHUNK
    R = seq // C
    half = dk // 2
    lg = lg_ref[pl.program_id(1)]
    ret_scale = dk ** -0.5

    def rows_of(n):
        return pl.ds(pl.multiple_of(n * C, C), C)

    pos = lax.broadcasted_iota(jnp.int32, (C, 1), 0).astype(F32)
    qdec_f = jnp.exp(lg * (pos + 1.0))
    qdec_b = jnp.exp(lg * (C - pos))
    kdec_f = jnp.exp(lg * (C - 1.0 - pos))
    kdec_b = jnp.exp(lg * pos)
    chunk_dec = jnp.exp(lg * C)
    ii = lax.broadcasted_iota(jnp.int32, (C, C), 0)
    jj = lax.broadcasted_iota(jnp.int32, (C, C), 1)
    decay = jnp.exp(lg * jnp.abs(ii - jj).astype(F32))

    run_s[...] = jnp.zeros_like(run_s)

    def proj(t, carry):
        rows = pl.ds(pl.multiple_of(t * ROW_TILE, ROW_TILE), ROW_TILE)
        xc = xn_ref[rows, :]
        cos = cos_ref[rows, :]
        sin = sin_ref[rows, :]
        q = _dot(xc, wq_ref[...])
        q1, q2 = q[:, :half], q[:, half:]
        q = jnp.concatenate([q1 * cos - q2 * sin, q1 * sin + q2 * cos], axis=-1)
        k = _dot(xc, wk_ref[...]) * ret_scale
        k1, k2 = k[:, :half], k[:, half:]
        k = jnp.concatenate([k1 * cos - k2 * sin, k1 * sin + k2 * cos], axis=-1)
        v = _dot(xc, wv_ref[...]).astype(BF16)
        g = _dot(xc, wg_ref[...])
        qi_s[rows, :] = q.astype(BF16)
        ki_s[rows, :] = k.astype(BF16)
        v_s[rows, :] = v
        g_s[rows, :] = _silu(g).astype(BF16)
        for j in range(ROW_TILE // C):
            n = t * (ROW_TILE // C) + j
            sl = slice(j * C, (j + 1) * C)
            qd_s[rows_of(n), :] = jnp.concatenate([q[sl] * qdec_f, q[sl] * qdec_b],
                                                  axis=-1).astype(BF16)
            st_s[n, pl.ds(0, dk), :] = run_s[...].astype(BF16)
            run_s[...] = run_s[...] * chunk_dec + _dot_tn((k[sl] * kdec_f).astype(BF16), v[sl])
            kvb_s[n] = _dot_tn((k[sl] * kdec_b).astype(BF16), v[sl])
        return carry

    lax.fori_loop(0, seq // ROW_TILE, proj, 0, unroll=2)

    run_s[...] = jnp.zeros_like(run_s)

    def bwd(t, carry):
        n = R - 1 - t
        st_s[n, pl.ds(dk, dk), :] = run_s[...].astype(BF16)
        run_s[...] = run_s[...] * chunk_dec + kvb_s[n]
        return carry

    lax.fori_loop(0, R, bwd, 0)

    def out(n, carry):
        rows = rows_of(n)
        s = _dot_nt(qi_s[rows, :], ki_s[rows, :]) * decay
        o = _dot(s.astype(BF16), v_s[rows, :]) + _dot(qd_s[rows, :], st_s[n])
        o_ref[rows, :] = (_rms(o) * g_s[rows, :].astype(F32)).astype(o_ref.dtype)
        return carry

    lax.fori_loop(0, R, out, 0, unroll=4)


def retention_branch(xn3, w_in_b, layer, cos, sin, log_gamma):
    B, S, D = xn3.shape
    dk = D // RET_HEADS
    dv = 2 * dk
    H = RET_HEADS
    qk_blocks = D // dk
    v_off = 2 * D // dv
    g_off = v_off + H
    kern = functools.partial(_ret_kernel, seq=S, dk=dk, dv=dv)
    return pl.pallas_call(
        kern,
        out_shape=jax.ShapeDtypeStruct((B, S, H * dv), BF16),
        grid=(B, H),
        in_specs=[
            pl.BlockSpec(memory_space=pltpu.SMEM),
            pl.BlockSpec((None, S, D), lambda b, h: (b, 0, 0)),
            pl.BlockSpec((None, D, dk), lambda b, h: (layer, 0, h)),
            pl.BlockSpec((None, D, dk), lambda b, h: (layer, 0, qk_blocks + h)),
            pl.BlockSpec((None, D, dv), lambda b, h: (layer, 0, v_off + h)),
            pl.BlockSpec((None, D, dv), lambda b, h: (layer, 0, g_off + h)),
            pl.BlockSpec((None, S, dk // 2), lambda b, h: (b, 0, 0)),
            pl.BlockSpec((None, S, dk // 2), lambda b, h: (b, 0, 0)),
        ],
        out_specs=pl.BlockSpec((None, S, dv), lambda b, h: (b, 0, h)),
        scratch_shapes=[
            pltpu.VMEM((S, dk), BF16),
            pltpu.VMEM((S, 2 * dk), BF16),
            pltpu.VMEM((S, dk), BF16),
            pltpu.VMEM((S, dv), BF16),
            pltpu.VMEM((S, dv), BF16),
            pltpu.VMEM((S // RET_CHUNK, 2 * dk, dv), BF16),
            pltpu.VMEM((S // RET_CHUNK, dk, dv), F32),
            pltpu.VMEM((dk, dv), F32),
        ],
        compiler_params=_params("parallel", "arbitrary"),
        name="retention",
    )(log_gamma, xn3, w_in_b, w_in_b, w_in_b, w_in_b, cos, sin)


def _hgrn_gate(z, lb):
    e = jnp.exp2(jnp.abs(z) * (-LOG2_E))
    pos = z >= 0.0
    sig_neg_num = jnp.where(pos, e, 1.0)
    num = jnp.where(pos, 1.0, e) + jnp.maximum(lb, LB_FLOOR) * sig_neg_num
    inv = 1.0 / (1.0 + e)
    log2_f = jnp.log2(num * inv)
    return log2_f, (1.0 - lb) * sig_neg_num * inv


def _boundary_rows(cum_ref, base, m, reverse, row_in_group):
    C = HGRN_CHUNK
    d = cum_ref.shape[1]
    blk = 2 * m
    off = m if reverse else m - 1
    pieces = []
    if blk >= SUBLANES:
        for b in range(C // blk):
            pieces.append(jnp.broadcast_to(cum_ref[pl.ds(base + (b * blk + off), 1), :], (blk, d)))
    else:
        for g in range(C // SUBLANES):
            val = None
            for u in range(SUBLANES // blk):
                row = g * SUBLANES + u * blk + off
                piece = jnp.broadcast_to(cum_ref[pl.ds(base + row, 1), :], (SUBLANES, d))
                val = piece if val is None else jnp.where(row_in_group >= u * blk, piece, val)
            pieces.append(val)
    return jnp.concatenate(pieces, axis=0) if len(pieces) > 1 else pieces[0]


def _level_operands(level, q, k, cum, cum_ref, base, consts, reverse):
    C = HGRN_CHUNK
    _, row_in_group, signs, _ = consts
    m = 2 ** level
    if 2 * m <= SUBLANES:
        sign = signs[level]
        ref_pt = _boundary_rows(cum_ref, base, m, reverse, row_in_group)
        x = (jnp.where(sign > 0.0, q, k) * jnp.exp2((cum - ref_pt) * sign)).astype(BF16)
        return x, x, list(range(C // SUBLANES))
    xq, xall, q_groups = [], [], []
    for b in range(C // (2 * m)):
        first = slice(b * 2 * m, b * 2 * m + m)
        second = slice(b * 2 * m + m, (b + 1) * 2 * m)
        q_rows, k_rows = (first, second) if reverse else (second, first)
        edge = k_rows.start if reverse else k_rows.stop - 1
        ref_pt = cum_ref[pl.ds(base + edge, 1), :]
        xq_b = q[q_rows] * jnp.exp2(cum[q_rows] - ref_pt)
        xk_b = k[k_rows] * jnp.exp2(ref_pt - cum[k_rows])
        xq.append(xq_b)
        xall.extend([xq_b, xk_b] if reverse else [xk_b, xq_b])
        q_groups.extend(range(q_rows.start // SUBLANES, q_rows.stop // SUBLANES))
    return (jnp.concatenate(xq, axis=0).astype(BF16), jnp.concatenate(xall, axis=0).astype(BF16),
            q_groups)


def _paired_dot_nt(lhs_a, rhs_a, lhs_b, rhs_b):
    rhs = jnp.concatenate([rhs_a, rhs_b], axis=1)
    lhs = jnp.concatenate(
        [jnp.concatenate([lhs_a, jnp.zeros_like(lhs_a)], axis=1),
         jnp.concatenate([jnp.zeros_like(lhs_b), lhs_b], axis=1)], axis=0)
    s = _dot_nt(lhs, rhs)
    return s[:lhs_a.shape[0]], s[lhs_a.shape[0]:]


def _assemble_scores(scores, level_id):
    C = HGRN_CHUNK
    rows = [jnp.zeros((SUBLANES, C), F32) for _ in range(C // SUBLANES)]
    for level, entry in enumerate(scores):
        if entry is None:
            continue
        s, q_groups = entry
        for i, g in enumerate(q_groups):
            lid = level_id[g * SUBLANES:(g + 1) * SUBLANES]
            rows[g] = jnp.where(lid == level, s[i * SUBLANES:(i + 1) * SUBLANES], rows[g])
    return jnp.concatenate(rows, axis=0)


def _hgrn_intra_pair(fwd, bwd, consts):
    C = HGRN_CHUNK
    args = ((fwd, consts[0], False), (bwd, consts[1], True))
    n_vpu = len(HGRN_VPU_LEVELS)
    scores = ([None] * n_vpu, [None] * n_vpu)
    for level in range(n_vpu, C.bit_length() - 1):
        ops = [_level_operands(level, q, k, cum, cum_ref, base, cst, rev)
               for (q, k, _, cum, cum_ref, base), cst, rev in args]
        s_f, s_b = _paired_dot_nt(ops[0][0], ops[0][1], ops[1][0], ops[1][1])
        scores[0].append((s_f, ops[0][2]))
        scores[1].append((s_b, ops[1][2]))
    outs = []
    for idx, ((q, k, v, cum, _, _), cst, rev) in enumerate(args):
        attn = _assemble_scores(scores[idx], cst[0])
        o = _dot(attn.astype(BF16), v.astype(BF16))
        o += jnp.sum(q * k, axis=-1, keepdims=True) * v
        outs.append(o + _near_pairs(q, k, v, cum, rev, cst[3]))
    return outs


def _near_pairs(q, k, v, cum, reverse, masks):
    C, d = q.shape
    shape3 = (C // SUBLANES, SUBLANES, d)
    out = jnp.zeros((C, d), F32)
    for offset, mask in enumerate(masks, start=1):
        valid = mask != 0
        shift = (SUBLANES - offset) if reverse else offset

        def key_row(x):
            return pltpu.roll(x.reshape(shape3), shift, 1).reshape(C, d)

        w = jnp.exp2(jnp.where(valid, cum - key_row(cum), 0.0))
        score = jnp.sum(q * key_row(k) * w, axis=-1, keepdims=True)
        out += jnp.where(valid, score * key_row(v), 0.0)
    return out


def _near_pair_masks(C, d, reverse, levels):
    pos = lax.broadcasted_iota(jnp.int32, (C, d), 0) % SUBLANES
    masks = []
    for offset in range(1, 2 ** (max(levels) + 1)):
        valid = jnp.zeros((C, d), jnp.int32)
        for level in levels:
            m, blk = 2 ** level, 2 ** (level + 1)
            r = pos % blk
            key = (r + offset) if reverse else (r - offset)
            if reverse:
                ok = (r < m) & (key >= m) & (key < blk)
            else:
                ok = (r >= m) & (key >= 0) & (key < m)
            valid = jnp.where(ok, 1, valid)
        masks.append(valid)
    return masks


def _hgrn_consts(reverse, d):
    C = HGRN_CHUNK
    ii = lax.broadcasted_iota(jnp.int32, (C, C), 0)
    jj = lax.broadcasted_iota(jnp.int32, (C, C), 1)
    diff = ii ^ jj
    level_id = jnp.full((C, C), -1, jnp.int32)
    n_levels = C.bit_length() - 1
    for level in range(n_levels):
        level_id = jnp.where((diff >> level) == 1, level, level_id)
    level_id = jnp.where((ii < jj) if reverse else (ii > jj), level_id, -1)
    rows = lax.broadcasted_iota(jnp.int32, (C, d), 0)
    row_in_group = lax.broadcasted_iota(jnp.int32, (SUBLANES, d), 0)
    signs = []
    for level in range(SUBLANES.bit_length() - 1):
        second = ((rows >> level) & 1) == 1
        is_query = jnp.logical_not(second) if reverse else second
        signs.append(jnp.where(is_query, 1.0, -1.0))
    return level_id, row_in_group, signs, _near_pair_masks(C, d, reverse, HGRN_VPU_LEVELS)


def _chunk_cumsum(x, reverse, row_in_group):
    rows, d = x.shape
    groups = rows // SUBLANES
    per_chunk = HGRN_CHUNK // SUBLANES
    y = x.reshape(groups, SUBLANES, d)
    step = 1
    while step < SUBLANES:
        rolled = pltpu.roll(y, (SUBLANES - step) if reverse else step, 1)
        valid = (row_in_group < SUBLANES - step) if reverse else (row_in_group >= step)
        y = y + jnp.where(valid, rolled, 0.0)
        step *= 2
    out = [None] * groups
    for c in range(rows // HGRN_CHUNK):
        order = range(c * per_chunk, (c + 1) * per_chunk)
        carry = None
        for g in (reversed(order) if reverse else order):
            yg = y[g] if carry is None else y[g] + carry
            out[g] = yg
            edge = 0 if reverse else SUBLANES - 1
            carry = jnp.broadcast_to(yg[edge:edge + 1, :], (SUBLANES, d))
    return jnp.concatenate(out, axis=0)


def _hgrn_kernel(xn_ref, wq_ref, wzf_ref, wzb_ref, wi_ref, wg_ref, lbf_ref, lbb_ref, nw_ref, o_ref,
                 q_s, v_s, g_s, acc_s, kf_s, kb_s, cumf_s, cumb_s,
                 qef_s, qeb_s, ktf_s, ktb_s, decf_s, decb_s, *, seq, dk, heads):
    C = HGRN_CHUNK
    R = seq // C
    row_tile = HGRN_ROW_TILE
    per_tile = row_tile // C
    scale = dk ** -0.5

    def tile_rows(n):
        return pl.ds(pl.multiple_of(n * row_tile, row_tile), row_tile)

    def head_cols(h):
        return slice(h * dk, (h + 1) * dk)

    dirs = [((False, lbf_ref, kf_s.at[h], cumf_s.at[h], qef_s.at[h], ktf_s.at[h], decf_s.at[h]),
             (True, lbb_ref, kb_s.at[h], cumb_s.at[h], qeb_s.at[h], ktb_s.at[h], decb_s.at[h]))
            for h in range(heads)]
    row_in_group = lax.broadcasted_iota(jnp.int32, (1, SUBLANES, dk), 1)

    def proj(n, carry):
        rows = tile_rows(n)
        xc = xn_ref[rows, :]
        hq_all = _dot(xc, wq_ref[...])
        z_all = (_dot(xc, wzf_ref[...]), _dot(xc, wzb_ref[...]))
        v_all = _dot(xc, wi_ref[...])
        hg_all = _dot(xc, wg_ref[...])
        for h in range(heads):
            cols = head_cols(h)
            hq = hq_all[:, cols]
            q = _silu(hq, scale)
            q_s[h, rows, :] = q
            for idx, (reverse, lb_ref, k_s, cum_s, qe_s, kt_s, dec_s) in enumerate(dirs[h]):
                lf, kk = _hgrn_gate(z_all[idx][:, cols], lb_ref[:, cols])
                k_s[rows, :] = kk
                cum = _chunk_cumsum(lf, reverse, row_in_group)
                cum_s[rows, :] = cum
                for j in range(per_tile):
                    sl = slice(j * C, (j + 1) * C)
                    edge = j * C if reverse else (j + 1) * C - 1
                    total = cum[edge:edge + 1, :]
                    r0 = pl.multiple_of(n * row_tile + j * C, C)
                    qe_s[pl.ds(r0, C), :] = (q[sl] * jnp.exp2(cum[sl])).astype(BF16)
                    kt_s[pl.ds(r0, C), :] = (kk[sl] * jnp.exp2(total - cum[sl])).astype(BF16)
                    dec_s[n * per_tile + j] = jnp.broadcast_to(jnp.exp2(total), (SUBLANES, dk))
            v_s[h, rows, :] = v_all[:, cols]
            hg = hg_all[:, cols]
            g_s[h, rows, :] = _silu(hg)
            acc_s[h, rows, :] = jnp.zeros((row_tile, dk), F32)
        return carry

    lax.fori_loop(0, seq // row_tile, proj, 0, unroll=2)

    consts = (_hgrn_consts(False, dk), _hgrn_consts(True, dk))

    def chunk_pair(h, cf, cb, states):
        data, rows = [], []
        for c, (_, _, k_s, cum_s, _, _, _) in zip((cf, cb), dirs[h]):
            base = pl.multiple_of(c * C, C)
            r = pl.ds(base, C)
            data.append((q_s[h, r, :], k_s[r, :], v_s[h, r, :], cum_s[r, :], cum_s, base))
            rows.append(r)
        o_f, o_b = _hgrn_intra_pair(data[0], data[1], consts)
        i_f, i_b = _paired_dot_nt(dirs[h][0][4][rows[0], :], states[0].astype(BF16),
                                  dirs[h][1][4][rows[1], :], states[1].astype(BF16))
        acc_s[h, rows[0], :] += o_f + i_f
        acc_s[h, rows[1], :] += o_b + i_b
        new_states = []
        for c, r, st, (_, _, v, _, _, _), (_, _, _, _, _, kt_s, dec_s) in zip(
                (cf, cb), rows, states, data, dirs[h]):
            dec = jnp.tile(dec_s[c], (dk // SUBLANES, 1))
            new_states.append(st * dec + _dot_tn(v.astype(BF16), kt_s[r, :]))
        return tuple(new_states)

    def step(i, states):
        return tuple(chunk_pair(h, i, R - 1 - i, states[h]) for h in range(heads))

    zero = jnp.zeros((dk, dk), F32)
    lax.fori_loop(0, R, step, tuple((zero, zero) for _ in range(heads)))

    def finish(n, carry):
        rows = tile_rows(n)
        for h in range(heads):
            o_ref[rows, head_cols(h)] = (_rms(acc_s[h, rows, :]) * nw_ref[...]
                                         * g_s[h, rows, :]).astype(o_ref.dtype)
        return carry

    lax.fori_loop(0, seq // row_tile, finish, 0)


def hgrn_branch(xn3, w_in_b, layer, hgrn_off, lb_f, lb_b, norm_w):
    B, S, D = xn3.shape
    dk = D // HGRN_HEADS
    hp = HGRN_HEADS_PER_STEP
    wide = hp * dk
    kern = functools.partial(_hgrn_kernel, seq=S, dk=dk, heads=hp)
    vec = lambda: pltpu.VMEM((hp, S, dk), F32)
    half = lambda: pltpu.VMEM((hp, S, dk), BF16)
    dec = lambda: pltpu.VMEM((hp, S // HGRN_CHUNK, SUBLANES, dk), F32)

    def w_spec(group):
        first = (hgrn_off + group * D) // wide
        return pl.BlockSpec((None, D, wide), lambda b, j: (layer, 0, first + j))

    lb_spec = pl.BlockSpec((1, wide), lambda b, j: (0, j))
    return pl.pallas_call(
        kern,
        out_shape=jax.ShapeDtypeStruct((B, S, D), BF16),
        grid=(B, HGRN_HEADS // hp),
        in_specs=[pl.BlockSpec((None, S, D), lambda b, j: (b, 0, 0)),
                  w_spec(0), w_spec(1), w_spec(2), w_spec(3), w_spec(4),
                  lb_spec, lb_spec,
                  pl.BlockSpec((1, dk), lambda b, j: (0, 0))],
        out_specs=pl.BlockSpec((None, S, wide), lambda b, j: (b, 0, j)),
        scratch_shapes=[vec(), vec(), vec(), vec(), vec(), vec(), vec(), vec(),
                        half(), half(), half(), half(), dec(), dec()],
        compiler_params=_params("parallel", "arbitrary"),
        name="hgrn2",
    )(xn3, w_in_b, w_in_b, w_in_b, w_in_b, w_in_b,
      lb_f.reshape(1, D), lb_b.reshape(1, D), norm_w.reshape(1, dk))


def _fnet_proj_kernel(xn_ref, w_ref, cs_ref, o_ref, *, gdim):
    fu = _dot(xn_ref[...], w_ref[...]).astype(BF16)
    for g in range(FNET_GROUPS):
        t = _dot(fu[:, g * gdim:(g + 1) * gdim], cs_ref[...])
        o_ref[0, :, g * gdim:(g + 1) * gdim] = t[:, :gdim].astype(o_ref.dtype)
        o_ref[1, :, g * gdim:(g + 1) * gdim] = t[:, gdim:].astype(o_ref.dtype)


def _seq_dft_kernel(cos_ref, sin_ref, perm_ref, rhs_ref, o_ref, fold_s, mir_s, *, seq):
    n = seq
    half = n // 2
    blk = perm_ref.shape[0]
    c0 = 1.0 / math.sqrt(n)

    for part, sign in ((0, 1.0), (1, -1.0)):
        base = part * n
        for j in range(half // blk):
            own = rhs_ref[pl.ds(base + j * blk, blk), :].astype(F32)
            if j == 0:
                mirror = _dot(perm_ref[:, :blk], rhs_ref[pl.ds(base + n - blk, blk), :])
            else:
                mirror = _dot(perm_ref[...], rhs_ref[pl.ds(base + n - (j + 1) * blk, 2 * blk), :])
            fold_s[pl.ds(part * half + j * blk, blk), :] = (own + sign * mirror).astype(fold_s.dtype)

    mid = rhs_ref[pl.ds(half, BF16_ROWS), :].astype(F32)[0:1] * c0
    p_ext = _dot(cos_ref[...], fold_s[pl.ds(0, half), :])
    q = _dot(sin_ref[...], fold_s[pl.ds(half, half), :])
    row = lax.broadcasted_iota(jnp.int32, q.shape, 0)
    p = p_ext[:half] + jnp.where((row & 1) == 0, 1.0, -1.0) * mid
    o_ref[pl.ds(0, half), :] = (p - q).astype(o_ref.dtype)

    mir_s[pl.ds(0, half), :] = (p + q).astype(mir_s.dtype)
    first = jnp.where(lax.broadcasted_iota(jnp.int32, (BF16_ROWS, q.shape[1]), 0) == 0, 1.0, 0.0)
    tail = (p_ext[half:half + BF16_ROWS] + mid) * first
    mir_s[pl.ds(half, BF16_ROWS), :] = tail.astype(mir_s.dtype)
    mir_s[pl.ds(half + BF16_ROWS, blk - BF16_ROWS), :] = jnp.zeros(
        (blk - BF16_ROWS, tail.shape[1]), mir_s.dtype)
    for j in range(half // blk):
        window = mir_s[pl.ds(half - (j + 1) * blk, 2 * blk), :]
        o_ref[pl.ds(half + j * blk, blk), :] = _dot(perm_ref[...], window).astype(o_ref.dtype)


DFT_FOLD_BLOCK = 128


def _half_dft_tables(n):
    c, s = _dft_tables(n)
    half = n // 2
    cos_ext = np.zeros((half + BF16_ROWS, half))
    cos_ext[:half + 1] = c[:half + 1, :half]
    return cos_ext, s[:half, :half]


def _mirror_permutation(blk):
    p = np.zeros((blk, 2 * blk), np.float32)
    i = np.arange(blk)
    p[i, blk - i] = 1.0
    return p


def _dft_tables(n):
    idx = np.arange(n, dtype=np.int64)
    ang = 2.0 * np.pi * ((idx[:, None] * idx[None, :]) % n).astype(np.float64) / n
    s = 1.0 / math.sqrt(n)
    return np.cos(ang) * s, np.sin(ang) * s


def fourier_branch(xn3, w_in_b, layer, fu_off_blocks, tm=512):
    B, S, D = xn3.shape
    W = D
    gdim = W // FNET_GROUPS
    c_small, s_small = _dft_tables(gdim)
    cs_small = jnp.asarray(np.concatenate([c_small, s_small], axis=1), dtype=BF16)
    half = S // 2
    assert half % 2 == 0 and half % DFT_FOLD_BLOCK == 0
    cos_np, sin_np = _half_dft_tables(S)
    dft_cos = jnp.asarray(cos_np, dtype=BF16)
    dft_sin = jnp.asarray(sin_np, dtype=BF16)
    perm = jnp.asarray(_mirror_permutation(DFT_FOLD_BLOCK), dtype=BF16)
    tiles = S // tm
    rhs = pl.pallas_call(
        functools.partial(_fnet_proj_kernel, gdim=gdim),
        out_shape=jax.ShapeDtypeStruct((B, 2, S, W), BF16),
        grid=(B, tiles),
        in_specs=[pl.BlockSpec((None, tm, D), lambda b, r: (b, r, 0)),
                  pl.BlockSpec((None, D, W), lambda b, r: (layer, 0, fu_off_blocks)),
                  pl.BlockSpec((gdim, 2 * gdim), lambda b, r: (0, 0))],
        out_specs=pl.BlockSpec((None, 2, tm, W), lambda b, r: (b, 0, r, 0)),
        compiler_params=_params("parallel", "parallel"),
        name="fnet_proj",
    )(xn3, w_in_b, cs_small)
    rhs = rhs.reshape(B, 2 * S, W)
    return pl.pallas_call(
        functools.partial(_seq_dft_kernel, seq=S),
        out_shape=jax.ShapeDtypeStruct((B, S, W), BF16),
        grid=(B,),
        in_specs=[_resident((half + BF16_ROWS, half), lambda b: (0, 0)),
                  _resident((half, half), lambda b: (0, 0)),
                  pl.BlockSpec((DFT_FOLD_BLOCK, 2 * DFT_FOLD_BLOCK), lambda b: (0, 0)),
                  pl.BlockSpec((None, 2 * S, W), lambda b: (b, 0, 0))],
        out_specs=pl.BlockSpec((None, S, W), lambda b: (b, 0, 0)),
        scratch_shapes=[pltpu.VMEM((S, W), BF16),
                        pltpu.VMEM((half + DFT_FOLD_BLOCK, W), BF16)],
        compiler_params=_params("parallel"),
        name="fnet_seq_dft",
    )(dft_cos, dft_sin, perm, rhs)


def _merge_kernel(x_ref, xn_ref, ro_ref, ho_ref, fo_ref, wga_ref, wro_ref, who_ref,
                  wf_ref, wout_ref, nw_ref, o_ref, *, d):
    xn = xn_ref[...]

    def gate(i):
        return jax.nn.sigmoid(_dot(xn, wga_ref[:, i * d:(i + 1) * d]))

    mix = gate(0) * _dot(ro_ref[...], wro_ref[...])
    mix += gate(1) * _dot(ho_ref[...], who_ref[...])
    mix += gate(2) * _dot(fo_ref[...], wf_ref[...])
    y = _dot(mix.astype(BF16), wout_ref[...])
    o_ref[...] = x_ref[...] + _rms(y) * nw_ref[...]


def merge_branches(x2, xn2, ro2, ho2, fo2, w_in_b, layer, ga_off_blocks,
                   w_ret_o, w_hgrn_o, w_fnet, w_out, norm_w, tm=512):
    T, D = x2.shape
    RV = ro2.shape[1]
    tile = lambda w: pl.BlockSpec((tm, w), lambda i: (i, 0))
    return pl.pallas_call(
        functools.partial(_merge_kernel, d=D),
        out_shape=jax.ShapeDtypeStruct((T, D), F32),
        grid=(T // tm,),
        in_specs=[tile(D), tile(D), tile(RV), tile(D), tile(D),
                  _resident((None, D, N_BRANCH * D), lambda i: (layer, 0, ga_off_blocks)),
                  _resident((None, RV, D), lambda i: (layer, 0, 0)),
                  _resident((None, D, D), lambda i: (layer, 0, 0)),
                  _resident((None, D, D), lambda i: (layer, 0, 0)),
                  _resident((None, D, D), lambda i: (layer, 0, 0)),
                  pl.BlockSpec((1, D), lambda i: (0, 0))],
        out_specs=tile(D),
        compiler_params=_params("parallel"),
        name="merge",
    )(x2, xn2, ro2, ho2, fo2, w_in_b, w_ret_o, w_hgrn_o, w_fnet, w_out, norm_w.reshape(1, D))


def _ffn_kernel(x_ref, xp_ref, xnx_ref, nw_in_ref, wup_ref, cw_ref, cb_ref, wdn_ref, nw_out_ref,
                *rest, tm, tiles_per_seq, d_ff, fc, emit_next):
    if emit_next:
        nw_next_ref, o_ref, xn_ref, hn_s, acc_s = rest
    else:
        (o_ref, hn_s, acc_s), nw_next_ref, xn_ref = rest, None, None
    i = pl.program_id(0)
    r = i % tiles_per_seq
    halo = BF16_ROWS
    x = x_ref[...]
    nw = nw_in_ref[...]
    hp = jnp.where(r == 0, 0.0, _rms(xp_ref[...]) * nw)
    hx = jnp.where(r == tiles_per_seq - 1, 0.0, _rms(xnx_ref[...]) * nw)
    hn = jnp.concatenate([hp, _rms(x) * nw, hx], axis=0).astype(BF16)
    n_ext = tm + 2 * halo
    hn_s[...] = hn
    acc_s[...] = jnp.zeros_like(acc_s)

    def conv(col, scale):
        cols = pl.ds(pl.multiple_of(col, fc), fc)
        h = _dot(hn_s[...], wup_ref[:, cols])
        cw = cw_ref[:, cols] * scale
        prev = pltpu.roll(h, 1, 0)[halo:halo + tm]
        nxt = pltpu.roll(h, n_ext - 1, 0)[halo:halo + tm]
        return (cb_ref[:, cols] * scale + prev * cw[0:1] + h[halo:halo + tm] * cw[1:2]
                + nxt * cw[2:3])

    def chunk(c, carry):
        gate = conv(c * fc, 1.0)
        half_up = conv(d_ff + c * fc, 0.5)
        inner = gate * (GELU_C0 + GELU_C1 * (gate * gate))
        act = (gate * (1.0 + jnp.tanh(inner)) * half_up).astype(BF16)
        acc_s[...] += _dot(act, wdn_ref[pl.ds(pl.multiple_of(c * fc, fc), fc), :])
        return carry

    for c in range(d_ff // fc):
        chunk(c, 0)
    y = x + _rms(acc_s[...]) * nw_out_ref[...]
    o_ref[...] = y
    if xn_ref is not None:
        xn_ref[...] = (_rms(y) * nw_next_ref[...]).astype(xn_ref.dtype)


def conv_ffn_block(x2, seq, w_up, conv_w, conv_b, w_down, nw_in, nw_out, layer, nw_next=None,
                   tm=1024, fc=256):
    T, D = x2.shape
    d_ff = w_down.shape[1]
    halo = BF16_ROWS
    tps = seq // tm
    hb = tm // halo
    n_hb = T // halo
    assert seq % tm == 0, "a row tile must not straddle two sequences (the conv zero-pads each)"
    emit_next = nw_next is not None
    kern = functools.partial(_ffn_kernel, tm=tm, tiles_per_seq=tps, d_ff=d_ff, fc=fc,
                             emit_next=emit_next)
    vec = pl.BlockSpec((1, D), lambda i: (0, 0))
    tile = pl.BlockSpec((tm, D), lambda i: (i, 0))
    in_specs = [tile,
                pl.BlockSpec((halo, D), lambda i: (jnp.maximum(i * hb - 1, 0), 0)),
                pl.BlockSpec((halo, D), lambda i: (jnp.minimum((i + 1) * hb, n_hb - 1), 0)),
                vec,
                _resident((None, D, 2 * d_ff), lambda i: (layer, 0, 0)),
                pl.BlockSpec((None, CONV_W, 2 * d_ff), lambda i: (layer, 0, 0)),
                pl.BlockSpec((None, 1, 2 * d_ff), lambda i: (layer, 0, 0)),
                _resident((None, d_ff, D), lambda i: (layer, 0, 0)),
                vec]
    args = [x2, x2, x2, nw_in.reshape(1, D), w_up, conv_w, conv_b, w_down, nw_out.reshape(1, D)]
    out_shape = jax.ShapeDtypeStruct((T, D), F32)
    out_specs = tile
    if emit_next:
        in_specs.append(vec)
        args.append(nw_next.reshape(1, D))
        out_shape = (out_shape, jax.ShapeDtypeStruct((T, D), BF16))
        out_specs = (tile, tile)
    return pl.pallas_call(
        kern,
        out_shape=out_shape,
        grid=(T // tm,),
        in_specs=in_specs,
        out_specs=out_specs,
        scratch_shapes=[pltpu.VMEM((tm + 2 * halo, D), BF16),
                        pltpu.VMEM((tm, D), F32)],
        compiler_params=_params("parallel"),
        name="conv_ffn",
    )(*args)


def kernel(x, positions, norm_w, w_in, hgrn_lb_logits, hgrn_norm_w, w_ret_o, w_hgrn_o,
           w_fnet, w_out, w_up, conv_w, conv_b, w_down):
    B, S, D = x.shape
    depth = w_in.shape[0]
    T = B * S

    hgrn_off = 2 * D + 2 * 2 * D
    fu_off = hgrn_off + 5 * D
    ga_off = fu_off + D

    w_in_b = w_in.astype(BF16)
    w_ret_o_b = w_ret_o.astype(BF16)
    w_hgrn_o_b = w_hgrn_o.astype(BF16)
    w_fnet_b = w_fnet.astype(BF16)
    w_out_b = w_out.astype(BF16)
    w_up_b = w_up.astype(BF16)
    w_down_b = w_down.astype(BF16)
    conv_b3 = conv_b.reshape(depth, 1, -1)

    log_gamma = jnp.log(1.0 - 2.0 ** (-5.0 - jnp.arange(RET_HEADS, dtype=F32)))
    p = jax.nn.softmax(hgrn_lb_logits.astype(F32), axis=1)
    lower_bounds = jnp.cumsum(p, axis=1) - p[:, :1]

    cos, sin = rope_tables(positions, D // RET_HEADS // 2)

    x2 = x.reshape(T, D)
    xn2 = rms_norm_bf16(x2, norm_w[0, 0])
    for l in range(depth):
        xn3 = xn2.reshape(B, S, D)
        ro = retention_branch(xn3, w_in_b, l, cos, sin, log_gamma)
        ho = hgrn_branch(xn3, w_in_b, l, hgrn_off, lower_bounds[0, l], lower_bounds[1, l],
                         hgrn_norm_w[l])
        fo = fourier_branch(xn3, w_in_b, l, fu_off // D)
        x2 = merge_branches(x2, xn2, ro.reshape(T, -1), ho.reshape(T, D), fo.reshape(T, D),
                            w_in_b, l, ga_off // (N_BRANCH * D),
                            w_ret_o_b, w_hgrn_o_b, w_fnet_b, w_out_b, norm_w[l, 1])
        if l + 1 < depth:
            x2, xn2 = conv_ffn_block(x2, S, w_up_b, conv_w, conv_b3, w_down_b,
                                     norm_w[l, 2], norm_w[l, 3], l, nw_next=norm_w[l + 1, 0])
        else:
            x2 = conv_ffn_block(x2, S, w_up_b, conv_w, conv_b3, w_down_b,
                                norm_w[l, 2], norm_w[l, 3], l)
    return x2.reshape(B, S, D)
```

```python
import functools
import math

import numpy as np
import jax
import jax.numpy as jnp
from jax import lax
from jax.experimental import pallas as pl
from jax.experimental.pallas import tpu as pltpu

F32 = jnp.float32
BF16 = jnp.bfloat16

RET_HEADS = 4
HGRN_HEADS = 8
FNET_GROUPS = 4
N_BRANCH = 3
CONV_W = 3
ROPE_BASE = 10000.0
LB_FLOOR = 1e-30
EPS = 1e-6
LOG2_E = 1.4426950408889634
GELU_C0 = math.sqrt(2.0 / math.pi)
GELU_C1 = GELU_C0 * 0.044715

V7X_VMEM_LIMIT_BYTES = 56 * 1024 * 1024
SUBLANES = 8
BF16_ROWS = 16

RET_CHUNK = 256
HGRN_CHUNK = 128
HGRN_HEADS_PER_STEP = 2
HGRN_VPU_LEVELS = (0,)
ROW_TILE = 512
HGRN_ROW_TILE = 512


def _dot(a, b):
    return jnp.dot(a, b, preferred_element_type=F32)


def _dot_nt(a, b):
    return lax.dot_general(a, b, (((1,), (1,)), ((), ())), preferred_element_type=F32)


def _dot_tn(a, b):
    return lax.dot_general(a, b, (((0,), (0,)), ((), ())), preferred_element_type=F32)


def _silu(x, scale=1.0):
    return (x * scale if scale != 1.0 else x) / (1.0 + jnp.exp2(x * (-LOG2_E)))


def _rms(x):
    return x * lax.rsqrt(jnp.mean(x * x, axis=-1, keepdims=True) + EPS)


def _params(*sem):
    return pltpu.CompilerParams(dimension_semantics=sem,
                                vmem_limit_bytes=V7X_VMEM_LIMIT_BYTES)


def _resident(shape, index_map):
    return pl.BlockSpec(shape, index_map, pipeline_mode=pl.Buffered(1))


def _rope_norm_kernel(pos_ref, invf_ref, x_ref, w_ref, cos_ref, sin_ref, xn_ref):
    ang = pos_ref[...] * invf_ref[...]
    cos_ref[...] = jnp.cos(ang)
    sin_ref[...] = jnp.sin(ang)
    xn_ref[...] = (_rms(x_ref[...]) * w_ref[...]).astype(xn_ref.dtype)


def rope_tables_and_norm(positions, half, x, w):
    B, S = positions.shape
    D = x.shape[-1]
    pos = positions.astype(F32).reshape(B, S, 1)
    inv_freq = (ROPE_BASE ** (-jnp.arange(half, dtype=F32) / half)).reshape(1, half)
    table = jax.ShapeDtypeStruct((B, S, half), F32)
    table_spec = pl.BlockSpec((None, S, half), lambda b: (b, 0, 0))
    rows_spec = pl.BlockSpec((None, S, D), lambda b: (b, 0, 0))
    return pl.pallas_call(
        _rope_norm_kernel,
        out_shape=(table, table, jax.ShapeDtypeStruct((B, S, D), BF16)),
        grid=(B,),
        in_specs=[pl.BlockSpec((None, S, 1), lambda b: (b, 0, 0)),
                  pl.BlockSpec((1, half), lambda b: (0, 0)),
                  rows_spec,
                  pl.BlockSpec((1, D), lambda b: (0, 0))],
        out_specs=(table_spec, table_spec, rows_spec),
        compiler_params=_params("parallel"),
        name="rope_tables_norm",
    )(pos, inv_freq, x, w.reshape(1, D))


def _ret_kernel(lg_ref, xn_ref, wq_ref, wk_ref, wv_ref, wg_ref, cos_ref, sin_ref,
                o_ref, qi_s, qd_s, ki_s, v_s, g_s, st_s, kvb_s, run_s, *, seq, dk, dv):
    C = RET_CHUNK
    R = seq // C
    half = dk // 2
    lg = lg_ref[pl.program_id(1)]
    ret_scale = dk ** -0.5

    def rows_of(n):
        return pl.ds(pl.multiple_of(n * C, C), C)

    pos = lax.broadcasted_iota(jnp.int32, (C, 1), 0).astype(F32)
    qdec_f = jnp.exp(lg * (pos + 1.0))
    qdec_b = jnp.exp(lg * (C - pos))
    kdec_f = jnp.exp(lg * (C - 1.0 - pos))
    kdec_b = jnp.exp(lg * pos)
    chunk_dec = jnp.exp(lg * C)
    ii = lax.broadcasted_iota(jnp.int32, (C, C), 0)
    jj = lax.broadcasted_iota(jnp.int32, (C, C), 1)
    decay = jnp.exp(lg * jnp.abs(ii - jj).astype(F32))

    run_s[...] = jnp.zeros_like(run_s)

    def proj(t, carry):
        rows = pl.ds(pl.multiple_of(t * ROW_TILE, ROW_TILE), ROW_TILE)
        xc = xn_ref[rows, :]
        cos = cos_ref[rows, :]
        sin = sin_ref[rows, :]
        q = _dot(xc, wq_ref[...])
        q1, q2 = q[:, :half], q[:, half:]
        q = jnp.concatenate([q1 * cos - q2 * sin, q1 * sin + q2 * cos], axis=-1)
        k = _dot(xc, wk_ref[...]) * ret_scale
        k1, k2 = k[:, :half], k[:, half:]
        k = jnp.concatenate([k1 * cos - k2 * sin, k1 * sin + k2 * cos], axis=-1)
        v = _dot(xc, wv_ref[...]).astype(BF16)
        g = _dot(xc, wg_ref[...])
        qi_s[rows, :] = q.astype(BF16)
        ki_s[rows, :] = k.astype(BF16)
        v_s[rows, :] = v
        g_s[rows, :] = _silu(g).astype(BF16)
        for j in range(ROW_TILE // C):
            n = t * (ROW_TILE // C) + j
            sl = slice(j * C, (j + 1) * C)
            qd_s[rows_of(n), :] = jnp.concatenate([q[sl] * qdec_f, q[sl] * qdec_b],
                                                  axis=-1).astype(BF16)
            st_s[n, pl.ds(0, dk), :] = run_s[...].astype(BF16)
            run_s[...] = run_s[...] * chunk_dec + _dot_tn((k[sl] * kdec_f).astype(BF16), v[sl])
            kvb_s[n] = _dot_tn((k[sl] * kdec_b).astype(BF16), v[sl])
        return carry

    lax.fori_loop(0, seq // ROW_TILE, proj, 0, unroll=2)

    run_s[...] = jnp.zeros_like(run_s)

    def out(t, carry):
        n = R - 1 - t
        rows = rows_of(n)
        st_s[n, pl.ds(dk, dk), :] = run_s[...].astype(BF16)
        s = _dot_nt(qi_s[rows, :], ki_s[rows, :]) * decay
        o = _dot(s.astype(BF16), v_s[rows, :]) + _dot(qd_s[rows, :], st_s[n])
        o_ref[rows, :] = (_rms(o) * g_s[rows, :].astype(F32)).astype(o_ref.dtype)
        run_s[...] = run_s[...] * chunk_dec + kvb_s[n]
        return carry

    lax.fori_loop(0, R, out, 0, unroll=4)


def retention_branch(xn3, w_in_b, layer, cos, sin, log_gamma):
    B, S, D = xn3.shape
    dk = D // RET_HEADS
    dv = 2 * dk
    H = RET_HEADS
    qk_blocks = D // dk
    v_off = 2 * D // dv
    g_off = v_off + H
    kern = functools.partial(_ret_kernel, seq=S, dk=dk, dv=dv)
    return pl.pallas_call(
        kern,
        out_shape=jax.ShapeDtypeStruct((B, S, H * dv), BF16),
        grid=(B, H),
        in_specs=[
            pl.BlockSpec(memory_space=pltpu.SMEM),
            pl.BlockSpec((None, S, D), lambda b, h: (b, 0, 0)),
            pl.BlockSpec((None, D, dk), lambda b, h: (layer, 0, h)),
            pl.BlockSpec((None, D, dk), lambda b, h: (layer, 0, qk_blocks + h)),
            pl.BlockSpec((None, D, dv), lambda b, h: (layer, 0, v_off + h)),
            pl.BlockSpec((None, D, dv), lambda b, h: (layer, 0, g_off + h)),
            pl.BlockSpec((None, S, dk // 2), lambda b, h: (b, 0, 0)),
            pl.BlockSpec((None, S, dk // 2), lambda b, h: (b, 0, 0)),
        ],
        out_specs=pl.BlockSpec((None, S, dv), lambda b, h: (b, 0, h)),
        scratch_shapes=[
            pltpu.VMEM((S, dk), BF16),
            pltpu.VMEM((S, 2 * dk), BF16),
            pltpu.VMEM((S, dk), BF16),
            pltpu.VMEM((S, dv), BF16),
            pltpu.VMEM((S, dv), BF16),
            pltpu.VMEM((S // RET_CHUNK, 2 * dk, dv), BF16),
            pltpu.VMEM((S // RET_CHUNK, dk, dv), F32),
            pltpu.VMEM((dk, dv), F32),
        ],
        compiler_params=_params("parallel", "arbitrary"),
        name="retention",
    )(log_gamma, xn3, w_in_b, w_in_b, w_in_b, w_in_b, cos, sin)


def _hgrn_gate(z, lb):
    e = jnp.exp2(jnp.abs(z) * (-LOG2_E))
    pos = z >= 0.0
    sig_neg_num = jnp.where(pos, e, 1.0)
    num = jnp.where(pos, 1.0, e) + jnp.maximum(lb, LB_FLOOR) * sig_neg_num
    inv = 1.0 / (1.0 + e)
    log2_f = jnp.log2(num * inv)
    return log2_f, (1.0 - lb) * sig_neg_num * inv


def _boundary_rows(cum_ref, base, m, reverse, row_in_group):
    C = HGRN_CHUNK
    d = cum_ref.shape[1]
    blk = 2 * m
    off = m if reverse else m - 1
    pieces = []
    if blk >= SUBLANES:
        for b in range(C // blk):
            pieces.append(jnp.broadcast_to(cum_ref[pl.ds(base + (b * blk + off), 1), :], (blk, d)))
    else:
        for g in range(C // SUBLANES):
            val = None
            for u in range(SUBLANES // blk):
                row = g * SUBLANES + u * blk + off
                piece = jnp.broadcast_to(cum_ref[pl.ds(base + row, 1), :], (SUBLANES, d))
                val = piece if val is None else jnp.where(row_in_group >= u * blk, piece, val)
            pieces.append(val)
    return jnp.concatenate(pieces, axis=0) if len(pieces) > 1 else pieces[0]


def _level_operands(level, q, k, cum, cum_ref, base, consts, reverse):
    C = HGRN_CHUNK
    _, row_in_group, signs, _ = consts
    m = 2 ** level
    if 2 * m <= SUBLANES:
        sign = signs[level]
        ref_pt = _boundary_rows(cum_ref, base, m, reverse, row_in_group)
        x = (jnp.where(sign > 0.0, q, k) * jnp.exp2((cum - ref_pt) * sign)).astype(BF16)
        return x, x, list(range(C // SUBLANES))
    xq, xall, q_groups = [], [], []
    for b in range(C // (2 * m)):
        first = slice(b * 2 * m, b * 2 * m + m)
        second = slice(b * 2 * m + m, (b + 1) * 2 * m)
        q_rows, k_rows = (first, second) if reverse else (second, first)
        edge = k_rows.start if reverse else k_rows.stop - 1
        ref_pt = cum_ref[pl.ds(base + edge, 1), :]
        xq_b = q[q_rows] * jnp.exp2(cum[q_rows] - ref_pt)
        xk_b = k[k_rows] * jnp.exp2(ref_pt - cum[k_rows])
        xq.append(xq_b)
        xall.extend([xq_b, xk_b] if reverse else [xk_b, xq_b])
        q_groups.extend(range(q_rows.start // SUBLANES, q_rows.stop // SUBLANES))
    return (jnp.concatenate(xq, axis=0).astype(BF16), jnp.concatenate(xall, axis=0).astype(BF16),
            q_groups)


def _paired_dot_nt(lhs_a, rhs_a, lhs_b, rhs_b):
    rhs = jnp.concatenate([rhs_a, rhs_b], axis=1)
    lhs = jnp.concatenate(
        [jnp.concatenate([lhs_a, jnp.zeros_like(lhs_a)], axis=1),
         jnp.concatenate([jnp.zeros_like(lhs_b), lhs_b], axis=1)], axis=0)
    s = _dot_nt(lhs, rhs)
    return s[:lhs_a.shape[0]], s[lhs_a.shape[0]:]


def _assemble_scores(scores, level_id):
    C = HGRN_CHUNK
    rows = [jnp.zeros((SUBLANES, C), F32) for _ in range(C // SUBLANES)]
    for level, entry in enumerate(scores):
        if entry is None:
            continue
        s, q_groups = entry
        for i, g in enumerate(q_groups):
            lid = level_id[g * SUBLANES:(g + 1) * SUBLANES]
            rows[g] = jnp.where(lid == level, s[i * SUBLANES:(i + 1) * SUBLANES], rows[g])
    return jnp.concatenate(rows, axis=0)


def _hgrn_intra_pair(fwd, bwd, consts):
    C = HGRN_CHUNK
    args = ((fwd, consts[0], False), (bwd, consts[1], True))
    n_vpu = len(HGRN_VPU_LEVELS)
    scores = ([None] * n_vpu, [None] * n_vpu)
    for level in range(n_vpu, C.bit_length() - 1):
        ops = [_level_operands(level, q, k, cum, cum_ref, base, cst, rev)
               for (q, k, _, cum, cum_ref, base), cst, rev in args]
        s_f, s_b = _paired_dot_nt(ops[0][0], ops[0][1], ops[1][0], ops[1][1])
        scores[0].append((s_f, ops[0][2]))
        scores[1].append((s_b, ops[1][2]))
    outs = []
    for idx, ((q, k, v, cum, _, _), cst, rev) in enumerate(args):
        attn = _assemble_scores(scores[idx], cst[0])
        o = _dot(attn.astype(BF16), v.astype(BF16))
        o += jnp.sum(q * k, axis=-1, keepdims=True) * v
        outs.append(o + _near_pairs(q, k, v, cum, rev, cst[3]))
    return outs


def _near_pairs(q, k, v, cum, reverse, masks):
    C, d = q.shape
    shape3 = (C // SUBLANES, SUBLANES, d)
    out = jnp.zeros((C, d), F32)
    for offset, mask in enumerate(masks, start=1):
        valid = mask != 0
        shift = (SUBLANES - offset) if reverse else offset

        def key_row(x):
            return pltpu.roll(x.reshape(shape3), shift, 1).reshape(C, d)

        w = jnp.exp2(jnp.where(valid, cum - key_row(cum), 0.0))
        score = jnp.sum(q * key_row(k) * w, axis=-1, keepdims=True)
        out += jnp.where(valid, score * key_row(v), 0.0)
    return out


def _near_pair_masks(C, d, reverse, levels):
    pos = lax.broadcasted_iota(jnp.int32, (C, d), 0) % SUBLANES
    masks = []
    for offset in range(1, 2 ** (max(levels) + 1)):
        valid = jnp.zeros((C, d), jnp.int32)
        for level in levels:
            m, blk = 2 ** level, 2 ** (level + 1)
            r = pos % blk
            key = (r + offset) if reverse else (r - offset)
            if reverse:
                ok = (r < m) & (key >= m) & (key < blk)
            else:
                ok = (r >= m) & (key >= 0) & (key < m)
            valid = jnp.where(ok, 1, valid)
        masks.append(valid)
    return masks


def _hgrn_consts(reverse, d):
    C = HGRN_CHUNK
    ii = lax.broadcasted_iota(jnp.int32, (C, C), 0)
    jj = lax.broadcasted_iota(jnp.int32, (C, C), 1)
    diff = ii ^ jj
    level_id = jnp.full((C, C), -1, jnp.int32)
    n_levels = C.bit_length() - 1
    for level in range(n_levels):
        level_id = jnp.where((diff >> level) == 1, level, level_id)
    level_id = jnp.where((ii < jj) if reverse else (ii > jj), level_id, -1)
    rows = lax.broadcasted_iota(jnp.int32, (C, d), 0)
    row_in_group = lax.broadcasted_iota(jnp.int32, (SUBLANES, d), 0)
    signs = []
    for level in range(SUBLANES.bit_length() - 1):
        second = ((rows >> level) & 1) == 1
        is_query = jnp.logical_not(second) if reverse else second
        signs.append(jnp.where(is_query, 1.0, -1.0))
    return level_id, row_in_group, signs, _near_pair_masks(C, d, reverse, HGRN_VPU_LEVELS)


def _chunk_cumsum(x, reverse, row_in_group):
    rows, d = x.shape
    groups = rows // SUBLANES
    per_chunk = HGRN_CHUNK // SUBLANES
    y = x.reshape(groups, SUBLANES, d)
    step = 1
    while step < SUBLANES:
        rolled = pltpu.roll(y, (SUBLANES - step) if reverse else step, 1)
        valid = (row_in_group < SUBLANES - step) if reverse else (row_in_group >= step)
        y = y + jnp.where(valid, rolled, 0.0)
        step *= 2
    out = [None] * groups
    for c in range(rows // HGRN_CHUNK):
        order = range(c * per_chunk, (c + 1) * per_chunk)
        carry = None
        for g in (reversed(order) if reverse else order):
            yg = y[g] if carry is None else y[g] + carry
            out[g] = yg
            edge = 0 if reverse else SUBLANES - 1
            carry = jnp.broadcast_to(yg[edge:edge + 1, :], (SUBLANES, d))
    return jnp.concatenate(out, axis=0)


def _hgrn_kernel(xn_ref, wq_ref, wzf_ref, wzb_ref, wi_ref, wg_ref, lbf_ref, lbb_ref, nw_ref, o_ref,
                 q_s, v_s, g_s, acc_s, kf_s, kb_s, cumf_s, cumb_s,
                 qef_s, qeb_s, ktf_s, ktb_s, decf_s, decb_s, *, seq, dk, heads):
    C = HGRN_CHUNK
    R = seq // C
    row_tile = HGRN_ROW_TILE
    per_tile = row_tile // C
    scale = dk ** -0.5

    def tile_rows(n):
        return pl.ds(pl.multiple_of(n * row_tile, row_tile), row_tile)

    def head_cols(h):
        return slice(h * dk, (h + 1) * dk)

    dirs = [((False, lbf_ref, kf_s.at[h], cumf_s.at[h], qef_s.at[h], ktf_s.at[h], decf_s.at[h]),
             (True, lbb_ref, kb_s.at[h], cumb_s.at[h], qeb_s.at[h], ktb_s.at[h], decb_s.at[h]))
            for h in range(heads)]
    row_in_group = lax.broadcasted_iota(jnp.int32, (1, SUBLANES, dk), 1)

    def proj(n, carry):
        rows = tile_rows(n)
        xc = xn_ref[rows, :]
        hq_all = _dot(xc, wq_ref[...])
        z_all = (_dot(xc, wzf_ref[...]), _dot(xc, wzb_ref[...]))
        v_all = _dot(xc, wi_ref[...])
        hg_all = _dot(xc, wg_ref[...])
        for h in range(heads):
            cols = head_cols(h)
            hq = hq_all[:, cols]
            q = _silu(hq, scale)
            q_s[h, rows, :] = q
            for idx, (reverse, lb_ref, k_s, cum_s, qe_s, kt_s, dec_s) in enumerate(dirs[h]):
                lf, kk = _hgrn_gate(z_all[idx][:, cols], lb_ref[:, cols])
                k_s[rows, :] = kk
                cum = _chunk_cumsum(lf, reverse, row_in_group)
                cum_s[rows, :] = cum
                for j in range(per_tile):
                    sl = slice(j * C, (j + 1) * C)
                    edge = j * C if reverse else (j + 1) * C - 1
                    total = cum[edge:edge + 1, :]
                    r0 = pl.multiple_of(n * row_tile + j * C, C)
                    qe_s[pl.ds(r0, C), :] = (q[sl] * jnp.exp2(cum[sl])).astype(BF16)
                    kt_s[pl.ds(r0, C), :] = (kk[sl] * jnp.exp2(total - cum[sl])).astype(BF16)
                    dec_s[n * per_tile + j] = jnp.broadcast_to(jnp.exp2(total), (SUBLANES, dk))
            v_s[h, rows, :] = v_all[:, cols]
            hg = hg_all[:, cols]
            g_s[h, rows, :] = _silu(hg)
            acc_s[h, rows, :] = jnp.zeros((row_tile, dk), F32)
        return carry

    lax.fori_loop(0, seq // row_tile, proj, 0, unroll=2)

    consts = (_hgrn_consts(False, dk), _hgrn_consts(True, dk))

    def chunk_pair(h, cf, cb, states):
        data, rows = [], []
        for c, (_, _, k_s, cum_s, _, _, _) in zip((cf, cb), dirs[h]):
            base = pl.multiple_of(c * C, C)
            r = pl.ds(base, C)
            data.append((q_s[h, r, :], k_s[r, :], v_s[h, r, :], cum_s[r, :], cum_s, base))
            rows.append(r)
        o_f, o_b = _hgrn_intra_pair(data[0], data[1], consts)
        i_f, i_b = _paired_dot_nt(dirs[h][0][4][rows[0], :], states[0].astype(BF16),
                                  dirs[h][1][4][rows[1], :], states[1].astype(BF16))
        acc_s[h, rows[0], :] += o_f + i_f
        acc_s[h, rows[1], :] += o_b + i_b
        new_states = []
        for c, r, st, (_, _, v, _, _, _), (_, _, _, _, _, kt_s, dec_s) in zip(
                (cf, cb), rows, states, data, dirs[h]):
            dec = jnp.tile(dec_s[c], (dk // SUBLANES, 1))
            new_states.append(st * dec + _dot_tn(v.astype(BF16), kt_s[r, :]))
        return tuple(new_states)

    def step(i, states):
        return tuple(chunk_pair(h, i, R - 1 - i, states[h]) for h in range(heads))

    zero = jnp.zeros((dk, dk), F32)
    lax.fori_loop(0, R, step, tuple((zero, zero) for _ in range(heads)))

    def finish(n, carry):
        rows = tile_rows(n)
        for h in range(heads):
            o_ref[rows, head_cols(h)] = (_rms(acc_s[h, rows, :]) * nw_ref[...]
                                         * g_s[h, rows, :]).astype(o_ref.dtype)
        return carry

    lax.fori_loop(0, seq // row_tile, finish, 0)


def hgrn_branch(xn3, w_in_b, layer, hgrn_off, lb_f, lb_b, norm_w):
    B, S, D = xn3.shape
    dk = D // HGRN_HEADS
    hp = HGRN_HEADS_PER_STEP
    wide = hp * dk
    kern = functools.partial(_hgrn_kernel, seq=S, dk=dk, heads=hp)
    vec = lambda: pltpu.VMEM((hp, S, dk), F32)
    half = lambda: pltpu.VMEM((hp, S, dk), BF16)
    dec = lambda: pltpu.VMEM((hp, S // HGRN_CHUNK, SUBLANES, dk), F32)

    def w_spec(group):
        first = (hgrn_off + group * D) // wide
        return pl.BlockSpec((None, D, wide), lambda b, j: (layer, 0, first + j))

    lb_spec = pl.BlockSpec((1, wide), lambda b, j: (0, j))
    return pl.pallas_call(
        kern,
        out_shape=jax.ShapeDtypeStruct((B, S, D), BF16),
        grid=(B, HGRN_HEADS // hp),
        in_specs=[pl.BlockSpec((None, S, D), lambda b, j: (b, 0, 0)),
                  w_spec(0), w_spec(1), w_spec(2), w_spec(3), w_spec(4),
                  lb_spec, lb_spec,
                  pl.BlockSpec((1, dk), lambda b, j: (0, 0))],
        out_specs=pl.BlockSpec((None, S, wide), lambda b, j: (b, 0, j)),
        scratch_shapes=[vec(), vec(), vec(), vec(), vec(), vec(), vec(), vec(),
                        half(), half(), half(), half(), dec(), dec()],
        compiler_params=_params("parallel", "arbitrary"),
        name="hgrn2",
    )(xn3, w_in_b, w_in_b, w_in_b, w_in_b, w_in_b,
      lb_f.reshape(1, D), lb_b.reshape(1, D), norm_w.reshape(1, dk))


def _fnet_proj_kernel(xn_ref, w_ref, cs_ref, o_ref, *, gdim):
    fu = _dot(xn_ref[...], w_ref[...]).astype(BF16)
    for g in range(FNET_GROUPS):
        t = _dot(fu[:, g * gdim:(g + 1) * gdim], cs_ref[...])
        o_ref[0, :, g * gdim:(g + 1) * gdim] = t[:, :gdim].astype(o_ref.dtype)
        o_ref[1, :, g * gdim:(g + 1) * gdim] = t[:, gdim:].astype(o_ref.dtype)


def _seq_dft_kernel(cos_ref, sin_ref, perm_ref, rhs_ref, o_ref, fold_s, mir_s, *, seq):
    n = seq
    half = n // 2
    blk = perm_ref.shape[0]
    c0 = 1.0 / math.sqrt(n)

    for part, sign in ((0, 1.0), (1, -1.0)):
        base = part * n
        for j in range(half // blk):
            own = rhs_ref[pl.ds(base + j * blk, blk), :].astype(F32)
            if j == 0:
                mirror = _dot(perm_ref[:, :blk], rhs_ref[pl.ds(base + n - blk, blk), :])
            else:
                mirror = _dot(perm_ref[...], rhs_ref[pl.ds(base + n - (j + 1) * blk, 2 * blk), :])
            fold_s[pl.ds(part * half + j * blk, blk), :] = (own + sign * mirror).astype(fold_s.dtype)

    mid = rhs_ref[pl.ds(half, BF16_ROWS), :].astype(F32)[0:1] * c0
    p_ext = _dot(cos_ref[...], fold_s[pl.ds(0, half), :])
    q = _dot(sin_ref[...], fold_s[pl.ds(half, half), :])
    row = lax.broadcasted_iota(jnp.int32, q.shape, 0)
    p = p_ext[:half] + jnp.where((row & 1) == 0, 1.0, -1.0) * mid
    o_ref[pl.ds(0, half), :] = (p - q).astype(o_ref.dtype)

    mir_s[pl.ds(0, half), :] = (p + q).astype(mir_s.dtype)
    first = jnp.where(lax.broadcasted_iota(jnp.int32, (BF16_ROWS, q.shape[1]), 0) == 0, 1.0, 0.0)
    tail = (p_ext[half:half + BF16_ROWS] + mid) * first
    mir_s[pl.ds(half, BF16_ROWS), :] = tail.astype(mir_s.dtype)
    mir_s[pl.ds(half + BF16_ROWS, blk - BF16_ROWS), :] = jnp.zeros(
        (blk - BF16_ROWS, tail.shape[1]), mir_s.dtype)
    for j in range(half // blk):
        window = mir_s[pl.ds(half - (j + 1) * blk, 2 * blk), :]
        o_ref[pl.ds(half + j * blk, blk), :] = _dot(perm_ref[...], window).astype(o_ref.dtype)


DFT_FOLD_BLOCK = 128


def _half_dft_tables(n):
    c, s = _dft_tables(n)
    half = n // 2
    cos_ext = np.zeros((half + BF16_ROWS, half))
    cos_ext[:half + 1] = c[:half + 1, :half]
    return cos_ext, s[:half, :half]


def _mirror_permutation(blk):
    p = np.zeros((blk, 2 * blk), np.float32)
    i = np.arange(blk)
    p[i, blk - i] = 1.0
    return p


def _dft_tables(n):
    idx = np.arange(n, dtype=np.int64)
    ang = 2.0 * np.pi * ((idx[:, None] * idx[None, :]) % n).astype(np.float64) / n
    s = 1.0 / math.sqrt(n)
    return np.cos(ang) * s, np.sin(ang) * s


def fourier_branch(xn3, w_in_b, layer, fu_off_blocks, tm=512):
    B, S, D = xn3.shape
    W = D
    gdim = W // FNET_GROUPS
    c_small, s_small = _dft_tables(gdim)
    cs_small = jnp.asarray(np.concatenate([c_small, s_small], axis=1), dtype=BF16)
    half = S // 2
    assert half % 2 == 0 and half % DFT_FOLD_BLOCK == 0
    cos_np, sin_np = _half_dft_tables(S)
    dft_cos = jnp.asarray(cos_np, dtype=BF16)
    dft_sin = jnp.asarray(sin_np, dtype=BF16)
    perm = jnp.asarray(_mirror_permutation(DFT_FOLD_BLOCK), dtype=BF16)
    tiles = S // tm
    rhs = pl.pallas_call(
        functools.partial(_fnet_proj_kernel, gdim=gdim),
        out_shape=jax.ShapeDtypeStruct((B, 2, S, W), BF16),
        grid=(B, tiles),
        in_specs=[pl.BlockSpec((None, tm, D), lambda b, r: (b, r, 0)),
                  pl.BlockSpec((None, D, W), lambda b, r: (layer, 0, fu_off_blocks)),
                  pl.BlockSpec((gdim, 2 * gdim), lambda b, r: (0, 0))],
        out_specs=pl.BlockSpec((None, 2, tm, W), lambda b, r: (b, 0, r, 0)),
        compiler_params=_params("parallel", "parallel"),
        name="fnet_proj",
    )(xn3, w_in_b, cs_small)
    rhs = rhs.reshape(B, 2 * S, W)
    return pl.pallas_call(
        functools.partial(_seq_dft_kernel, seq=S),
        out_shape=jax.ShapeDtypeStruct((B, S, W), BF16),
        grid=(B,),
        in_specs=[_resident((half + BF16_ROWS, half), lambda b: (0, 0)),
                  _resident((half, half), lambda b: (0, 0)),
                  pl.BlockSpec((DFT_FOLD_BLOCK, 2 * DFT_FOLD_BLOCK), lambda b: (0, 0)),
                  pl.BlockSpec((None, 2 * S, W), lambda b: (b, 0, 0))],
        out_specs=pl.BlockSpec((None, S, W), lambda b: (b, 0, 0)),
        scratch_shapes=[pltpu.VMEM((S, W), BF16),
                        pltpu.VMEM((half + DFT_FOLD_BLOCK, W), BF16)],
        compiler_params=_params("parallel"),
        name="fnet_seq_dft",
    )(dft_cos, dft_sin, perm, rhs)


def _merge_kernel(x_ref, xn_ref, ro_ref, ho_ref, fo_ref, wga_ref, wro_ref, who_ref,
                  wf_ref, wout_ref, nw_ref, o_ref, *, d):
    xn = xn_ref[...]

    def gate(i):
        return jax.nn.sigmoid(_dot(xn, wga_ref[:, i * d:(i + 1) * d]))

    mix = gate(0) * _dot(ro_ref[...], wro_ref[...])
    mix += gate(1) * _dot(ho_ref[...], who_ref[...])
    mix += gate(2) * _dot(fo_ref[...], wf_ref[...])
    y = _dot(mix.astype(BF16), wout_ref[...])
    o_ref[...] = x_ref[...] + _rms(y) * nw_ref[...]


def merge_branches(x2, xn2, ro2, ho2, fo2, w_in_b, layer, ga_off_blocks,
                   w_ret_o, w_hgrn_o, w_fnet, w_out, norm_w, tm=512):
    T, D = x2.shape
    RV = ro2.shape[1]
    tile = lambda w: pl.BlockSpec((tm, w), lambda i: (i, 0))
    return pl.pallas_call(
        functools.partial(_merge_kernel, d=D),
        out_shape=jax.ShapeDtypeStruct((T, D), F32),
        grid=(T // tm,),
        in_specs=[tile(D), tile(D), tile(RV), tile(D), tile(D),
                  _resident((None, D, N_BRANCH * D), lambda i: (layer, 0, ga_off_blocks)),
                  _resident((None, RV, D), lambda i: (layer, 0, 0)),
                  _resident((None, D, D), lambda i: (layer, 0, 0)),
                  _resident((None, D, D), lambda i: (layer, 0, 0)),
                  _resident((None, D, D), lambda i: (layer, 0, 0)),
                  pl.BlockSpec((1, D), lambda i: (0, 0))],
        out_specs=tile(D),
        compiler_params=_params("parallel"),
        name="merge",
    )(x2, xn2, ro2, ho2, fo2, w_in_b, w_ret_o, w_hgrn_o, w_fnet, w_out, norm_w.reshape(1, D))


def _ffn_kernel(x_ref, xp_ref, xnx_ref, nw_in_ref, wup_ref, cw_ref, cb_ref, wdn_ref, nw_out_ref,
                *rest, tm, tiles_per_seq, d_ff, fc, emit_next):
    if emit_next:
        nw_next_ref, o_ref, xn_ref, hn_s, acc_s = rest
    else:
        (o_ref, hn_s, acc_s), nw_next_ref, xn_ref = rest, None, None
    i = pl.program_id(0)
    r = i % tiles_per_seq
    halo = BF16_ROWS
    x = x_ref[...]
    nw = nw_in_ref[...]
    hp = jnp.where(r == 0, 0.0, _rms(xp_ref[...]) * nw)
    hx = jnp.where(r == tiles_per_seq - 1, 0.0, _rms(xnx_ref[...]) * nw)
    hn = jnp.concatenate([hp, _rms(x) * nw, hx], axis=0).astype(BF16)
    n_ext = tm + 2 * halo
    hn_s[...] = hn
    acc_s[...] = jnp.zeros_like(acc_s)

    def conv(col, scale):
        cols = pl.ds(pl.multiple_of(col, fc), fc)
        h = _dot(hn_s[...], wup_ref[:, cols])
        cw = cw_ref[:, cols] * scale
        prev = pltpu.roll(h, 1, 0)[halo:halo + tm]
        nxt = pltpu.roll(h, n_ext - 1, 0)[halo:halo + tm]
        return (cb_ref[:, cols] * scale + prev * cw[0:1] + h[halo:halo + tm] * cw[1:2]
                + nxt * cw[2:3])

    def chunk(c, carry):
        gate = conv(c * fc, 1.0)
        half_up = conv(d_ff + c * fc, 0.5)
        inner = gate * (GELU_C0 + GELU_C1 * (gate * gate))
        act = (gate * (1.0 + jnp.tanh(inner)) * half_up).astype(BF16)
        acc_s[...] += _dot(act, wdn_ref[pl.ds(pl.multiple_of(c * fc, fc), fc), :])
        return carry

    for c in range(d_ff // fc):
        chunk(c, 0)
    y = x + _rms(acc_s[...]) * nw_out_ref[...]
    o_ref[...] = y
    if xn_ref is not None:
        xn_ref[...] = (_rms(y) * nw_next_ref[...]).astype(xn_ref.dtype)


def conv_ffn_block(x2, seq, w_up, conv_w, conv_b, w_down, nw_in, nw_out, layer, nw_next=None,
                   tm=1024, fc=256):
    T, D = x2.shape
    d_ff = w_down.shape[1]
    halo = BF16_ROWS
    tps = seq // tm
    hb = tm // halo
    n_hb = T // halo
    assert seq % tm == 0, "a row tile must not straddle two sequences (the conv zero-pads each)"
    emit_next = nw_next is not None
    kern = functools.partial(_ffn_kernel, tm=tm, tiles_per_seq=tps, d_ff=d_ff, fc=fc,
                             emit_next=emit_next)
    vec = pl.BlockSpec((1, D), lambda i: (0, 0))
    tile = pl.BlockSpec((tm, D), lambda i: (i, 0))
    in_specs = [tile,
                pl.BlockSpec((halo, D), lambda i: (jnp.maximum(i * hb - 1, 0), 0)),
                pl.BlockSpec((halo, D), lambda i: (jnp.minimum((i + 1) * hb, n_hb - 1), 0)),
                vec,
                _resident((None, D, 2 * d_ff), lambda i: (layer, 0, 0)),
                pl.BlockSpec((None, CONV_W, 2 * d_ff), lambda i: (layer, 0, 0)),
                pl.BlockSpec((None, 1, 2 * d_ff), lambda i: (layer, 0, 0)),
                _resident((None, d_ff, D), lambda i: (layer, 0, 0)),
                vec]
    args = [x2, x2, x2, nw_in.reshape(1, D), w_up, conv_w, conv_b, w_down, nw_out.reshape(1, D)]
    out_shape = jax.ShapeDtypeStruct((T, D), F32)
    out_specs = tile
    if emit_next:
        in_specs.append(vec)
        args.append(nw_next.reshape(1, D))
        out_shape = (out_shape, jax.ShapeDtypeStruct((T, D), BF16))
        out_specs = (tile, tile)
    return pl.pallas_call(
        kern,
        out_shape=out_shape,
        grid=(T // tm,),
        in_specs=in_specs,
        out_specs=out_specs,
        scratch_shapes=[pltpu.VMEM((tm + 2 * halo, D), BF16),
                        pltpu.VMEM((tm, D), F32)],
        compiler_params=_params("parallel"),
        name="conv_ffn",
    )(*args)


def kernel(x, positions, norm_w, w_in, hgrn_lb_logits, hgrn_norm_w, w_ret_o, w_hgrn_o,
           w_fnet, w_out, w_up, conv_w, conv_b, w_down):
    B, S, D = x.shape
    depth = w_in.shape[0]
    T = B * S

    hgrn_off = 2 * D + 2 * 2 * D
    fu_off = hgrn_off + 5 * D
    ga_off = fu_off + D

    w_in_b = w_in.astype(BF16)
    w_ret_o_b = w_ret_o.astype(BF16)
    w_hgrn_o_b = w_hgrn_o.astype(BF16)
    w_fnet_b = w_fnet.astype(BF16)
    w_out_b = w_out.astype(BF16)
    w_up_b = w_up.astype(BF16)
    w_down_b = w_down.astype(BF16)
    conv_b3 = conv_b.reshape(depth, 1, -1)

    log_gamma = jnp.log(1.0 - 2.0 ** (-5.0 - jnp.arange(RET_HEADS, dtype=F32)))
    p = jax.nn.softmax(hgrn_lb_logits.astype(F32), axis=1)
    lower_bounds = jnp.cumsum(p, axis=1) - p[:, :1]

    cos, sin, xn3 = rope_tables_and_norm(positions, D // RET_HEADS // 2, x, norm_w[0, 0])

    x2 = x.reshape(T, D)
    xn2 = xn3.reshape(T, D)
    for l in range(depth):
        xn3 = xn2.reshape(B, S, D)
        ro = retention_branch(xn3, w_in_b, l, cos, sin, log_gamma)
        ho = hgrn_branch(xn3, w_in_b, l, hgrn_off, lower_bounds[0, l], lower_bounds[1, l],
                         hgrn_norm_w[l])
        fo = fourier_branch(xn3, w_in_b, l, fu_off // D)
        x2 = merge_branches(x2, xn2, ro.reshape(T, -1), ho.reshape(T, D), fo.reshape(T, D),
                            w_in_b, l, ga_off // (N_BRANCH * D),
                            w_ret_o_b, w_hgrn_o_b, w_fnet_b, w_out_b, norm_w[l, 1])
        if l + 1 < depth:
            x2, xn2 = conv_ffn_block(x2, S, w_up_b, conv_w, conv_b3, w_down_b,
                                     norm_w[l, 2], norm_w[l, 3], l, nw_next=norm_w[l + 1, 0])
        else:
            x2 = conv_ffn_block(x2, S, w_up_b, conv_w, conv_b3, w_down_b,
                                norm_w[l, 2], norm_w[l, 3], l)
    return x2.reshape(B, S, D)
```

```python
import functools
import math

import numpy as np
import jax
import jax.numpy as jnp
from jax import lax
from jax.experimental import pallas as pl
from jax.experimental.pallas import tpu as pltpu

F32 = jnp.float32
BF16 = jnp.bfloat16

RET_HEADS = 4
HGRN_HEADS = 8
FNET_GROUPS = 4
N_BRANCH = 3
CONV_W = 3
ROPE_BASE = 10000.0
LB_FLOOR = 1e-30
EPS = 1e-6
LOG2_E = 1.4426950408889634
GELU_C0 = math.sqrt(2.0 / math.pi)
GELU_C1 = GELU_C0 * 0.044715

V7X_VMEM_LIMIT_BYTES = 56 * 1024 * 1024
SUBLANES = 8
BF16_ROWS = 16

RET_CHUNK = 256
HGRN_CHUNK = 128
HGRN_HEADS_PER_STEP = 2
HGRN_VPU_LEVELS = (0,)
ROW_TILE = 512
HGRN_ROW_TILE = 512


def _dot(a, b):
    return jnp.dot(a, b, preferred_element_type=F32)


def _dot_nt(a, b):
    return lax.dot_general(a, b, (((1,), (1,)), ((), ())), preferred_element_type=F32)


def _dot_tn(a, b):
    return lax.dot_general(a, b, (((0,), (0,)), ((), ())), preferred_element_type=F32)


def _silu(x, scale=1.0):
    return (x * scale if scale != 1.0 else x) / (1.0 + jnp.exp2(x * (-LOG2_E)))


def _rms(x):
    return x * lax.rsqrt(jnp.mean(x * x, axis=-1, keepdims=True) + EPS)


def _params(*sem):
    return pltpu.CompilerParams(dimension_semantics=sem,
                                vmem_limit_bytes=V7X_VMEM_LIMIT_BYTES)


def _resident(shape, index_map):
    return pl.BlockSpec(shape, index_map, pipeline_mode=pl.Buffered(1))


def _rope_norm_kernel(pos_ref, invf_ref, x_ref, w_ref, cos_ref, sin_ref, xn_ref):
    ang = pos_ref[...] * invf_ref[...]
    cos_ref[...] = jnp.cos(ang)
    sin_ref[...] = jnp.sin(ang)
    xn_ref[...] = (_rms(x_ref[...]) * w_ref[...]).astype(xn_ref.dtype)


def rope_tables_and_norm(positions, half, x, w):
    B, S = positions.shape
    D = x.shape[-1]
    pos = positions.astype(F32).reshape(B, S, 1)
    inv_freq = (ROPE_BASE ** (-jnp.arange(half, dtype=F32) / half)).reshape(1, half)
    table = jax.ShapeDtypeStruct((B, S, half), F32)
    table_spec = pl.BlockSpec((None, S, half), lambda b: (b, 0, 0))
    rows_spec = pl.BlockSpec((None, S, D), lambda b: (b, 0, 0))
    return pl.pallas_call(
        _rope_norm_kernel,
        out_shape=(table, table, jax.ShapeDtypeStruct((B, S, D), BF16)),
        grid=(B,),
        in_specs=[pl.BlockSpec((None, S, 1), lambda b: (b, 0, 0)),
                  pl.BlockSpec((1, half), lambda b: (0, 0)),
                  rows_spec,
                  pl.BlockSpec((1, D), lambda b: (0, 0))],
        out_specs=(table_spec, table_spec, rows_spec),
        compiler_params=_params("parallel"),
        name="rope_tables_norm",
    )(pos, inv_freq, x, w.reshape(1, D))


def _ret_kernel(lg_ref, xn_ref, wq_ref, wk_ref, wv_ref, wg_ref, cos_ref, sin_ref,
                o_ref, qi_s, qd_s, ki_s, v_s, g_s, st_s, kvb_s, run_s, *, seq, dk, dv):
    C = RET_CHUNK
    R = seq // C
    half = dk // 2
    lg = lg_ref[pl.program_id(1)]
    ret_scale = dk ** -0.5

    def rows_of(n):
        return pl.ds(pl.multiple_of(n * C, C), C)

    pos = lax.broadcasted_iota(jnp.int32, (C, 1), 0).astype(F32)
    qdec_f = jnp.exp(lg * (pos + 1.0))
    qdec_b = jnp.exp(lg * (C - pos))
    kdec_f = jnp.exp(lg * (C - 1.0 - pos))
    kdec_b = jnp.exp(lg * pos)
    chunk_dec = jnp.exp(lg * C)
    ii = lax.broadcasted_iota(jnp.int32, (C, C), 0)
    jj = lax.broadcasted_iota(jnp.int32, (C, C), 1)
    decay = jnp.exp(lg * jnp.abs(ii - jj).astype(F32))

    run_s[...] = jnp.zeros_like(run_s)

    def proj(t, carry):
        rows = pl.ds(pl.multiple_of(t * ROW_TILE, ROW_TILE), ROW_TILE)
        xc = xn_ref[rows, :]
        cos = cos_ref[rows, :]
        sin = sin_ref[rows, :]
        q = _dot(xc, wq_ref[...])
        q1, q2 = q[:, :half], q[:, half:]
        q = jnp.concatenate([q1 * cos - q2 * sin, q1 * sin + q2 * cos], axis=-1)
        k = _dot(xc, wk_ref[...]) * ret_scale
        k1, k2 = k[:, :half], k[:, half:]
        k = jnp.concatenate([k1 * cos - k2 * sin, k1 * sin + k2 * cos], axis=-1)
        v = _dot(xc, wv_ref[...]).astype(BF16)
        g = _dot(xc, wg_ref[...])
        qi_s[rows, :] = q.astype(BF16)
        ki_s[rows, :] = k.astype(BF16)
        v_s[rows, :] = v
        g_s[rows, :] = _silu(g).astype(BF16)
        for j in range(ROW_TILE // C):
            n = t * (ROW_TILE // C) + j
            sl = slice(j * C, (j + 1) * C)
            qd_s[rows_of(n), :] = jnp.concatenate([q[sl] * qdec_f, q[sl] * qdec_b],
                                                  axis=-1).astype(BF16)
            st_s[n, pl.ds(0, dk), :] = run_s[...].astype(BF16)
            run_s[...] = run_s[...] * chunk_dec + _dot_tn((k[sl] * kdec_f).astype(BF16), v[sl])
            kvb_s[n] = _dot_tn((k[sl] * kdec_b).astype(BF16), v[sl])
        return carry

    lax.fori_loop(0, seq // ROW_TILE, proj, 0, unroll=2)

    run_s[...] = jnp.zeros_like(run_s)

    def out(t, carry):
        n = R - 1 - t
        rows = rows_of(n)
        st_s[n, pl.ds(dk, dk), :] = run_s[...].astype(BF16)
        s = _dot_nt(qi_s[rows, :], ki_s[rows, :]) * decay
        o = _dot(s.astype(BF16), v_s[rows, :]) + _dot(qd_s[rows, :], st_s[n])
        o_ref[rows, :] = (_rms(o) * g_s[rows, :].astype(F32)).astype(o_ref.dtype)
        run_s[...] = run_s[...] * chunk_dec + kvb_s[n]
        return carry

    lax.fori_loop(0, R, out, 0, unroll=4)


def retention_branch(xn3, w_in_b, layer, cos, sin, log_gamma):
    B, S, D = xn3.shape
    dk = D // RET_HEADS
    dv = 2 * dk
    H = RET_HEADS
    qk_blocks = D // dk
    v_off = 2 * D // dv
    g_off = v_off + H
    kern = functools.partial(_ret_kernel, seq=S, dk=dk, dv=dv)
    return pl.pallas_call(
        kern,
        out_shape=jax.ShapeDtypeStruct((B, S, H * dv), BF16),
        grid=(B, H),
        in_specs=[
            pl.BlockSpec(memory_space=pltpu.SMEM),
            pl.BlockSpec((None, S, D), lambda b, h: (b, 0, 0)),
            pl.BlockSpec((None, D, dk), lambda b, h: (layer, 0, h)),
            pl.BlockSpec((None, D, dk), lambda b, h: (layer, 0, qk_blocks + h)),
            pl.BlockSpec((None, D, dv), lambda b, h: (layer, 0, v_off + h)),
            pl.BlockSpec((None, D, dv), lambda b, h: (layer, 0, g_off + h)),
            pl.BlockSpec((None, S, dk // 2), lambda b, h: (b, 0, 0)),
            pl.BlockSpec((None, S, dk // 2), lambda b, h: (b, 0, 0)),
        ],
        out_specs=pl.BlockSpec((None, S, dv), lambda b, h: (b, 0, h)),
        scratch_shapes=[
            pltpu.VMEM((S, dk), BF16),
            pltpu.VMEM((S, 2 * dk), BF16),
            pltpu.VMEM((S, dk), BF16),
            pltpu.VMEM((S, dv), BF16),
            pltpu.VMEM((S, dv), BF16),
            pltpu.VMEM((S // RET_CHUNK, 2 * dk, dv), BF16),
            pltpu.VMEM((S // RET_CHUNK, dk, dv), F32),
            pltpu.VMEM((dk, dv), F32),
        ],
        compiler_params=_params("parallel", "arbitrary"),
        name="retention",
    )(log_gamma, xn3, w_in_b, w_in_b, w_in_b, w_in_b, cos, sin)


def _hgrn_gate(z, lb):
    e = jnp.exp2(jnp.abs(z) * (-LOG2_E))
    pos = z >= 0.0
    sig_neg_num = jnp.where(pos, e, 1.0)
    num = jnp.where(pos, 1.0, e) + jnp.maximum(lb, LB_FLOOR) * sig_neg_num
    inv = 1.0 / (1.0 + e)
    log2_f = jnp.log2(num * inv)
    return log2_f, (1.0 - lb) * sig_neg_num * inv


def _boundary_rows(cum_ref, base, m, reverse, row_in_group):
    C = HGRN_CHUNK
    d = cum_ref.shape[1]
    blk = 2 * m
    off = m if reverse else m - 1
    pieces = []
    if blk >= SUBLANES:
        for b in range(C // blk):
            pieces.append(jnp.broadcast_to(cum_ref[pl.ds(base + (b * blk + off), 1), :], (blk, d)))
    else:
        for g in range(C // SUBLANES):
            val = None
            for u in range(SUBLANES // blk):
                row = g * SUBLANES + u * blk + off
                piece = jnp.broadcast_to(cum_ref[pl.ds(base + row, 1), :], (SUBLANES, d))
                val = piece if val is None else jnp.where(row_in_group >= u * blk, piece, val)
            pieces.append(val)
    return jnp.concatenate(pieces, axis=0) if len(pieces) > 1 else pieces[0]


def _level_operands(level, q, k, cum, cum_ref, base, consts, reverse):
    C = HGRN_CHUNK
    _, row_in_group, signs, _ = consts
    m = 2 ** level
    if 2 * m <= SUBLANES:
        sign = signs[level]
        ref_pt = _boundary_rows(cum_ref, base, m, reverse, row_in_group)
        x = (jnp.where(sign > 0.0, q, k) * jnp.exp2((cum - ref_pt) * sign)).astype(BF16)
        return x, x, list(range(C // SUBLANES))
    xq, xall, q_groups = [], [], []
    for b in range(C // (2 * m)):
        first = slice(b * 2 * m, b * 2 * m + m)
        second = slice(b * 2 * m + m, (b + 1) * 2 * m)
        q_rows, k_rows = (first, second) if reverse else (second, first)
        edge = k_rows.start if reverse else k_rows.stop - 1
        ref_pt = cum_ref[pl.ds(base + edge, 1), :]
        xq_b = q[q_rows] * jnp.exp2(cum[q_rows] - ref_pt)
        xk_b = k[k_rows] * jnp.exp2(ref_pt - cum[k_rows])
        xq.append(xq_b)
        xall.extend([xq_b, xk_b] if reverse else [xk_b, xq_b])
        q_groups.extend(range(q_rows.start // SUBLANES, q_rows.stop // SUBLANES))
    return (jnp.concatenate(xq, axis=0).astype(BF16), jnp.concatenate(xall, axis=0).astype(BF16),
            q_groups)


def _paired_dot_nt(lhs_a, rhs_a, lhs_b, rhs_b):
    rhs = jnp.concatenate([rhs_a, rhs_b], axis=1)
    lhs = jnp.concatenate(
        [jnp.concatenate([lhs_a, jnp.zeros_like(lhs_a)], axis=1),
         jnp.concatenate([jnp.zeros_like(lhs_b), lhs_b], axis=1)], axis=0)
    s = _dot_nt(lhs, rhs)
    return s[:lhs_a.shape[0]], s[lhs_a.shape[0]:]


def _assemble_scores(scores, level_id):
    C = HGRN_CHUNK
    rows = [jnp.zeros((SUBLANES, C), F32) for _ in range(C // SUBLANES)]
    for level, entry in enumerate(scores):
        if entry is None:
            continue
        s, q_groups = entry
        for i, g in enumerate(q_groups):
            lid = level_id[g * SUBLANES:(g + 1) * SUBLANES]
            rows[g] = jnp.where(lid == level, s[i * SUBLANES:(i + 1) * SUBLANES], rows[g])
    return jnp.concatenate(rows, axis=0)


def _hgrn_intra_pair(fwd, bwd, consts):
    C = HGRN_CHUNK
    args = ((fwd, consts[0], False), (bwd, consts[1], True))
    n_vpu = len(HGRN_VPU_LEVELS)
    scores = ([None] * n_vpu, [None] * n_vpu)
    for level in range(n_vpu, C.bit_length() - 1):
        ops = [_level_operands(level, q, k, cum, cum_ref, base, cst, rev)
               for (q, k, _, cum, cum_ref, base), cst, rev in args]
        s_f, s_b = _paired_dot_nt(ops[0][0], ops[0][1], ops[1][0], ops[1][1])
        scores[0].append((s_f, ops[0][2]))
        scores[1].append((s_b, ops[1][2]))
    outs = []
    for idx, ((q, k, v, cum, _, _), cst, rev) in enumerate(args):
        attn = _assemble_scores(scores[idx], cst[0])
        o = _dot(attn.astype(BF16), v.astype(BF16))
        o += jnp.sum(q * k, axis=-1, keepdims=True) * v
        outs.append(o + _near_pairs(q, k, v, cum, rev, cst[3]))
    return outs


def _near_pairs(q, k, v, cum, reverse, masks):
    C, d = q.shape
    shape3 = (C // SUBLANES, SUBLANES, d)
    out = jnp.zeros((C, d), F32)
    for offset, mask in enumerate(masks, start=1):
        valid = mask != 0
        shift = (SUBLANES - offset) if reverse else offset

        def key_row(x):
            return pltpu.roll(x.reshape(shape3), shift, 1).reshape(C, d)

        w = jnp.exp2(jnp.where(valid, cum - key_row(cum), 0.0))
        score = jnp.sum(q * key_row(k) * w, axis=-1, keepdims=True)
        out += jnp.where(valid, score * key_row(v), 0.0)
    return out


def _near_pair_masks(C, d, reverse, levels):
    pos = lax.broadcasted_iota(jnp.int32, (C, d), 0) % SUBLANES
    masks = []
    for offset in range(1, 2 ** (max(levels) + 1)):
        valid = jnp.zeros((C, d), jnp.int32)
        for level in levels:
            m, blk = 2 ** level, 2 ** (level + 1)
            r = pos % blk
            key = (r + offset) if reverse else (r - offset)
            if reverse:
                ok = (r < m) & (key >= m) & (key < blk)
            else:
                ok = (r >= m) & (key >= 0) & (key < m)
            valid = jnp.where(ok, 1, valid)
        masks.append(valid)
    return masks


def _hgrn_consts(reverse, d):
    C = HGRN_CHUNK
    ii = lax.broadcasted_iota(jnp.int32, (C, C), 0)
    jj = lax.broadcasted_iota(jnp.int32, (C, C), 1)
    diff = ii ^ jj
    level_id = jnp.full((C, C), -1, jnp.int32)
    n_levels = C.bit_length() - 1
    for level in range(n_levels):
        level_id = jnp.where((diff >> level) == 1, level, level_id)
    level_id = jnp.where((ii < jj) if reverse else (ii > jj), level_id, -1)
    rows = lax.broadcasted_iota(jnp.int32, (C, d), 0)
    row_in_group = lax.broadcasted_iota(jnp.int32, (SUBLANES, d), 0)
    signs = []
    for level in range(SUBLANES.bit_length() - 1):
        second = ((rows >> level) & 1) == 1
        is_query = jnp.logical_not(second) if reverse else second
        signs.append(jnp.where(is_query, 1.0, -1.0))
    return level_id, row_in_group, signs, _near_pair_masks(C, d, reverse, HGRN_VPU_LEVELS)


def _chunk_cumsum(x, reverse, row_in_group):
    rows, d = x.shape
    groups = rows // SUBLANES
    per_chunk = HGRN_CHUNK // SUBLANES
    y = x.reshape(groups, SUBLANES, d)
    step = 1
    while step < SUBLANES:
        rolled = pltpu.roll(y, (SUBLANES - step) if reverse else step, 1)
        valid = (row_in_group < SUBLANES - step) if reverse else (row_in_group >= step)
        y = y + jnp.where(valid, rolled, 0.0)
        step *= 2
    out = [None] * groups
    for c in range(rows // HGRN_CHUNK):
        order = range(c * per_chunk, (c + 1) * per_chunk)
        carry = None
        for g in (reversed(order) if reverse else order):
            yg = y[g] if carry is None else y[g] + carry
            out[g] = yg
            edge = 0 if reverse else SUBLANES - 1
            carry = jnp.broadcast_to(yg[edge:edge + 1, :], (SUBLANES, d))
    return jnp.concatenate(out, axis=0)


def _hgrn_kernel(xn_ref, wq_ref, wzf_ref, wzb_ref, wi_ref, wg_ref, lbf_ref, lbb_ref, nw_ref, o_ref,
                 q_s, v_s, g_s, acc_s, kf_s, kb_s, cumf_s, cumb_s,
                 qef_s, qeb_s, ktf_s, ktb_s, decf_s, decb_s, *, seq, dk, heads):
    C = HGRN_CHUNK
    R = seq // C
    row_tile = HGRN_ROW_TILE
    per_tile = row_tile // C
    scale = dk ** -0.5

    def tile_rows(n):
        return pl.ds(pl.multiple_of(n * row_tile, row_tile), row_tile)

    def head_cols(h):
        return slice(h * dk, (h + 1) * dk)

    dirs = [((False, lbf_ref, kf_s.at[h], cumf_s.at[h], qef_s.at[h], ktf_s.at[h], decf_s.at[h]),
             (True, lbb_ref, kb_s.at[h], cumb_s.at[h], qeb_s.at[h], ktb_s.at[h], decb_s.at[h]))
            for h in range(heads)]
    row_in_group = lax.broadcasted_iota(jnp.int32, (1, SUBLANES, dk), 1)

    def proj(n, carry):
        rows = tile_rows(n)
        xc = xn_ref[rows, :]
        hq_all = _dot(xc, wq_ref[...])
        z_all = (_dot(xc, wzf_ref[...]), _dot(xc, wzb_ref[...]))
        v_all = _dot(xc, wi_ref[...])
        hg_all = _dot(xc, wg_ref[...])
        for h in range(heads):
            cols = head_cols(h)
            hq = hq_all[:, cols]
            q = _silu(hq, scale)
            q_s[h, rows, :] = q
            for idx, (reverse, lb_ref, k_s, cum_s, qe_s, kt_s, dec_s) in enumerate(dirs[h]):
                lf, kk = _hgrn_gate(z_all[idx][:, cols], lb_ref[:, cols])
                k_s[rows, :] = kk
                cum = _chunk_cumsum(lf, reverse, row_in_group)
                cum_s[rows, :] = cum
                for j in range(per_tile):
                    sl = slice(j * C, (j + 1) * C)
                    edge = j * C if reverse else (j + 1) * C - 1
                    total = cum[edge:edge + 1, :]
                    r0 = pl.multiple_of(n * row_tile + j * C, C)
                    qe_s[pl.ds(r0, C), :] = (q[sl] * jnp.exp2(cum[sl])).astype(BF16)
                    kt_s[pl.ds(r0, C), :] = (kk[sl] * jnp.exp2(total - cum[sl])).astype(BF16)
                    dec_s[n * per_tile + j] = jnp.broadcast_to(jnp.exp2(total), (SUBLANES, dk))
            v_s[h, rows, :] = v_all[:, cols]
            hg = hg_all[:, cols]
            g_s[h, rows, :] = _silu(hg)
        return carry

    lax.fori_loop(0, seq // row_tile, proj, 0, unroll=2)

    consts = (_hgrn_consts(False, dk), _hgrn_consts(True, dk))

    def chunk_pair(h, cf, cb, states, second_visit):
        data, rows = [], []
        for c, (_, _, k_s, cum_s, _, _, _) in zip((cf, cb), dirs[h]):
            base = pl.multiple_of(c * C, C)
            r = pl.ds(base, C)
            data.append((q_s[h, r, :], k_s[r, :], v_s[h, r, :], cum_s[r, :], cum_s, base))
            rows.append(r)
        o_f, o_b = _hgrn_intra_pair(data[0], data[1], consts)
        i_f, i_b = _paired_dot_nt(dirs[h][0][4][rows[0], :], states[0].astype(BF16),
                                  dirs[h][1][4][rows[1], :], states[1].astype(BF16))
        for r, o in ((rows[0], o_f + i_f), (rows[1], o_b + i_b)):
            if second_visit:
                ho = acc_s[h, r, :] + o
                o_ref[r, head_cols(h)] = (_rms(ho) * nw_ref[...] * g_s[h, r, :]).astype(o_ref.dtype)
            else:
                acc_s[h, r, :] = o
        new_states = []
        for c, r, st, (_, _, v, _, _, _), (_, _, _, _, _, kt_s, dec_s) in zip(
                (cf, cb), rows, states, data, dirs[h]):
            dec = jnp.tile(dec_s[c], (dk // SUBLANES, 1))
            new_states.append(st * dec + _dot_tn(v.astype(BF16), kt_s[r, :]))
        return tuple(new_states)

    def step(second_visit, i, states):
        return tuple(chunk_pair(h, i, R - 1 - i, states[h], second_visit) for h in range(heads))

    zero = jnp.zeros((dk, dk), F32)
    states = lax.fori_loop(0, R // 2, functools.partial(step, False),
                           tuple((zero, zero) for _ in range(heads)))
    lax.fori_loop(R // 2, R, functools.partial(step, True), states)


def hgrn_branch(xn3, w_in_b, layer, hgrn_off, lb_f, lb_b, norm_w):
    B, S, D = xn3.shape
    dk = D // HGRN_HEADS
    hp = HGRN_HEADS_PER_STEP
    assert (S // HGRN_CHUNK) % 2 == 0, "the two scans must cross between two chunks"
    wide = hp * dk
    kern = functools.partial(_hgrn_kernel, seq=S, dk=dk, heads=hp)
    vec = lambda: pltpu.VMEM((hp, S, dk), F32)
    half = lambda: pltpu.VMEM((hp, S, dk), BF16)
    dec = lambda: pltpu.VMEM((hp, S // HGRN_CHUNK, SUBLANES, dk), F32)

    def w_spec(group):
        first = (hgrn_off + group * D) // wide
        return pl.BlockSpec((None, D, wide), lambda b, j: (layer, 0, first + j))

    lb_spec = pl.BlockSpec((1, wide), lambda b, j: (0, j))
    return pl.pallas_call(
        kern,
        out_shape=jax.ShapeDtypeStruct((B, S, D), BF16),
        grid=(B, HGRN_HEADS // hp),
        in_specs=[pl.BlockSpec((None, S, D), lambda b, j: (b, 0, 0)),
                  w_spec(0), w_spec(1), w_spec(2), w_spec(3), w_spec(4),
                  lb_spec, lb_spec,
                  pl.BlockSpec((1, dk), lambda b, j: (0, 0))],
        out_specs=pl.BlockSpec((None, S, wide), lambda b, j: (b, 0, j)),
        scratch_shapes=[vec(), vec(), vec(), vec(), vec(), vec(), vec(), vec(),
                        half(), half(), half(), half(), dec(), dec()],
        compiler_params=_params("parallel", "arbitrary"),
        name="hgrn2",
    )(xn3, w_in_b, w_in_b, w_in_b, w_in_b, w_in_b,
      lb_f.reshape(1, D), lb_b.reshape(1, D), norm_w.reshape(1, dk))


def _fnet_proj_kernel(xn_ref, w_ref, cs_ref, o_ref, *, gdim):
    fu = _dot(xn_ref[...], w_ref[...]).astype(BF16)
    for g in range(FNET_GROUPS):
        t = _dot(fu[:, g * gdim:(g + 1) * gdim], cs_ref[...])
        o_ref[0, :, g * gdim:(g + 1) * gdim] = t[:, :gdim].astype(o_ref.dtype)
        o_ref[1, :, g * gdim:(g + 1) * gdim] = t[:, gdim:].astype(o_ref.dtype)


def _seq_dft_kernel(cos_ref, sin_ref, perm_ref, rhs_ref, o_ref, fold_s, mir_s, *, seq):
    n = seq
    half = n // 2
    blk = perm_ref.shape[0]
    c0 = 1.0 / math.sqrt(n)

    for part, sign in ((0, 1.0), (1, -1.0)):
        base = part * n
        for j in range(half // blk):
            own = rhs_ref[pl.ds(base + j * blk, blk), :].astype(F32)
            if j == 0:
                mirror = _dot(perm_ref[:, :blk], rhs_ref[pl.ds(base + n - blk, blk), :])
            else:
                mirror = _dot(perm_ref[...], rhs_ref[pl.ds(base + n - (j + 1) * blk, 2 * blk), :])
            fold_s[pl.ds(part * half + j * blk, blk), :] = (own + sign * mirror).astype(fold_s.dtype)

    mid = rhs_ref[pl.ds(half, BF16_ROWS), :].astype(F32)[0:1] * c0
    p_ext = _dot(cos_ref[...], fold_s[pl.ds(0, half), :])
    q = _dot(sin_ref[...], fold_s[pl.ds(half, half), :])
    row = lax.broadcasted_iota(jnp.int32, q.shape, 0)
    p = p_ext[:half] + jnp.where((row & 1) == 0, 1.0, -1.0) * mid
    o_ref[pl.ds(0, half), :] = (p - q).astype(o_ref.dtype)

    mir_s[pl.ds(0, half), :] = (p + q).astype(mir_s.dtype)
    first = jnp.where(lax.broadcasted_iota(jnp.int32, (BF16_ROWS, q.shape[1]), 0) == 0, 1.0, 0.0)
    tail = (p_ext[half:half + BF16_ROWS] + mid) * first
    mir_s[pl.ds(half, BF16_ROWS), :] = tail.astype(mir_s.dtype)
    mir_s[pl.ds(half + BF16_ROWS, blk - BF16_ROWS), :] = jnp.zeros(
        (blk - BF16_ROWS, tail.shape[1]), mir_s.dtype)
    for j in range(half // blk):
        window = mir_s[pl.ds(half - (j + 1) * blk, 2 * blk), :]
        o_ref[pl.ds(half + j * blk, blk), :] = _dot(perm_ref[...], window).astype(o_ref.dtype)


DFT_FOLD_BLOCK = 128


def _half_dft_tables(n):
    c, s = _dft_tables(n)
    half = n // 2
    cos_ext = np.zeros((half + BF16_ROWS, half))
    cos_ext[:half + 1] = c[:half + 1, :half]
    return cos_ext, s[:half, :half]


def _mirror_permutation(blk):
    p = np.zeros((blk, 2 * blk), np.float32)
    i = np.arange(blk)
    p[i, blk - i] = 1.0
    return p


def _dft_tables(n):
    idx = np.arange(n, dtype=np.int64)
    ang = 2.0 * np.pi * ((idx[:, None] * idx[None, :]) % n).astype(np.float64) / n
    s = 1.0 / math.sqrt(n)
    return np.cos(ang) * s, np.sin(ang) * s


def fourier_branch(xn3, w_in_b, layer, fu_off_blocks, tm=512):
    B, S, D = xn3.shape
    W = D
    gdim = W // FNET_GROUPS
    c_small, s_small = _dft_tables(gdim)
    cs_small = jnp.asarray(np.concatenate([c_small, s_small], axis=1), dtype=BF16)
    half = S // 2
    assert half % 2 == 0 and half % DFT_FOLD_BLOCK == 0
    cos_np, sin_np = _half_dft_tables(S)
    dft_cos = jnp.asarray(cos_np, dtype=BF16)
    dft_sin = jnp.asarray(sin_np, dtype=BF16)
    perm = jnp.asarray(_mirror_permutation(DFT_FOLD_BLOCK), dtype=BF16)
    tiles = S // tm
    rhs = pl.pallas_call(
        functools.partial(_fnet_proj_kernel, gdim=gdim),
        out_shape=jax.ShapeDtypeStruct((B, 2, S, W), BF16),
        grid=(B, tiles),
        in_specs=[pl.BlockSpec((None, tm, D), lambda b, r: (b, r, 0)),
                  pl.BlockSpec((None, D, W), lambda b, r: (layer, 0, fu_off_blocks)),
                  pl.BlockSpec((gdim, 2 * gdim), lambda b, r: (0, 0))],
        out_specs=pl.BlockSpec((None, 2, tm, W), lambda b, r: (b, 0, r, 0)),
        compiler_params=_params("parallel", "parallel"),
        name="fnet_proj",
    )(xn3, w_in_b, cs_small)
    rhs = rhs.reshape(B, 2 * S, W)
    return pl.pallas_call(
        functools.partial(_seq_dft_kernel, seq=S),
        out_shape=jax.ShapeDtypeStruct((B, S, W), BF16),
        grid=(B,),
        in_specs=[_resident((half + BF16_ROWS, half), lambda b: (0, 0)),
                  _resident((half, half), lambda b: (0, 0)),
                  pl.BlockSpec((DFT_FOLD_BLOCK, 2 * DFT_FOLD_BLOCK), lambda b: (0, 0)),
                  pl.BlockSpec((None, 2 * S, W), lambda b: (b, 0, 0))],
        out_specs=pl.BlockSpec((None, S, W), lambda b: (b, 0, 0)),
        scratch_shapes=[pltpu.VMEM((S, W), BF16),
                        pltpu.VMEM((half + DFT_FOLD_BLOCK, W), BF16)],
        compiler_params=_params("parallel"),
        name="fnet_seq_dft",
    )(dft_cos, dft_sin, perm, rhs)


def _merge_kernel(x_ref, xn_ref, ro_ref, ho_ref, fo_ref, wga_ref, wro_ref, who_ref,
                  wf_ref, wout_ref, nw_ref, o_ref, *, d):
    xn = xn_ref[...]

    def gate(i):
        return jax.nn.sigmoid(_dot(xn, wga_ref[:, i * d:(i + 1) * d]))

    mix = gate(0) * _dot(ro_ref[...], wro_ref[...])
    mix += gate(1) * _dot(ho_ref[...], who_ref[...])
    mix += gate(2) * _dot(fo_ref[...], wf_ref[...])
    y = _dot(mix.astype(BF16), wout_ref[...])
    o_ref[...] = x_ref[...] + _rms(y) * nw_ref[...]


def merge_branches(x2, xn2, ro2, ho2, fo2, w_in_b, layer, ga_off_blocks,
                   w_ret_o, w_hgrn_o, w_fnet, w_out, norm_w, tm=512):
    T, D = x2.shape
    RV = ro2.shape[1]
    tile = lambda w: pl.BlockSpec((tm, w), lambda i: (i, 0))
    return pl.pallas_call(
        functools.partial(_merge_kernel, d=D),
        out_shape=jax.ShapeDtypeStruct((T, D), F32),
        grid=(T // tm,),
        in_specs=[tile(D), tile(D), tile(RV), tile(D), tile(D),
                  _resident((None, D, N_BRANCH * D), lambda i: (layer, 0, ga_off_blocks)),
                  _resident((None, RV, D), lambda i: (layer, 0, 0)),
                  _resident((None, D, D), lambda i: (layer, 0, 0)),
                  _resident((None, D, D), lambda i: (layer, 0, 0)),
                  _resident((None, D, D), lambda i: (layer, 0, 0)),
                  pl.BlockSpec((1, D), lambda i: (0, 0))],
        out_specs=tile(D),
        compiler_params=_params("parallel"),
        name="merge",
    )(x2, xn2, ro2, ho2, fo2, w_in_b, w_ret_o, w_hgrn_o, w_fnet, w_out, norm_w.reshape(1, D))


def _ffn_kernel(x_ref, xp_ref, xnx_ref, nw_in_ref, wup_ref, cw_ref, cb_ref, wdn_ref, nw_out_ref,
                *rest, tm, tiles_per_seq, d_ff, fc, emit_next):
    if emit_next:
        nw_next_ref, o_ref, xn_ref, hn_s, acc_s = rest
    else:
        (o_ref, hn_s, acc_s), nw_next_ref, xn_ref = rest, None, None
    i = pl.program_id(0)
    r = i % tiles_per_seq
    halo = BF16_ROWS
    x = x_ref[...]
    nw = nw_in_ref[...]
    hp = jnp.where(r == 0, 0.0, _rms(xp_ref[...]) * nw)
    hx = jnp.where(r == tiles_per_seq - 1, 0.0, _rms(xnx_ref[...]) * nw)
    hn = jnp.concatenate([hp, _rms(x) * nw, hx], axis=0).astype(BF16)
    n_ext = tm + 2 * halo
    hn_s[...] = hn
    acc_s[...] = jnp.zeros_like(acc_s)

    def conv(col, scale):
        cols = pl.ds(pl.multiple_of(col, fc), fc)
        h = _dot(hn_s[...], wup_ref[:, cols])
        cw = cw_ref[:, cols] * scale
        prev = pltpu.roll(h, 1, 0)[halo:halo + tm]
        nxt = pltpu.roll(h, n_ext - 1, 0)[halo:halo + tm]
        return (cb_ref[:, cols] * scale + prev * cw[0:1] + h[halo:halo + tm] * cw[1:2]
                + nxt * cw[2:3])

    def chunk(c, carry):
        gate = conv(c * fc, 1.0)
        half_up = conv(d_ff + c * fc, 0.5)
        inner = gate * (GELU_C0 + GELU_C1 * (gate * gate))
        act = (gate * (1.0 + jnp.tanh(inner)) * half_up).astype(BF16)
        acc_s[...] += _dot(act, wdn_ref[pl.ds(pl.multiple_of(c * fc, fc), fc), :])
        return carry

    for c in range(d_ff // fc):
        chunk(c, 0)
    y = x + _rms(acc_s[...]) * nw_out_ref[...]
    o_ref[...] = y
    if xn_ref is not None:
        xn_ref[...] = (_rms(y) * nw_next_ref[...]).astype(xn_ref.dtype)


def conv_ffn_block(x2, seq, w_up, conv_w, conv_b, w_down, nw_in, nw_out, layer, nw_next=None,
                   tm=1024, fc=256):
    T, D = x2.shape
    d_ff = w_down.shape[1]
    halo = BF16_ROWS
    tps = seq // tm
    hb = tm // halo
    n_hb = T // halo
    assert seq % tm == 0, "a row tile must not straddle two sequences (the conv zero-pads each)"
    emit_next = nw_next is not None
    kern = functools.partial(_ffn_kernel, tm=tm, tiles_per_seq=tps, d_ff=d_ff, fc=fc,
                             emit_next=emit_next)
    vec = pl.BlockSpec((1, D), lambda i: (0, 0))
    tile = pl.BlockSpec((tm, D), lambda i: (i, 0))
    in_specs = [tile,
                pl.BlockSpec((halo, D), lambda i: (jnp.maximum(i * hb - 1, 0), 0)),
                pl.BlockSpec((halo, D), lambda i: (jnp.minimum((i + 1) * hb, n_hb - 1), 0)),
                vec,
                _resident((None, D, 2 * d_ff), lambda i: (layer, 0, 0)),
                pl.BlockSpec((None, CONV_W, 2 * d_ff), lambda i: (layer, 0, 0)),
                pl.BlockSpec((None, 1, 2 * d_ff), lambda i: (layer, 0, 0)),
                _resident((None, d_ff, D), lambda i: (layer, 0, 0)),
                vec]
    args = [x2, x2, x2, nw_in.reshape(1, D), w_up, conv_w, conv_b, w_down, nw_out.reshape(1, D)]
    out_shape = jax.ShapeDtypeStruct((T, D), F32)
    out_specs = tile
    if emit_next:
        in_specs.append(vec)
        args.append(nw_next.reshape(1, D))
        out_shape = (out_shape, jax.ShapeDtypeStruct((T, D), BF16))
        out_specs = (tile, tile)
    return pl.pallas_call(
        kern,
        out_shape=out_shape,
        grid=(T // tm,),
        in_specs=in_specs,
        out_specs=out_specs,
        scratch_shapes=[pltpu.VMEM((tm + 2 * halo, D), BF16),
                        pltpu.VMEM((tm, D), F32)],
        compiler_params=_params("parallel"),
        name="conv_ffn",
    )(*args)


def kernel(x, positions, norm_w, w_in, hgrn_lb_logits, hgrn_norm_w, w_ret_o, w_hgrn_o,
           w_fnet, w_out, w_up, conv_w, conv_b, w_down):
    B, S, D = x.shape
    depth = w_in.shape[0]
    T = B * S

    hgrn_off = 2 * D + 2 * 2 * D
    fu_off = hgrn_off + 5 * D
    ga_off = fu_off + D

    w_in_b = w_in.astype(BF16)
    w_ret_o_b = w_ret_o.astype(BF16)
    w_hgrn_o_b = w_hgrn_o.astype(BF16)
    w_fnet_b = w_fnet.astype(BF16)
    w_out_b = w_out.astype(BF16)
    w_up_b = w_up.astype(BF16)
    w_down_b = w_down.astype(BF16)
    conv_b3 = conv_b.reshape(depth, 1, -1)

    log_gamma = jnp.log(1.0 - 2.0 ** (-5.0 - jnp.arange(RET_HEADS, dtype=F32)))
    p = jax.nn.softmax(hgrn_lb_logits.astype(F32), axis=1)
    lower_bounds = jnp.cumsum(p, axis=1) - p[:, :1]

    cos, sin, xn3 = rope_tables_and_norm(positions, D // RET_HEADS // 2, x, norm_w[0, 0])

    x2 = x.reshape(T, D)
    xn2 = xn3.reshape(T, D)
    for l in range(depth):
        xn3 = xn2.reshape(B, S, D)
        ro = retention_branch(xn3, w_in_b, l, cos, sin, log_gamma)
        ho = hgrn_branch(xn3, w_in_b, l, hgrn_off, lower_bounds[0, l], lower_bounds[1, l],
                         hgrn_norm_w[l])
        fo = fourier_branch(xn3, w_in_b, l, fu_off // D)
        x2 = merge_branches(x2, xn2, ro.reshape(T, -1), ho.reshape(T, D), fo.reshape(T, D),
                            w_in_b, l, ga_off // (N_BRANCH * D),
                            w_ret_o_b, w_hgrn_o_b, w_fnet_b, w_out_b, norm_w[l, 1])
        if l + 1 < depth:
            x2, xn2 = conv_ffn_block(x2, S, w_up_b, conv_w, conv_b3, w_down_b,
                                     norm_w[l, 2], norm_w[l, 3], l, nw_next=norm_w[l + 1, 0])
        else:
            x2 = conv_ffn_block(x2, S, w_up_b, conv_w, conv_b3, w_down_b,
                                norm_w[l, 2], norm_w[l, 3], l)
    return x2.reshape(B, S, D)
```

```python
import functools
import math

import numpy as np
import jax
import jax.numpy as jnp
from jax import lax
from jax.experimental import pallas as pl
from jax.experimental.pallas import tpu as pltpu

F32 = jnp.float32
BF16 = jnp.bfloat16

RET_HEADS = 4
HGRN_HEADS = 8
FNET_GROUPS = 4
N_BRANCH = 3
CONV_W = 3
ROPE_BASE = 10000.0
LB_FLOOR = 1e-30
EPS = 1e-6
LOG2_E = 1.4426950408889634
GELU_C0 = math.sqrt(2.0 / math.pi)
GELU_C1 = GELU_C0 * 0.044715

V7X_VMEM_LIMIT_BYTES = 56 * 1024 * 1024
SUBLANES = 8
BF16_ROWS = 16

RET_CHUNK = 256
HGRN_CHUNK = 128
HGRN_HEADS_PER_STEP = 2
HGRN_VPU_LEVELS = (0,)
ROW_TILE = 512
HGRN_ROW_TILE = 512


def _dot(a, b):
    return jnp.dot(a, b, preferred_element_type=F32)


def _dot_nt(a, b):
    return lax.dot_general(a, b, (((1,), (1,)), ((), ())), preferred_element_type=F32)


def _dot_tn(a, b):
    return lax.dot_general(a, b, (((0,), (0,)), ((), ())), preferred_element_type=F32)


def _silu(x, scale=1.0):
    return (x * scale if scale != 1.0 else x) / (1.0 + jnp.exp2(x * (-LOG2_E)))


def _rms(x):
    return x * lax.rsqrt(jnp.mean(x * x, axis=-1, keepdims=True) + EPS)


def _params(*sem):
    return pltpu.CompilerParams(dimension_semantics=sem,
                                vmem_limit_bytes=V7X_VMEM_LIMIT_BYTES)


def _resident(shape, index_map):
    return pl.BlockSpec(shape, index_map, pipeline_mode=pl.Buffered(1))


def _rope_norm_kernel(pos_ref, invf_ref, x_ref, w_ref, cos_ref, sin_ref, xn_ref):
    ang = pos_ref[...] * invf_ref[...]
    cos_ref[...] = jnp.cos(ang)
    sin_ref[...] = jnp.sin(ang)
    xn_ref[...] = (_rms(x_ref[...]) * w_ref[...]).astype(xn_ref.dtype)


def rope_tables_and_norm(positions, half, x, w):
    B, S = positions.shape
    D = x.shape[-1]
    pos = positions.astype(F32).reshape(B, S, 1)
    inv_freq = (ROPE_BASE ** (-jnp.arange(half, dtype=F32) / half)).reshape(1, half)
    table = jax.ShapeDtypeStruct((B, S, half), F32)
    table_spec = pl.BlockSpec((None, S, half), lambda b: (b, 0, 0))
    rows_spec = pl.BlockSpec((None, S, D), lambda b: (b, 0, 0))
    return pl.pallas_call(
        _rope_norm_kernel,
        out_shape=(table, table, jax.ShapeDtypeStruct((B, S, D), BF16)),
        grid=(B,),
        in_specs=[pl.BlockSpec((None, S, 1), lambda b: (b, 0, 0)),
                  pl.BlockSpec((1, half), lambda b: (0, 0)),
                  rows_spec,
                  pl.BlockSpec((1, D), lambda b: (0, 0))],
        out_specs=(table_spec, table_spec, rows_spec),
        compiler_params=_params("parallel"),
        name="rope_tables_norm",
    )(pos, inv_freq, x, w.reshape(1, D))


def _ret_kernel(lg_ref, xn_ref, wq_ref, wk_ref, wv_ref, wg_ref, cos_ref, sin_ref,
                o_ref, qi_s, qd_s, ki_s, v_s, g_s, st_s, kvb_s, run_s, *, seq, dk, dv):
    C = RET_CHUNK
    R = seq // C
    half = dk // 2
    lg = lg_ref[pl.program_id(1)]
    ret_scale = dk ** -0.5

    def rows_of(n):
        return pl.ds(pl.multiple_of(n * C, C), C)

    pos = lax.broadcasted_iota(jnp.int32, (C, 1), 0).astype(F32)
    qdec_f = jnp.exp(lg * (pos + 1.0))
    qdec_b = jnp.exp(lg * (C - pos))
    kdec_f = jnp.exp(lg * (C - 1.0 - pos))
    kdec_b = jnp.exp(lg * pos)
    chunk_dec = jnp.exp(lg * C)
    ii = lax.broadcasted_iota(jnp.int32, (C, C), 0)
    jj = lax.broadcasted_iota(jnp.int32, (C, C), 1)
    decay = jnp.exp(lg * jnp.abs(ii - jj).astype(F32))

    run_s[...] = jnp.zeros_like(run_s)

    def proj(t, carry):
        rows = pl.ds(pl.multiple_of(t * ROW_TILE, ROW_TILE), ROW_TILE)
        xc = xn_ref[rows, :]
        cos = cos_ref[rows, :]
        sin = sin_ref[rows, :]
        q = _dot(xc, wq_ref[...])
        q1, q2 = q[:, :half], q[:, half:]
        q = jnp.concatenate([q1 * cos - q2 * sin, q1 * sin + q2 * cos], axis=-1)
        k = _dot(xc, wk_ref[...]) * ret_scale
        k1, k2 = k[:, :half], k[:, half:]
        k = jnp.concatenate([k1 * cos - k2 * sin, k1 * sin + k2 * cos], axis=-1)
        v = _dot(xc, wv_ref[...]).astype(BF16)
        g = _dot(xc, wg_ref[...])
        qi_s[rows, :] = q.astype(BF16)
        ki_s[rows, :] = k.astype(BF16)
        v_s[rows, :] = v
        g_s[rows, :] = _silu(g).astype(BF16)
        for j in range(ROW_TILE // C):
            n = t * (ROW_TILE // C) + j
            sl = slice(j * C, (j + 1) * C)
            qd_s[rows_of(n), :] = jnp.concatenate([q[sl] * qdec_f, q[sl] * qdec_b],
                                                  axis=-1).astype(BF16)
            st_s[n, pl.ds(0, dk), :] = run_s[...].astype(BF16)
            run_s[...] = run_s[...] * chunk_dec + _dot_tn((k[sl] * kdec_f).astype(BF16), v[sl])
            kvb_s[n] = _dot_tn((k[sl] * kdec_b).astype(BF16), v[sl])
        return carry

    lax.fori_loop(0, seq // ROW_TILE, proj, 0, unroll=2)

    run_s[...] = jnp.zeros_like(run_s)

    def out(t, carry):
        n = R - 1 - t
        rows = rows_of(n)
        st_s[n, pl.ds(dk, dk), :] = run_s[...].astype(BF16)
        s = _dot_nt(qi_s[rows, :], ki_s[rows, :]) * decay
        o = _dot(s.astype(BF16), v_s[rows, :]) + _dot(qd_s[rows, :], st_s[n])
        o_ref[rows, :] = (_rms(o) * g_s[rows, :].astype(F32)).astype(o_ref.dtype)
        run_s[...] = run_s[...] * chunk_dec + kvb_s[n]
        return carry

    lax.fori_loop(0, R, out, 0, unroll=4)


def retention_branch(xn3, w_in_b, layer, cos, sin, log_gamma):
    B, S, D = xn3.shape
    dk = D // RET_HEADS
    dv = 2 * dk
    H = RET_HEADS
    qk_blocks = D // dk
    v_off = 2 * D // dv
    g_off = v_off + H
    kern = functools.partial(_ret_kernel, seq=S, dk=dk, dv=dv)
    return pl.pallas_call(
        kern,
        out_shape=jax.ShapeDtypeStruct((B, S, H * dv), BF16),
        grid=(B, H),
        in_specs=[
            pl.BlockSpec(memory_space=pltpu.SMEM),
            pl.BlockSpec((None, S, D), lambda b, h: (b, 0, 0)),
            pl.BlockSpec((None, D, dk), lambda b, h: (layer, 0, h)),
            pl.BlockSpec((None, D, dk), lambda b, h: (layer, 0, qk_blocks + h)),
            pl.BlockSpec((None, D, dv), lambda b, h: (layer, 0, v_off + h)),
            pl.BlockSpec((None, D, dv), lambda b, h: (layer, 0, g_off + h)),
            pl.BlockSpec((None, S, dk // 2), lambda b, h: (b, 0, 0)),
            pl.BlockSpec((None, S, dk // 2), lambda b, h: (b, 0, 0)),
        ],
        out_specs=pl.BlockSpec((None, S, dv), lambda b, h: (b, 0, h)),
        scratch_shapes=[
            pltpu.VMEM((S, dk), BF16),
            pltpu.VMEM((S, 2 * dk), BF16),
            pltpu.VMEM((S, dk), BF16),
            pltpu.VMEM((S, dv), BF16),
            pltpu.VMEM((S, dv), BF16),
            pltpu.VMEM((S // RET_CHUNK, 2 * dk, dv), BF16),
            pltpu.VMEM((S // RET_CHUNK, dk, dv), F32),
            pltpu.VMEM((dk, dv), F32),
        ],
        compiler_params=_params("parallel", "arbitrary"),
        name="retention",
    )(log_gamma, xn3, w_in_b, w_in_b, w_in_b, w_in_b, cos, sin)


def _hgrn_gate(z, lb):
    e = jnp.exp2(jnp.abs(z) * (-LOG2_E))
    pos = z >= 0.0
    sig_neg_num = jnp.where(pos, e, 1.0)
    num = jnp.where(pos, 1.0, e) + jnp.maximum(lb, LB_FLOOR) * sig_neg_num
    inv = 1.0 / (1.0 + e)
    log2_f = jnp.log2(num * inv)
    return log2_f, (1.0 - lb) * sig_neg_num * inv


def _boundary_rows(cum_ref, base, m, reverse, row_in_group):
    C = HGRN_CHUNK
    d = cum_ref.shape[1]
    blk = 2 * m
    off = m if reverse else m - 1
    pieces = []
    if blk >= SUBLANES:
        for b in range(C // blk):
            pieces.append(jnp.broadcast_to(cum_ref[pl.ds(base + (b * blk + off), 1), :], (blk, d)))
    else:
        for g in range(C // SUBLANES):
            val = None
            for u in range(SUBLANES // blk):
                row = g * SUBLANES + u * blk + off
                piece = jnp.broadcast_to(cum_ref[pl.ds(base + row, 1), :], (SUBLANES, d))
                val = piece if val is None else jnp.where(row_in_group >= u * blk, piece, val)
            pieces.append(val)
    return jnp.concatenate(pieces, axis=0) if len(pieces) > 1 else pieces[0]


def _level_operands(level, q, k, cum, cum_ref, base, consts, reverse):
    C = HGRN_CHUNK
    _, row_in_group, signs, _ = consts
    m = 2 ** level
    if 2 * m <= SUBLANES:
        sign = signs[level]
        ref_pt = _boundary_rows(cum_ref, base, m, reverse, row_in_group)
        x = (jnp.where(sign > 0.0, q, k) * jnp.exp2((cum - ref_pt) * sign)).astype(BF16)
        return x, x, list(range(C // SUBLANES))
    xq, xall, q_groups = [], [], []
    for b in range(C // (2 * m)):
        first = slice(b * 2 * m, b * 2 * m + m)
        second = slice(b * 2 * m + m, (b + 1) * 2 * m)
        q_rows, k_rows = (first, second) if reverse else (second, first)
        edge = k_rows.start if reverse else k_rows.stop - 1
        ref_pt = cum_ref[pl.ds(base + edge, 1), :]
        xq_b = q[q_rows] * jnp.exp2(cum[q_rows] - ref_pt)
        xk_b = k[k_rows] * jnp.exp2(ref_pt - cum[k_rows])
        xq.append(xq_b)
        xall.extend([xq_b, xk_b] if reverse else [xk_b, xq_b])
        q_groups.extend(range(q_rows.start // SUBLANES, q_rows.stop // SUBLANES))
    return (jnp.concatenate(xq, axis=0).astype(BF16), jnp.concatenate(xall, axis=0).astype(BF16),
            q_groups)


def _paired_dot_nt(lhs_a, rhs_a, lhs_b, rhs_b):
    rhs = jnp.concatenate([rhs_a, rhs_b], axis=1)
    lhs = jnp.concatenate(
        [jnp.concatenate([lhs_a, jnp.zeros_like(lhs_a)], axis=1),
         jnp.concatenate([jnp.zeros_like(lhs_b), lhs_b], axis=1)], axis=0)
    s = _dot_nt(lhs, rhs)
    return s[:lhs_a.shape[0]], s[lhs_a.shape[0]:]


def _assemble_scores(scores, level_id):
    C = HGRN_CHUNK
    rows = [jnp.zeros((SUBLANES, C), F32) for _ in range(C // SUBLANES)]
    for level, entry in enumerate(scores):
        if entry is None:
            continue
        s, q_groups = entry
        for i, g in enumerate(q_groups):
            lid = level_id[g * SUBLANES:(g + 1) * SUBLANES]
            rows[g] = jnp.where(lid == level, s[i * SUBLANES:(i + 1) * SUBLANES], rows[g])
    return jnp.concatenate(rows, axis=0)


def _hgrn_intra_pair(fwd, bwd, consts):
    C = HGRN_CHUNK
    args = ((fwd, consts[0], False), (bwd, consts[1], True))
    n_vpu = len(HGRN_VPU_LEVELS)
    scores = ([None] * n_vpu, [None] * n_vpu)
    for level in range(n_vpu, C.bit_length() - 1):
        ops = [_level_operands(level, q, k, cum, cum_ref, base, cst, rev)
               for (q, k, _, cum, cum_ref, base), cst, rev in args]
        s_f, s_b = _paired_dot_nt(ops[0][0], ops[0][1], ops[1][0], ops[1][1])
        scores[0].append((s_f, ops[0][2]))
        scores[1].append((s_b, ops[1][2]))
    outs = []
    for idx, ((q, k, v, cum, _, _), cst, rev) in enumerate(args):
        attn = _assemble_scores(scores[idx], cst[0])
        o = _dot(attn.astype(BF16), v.astype(BF16))
        o += jnp.sum(q * k, axis=-1, keepdims=True) * v
        outs.append(o + _near_pairs(q, k, v, cum, rev, cst[3]))
    return outs


def _near_pairs(q, k, v, cum, reverse, masks):
    C, d = q.shape
    shape3 = (C // SUBLANES, SUBLANES, d)
    out = jnp.zeros((C, d), F32)
    for offset, mask in enumerate(masks, start=1):
        valid = mask != 0
        shift = (SUBLANES - offset) if reverse else offset

        def key_row(x):
            return pltpu.roll(x.reshape(shape3), shift, 1).reshape(C, d)

        w = jnp.exp2(jnp.where(valid, cum - key_row(cum), 0.0))
        score = jnp.sum(q * key_row(k) * w, axis=-1, keepdims=True)
        out += jnp.where(valid, score * key_row(v), 0.0)
    return out


def _near_pair_masks(C, d, reverse, levels):
    pos = lax.broadcasted_iota(jnp.int32, (C, d), 0) % SUBLANES
    masks = []
    for offset in range(1, 2 ** (max(levels) + 1)):
        valid = jnp.zeros((C, d), jnp.int32)
        for level in levels:
            m, blk = 2 ** level, 2 ** (level + 1)
            r = pos % blk
            key = (r + offset) if reverse else (r - offset)
            if reverse:
                ok = (r < m) & (key >= m) & (key < blk)
            else:
                ok = (r >= m) & (key >= 0) & (key < m)
            valid = jnp.where(ok, 1, valid)
        masks.append(valid)
    return masks


def _hgrn_consts(reverse, d):
    C = HGRN_CHUNK
    ii = lax.broadcasted_iota(jnp.int32, (C, C), 0)
    jj = lax.broadcasted_iota(jnp.int32, (C, C), 1)
    diff = ii ^ jj
    level_id = jnp.full((C, C), -1, jnp.int32)
    n_levels = C.bit_length() - 1
    for level in range(n_levels):
        level_id = jnp.where((diff >> level) == 1, level, level_id)
    level_id = jnp.where((ii < jj) if reverse else (ii > jj), level_id, -1)
    rows = lax.broadcasted_iota(jnp.int32, (C, d), 0)
    row_in_group = lax.broadcasted_iota(jnp.int32, (SUBLANES, d), 0)
    signs = []
    for level in range(SUBLANES.bit_length() - 1):
        second = ((rows >> level) & 1) == 1
        is_query = jnp.logical_not(second) if reverse else second
        signs.append(jnp.where(is_query, 1.0, -1.0))
    return level_id, row_in_group, signs, _near_pair_masks(C, d, reverse, HGRN_VPU_LEVELS)


def _chunk_cumsum(x, reverse, row_in_group):
    rows, d = x.shape
    groups = rows // SUBLANES
    per_chunk = HGRN_CHUNK // SUBLANES
    y = x.reshape(groups, SUBLANES, d)
    step = 1
    while step < SUBLANES:
        rolled = pltpu.roll(y, (SUBLANES - step) if reverse else step, 1)
        valid = (row_in_group < SUBLANES - step) if reverse else (row_in_group >= step)
        y = y + jnp.where(valid, rolled, 0.0)
        step *= 2
    out = [None] * groups
    for c in range(rows // HGRN_CHUNK):
        order = range(c * per_chunk, (c + 1) * per_chunk)
        carry = None
        for g in (reversed(order) if reverse else order):
            yg = y[g] if carry is None else y[g] + carry
            out[g] = yg
            edge = 0 if reverse else SUBLANES - 1
            carry = jnp.broadcast_to(yg[edge:edge + 1, :], (SUBLANES, d))
    return jnp.concatenate(out, axis=0)


def _hgrn_kernel(xn_ref, wq_ref, wzf_ref, wzb_ref, wi_ref, wg_ref, lbf_ref, lbb_ref, nw_ref, o_ref,
                 q_s, v_s, g_s, acc_s, kf_s, kb_s, cumf_s, cumb_s, *, seq, dk, heads):
    C = HGRN_CHUNK
    R = seq // C
    row_tile = HGRN_ROW_TILE
    scale = dk ** -0.5

    def tile_rows(n):
        return pl.ds(pl.multiple_of(n * row_tile, row_tile), row_tile)

    def head_cols(h):
        return slice(h * dk, (h + 1) * dk)

    dirs = [((False, lbf_ref, kf_s.at[h], cumf_s.at[h]),
             (True, lbb_ref, kb_s.at[h], cumb_s.at[h]))
            for h in range(heads)]
    row_in_group = lax.broadcasted_iota(jnp.int32, (1, SUBLANES, dk), 1)

    def proj(n, carry):
        rows = tile_rows(n)
        xc = xn_ref[rows, :]
        hq_all = _dot(xc, wq_ref[...])
        z_all = (_dot(xc, wzf_ref[...]), _dot(xc, wzb_ref[...]))
        v_all = _dot(xc, wi_ref[...])
        hg_all = _dot(xc, wg_ref[...])
        for h in range(heads):
            cols = head_cols(h)
            hq = hq_all[:, cols]
            q = _silu(hq, scale)
            q_s[h, rows, :] = q
            for idx, (reverse, lb_ref, k_s, cum_s) in enumerate(dirs[h]):
                lf, kk = _hgrn_gate(z_all[idx][:, cols], lb_ref[:, cols])
                k_s[rows, :] = kk
                cum_s[rows, :] = _chunk_cumsum(lf, reverse, row_in_group)
            v_s[h, rows, :] = v_all[:, cols]
            hg = hg_all[:, cols]
            g_s[h, rows, :] = _silu(hg)
            acc_s[h, rows, :] = jnp.zeros((row_tile, dk), F32)
        return carry

    lax.fori_loop(0, seq // row_tile, proj, 0, unroll=2)

    consts = (_hgrn_consts(False, dk), _hgrn_consts(True, dk))

    def chunk_pair(h, cf, cb, states):
        data, rows, q_dec, k_dec, decs = [], [], [], [], []
        for c, (reverse, _, k_s, cum_s) in zip((cf, cb), dirs[h]):
            base = pl.multiple_of(c * C, C)
            r = pl.ds(base, C)
            q, k, cum = q_s[h, r, :], k_s[r, :], cum_s[r, :]
            data.append((q, k, v_s[h, r, :], cum, cum_s, base))
            rows.append(r)
            total = cum_s[pl.ds(base + (0 if reverse else C - 1), 1), :]
            q_dec.append((q * jnp.exp2(cum)).astype(BF16))
            k_dec.append((k * jnp.exp2(total - cum)).astype(BF16))
            decs.append(jnp.exp2(total))
        o_f, o_b = _hgrn_intra_pair(data[0], data[1], consts)
        i_f, i_b = _paired_dot_nt(q_dec[0], states[0].astype(BF16), q_dec[1], states[1].astype(BF16))
        acc_s[h, rows[0], :] += o_f + i_f
        acc_s[h, rows[1], :] += o_b + i_b
        return tuple(st * dec + _dot_tn(d[2].astype(BF16), kd)
                     for st, dec, d, kd in zip(states, decs, data, k_dec))

    def step(i, states):
        return tuple(chunk_pair(h, i, R - 1 - i, states[h]) for h in range(heads))

    zero = jnp.zeros((dk, dk), F32)
    lax.fori_loop(0, R, step, tuple((zero, zero) for _ in range(heads)))

    def finish(n, carry):
        rows = tile_rows(n)
        for h in range(heads):
            o_ref[rows, head_cols(h)] = (_rms(acc_s[h, rows, :]) * nw_ref[...]
                                         * g_s[h, rows, :]).astype(o_ref.dtype)
        return carry

    lax.fori_loop(0, seq // row_tile, finish, 0)


def hgrn_branch(xn3, w_in_b, layer, hgrn_off, lb_f, lb_b, norm_w):
    B, S, D = xn3.shape
    dk = D // HGRN_HEADS
    hp = HGRN_HEADS_PER_STEP
    wide = hp * dk
    kern = functools.partial(_hgrn_kernel, seq=S, dk=dk, heads=hp)
    vec = lambda: pltpu.VMEM((hp, S, dk), F32)

    def w_spec(group):
        first = (hgrn_off + group * D) // wide
        return pl.BlockSpec((None, D, wide), lambda b, j: (layer, 0, first + j))

    lb_spec = pl.BlockSpec((1, wide), lambda b, j: (0, j))
    return pl.pallas_call(
        kern,
        out_shape=jax.ShapeDtypeStruct((B, S, D), BF16),
        grid=(B, HGRN_HEADS // hp),
        in_specs=[pl.BlockSpec((None, S, D), lambda b, j: (b, 0, 0)),
                  w_spec(0), w_spec(1), w_spec(2), w_spec(3), w_spec(4),
                  lb_spec, lb_spec,
                  pl.BlockSpec((1, dk), lambda b, j: (0, 0))],
        out_specs=pl.BlockSpec((None, S, wide), lambda b, j: (b, 0, j)),
        scratch_shapes=[vec(), vec(), vec(), vec(), vec(), vec(), vec(), vec()],
        compiler_params=_params("parallel", "arbitrary"),
        name="hgrn2",
    )(xn3, w_in_b, w_in_b, w_in_b, w_in_b, w_in_b,
      lb_f.reshape(1, D), lb_b.reshape(1, D), norm_w.reshape(1, dk))


def _fnet_proj_kernel(xn_ref, w_ref, cs_ref, o_ref, *, gdim):
    fu = _dot(xn_ref[...], w_ref[...]).astype(BF16)
    for g in range(FNET_GROUPS):
        t = _dot(fu[:, g * gdim:(g + 1) * gdim], cs_ref[...])
        o_ref[0, :, g * gdim:(g + 1) * gdim] = t[:, :gdim].astype(o_ref.dtype)
        o_ref[1, :, g * gdim:(g + 1) * gdim] = t[:, gdim:].astype(o_ref.dtype)


def _seq_dft_kernel(cos_ref, sin_ref, perm_ref, rhs_ref, o_ref, fold_s, mir_s, *, seq):
    n = seq
    half = n // 2
    blk = perm_ref.shape[0]
    c0 = 1.0 / math.sqrt(n)

    for part, sign in ((0, 1.0), (1, -1.0)):
        base = part * n
        for j in range(half // blk):
            own = rhs_ref[pl.ds(base + j * blk, blk), :].astype(F32)
            if j == 0:
                mirror = _dot(perm_ref[:, :blk], rhs_ref[pl.ds(base + n - blk, blk), :])
            else:
                mirror = _dot(perm_ref[...], rhs_ref[pl.ds(base + n - (j + 1) * blk, 2 * blk), :])
            fold_s[pl.ds(part * half + j * blk, blk), :] = (own + sign * mirror).astype(fold_s.dtype)

    mid = rhs_ref[pl.ds(half, BF16_ROWS), :].astype(F32)[0:1] * c0
    p_ext = _dot(cos_ref[...], fold_s[pl.ds(0, half), :])
    q = _dot(sin_ref[...], fold_s[pl.ds(half, half), :])
    row = lax.broadcasted_iota(jnp.int32, q.shape, 0)
    p = p_ext[:half] + jnp.where((row & 1) == 0, 1.0, -1.0) * mid
    o_ref[pl.ds(0, half), :] = (p - q).astype(o_ref.dtype)

    mir_s[pl.ds(0, half), :] = (p + q).astype(mir_s.dtype)
    first = jnp.where(lax.broadcasted_iota(jnp.int32, (BF16_ROWS, q.shape[1]), 0) == 0, 1.0, 0.0)
    tail = (p_ext[half:half + BF16_ROWS] + mid) * first
    mir_s[pl.ds(half, BF16_ROWS), :] = tail.astype(mir_s.dtype)
    mir_s[pl.ds(half + BF16_ROWS, blk - BF16_ROWS), :] = jnp.zeros(
        (blk - BF16_ROWS, tail.shape[1]), mir_s.dtype)
    for j in range(half // blk):
        window = mir_s[pl.ds(half - (j + 1) * blk, 2 * blk), :]
        o_ref[pl.ds(half + j * blk, blk), :] = _dot(perm_ref[...], window).astype(o_ref.dtype)


DFT_FOLD_BLOCK = 128


def _half_dft_tables(n):
    c, s = _dft_tables(n)
    half = n // 2
    cos_ext = np.zeros((half + BF16_ROWS, half))
    cos_ext[:half + 1] = c[:half + 1, :half]
    return cos_ext, s[:half, :half]


def _mirror_permutation(blk):
    p = np.zeros((blk, 2 * blk), np.float32)
    i = np.arange(blk)
    p[i, blk - i] = 1.0
    return p


def _dft_tables(n):
    idx = np.arange(n, dtype=np.int64)
    ang = 2.0 * np.pi * ((idx[:, None] * idx[None, :]) % n).astype(np.float64) / n
    s = 1.0 / math.sqrt(n)
    return np.cos(ang) * s, np.sin(ang) * s


def fourier_branch(xn3, w_in_b, layer, fu_off_blocks, tm=512):
    B, S, D = xn3.shape
    W = D
    gdim = W // FNET_GROUPS
    c_small, s_small = _dft_tables(gdim)
    cs_small = jnp.asarray(np.concatenate([c_small, s_small], axis=1), dtype=BF16)
    half = S // 2
    assert half % 2 == 0 and half % DFT_FOLD_BLOCK == 0
    cos_np, sin_np = _half_dft_tables(S)
    dft_cos = jnp.asarray(cos_np, dtype=BF16)
    dft_sin = jnp.asarray(sin_np, dtype=BF16)
    perm = jnp.asarray(_mirror_permutation(DFT_FOLD_BLOCK), dtype=BF16)
    tiles = S // tm
    rhs = pl.pallas_call(
        functools.partial(_fnet_proj_kernel, gdim=gdim),
        out_shape=jax.ShapeDtypeStruct((B, 2, S, W), BF16),
        grid=(B, tiles),
        in_specs=[pl.BlockSpec((None, tm, D), lambda b, r: (b, r, 0)),
                  pl.BlockSpec((None, D, W), lambda b, r: (layer, 0, fu_off_blocks)),
                  pl.BlockSpec((gdim, 2 * gdim), lambda b, r: (0, 0))],
        out_specs=pl.BlockSpec((None, 2, tm, W), lambda b, r: (b, 0, r, 0)),
        compiler_params=_params("parallel", "parallel"),
        name="fnet_proj",
    )(xn3, w_in_b, cs_small)
    rhs = rhs.reshape(B, 2 * S, W)
    return pl.pallas_call(
        functools.partial(_seq_dft_kernel, seq=S),
        out_shape=jax.ShapeDtypeStruct((B, S, W), BF16),
        grid=(B,),
        in_specs=[_resident((half + BF16_ROWS, half), lambda b: (0, 0)),
                  _resident((half, half), lambda b: (0, 0)),
                  pl.BlockSpec((DFT_FOLD_BLOCK, 2 * DFT_FOLD_BLOCK), lambda b: (0, 0)),
                  pl.BlockSpec((None, 2 * S, W), lambda b: (b, 0, 0))],
        out_specs=pl.BlockSpec((None, S, W), lambda b: (b, 0, 0)),
        scratch_shapes=[pltpu.VMEM((S, W), BF16),
                        pltpu.VMEM((half + DFT_FOLD_BLOCK, W), BF16)],
        compiler_params=_params("parallel"),
        name="fnet_seq_dft",
    )(dft_cos, dft_sin, perm, rhs)


def _merge_kernel(x_ref, xn_ref, ro_ref, ho_ref, fo_ref, wga_ref, wro_ref, who_ref,
                  wf_ref, wout_ref, nw_ref, o_ref, *, d):
    xn = xn_ref[...]

    def gate(i):
        return jax.nn.sigmoid(_dot(xn, wga_ref[:, i * d:(i + 1) * d]))

    mix = gate(0) * _dot(ro_ref[...], wro_ref[...])
    mix += gate(1) * _dot(ho_ref[...], who_ref[...])
    mix += gate(2) * _dot(fo_ref[...], wf_ref[...])
    y = _dot(mix.astype(BF16), wout_ref[...])
    o_ref[...] = x_ref[...] + _rms(y) * nw_ref[...]


def merge_branches(x2, xn2, ro2, ho2, fo2, w_in_b, layer, ga_off_blocks,
                   w_ret_o, w_hgrn_o, w_fnet, w_out, norm_w, tm=512):
    T, D = x2.shape
    RV = ro2.shape[1]
    tile = lambda w: pl.BlockSpec((tm, w), lambda i: (i, 0))
    return pl.pallas_call(
        functools.partial(_merge_kernel, d=D),
        out_shape=jax.ShapeDtypeStruct((T, D), F32),
        grid=(T // tm,),
        in_specs=[tile(D), tile(D), tile(RV), tile(D), tile(D),
                  _resident((None, D, N_BRANCH * D), lambda i: (layer, 0, ga_off_blocks)),
                  _resident((None, RV, D), lambda i: (layer, 0, 0)),
                  _resident((None, D, D), lambda i: (layer, 0, 0)),
                  _resident((None, D, D), lambda i: (layer, 0, 0)),
                  _resident((None, D, D), lambda i: (layer, 0, 0)),
                  pl.BlockSpec((1, D), lambda i: (0, 0))],
        out_specs=tile(D),
        compiler_params=_params("parallel"),
        name="merge",
    )(x2, xn2, ro2, ho2, fo2, w_in_b, w_ret_o, w_hgrn_o, w_fnet, w_out, norm_w.reshape(1, D))


def _ffn_kernel(x_ref, xp_ref, xnx_ref, nw_in_ref, wup_ref, cw_ref, cb_ref, wdn_ref, nw_out_ref,
                *rest, tm, tiles_per_seq, d_ff, fc, emit_next):
    if emit_next:
        nw_next_ref, o_ref, xn_ref, hn_s, acc_s = rest
    else:
        (o_ref, hn_s, acc_s), nw_next_ref, xn_ref = rest, None, None
    i = pl.program_id(0)
    r = i % tiles_per_seq
    halo = BF16_ROWS
    x = x_ref[...]
    nw = nw_in_ref[...]
    hp = jnp.where(r == 0, 0.0, _rms(xp_ref[...]) * nw)
    hx = jnp.where(r == tiles_per_seq - 1, 0.0, _rms(xnx_ref[...]) * nw)
    hn = jnp.concatenate([hp, _rms(x) * nw, hx], axis=0).astype(BF16)
    n_ext = tm + 2 * halo
    hn_s[...] = hn
    acc_s[...] = jnp.zeros_like(acc_s)

    def conv(col, scale):
        cols = pl.ds(pl.multiple_of(col, fc), fc)
        h = _dot(hn_s[...], wup_ref[:, cols])
        cw = cw_ref[:, cols] * scale
        prev = pltpu.roll(h, 1, 0)[halo:halo + tm]
        nxt = pltpu.roll(h, n_ext - 1, 0)[halo:halo + tm]
        return (cb_ref[:, cols] * scale + prev * cw[0:1] + h[halo:halo + tm] * cw[1:2]
                + nxt * cw[2:3])

    def chunk(c, carry):
        gate = conv(c * fc, 1.0)
        half_up = conv(d_ff + c * fc, 0.5)
        inner = gate * (GELU_C0 + GELU_C1 * (gate * gate))
        act = (gate * (1.0 + jnp.tanh(inner)) * half_up).astype(BF16)
        acc_s[...] += _dot(act, wdn_ref[pl.ds(pl.multiple_of(c * fc, fc), fc), :])
        return carry

    for c in range(d_ff // fc):
        chunk(c, 0)
    y = x + _rms(acc_s[...]) * nw_out_ref[...]
    o_ref[...] = y
    if xn_ref is not None:
        xn_ref[...] = (_rms(y) * nw_next_ref[...]).astype(xn_ref.dtype)


def conv_ffn_block(x2, seq, w_up, conv_w, conv_b, w_down, nw_in, nw_out, layer, nw_next=None,
                   tm=1024, fc=256):
    T, D = x2.shape
    d_ff = w_down.shape[1]
    halo = BF16_ROWS
    tps = seq // tm
    hb = tm // halo
    n_hb = T // halo
    assert seq % tm == 0, "a row tile must not straddle two sequences (the conv zero-pads each)"
    emit_next = nw_next is not None
    kern = functools.partial(_ffn_kernel, tm=tm, tiles_per_seq=tps, d_ff=d_ff, fc=fc,
                             emit_next=emit_next)
    vec = pl.BlockSpec((1, D), lambda i: (0, 0))
    tile = pl.BlockSpec((tm, D), lambda i: (i, 0))
    in_specs = [tile,
                pl.BlockSpec((halo, D), lambda i: (jnp.maximum(i * hb - 1, 0), 0)),
                pl.BlockSpec((halo, D), lambda i: (jnp.minimum((i + 1) * hb, n_hb - 1), 0)),
                vec,
                _resident((None, D, 2 * d_ff), lambda i: (layer, 0, 0)),
                pl.BlockSpec((None, CONV_W, 2 * d_ff), lambda i: (layer, 0, 0)),
                pl.BlockSpec((None, 1, 2 * d_ff), lambda i: (layer, 0, 0)),
                _resident((None, d_ff, D), lambda i: (layer, 0, 0)),
                vec]
    args = [x2, x2, x2, nw_in.reshape(1, D), w_up, conv_w, conv_b, w_down, nw_out.reshape(1, D)]
    out_shape = jax.ShapeDtypeStruct((T, D), F32)
    out_specs = tile
    if emit_next:
        in_specs.append(vec)
        args.append(nw_next.reshape(1, D))
        out_shape = (out_shape, jax.ShapeDtypeStruct((T, D), BF16))
        out_specs = (tile, tile)
    return pl.pallas_call(
        kern,
        out_shape=out_shape,
        grid=(T // tm,),
        in_specs=in_specs,
        out_specs=out_specs,
        scratch_shapes=[pltpu.VMEM((tm + 2 * halo, D), BF16),
                        pltpu.VMEM((tm, D), F32)],
        compiler_params=_params("parallel"),
        name="conv_ffn",
    )(*args)


def kernel(x, positions, norm_w, w_in, hgrn_lb_logits, hgrn_norm_w, w_ret_o, w_hgrn_o,
           w_fnet, w_out, w_up, conv_w, conv_b, w_down):
    B, S, D = x.shape
    depth = w_in.shape[0]
    T = B * S

    hgrn_off = 2 * D + 2 * 2 * D
    fu_off = hgrn_off + 5 * D
    ga_off = fu_off + D

    w_in_b = w_in.astype(BF16)
    w_ret_o_b = w_ret_o.astype(BF16)
    w_hgrn_o_b = w_hgrn_o.astype(BF16)
    w_fnet_b = w_fnet.astype(BF16)
    w_out_b = w_out.astype(BF16)
    w_up_b = w_up.astype(BF16)
    w_down_b = w_down.astype(BF16)
    conv_b3 = conv_b.reshape(depth, 1, -1)

    log_gamma = jnp.log(1.0 - 2.0 ** (-5.0 - jnp.arange(RET_HEADS, dtype=F32)))
    p = jax.nn.softmax(hgrn_lb_logits.astype(F32), axis=1)
    lower_bounds = jnp.cumsum(p, axis=1) - p[:, :1]

    cos, sin, xn3 = rope_tables_and_norm(positions, D // RET_HEADS // 2, x, norm_w[0, 0])

    x2 = x.reshape(T, D)
    xn2 = xn3.reshape(T, D)
    for l in range(depth):
        xn3 = xn2.reshape(B, S, D)
        ro = retention_branch(xn3, w_in_b, l, cos, sin, log_gamma)
        ho = hgrn_branch(xn3, w_in_b, l, hgrn_off, lower_bounds[0, l], lower_bounds[1, l],
                         hgrn_norm_w[l])
        fo = fourier_branch(xn3, w_in_b, l, fu_off // D)
        x2 = merge_branches(x2, xn2, ro.reshape(T, -1), ho.reshape(T, D), fo.reshape(T, D),
                            w_in_b, l, ga_off // (N_BRANCH * D),
                            w_ret_o_b, w_hgrn_o_b, w_fnet_b, w_out_b, norm_w[l, 1])
        if l + 1 < depth:
            x2, xn2 = conv_ffn_block(x2, S, w_up_b, conv_w, conv_b3, w_down_b,
                                     norm_w[l, 2], norm_w[l, 3], l, nw_next=norm_w[l + 1, 0])
        else:
            x2 = conv_ffn_block(x2, S, w_up_b, conv_w, conv_b3, w_down_b,
                                norm_w[l, 2], norm_w[l, 3], l)
    return x2.reshape(B, S, D)
```

```python
import functools
import math

import numpy as np
import jax
import jax.numpy as jnp
from jax import lax
from jax.experimental import pallas as pl
from jax.experimental.pallas import tpu as pltpu

F32 = jnp.float32
BF16 = jnp.bfloat16

RET_HEADS = 4
HGRN_HEADS = 8
FNET_GROUPS = 4
N_BRANCH = 3
CONV_W = 3
ROPE_BASE = 10000.0
LB_FLOOR = 1e-30
EPS = 1e-6
LOG2_E = 1.4426950408889634
GELU_C0 = math.sqrt(2.0 / math.pi)
GELU_C1 = GELU_C0 * 0.044715

V7X_VMEM_LIMIT_BYTES = 56 * 1024 * 1024
SUBLANES = 8
BF16_ROWS = 16

RET_CHUNK = 256
HGRN_CHUNK = 128
HGRN_HEADS_PER_STEP = 2
HGRN_VPU_LEVELS = (0,)
ROW_TILE = 512
HGRN_ROW_TILE = 512


def _dot(a, b):
    return jnp.dot(a, b, preferred_element_type=F32)


def _dot_nt(a, b):
    return lax.dot_general(a, b, (((1,), (1,)), ((), ())), preferred_element_type=F32)


def _dot_tn(a, b):
    return lax.dot_general(a, b, (((0,), (0,)), ((), ())), preferred_element_type=F32)


def _silu(x, scale=1.0):
    return (x * scale if scale != 1.0 else x) / (1.0 + jnp.exp2(x * (-LOG2_E)))


def _rms(x):
    return x * lax.rsqrt(jnp.mean(x * x, axis=-1, keepdims=True) + EPS)


def _params(*sem):
    return pltpu.CompilerParams(dimension_semantics=sem,
                                vmem_limit_bytes=V7X_VMEM_LIMIT_BYTES)


def _resident(shape, index_map):
    return pl.BlockSpec(shape, index_map, pipeline_mode=pl.Buffered(1))


def _rope_norm_kernel(pos_ref, invf_ref, x_ref, w_ref, cos_ref, sin_ref, xn_ref):
    ang = pos_ref[...] * invf_ref[...]
    cos_ref[...] = jnp.cos(ang)
    sin_ref[...] = jnp.sin(ang)
    xn_ref[...] = (_rms(x_ref[...]) * w_ref[...]).astype(xn_ref.dtype)


def rope_tables_and_norm(positions, half, x, w):
    B, S = positions.shape
    D = x.shape[-1]
    pos = positions.astype(F32).reshape(B, S, 1)
    inv_freq = (ROPE_BASE ** (-jnp.arange(half, dtype=F32) / half)).reshape(1, half)
    table = jax.ShapeDtypeStruct((B, S, half), F32)
    table_spec = pl.BlockSpec((None, S, half), lambda b: (b, 0, 0))
    rows_spec = pl.BlockSpec((None, S, D), lambda b: (b, 0, 0))
    return pl.pallas_call(
        _rope_norm_kernel,
        out_shape=(table, table, jax.ShapeDtypeStruct((B, S, D), BF16)),
        grid=(B,),
        in_specs=[pl.BlockSpec((None, S, 1), lambda b: (b, 0, 0)),
                  pl.BlockSpec((1, half), lambda b: (0, 0)),
                  rows_spec,
                  pl.BlockSpec((1, D), lambda b: (0, 0))],
        out_specs=(table_spec, table_spec, rows_spec),
        compiler_params=_params("parallel"),
        name="rope_tables_norm",
    )(pos, inv_freq, x, w.reshape(1, D))


def _ret_kernel(lg_ref, xn_ref, wq_ref, wk_ref, wv_ref, wg_ref, cos_ref, sin_ref,
                o_ref, qi_s, qd_s, ki_s, v_s, g_s, st_s, kvb_s, run_s, *, seq, dk, dv):
    C = RET_CHUNK
    R = seq // C
    half = dk // 2
    lg = lg_ref[pl.program_id(1)]
    ret_scale = dk ** -0.5

    def rows_of(n):
        return pl.ds(pl.multiple_of(n * C, C), C)

    pos = lax.broadcasted_iota(jnp.int32, (C, 1), 0).astype(F32)
    qdec_f = jnp.exp(lg * (pos + 1.0))
    qdec_b = jnp.exp(lg * (C - pos))
    kdec_f = jnp.exp(lg * (C - 1.0 - pos))
    kdec_b = jnp.exp(lg * pos)
    chunk_dec = jnp.exp(lg * C)
    ii = lax.broadcasted_iota(jnp.int32, (C, C), 0)
    jj = lax.broadcasted_iota(jnp.int32, (C, C), 1)
    decay = jnp.exp(lg * jnp.abs(ii - jj).astype(F32))

    run_s[...] = jnp.zeros_like(run_s)

    def proj(t, carry):
        rows = pl.ds(pl.multiple_of(t * ROW_TILE, ROW_TILE), ROW_TILE)
        xc = xn_ref[rows, :]
        cos = cos_ref[rows, :]
        sin = sin_ref[rows, :]
        q = _dot(xc, wq_ref[...])
        q1, q2 = q[:, :half], q[:, half:]
        q = jnp.concatenate([q1 * cos - q2 * sin, q1 * sin + q2 * cos], axis=-1)
        k = _dot(xc, wk_ref[...]) * ret_scale
        k1, k2 = k[:, :half], k[:, half:]
        k = jnp.concatenate([k1 * cos - k2 * sin, k1 * sin + k2 * cos], axis=-1)
        v = _dot(xc, wv_ref[...]).astype(BF16)
        g = _dot(xc, wg_ref[...])
        qi_s[rows, :] = q.astype(BF16)
        ki_s[rows, :] = k.astype(BF16)
        v_s[rows, :] = v
        g_s[rows, :] = _silu(g).astype(BF16)
        for j in range(ROW_TILE // C):
            n = t * (ROW_TILE // C) + j
            sl = slice(j * C, (j + 1) * C)
            qd_s[rows_of(n), :] = jnp.concatenate([q[sl] * qdec_f, q[sl] * qdec_b],
                                                  axis=-1).astype(BF16)
            st_s[n, pl.ds(0, dk), :] = run_s[...].astype(BF16)
            run_s[...] = run_s[...] * chunk_dec + _dot_tn((k[sl] * kdec_f).astype(BF16), v[sl])
            kvb_s[n] = _dot_tn((k[sl] * kdec_b).astype(BF16), v[sl])
        return carry

    lax.fori_loop(0, seq // ROW_TILE, proj, 0, unroll=2)

    run_s[...] = jnp.zeros_like(run_s)

    def out(t, carry):
        n = R - 1 - t
        rows = rows_of(n)
        st_s[n, pl.ds(dk, dk), :] = run_s[...].astype(BF16)
        s = _dot_nt(qi_s[rows, :], ki_s[rows, :]) * decay
        o = _dot(s.astype(BF16), v_s[rows, :]) + _dot(qd_s[rows, :], st_s[n])
        o_ref[rows, :] = (_rms(o) * g_s[rows, :].astype(F32)).astype(o_ref.dtype)
        run_s[...] = run_s[...] * chunk_dec + kvb_s[n]
        return carry

    lax.fori_loop(0, R, out, 0, unroll=4)


def retention_branch(xn3, w_in_b, layer, cos, sin, log_gamma):
    B, S, D = xn3.shape
    dk = D // RET_HEADS
    dv = 2 * dk
    H = RET_HEADS
    qk_blocks = D // dk
    v_off = 2 * D // dv
    g_off = v_off + H
    kern = functools.partial(_ret_kernel, seq=S, dk=dk, dv=dv)
    return pl.pallas_call(
        kern,
        out_shape=jax.ShapeDtypeStruct((B, S, H * dv), BF16),
        grid=(B, H),
        in_specs=[
            pl.BlockSpec(memory_space=pltpu.SMEM),
            pl.BlockSpec((None, S, D), lambda b, h: (b, 0, 0)),
            pl.BlockSpec((None, D, dk), lambda b, h: (layer, 0, h)),
            pl.BlockSpec((None, D, dk), lambda b, h: (layer, 0, qk_blocks + h)),
            pl.BlockSpec((None, D, dv), lambda b, h: (layer, 0, v_off + h)),
            pl.BlockSpec((None, D, dv), lambda b, h: (layer, 0, g_off + h)),
            pl.BlockSpec((None, S, dk // 2), lambda b, h: (b, 0, 0)),
            pl.BlockSpec((None, S, dk // 2), lambda b, h: (b, 0, 0)),
        ],
        out_specs=pl.BlockSpec((None, S, dv), lambda b, h: (b, 0, h)),
        scratch_shapes=[
            pltpu.VMEM((S, dk), BF16),
            pltpu.VMEM((S, 2 * dk), BF16),
            pltpu.VMEM((S, dk), BF16),
            pltpu.VMEM((S, dv), BF16),
            pltpu.VMEM((S, dv), BF16),
            pltpu.VMEM((S // RET_CHUNK, 2 * dk, dv), BF16),
            pltpu.VMEM((S // RET_CHUNK, dk, dv), F32),
            pltpu.VMEM((dk, dv), F32),
        ],
        compiler_params=_params("parallel", "arbitrary"),
        name="retention",
    )(log_gamma, xn3, w_in_b, w_in_b, w_in_b, w_in_b, cos, sin)


def _hgrn_gate(z, lb):
    e = jnp.exp2(jnp.abs(z) * (-LOG2_E))
    pos = z >= 0.0
    sig_neg_num = jnp.where(pos, e, 1.0)
    num = jnp.where(pos, 1.0, e) + jnp.maximum(lb, LB_FLOOR) * sig_neg_num
    inv = 1.0 / (1.0 + e)
    log2_f = jnp.log2(num * inv)
    return log2_f, (1.0 - lb) * sig_neg_num * inv


def _boundary_rows(cum_ref, base, m, reverse, row_in_group):
    C = HGRN_CHUNK
    d = cum_ref.shape[1]
    blk = 2 * m
    off = m if reverse else m - 1
    pieces = []
    if blk >= SUBLANES:
        for b in range(C // blk):
            pieces.append(jnp.broadcast_to(cum_ref[pl.ds(base + (b * blk + off), 1), :], (blk, d)))
    else:
        for g in range(C // SUBLANES):
            val = None
            for u in range(SUBLANES // blk):
                row = g * SUBLANES + u * blk + off
                piece = jnp.broadcast_to(cum_ref[pl.ds(base + row, 1), :], (SUBLANES, d))
                val = piece if val is None else jnp.where(row_in_group >= u * blk, piece, val)
            pieces.append(val)
    return jnp.concatenate(pieces, axis=0) if len(pieces) > 1 else pieces[0]


def _level_operands(level, q, k, cum, cum_ref, base, consts, reverse):
    C = HGRN_CHUNK
    _, row_in_group, signs, _ = consts
    m = 2 ** level
    if 2 * m <= SUBLANES:
        sign = signs[level]
        ref_pt = _boundary_rows(cum_ref, base, m, reverse, row_in_group)
        x = (jnp.where(sign > 0.0, q, k) * jnp.exp2((cum - ref_pt) * sign)).astype(BF16)
        return x, x, list(range(C // SUBLANES))
    xq, xall, q_groups = [], [], []
    for b in range(C // (2 * m)):
        first = slice(b * 2 * m, b * 2 * m + m)
        second = slice(b * 2 * m + m, (b + 1) * 2 * m)
        q_rows, k_rows = (first, second) if reverse else (second, first)
        edge = k_rows.start if reverse else k_rows.stop - 1
        ref_pt = cum_ref[pl.ds(base + edge, 1), :]
        xq_b = q[q_rows] * jnp.exp2(cum[q_rows] - ref_pt)
        xk_b = k[k_rows] * jnp.exp2(ref_pt - cum[k_rows])
        xq.append(xq_b)
        xall.extend([xq_b, xk_b] if reverse else [xk_b, xq_b])
        q_groups.extend(range(q_rows.start // SUBLANES, q_rows.stop // SUBLANES))
    return (jnp.concatenate(xq, axis=0).astype(BF16), jnp.concatenate(xall, axis=0).astype(BF16),
            q_groups)


def _paired_dot_nt(lhs_a, rhs_a, lhs_b, rhs_b):
    rhs = jnp.concatenate([rhs_a, rhs_b], axis=1)
    lhs = jnp.concatenate(
        [jnp.concatenate([lhs_a, jnp.zeros_like(lhs_a)], axis=1),
         jnp.concatenate([jnp.zeros_like(lhs_b), lhs_b], axis=1)], axis=0)
    s = _dot_nt(lhs, rhs)
    return s[:lhs_a.shape[0]], s[lhs_a.shape[0]:]


def _assemble_scores(scores, level_id):
    C = HGRN_CHUNK
    rows = [jnp.zeros((SUBLANES, C), F32) for _ in range(C // SUBLANES)]
    for level, entry in enumerate(scores):
        if entry is None:
            continue
        s, q_groups = entry
        for i, g in enumerate(q_groups):
            lid = level_id[g * SUBLANES:(g + 1) * SUBLANES]
            rows[g] = jnp.where(lid == level, s[i * SUBLANES:(i + 1) * SUBLANES], rows[g])
    return jnp.concatenate(rows, axis=0)


def _hgrn_intra_pair(fwd, bwd, consts):
    C = HGRN_CHUNK
    args = ((fwd, consts[0], False), (bwd, consts[1], True))
    n_vpu = len(HGRN_VPU_LEVELS)
    scores = ([None] * n_vpu, [None] * n_vpu)
    for level in range(n_vpu, C.bit_length() - 1):
        ops = [_level_operands(level, q, k, cum, cum_ref, base, cst, rev)
               for (q, k, _, cum, cum_ref, base), cst, rev in args]
        s_f, s_b = _paired_dot_nt(ops[0][0], ops[0][1], ops[1][0], ops[1][1])
        scores[0].append((s_f, ops[0][2]))
        scores[1].append((s_b, ops[1][2]))
    outs = []
    for idx, ((q, k, v, cum, _, _), cst, rev) in enumerate(args):
        attn = _assemble_scores(scores[idx], cst[0])
        o = _dot(attn.astype(BF16), v.astype(BF16))
        o += jnp.sum(q * k, axis=-1, keepdims=True) * v
        outs.append(o + _near_pairs(q, k, v, cum, rev, cst[3]))
    return outs


def _near_pairs(q, k, v, cum, reverse, masks):
    C, d = q.shape
    shape3 = (C // SUBLANES, SUBLANES, d)
    out = jnp.zeros((C, d), F32)
    for offset, mask in enumerate(masks, start=1):
        valid = mask != 0
        shift = (SUBLANES - offset) if reverse else offset

        def key_row(x):
            return pltpu.roll(x.reshape(shape3), shift, 1).reshape(C, d)

        w = jnp.exp2(jnp.where(valid, cum - key_row(cum), 0.0))
        score = jnp.sum(q * key_row(k) * w, axis=-1, keepdims=True)
        out += jnp.where(valid, score * key_row(v), 0.0)
    return out


def _near_pair_masks(C, d, reverse, levels):
    pos = lax.broadcasted_iota(jnp.int32, (C, d), 0) % SUBLANES
    masks = []
    for offset in range(1, 2 ** (max(levels) + 1)):
        valid = jnp.zeros((C, d), jnp.int32)
        for level in levels:
            m, blk = 2 ** level, 2 ** (level + 1)
            r = pos % blk
            key = (r + offset) if reverse else (r - offset)
            if reverse:
                ok = (r < m) & (key >= m) & (key < blk)
            else:
                ok = (r >= m) & (key >= 0) & (key < m)
            valid = jnp.where(ok, 1, valid)
        masks.append(valid)
    return masks


def _hgrn_consts(reverse, d):
    C = HGRN_CHUNK
    ii = lax.broadcasted_iota(jnp.int32, (C, C), 0)
    jj = lax.broadcasted_iota(jnp.int32, (C, C), 1)
    diff = ii ^ jj
    level_id = jnp.full((C, C), -1, jnp.int32)
    n_levels = C.bit_length() - 1
    for level in range(n_levels):
        level_id = jnp.where((diff >> level) == 1, level, level_id)
    level_id = jnp.where((ii < jj) if reverse else (ii > jj), level_id, -1)
    rows = lax.broadcasted_iota(jnp.int32, (C, d), 0)
    row_in_group = lax.broadcasted_iota(jnp.int32, (SUBLANES, d), 0)
    signs = []
    for level in range(SUBLANES.bit_length() - 1):
        second = ((rows >> level) & 1) == 1
        is_query = jnp.logical_not(second) if reverse else second
        signs.append(jnp.where(is_query, 1.0, -1.0))
    return level_id, row_in_group, signs, _near_pair_masks(C, d, reverse, HGRN_VPU_LEVELS)


def _chunk_cumsum(x, reverse, row_in_group):
    rows, d = x.shape
    groups = rows // SUBLANES
    per_chunk = HGRN_CHUNK // SUBLANES
    y = x.reshape(groups, SUBLANES, d)
    step = 1
    while step < SUBLANES:
        rolled = pltpu.roll(y, (SUBLANES - step) if reverse else step, 1)
        valid = (row_in_group < SUBLANES - step) if reverse else (row_in_group >= step)
        y = y + jnp.where(valid, rolled, 0.0)
        step *= 2
    out = [None] * groups
    for c in range(rows // HGRN_CHUNK):
        order = range(c * per_chunk, (c + 1) * per_chunk)
        carry = None
        for g in (reversed(order) if reverse else order):
            yg = y[g] if carry is None else y[g] + carry
            out[g] = yg
            edge = 0 if reverse else SUBLANES - 1
            carry = jnp.broadcast_to(yg[edge:edge + 1, :], (SUBLANES, d))
    return jnp.concatenate(out, axis=0)


def _hgrn_kernel(xn_ref, wq_ref, wzf_ref, wzb_ref, wi_ref, wg_ref, lbf_ref, lbb_ref, nw_ref, o_ref,
                 q_s, v_s, g_s, acc_s, kf_s, kb_s, lff_s, lfb_s, cum_s, *, seq, dk, heads):
    C = HGRN_CHUNK
    R = seq // C
    row_tile = HGRN_ROW_TILE
    scale = dk ** -0.5

    def tile_rows(n):
        return pl.ds(pl.multiple_of(n * row_tile, row_tile), row_tile)

    def head_cols(h):
        return slice(h * dk, (h + 1) * dk)

    dirs = [((False, lbf_ref, kf_s.at[h], lff_s.at[h], cum_s.at[h, 0]),
             (True, lbb_ref, kb_s.at[h], lfb_s.at[h], cum_s.at[h, 1]))
            for h in range(heads)]
    row_in_group = lax.broadcasted_iota(jnp.int32, (1, SUBLANES, dk), 1)

    def proj(n, carry):
        rows = tile_rows(n)
        xc = xn_ref[rows, :]
        hq_all = _dot(xc, wq_ref[...])
        z_all = (_dot(xc, wzf_ref[...]), _dot(xc, wzb_ref[...]))
        v_all = _dot(xc, wi_ref[...])
        hg_all = _dot(xc, wg_ref[...])
        for h in range(heads):
            cols = head_cols(h)
            hq = hq_all[:, cols]
            q = _silu(hq, scale)
            q_s[h, rows, :] = q
            for idx, (reverse, lb_ref, k_s, lf_s, _) in enumerate(dirs[h]):
                lf, kk = _hgrn_gate(z_all[idx][:, cols], lb_ref[:, cols])
                k_s[rows, :] = kk
                lf_s[rows, :] = lf
            v_s[h, rows, :] = v_all[:, cols]
            hg = hg_all[:, cols]
            g_s[h, rows, :] = _silu(hg)
            acc_s[h, rows, :] = jnp.zeros((row_tile, dk), F32)
        return carry

    lax.fori_loop(0, seq // row_tile, proj, 0, unroll=2)

    consts = (_hgrn_consts(False, dk), _hgrn_consts(True, dk))

    def chunk_pair(h, cf, cb, states):
        data, rows, q_dec, k_dec, decs = [], [], [], [], []
        for c, (reverse, _, k_s, lf_s, cum_c) in zip((cf, cb), dirs[h]):
            r = pl.ds(pl.multiple_of(c * C, C), C)
            q, k = q_s[h, r, :], k_s[r, :]
            cum = _chunk_cumsum(lf_s[r, :], reverse, row_in_group)
            cum_c[...] = cum
            data.append((q, k, v_s[h, r, :], cum, cum_c, 0))
            rows.append(r)
            total = cum_c[pl.ds(0 if reverse else C - 1, 1), :]
            q_dec.append((q * jnp.exp2(cum)).astype(BF16))
            k_dec.append((k * jnp.exp2(total - cum)).astype(BF16))
            decs.append(jnp.exp2(total))
        o_f, o_b = _hgrn_intra_pair(data[0], data[1], consts)
        i_f, i_b = _paired_dot_nt(q_dec[0], states[0].astype(BF16), q_dec[1], states[1].astype(BF16))
        acc_s[h, rows[0], :] += o_f + i_f
        acc_s[h, rows[1], :] += o_b + i_b
        return tuple(st * dec + _dot_tn(d[2].astype(BF16), kd)
                     for st, dec, d, kd in zip(states, decs, data, k_dec))

    def step(i, states):
        return tuple(chunk_pair(h, i, R - 1 - i, states[h]) for h in range(heads))

    zero = jnp.zeros((dk, dk), F32)
    lax.fori_loop(0, R, step, tuple((zero, zero) for _ in range(heads)))

    def finish(n, carry):
        rows = tile_rows(n)
        for h in range(heads):
            o_ref[rows, head_cols(h)] = (_rms(acc_s[h, rows, :]) * nw_ref[...]
                                         * g_s[h, rows, :]).astype(o_ref.dtype)
        return carry

    lax.fori_loop(0, seq // row_tile, finish, 0)


def hgrn_branch(xn3, w_in_b, layer, hgrn_off, lb_f, lb_b, norm_w):
    B, S, D = xn3.shape
    dk = D // HGRN_HEADS
    hp = HGRN_HEADS_PER_STEP
    wide = hp * dk
    kern = functools.partial(_hgrn_kernel, seq=S, dk=dk, heads=hp)
    vec = lambda: pltpu.VMEM((hp, S, dk), F32)

    def w_spec(group):
        first = (hgrn_off + group * D) // wide
        return pl.BlockSpec((None, D, wide), lambda b, j: (layer, 0, first + j))

    lb_spec = pl.BlockSpec((1, wide), lambda b, j: (0, j))
    return pl.pallas_call(
        kern,
        out_shape=jax.ShapeDtypeStruct((B, S, D), BF16),
        grid=(B, HGRN_HEADS // hp),
        in_specs=[pl.BlockSpec((None, S, D), lambda b, j: (b, 0, 0)),
                  w_spec(0), w_spec(1), w_spec(2), w_spec(3), w_spec(4),
                  lb_spec, lb_spec,
                  pl.BlockSpec((1, dk), lambda b, j: (0, 0))],
        out_specs=pl.BlockSpec((None, S, wide), lambda b, j: (b, 0, j)),
        scratch_shapes=[vec(), vec(), vec(), vec(), vec(), vec(), vec(), vec(),
                        pltpu.VMEM((hp, 2, HGRN_CHUNK, dk), F32)],
        compiler_params=_params("parallel", "arbitrary"),
        name="hgrn2",
    )(xn3, w_in_b, w_in_b, w_in_b, w_in_b, w_in_b,
      lb_f.reshape(1, D), lb_b.reshape(1, D), norm_w.reshape(1, dk))


def _fnet_proj_kernel(xn_ref, w_ref, cs_ref, o_ref, *, gdim):
    fu = _dot(xn_ref[...], w_ref[...]).astype(BF16)
    for g in range(FNET_GROUPS):
        t = _dot(fu[:, g * gdim:(g + 1) * gdim], cs_ref[...])
        o_ref[0, :, g * gdim:(g + 1) * gdim] = t[:, :gdim].astype(o_ref.dtype)
        o_ref[1, :, g * gdim:(g + 1) * gdim] = t[:, gdim:].astype(o_ref.dtype)


def _seq_dft_kernel(cos_ref, sin_ref, perm_ref, rhs_ref, o_ref, fold_s, mir_s, *, seq):
    n = seq
    half = n // 2
    blk = perm_ref.shape[0]
    c0 = 1.0 / math.sqrt(n)

    for part, sign in ((0, 1.0), (1, -1.0)):
        base = part * n
        for j in range(half // blk):
            own = rhs_ref[pl.ds(base + j * blk, blk), :].astype(F32)
            if j == 0:
                mirror = _dot(perm_ref[:, :blk], rhs_ref[pl.ds(base + n - blk, blk), :])
            else:
                mirror = _dot(perm_ref[...], rhs_ref[pl.ds(base + n - (j + 1) * blk, 2 * blk), :])
            fold_s[pl.ds(part * half + j * blk, blk), :] = (own + sign * mirror).astype(fold_s.dtype)

    mid = rhs_ref[pl.ds(half, BF16_ROWS), :].astype(F32)[0:1] * c0
    p_ext = _dot(cos_ref[...], fold_s[pl.ds(0, half), :])
    q = _dot(sin_ref[...], fold_s[pl.ds(half, half), :])
    row = lax.broadcasted_iota(jnp.int32, q.shape, 0)
    p = p_ext[:half] + jnp.where((row & 1) == 0, 1.0, -1.0) * mid
    o_ref[pl.ds(0, half), :] = (p - q).astype(o_ref.dtype)

    mir_s[pl.ds(0, half), :] = (p + q).astype(mir_s.dtype)
    first = jnp.where(lax.broadcasted_iota(jnp.int32, (BF16_ROWS, q.shape[1]), 0) == 0, 1.0, 0.0)
    tail = (p_ext[half:half + BF16_ROWS] + mid) * first
    mir_s[pl.ds(half, BF16_ROWS), :] = tail.astype(mir_s.dtype)
    mir_s[pl.ds(half + BF16_ROWS, blk - BF16_ROWS), :] = jnp.zeros(
        (blk - BF16_ROWS, tail.shape[1]), mir_s.dtype)
    for j in range(half // blk):
        window = mir_s[pl.ds(half - (j + 1) * blk, 2 * blk), :]
        o_ref[pl.ds(half + j * blk, blk), :] = _dot(perm_ref[...], window).astype(o_ref.dtype)


DFT_FOLD_BLOCK = 128


def _half_dft_tables(n):
    c, s = _dft_tables(n)
    half = n // 2
    cos_ext = np.zeros((half + BF16_ROWS, half))
    cos_ext[:half + 1] = c[:half + 1, :half]
    return cos_ext, s[:half, :half]


def _mirror_permutation(blk):
    p = np.zeros((blk, 2 * blk), np.float32)
    i = np.arange(blk)
    p[i, blk - i] = 1.0
    return p


def _dft_tables(n):
    idx = np.arange(n, dtype=np.int64)
    ang = 2.0 * np.pi * ((idx[:, None] * idx[None, :]) % n).astype(np.float64) / n
    s = 1.0 / math.sqrt(n)
    return np.cos(ang) * s, np.sin(ang) * s


def fourier_branch(xn3, w_in_b, layer, fu_off_blocks, tm=512):
    B, S, D = xn3.shape
    W = D
    gdim = W // FNET_GROUPS
    c_small, s_small = _dft_tables(gdim)
    cs_small = jnp.asarray(np.concatenate([c_small, s_small], axis=1), dtype=BF16)
    half = S // 2
    assert half % 2 == 0 and half % DFT_FOLD_BLOCK == 0
    cos_np, sin_np = _half_dft_tables(S)
    dft_cos = jnp.asarray(cos_np, dtype=BF16)
    dft_sin = jnp.asarray(sin_np, dtype=BF16)
    perm = jnp.asarray(_mirror_permutation(DFT_FOLD_BLOCK), dtype=BF16)
    tiles = S // tm
    rhs = pl.pallas_call(
        functools.partial(_fnet_proj_kernel, gdim=gdim),
        out_shape=jax.ShapeDtypeStruct((B, 2, S, W), BF16),
        grid=(B, tiles),
        in_specs=[pl.BlockSpec((None, tm, D), lambda b, r: (b, r, 0)),
                  pl.BlockSpec((None, D, W), lambda b, r: (layer, 0, fu_off_blocks)),
                  pl.BlockSpec((gdim, 2 * gdim), lambda b, r: (0, 0))],
        out_specs=pl.BlockSpec((None, 2, tm, W), lambda b, r: (b, 0, r, 0)),
        compiler_params=_params("parallel", "parallel"),
        name="fnet_proj",
    )(xn3, w_in_b, cs_small)
    rhs = rhs.reshape(B, 2 * S, W)
    return pl.pallas_call(
        functools.partial(_seq_dft_kernel, seq=S),
        out_shape=jax.ShapeDtypeStruct((B, S, W), BF16),
        grid=(B,),
        in_specs=[_resident((half + BF16_ROWS, half), lambda b: (0, 0)),
                  _resident((half, half), lambda b: (0, 0)),
                  pl.BlockSpec((DFT_FOLD_BLOCK, 2 * DFT_FOLD_BLOCK), lambda b: (0, 0)),
                  pl.BlockSpec((None, 2 * S, W), lambda b: (b, 0, 0))],
        out_specs=pl.BlockSpec((None, S, W), lambda b: (b, 0, 0)),
        scratch_shapes=[pltpu.VMEM((S, W), BF16),
                        pltpu.VMEM((half + DFT_FOLD_BLOCK, W), BF16)],
        compiler_params=_params("parallel"),
        name="fnet_seq_dft",
    )(dft_cos, dft_sin, perm, rhs)


def _merge_kernel(x_ref, xn_ref, ro_ref, ho_ref, fo_ref, wga_ref, wro_ref, who_ref,
                  wf_ref, wout_ref, nw_ref, o_ref, *, d):
    xn = xn_ref[...]

    def gate(i):
        return jax.nn.sigmoid(_dot(xn, wga_ref[:, i * d:(i + 1) * d]))

    mix = gate(0) * _dot(ro_ref[...], wro_ref[...])
    mix += gate(1) * _dot(ho_ref[...], who_ref[...])
    mix += gate(2) * _dot(fo_ref[...], wf_ref[...])
    y = _dot(mix.astype(BF16), wout_ref[...])
    o_ref[...] = x_ref[...] + _rms(y) * nw_ref[...]


def merge_branches(x2, xn2, ro2, ho2, fo2, w_in_b, layer, ga_off_blocks,
                   w_ret_o, w_hgrn_o, w_fnet, w_out, norm_w, tm=512):
    T, D = x2.shape
    RV = ro2.shape[1]
    tile = lambda w: pl.BlockSpec((tm, w), lambda i: (i, 0))
    return pl.pallas_call(
        functools.partial(_merge_kernel, d=D),
        out_shape=jax.ShapeDtypeStruct((T, D), F32),
        grid=(T // tm,),
        in_specs=[tile(D), tile(D), tile(RV), tile(D), tile(D),
                  _resident((None, D, N_BRANCH * D), lambda i: (layer, 0, ga_off_blocks)),
                  _resident((None, RV, D), lambda i: (layer, 0, 0)),
                  _resident((None, D, D), lambda i: (layer, 0, 0)),
                  _resident((None, D, D), lambda i: (layer, 0, 0)),
                  _resident((None, D, D), lambda i: (layer, 0, 0)),
                  pl.BlockSpec((1, D), lambda i: (0, 0))],
        out_specs=tile(D),
        compiler_params=_params("parallel"),
        name="merge",
    )(x2, xn2, ro2, ho2, fo2, w_in_b, w_ret_o, w_hgrn_o, w_fnet, w_out, norm_w.reshape(1, D))


def _ffn_kernel(x_ref, xp_ref, xnx_ref, nw_in_ref, wup_ref, cw_ref, cb_ref, wdn_ref, nw_out_ref,
                *rest, tm, tiles_per_seq, d_ff, fc, emit_next):
    if emit_next:
        nw_next_ref, o_ref, xn_ref, hn_s, acc_s = rest
    else:
        (o_ref, hn_s, acc_s), nw_next_ref, xn_ref = rest, None, None
    i = pl.program_id(0)
    r = i % tiles_per_seq
    halo = BF16_ROWS
    x = x_ref[...]
    nw = nw_in_ref[...]
    hp = jnp.where(r == 0, 0.0, _rms(xp_ref[...]) * nw)
    hx = jnp.where(r == tiles_per_seq - 1, 0.0, _rms(xnx_ref[...]) * nw)
    hn = jnp.concatenate([hp, _rms(x) * nw, hx], axis=0).astype(BF16)
    n_ext = tm + 2 * halo
    hn_s[...] = hn
    acc_s[...] = jnp.zeros_like(acc_s)

    def conv(col, scale):
        cols = pl.ds(pl.multiple_of(col, fc), fc)
        h = _dot(hn_s[...], wup_ref[:, cols])
        cw = cw_ref[:, cols] * scale
        prev = pltpu.roll(h, 1, 0)[halo:halo + tm]
        nxt = pltpu.roll(h, n_ext - 1, 0)[halo:halo + tm]
        return (cb_ref[:, cols] * scale + prev * cw[0:1] + h[halo:halo + tm] * cw[1:2]
                + nxt * cw[2:3])

    def chunk(c, carry):
        gate = conv(c * fc, 1.0)
        half_up = conv(d_ff + c * fc, 0.5)
        inner = gate * (GELU_C0 + GELU_C1 * (gate * gate))
        act = (gate * (1.0 + jnp.tanh(inner)) * half_up).astype(BF16)
        acc_s[...] += _dot(act, wdn_ref[pl.ds(pl.multiple_of(c * fc, fc), fc), :])
        return carry

    for c in range(d_ff // fc):
        chunk(c, 0)
    y = x + _rms(acc_s[...]) * nw_out_ref[...]
    o_ref[...] = y
    if xn_ref is not None:
        xn_ref[...] = (_rms(y) * nw_next_ref[...]).astype(xn_ref.dtype)


def conv_ffn_block(x2, seq, w_up, conv_w, conv_b, w_down, nw_in, nw_out, layer, nw_next=None,
                   tm=1024, fc=256):
    T, D = x2.shape
    d_ff = w_down.shape[1]
    halo = BF16_ROWS
    tps = seq // tm
    hb = tm // halo
    n_hb = T // halo
    assert seq % tm == 0, "a row tile must not straddle two sequences (the conv zero-pads each)"
    emit_next = nw_next is not None
    kern = functools.partial(_ffn_kernel, tm=tm, tiles_per_seq=tps, d_ff=d_ff, fc=fc,
                             emit_next=emit_next)
    vec = pl.BlockSpec((1, D), lambda i: (0, 0))
    tile = pl.BlockSpec((tm, D), lambda i: (i, 0))
    in_specs = [tile,
                pl.BlockSpec((halo, D), lambda i: (jnp.maximum(i * hb - 1, 0), 0)),
                pl.BlockSpec((halo, D), lambda i: (jnp.minimum((i + 1) * hb, n_hb - 1), 0)),
                vec,
                _resident((None, D, 2 * d_ff), lambda i: (layer, 0, 0)),
                pl.BlockSpec((None, CONV_W, 2 * d_ff), lambda i: (layer, 0, 0)),
                pl.BlockSpec((None, 1, 2 * d_ff), lambda i: (layer, 0, 0)),
                _resident((None, d_ff, D), lambda i: (layer, 0, 0)),
                vec]
    args = [x2, x2, x2, nw_in.reshape(1, D), w_up, conv_w, conv_b, w_down, nw_out.reshape(1, D)]
    out_shape = jax.ShapeDtypeStruct((T, D), F32)
    out_specs = tile
    if emit_next:
        in_specs.append(vec)
        args.append(nw_next.reshape(1, D))
        out_shape = (out_shape, jax.ShapeDtypeStruct((T, D), BF16))
        out_specs = (tile, tile)
    return pl.pallas_call(
        kern,
        out_shape=out_shape,
        grid=(T // tm,),
        in_specs=in_specs,
        out_specs=out_specs,
        scratch_shapes=[pltpu.VMEM((tm + 2 * halo, D), BF16),
                        pltpu.VMEM((tm, D), F32)],
        compiler_params=_params("parallel"),
        name="conv_ffn",
    )(*args)


def kernel(x, positions, norm_w, w_in, hgrn_lb_logits, hgrn_norm_w, w_ret_o, w_hgrn_o,
           w_fnet, w_out, w_up, conv_w, conv_b, w_down):
    B, S, D = x.shape
    depth = w_in.shape[0]
    T = B * S

    hgrn_off = 2 * D + 2 * 2 * D
    fu_off = hgrn_off + 5 * D
    ga_off = fu_off + D

    w_in_b = w_in.astype(BF16)
    w_ret_o_b = w_ret_o.astype(BF16)
    w_hgrn_o_b = w_hgrn_o.astype(BF16)
    w_fnet_b = w_fnet.astype(BF16)
    w_out_b = w_out.astype(BF16)
    w_up_b = w_up.astype(BF16)
    w_down_b = w_down.astype(BF16)
    conv_b3 = conv_b.reshape(depth, 1, -1)

    log_gamma = jnp.log(1.0 - 2.0 ** (-5.0 - jnp.arange(RET_HEADS, dtype=F32)))
    p = jax.nn.softmax(hgrn_lb_logits.astype(F32), axis=1)
    lower_bounds = jnp.cumsum(p, axis=1) - p[:, :1]

    cos, sin, xn3 = rope_tables_and_norm(positions, D // RET_HEADS // 2, x, norm_w[0, 0])

    x2 = x.reshape(T, D)
    xn2 = xn3.reshape(T, D)
    for l in range(depth):
        xn3 = xn2.reshape(B, S, D)
        ro = retention_branch(xn3, w_in_b, l, cos, sin, log_gamma)
        ho = hgrn_branch(xn3, w_in_b, l, hgrn_off, lower_bounds[0, l], lower_bounds[1, l],
                         hgrn_norm_w[l])
        fo = fourier_branch(xn3, w_in_b, l, fu_off // D)
        x2 = merge_branches(x2, xn2, ro.reshape(T, -1), ho.reshape(T, D), fo.reshape(T, D),
                            w_in_b, l, ga_off // (N_BRANCH * D),
                            w_ret_o_b, w_hgrn_o_b, w_fnet_b, w_out_b, norm_w[l, 1])
        if l + 1 < depth:
            x2, xn2 = conv_ffn_block(x2, S, w_up_b, conv_w, conv_b3, w_down_b,
                                     norm_w[l, 2], norm_w[l, 3], l, nw_next=norm_w[l + 1, 0])
        else:
            x2 = conv_ffn_block(x2, S, w_up_b, conv_w, conv_b3, w_down_b,
                                norm_w[l, 2], norm_w[l, 3], l)
    return x2.reshape(B, S, D)
```

```python
import functools
import math

import numpy as np
import jax
import jax.numpy as jnp
from jax import lax
from jax.experimental import pallas as pl
from jax.experimental.pallas import tpu as pltpu

F32 = jnp.float32
BF16 = jnp.bfloat16

RET_HEADS = 4
HGRN_HEADS = 8
FNET_GROUPS = 4
N_BRANCH = 3
CONV_W = 3
ROPE_BASE = 10000.0
LB_FLOOR = 1e-30
EPS = 1e-6
LOG2_E = 1.4426950408889634
GELU_C0 = math.sqrt(2.0 / math.pi)
GELU_C1 = GELU_C0 * 0.044715

V7X_VMEM_LIMIT_BYTES = 56 * 1024 * 1024
SUBLANES = 8
BF16_ROWS = 16

RET_CHUNK = 256
HGRN_CHUNK = 128
HGRN_HEADS_PER_STEP = 2
HGRN_VPU_LEVELS = (0,)
ROW_TILE = 512
HGRN_ROW_TILE = 1024


def _dot(a, b):
    return jnp.dot(a, b, preferred_element_type=F32)


def _dot_nt(a, b):
    return lax.dot_general(a, b, (((1,), (1,)), ((), ())), preferred_element_type=F32)


def _dot_tn(a, b):
    return lax.dot_general(a, b, (((0,), (0,)), ((), ())), preferred_element_type=F32)


def _silu(x, scale=1.0):
    return (x * scale if scale != 1.0 else x) / (1.0 + jnp.exp2(x * (-LOG2_E)))


def _rms(x):
    return x * lax.rsqrt(jnp.mean(x * x, axis=-1, keepdims=True) + EPS)


def _params(*sem):
    return pltpu.CompilerParams(dimension_semantics=sem,
                                vmem_limit_bytes=V7X_VMEM_LIMIT_BYTES)


def _resident(shape, index_map):
    return pl.BlockSpec(shape, index_map, pipeline_mode=pl.Buffered(1))


def _rope_norm_kernel(pos_ref, invf_ref, x_ref, w_ref, cos_ref, sin_ref, xn_ref):
    ang = pos_ref[...] * invf_ref[...]
    cos_ref[...] = jnp.cos(ang)
    sin_ref[...] = jnp.sin(ang)
    xn_ref[...] = (_rms(x_ref[...]) * w_ref[...]).astype(xn_ref.dtype)


def rope_tables_and_norm(positions, half, x, w):
    B, S = positions.shape
    D = x.shape[-1]
    pos = positions.astype(F32).reshape(B, S, 1)
    inv_freq = (ROPE_BASE ** (-jnp.arange(half, dtype=F32) / half)).reshape(1, half)
    table = jax.ShapeDtypeStruct((B, S, half), F32)
    table_spec = pl.BlockSpec((None, S, half), lambda b: (b, 0, 0))
    rows_spec = pl.BlockSpec((None, S, D), lambda b: (b, 0, 0))
    return pl.pallas_call(
        _rope_norm_kernel,
        out_shape=(table, table, jax.ShapeDtypeStruct((B, S, D), BF16)),
        grid=(B,),
        in_specs=[pl.BlockSpec((None, S, 1), lambda b: (b, 0, 0)),
                  pl.BlockSpec((1, half), lambda b: (0, 0)),
                  rows_spec,
                  pl.BlockSpec((1, D), lambda b: (0, 0))],
        out_specs=(table_spec, table_spec, rows_spec),
        compiler_params=_params("parallel"),
        name="rope_tables_norm",
    )(pos, inv_freq, x, w.reshape(1, D))


def _ret_kernel(lg_ref, xn_ref, wq_ref, wk_ref, wv_ref, wg_ref, cos_ref, sin_ref,
                o_ref, qi_s, qd_s, ki_s, v_s, g_s, st_s, kvb_s, run_s, *, seq, dk, dv):
    C = RET_CHUNK
    R = seq // C
    half = dk // 2
    lg = lg_ref[pl.program_id(1)]
    ret_scale = dk ** -0.5

    def rows_of(n):
        return pl.ds(pl.multiple_of(n * C, C), C)

    pos = lax.broadcasted_iota(jnp.int32, (C, 1), 0).astype(F32)
    qdec_f = jnp.exp(lg * (pos + 1.0))
    qdec_b = jnp.exp(lg * (C - pos))
    kdec_f = jnp.exp(lg * (C - 1.0 - pos))
    kdec_b = jnp.exp(lg * pos)
    chunk_dec = jnp.exp(lg * C)
    ii = lax.broadcasted_iota(jnp.int32, (C, C), 0)
    jj = lax.broadcasted_iota(jnp.int32, (C, C), 1)
    decay = jnp.exp(lg * jnp.abs(ii - jj).astype(F32))

    run_s[...] = jnp.zeros_like(run_s)

    def proj(t, carry):
        rows = pl.ds(pl.multiple_of(t * ROW_TILE, ROW_TILE), ROW_TILE)
        xc = xn_ref[rows, :]
        cos = cos_ref[rows, :]
        sin = sin_ref[rows, :]
        q = _dot(xc, wq_ref[...])
        q1, q2 = q[:, :half], q[:, half:]
        q = jnp.concatenate([q1 * cos - q2 * sin, q1 * sin + q2 * cos], axis=-1)
        k = _dot(xc, wk_ref[...]) * ret_scale
        k1, k2 = k[:, :half], k[:, half:]
        k = jnp.concatenate([k1 * cos - k2 * sin, k1 * sin + k2 * cos], axis=-1)
        v = _dot(xc, wv_ref[...]).astype(BF16)
        g = _dot(xc, wg_ref[...])
        qi_s[rows, :] = q.astype(BF16)
        ki_s[rows, :] = k.astype(BF16)
        v_s[rows, :] = v
        g_s[rows, :] = _silu(g).astype(BF16)
        for j in range(ROW_TILE // C):
            n = t * (ROW_TILE // C) + j
            sl = slice(j * C, (j + 1) * C)
            qd_s[rows_of(n), :] = jnp.concatenate([q[sl] * qdec_f, q[sl] * qdec_b],
                                                  axis=-1).astype(BF16)
            st_s[n, pl.ds(0, dk), :] = run_s[...].astype(BF16)
            run_s[...] = run_s[...] * chunk_dec + _dot_tn((k[sl] * kdec_f).astype(BF16), v[sl])
            kvb_s[n] = _dot_tn((k[sl] * kdec_b).astype(BF16), v[sl])
        return carry

    lax.fori_loop(0, seq // ROW_TILE, proj, 0, unroll=2)

    run_s[...] = jnp.zeros_like(run_s)

    def out(t, carry):
        n = R - 1 - t
        rows = rows_of(n)
        st_s[n, pl.ds(dk, dk), :] = run_s[...].astype(BF16)
        s = _dot_nt(qi_s[rows, :], ki_s[rows, :]) * decay
        o = _dot(s.astype(BF16), v_s[rows, :]) + _dot(qd_s[rows, :], st_s[n])
        o_ref[rows, :] = (_rms(o) * g_s[rows, :].astype(F32)).astype(o_ref.dtype)
        run_s[...] = run_s[...] * chunk_dec + kvb_s[n]
        return carry

    lax.fori_loop(0, R, out, 0, unroll=4)


def retention_branch(xn3, w_in_b, layer, cos, sin, log_gamma):
    B, S, D = xn3.shape
    dk = D // RET_HEADS
    dv = 2 * dk
    H = RET_HEADS
    qk_blocks = D // dk
    v_off = 2 * D // dv
    g_off = v_off + H
    kern = functools.partial(_ret_kernel, seq=S, dk=dk, dv=dv)
    return pl.pallas_call(
        kern,
        out_shape=jax.ShapeDtypeStruct((B, S, H * dv), BF16),
        grid=(B, H),
        in_specs=[
            pl.BlockSpec(memory_space=pltpu.SMEM),
            pl.BlockSpec((None, S, D), lambda b, h: (b, 0, 0)),
            pl.BlockSpec((None, D, dk), lambda b, h: (layer, 0, h)),
            pl.BlockSpec((None, D, dk), lambda b, h: (layer, 0, qk_blocks + h)),
            pl.BlockSpec((None, D, dv), lambda b, h: (layer, 0, v_off + h)),
            pl.BlockSpec((None, D, dv), lambda b, h: (layer, 0, g_off + h)),
            pl.BlockSpec((None, S, dk // 2), lambda b, h: (b, 0, 0)),
            pl.BlockSpec((None, S, dk // 2), lambda b, h: (b, 0, 0)),
        ],
        out_specs=pl.BlockSpec((None, S, dv), lambda b, h: (b, 0, h)),
        scratch_shapes=[
            pltpu.VMEM((S, dk), BF16),
            pltpu.VMEM((S, 2 * dk), BF16),
            pltpu.VMEM((S, dk), BF16),
            pltpu.VMEM((S, dv), BF16),
            pltpu.VMEM((S, dv), BF16),
            pltpu.VMEM((S // RET_CHUNK, 2 * dk, dv), BF16),
            pltpu.VMEM((S // RET_CHUNK, dk, dv), F32),
            pltpu.VMEM((dk, dv), F32),
        ],
        compiler_params=_params("parallel", "arbitrary"),
        name="retention",
    )(log_gamma, xn3, w_in_b, w_in_b, w_in_b, w_in_b, cos, sin)


def _hgrn_gate(z, lb):
    e = jnp.exp2(jnp.abs(z) * (-LOG2_E))
    pos = z >= 0.0
    sig_neg_num = jnp.where(pos, e, 1.0)
    num = jnp.where(pos, 1.0, e) + jnp.maximum(lb, LB_FLOOR) * sig_neg_num
    inv = 1.0 / (1.0 + e)
    log2_f = jnp.log2(num * inv)
    return log2_f, (1.0 - lb) * sig_neg_num * inv


def _boundary_rows(cum_ref, base, m, reverse, row_in_group):
    C = HGRN_CHUNK
    d = cum_ref.shape[1]
    blk = 2 * m
    off = m if reverse else m - 1
    pieces = []
    if blk >= SUBLANES:
        for b in range(C // blk):
            pieces.append(jnp.broadcast_to(cum_ref[pl.ds(base + (b * blk + off), 1), :], (blk, d)))
    else:
        for g in range(C // SUBLANES):
            val = None
            for u in range(SUBLANES // blk):
                row = g * SUBLANES + u * blk + off
                piece = jnp.broadcast_to(cum_ref[pl.ds(base + row, 1), :], (SUBLANES, d))
                val = piece if val is None else jnp.where(row_in_group >= u * blk, piece, val)
            pieces.append(val)
    return jnp.concatenate(pieces, axis=0) if len(pieces) > 1 else pieces[0]


def _level_operands(level, q, k, cum, cum_ref, base, consts, reverse):
    C = HGRN_CHUNK
    _, row_in_group, signs, _ = consts
    m = 2 ** level
    if 2 * m <= SUBLANES:
        sign = signs[level]
        ref_pt = _boundary_rows(cum_ref, base, m, reverse, row_in_group)
        x = (jnp.where(sign > 0.0, q, k) * jnp.exp2((cum - ref_pt) * sign)).astype(BF16)
        return x, x, list(range(C // SUBLANES))
    xq, xall, q_groups = [], [], []
    for b in range(C // (2 * m)):
        first = slice(b * 2 * m, b * 2 * m + m)
        second = slice(b * 2 * m + m, (b + 1) * 2 * m)
        q_rows, k_rows = (first, second) if reverse else (second, first)
        edge = k_rows.start if reverse else k_rows.stop - 1
        ref_pt = cum_ref[pl.ds(base + edge, 1), :]
        xq_b = q[q_rows] * jnp.exp2(cum[q_rows] - ref_pt)
        xk_b = k[k_rows] * jnp.exp2(ref_pt - cum[k_rows])
        xq.append(xq_b)
        xall.extend([xq_b, xk_b] if reverse else [xk_b, xq_b])
        q_groups.extend(range(q_rows.start // SUBLANES, q_rows.stop // SUBLANES))
    return (jnp.concatenate(xq, axis=0).astype(BF16), jnp.concatenate(xall, axis=0).astype(BF16),
            q_groups)


def _paired_dot_nt(lhs_a, rhs_a, lhs_b, rhs_b):
    rhs = jnp.concatenate([rhs_a, rhs_b], axis=1)
    lhs = jnp.concatenate(
        [jnp.concatenate([lhs_a, jnp.zeros_like(lhs_a)], axis=1),
         jnp.concatenate([jnp.zeros_like(lhs_b), lhs_b], axis=1)], axis=0)
    s = _dot_nt(lhs, rhs)
    return s[:lhs_a.shape[0]], s[lhs_a.shape[0]:]


def _assemble_scores(scores, level_id):
    C = HGRN_CHUNK
    rows = [jnp.zeros((SUBLANES, C), F32) for _ in range(C // SUBLANES)]
    for level, entry in enumerate(scores):
        if entry is None:
            continue
        s, q_groups = entry
        for i, g in enumerate(q_groups):
            lid = level_id[g * SUBLANES:(g + 1) * SUBLANES]
            rows[g] = jnp.where(lid == level, s[i * SUBLANES:(i + 1) * SUBLANES], rows[g])
    return jnp.concatenate(rows, axis=0)


def _hgrn_intra_pair(fwd, bwd, consts):
    C = HGRN_CHUNK
    args = ((fwd, consts[0], False), (bwd, consts[1], True))
    n_vpu = len(HGRN_VPU_LEVELS)
    scores = ([None] * n_vpu, [None] * n_vpu)
    for level in range(n_vpu, C.bit_length() - 1):
        ops = [_level_operands(level, q, k, cum, cum_ref, base, cst, rev)
               for (q, k, _, cum, cum_ref, base), cst, rev in args]
        s_f, s_b = _paired_dot_nt(ops[0][0], ops[0][1], ops[1][0], ops[1][1])
        scores[0].append((s_f, ops[0][2]))
        scores[1].append((s_b, ops[1][2]))
    outs = []
    for idx, ((q, k, v, cum, _, _), cst, rev) in enumerate(args):
        attn = _assemble_scores(scores[idx], cst[0])
        o = _dot(attn.astype(BF16), v.astype(BF16))
        o += jnp.sum(q * k, axis=-1, keepdims=True) * v
        outs.append(o + _near_pairs(q, k, v, cum, rev, cst[3]))
    return outs


def _near_pairs(q, k, v, cum, reverse, masks):
    C, d = q.shape
    shape3 = (C // SUBLANES, SUBLANES, d)
    out = jnp.zeros((C, d), F32)
    for offset, mask in enumerate(masks, start=1):
        valid = mask != 0
        shift = (SUBLANES - offset) if reverse else offset

        def key_row(x):
            return pltpu.roll(x.reshape(shape3), shift, 1).reshape(C, d)

        w = jnp.exp2(jnp.where(valid, cum - key_row(cum), 0.0))
        score = jnp.sum(q * key_row(k) * w, axis=-1, keepdims=True)
        out += jnp.where(valid, score * key_row(v), 0.0)
    return out


def _near_pair_masks(C, d, reverse, levels):
    pos = lax.broadcasted_iota(jnp.int32, (C, d), 0) % SUBLANES
    masks = []
    for offset in range(1, 2 ** (max(levels) + 1)):
        valid = jnp.zeros((C, d), jnp.int32)
        for level in levels:
            m, blk = 2 ** level, 2 ** (level + 1)
            r = pos % blk
            key = (r + offset) if reverse else (r - offset)
            if reverse:
                ok = (r < m) & (key >= m) & (key < blk)
            else:
                ok = (r >= m) & (key >= 0) & (key < m)
            valid = jnp.where(ok, 1, valid)
        masks.append(valid)
    return masks


def _hgrn_consts(reverse, d):
    C = HGRN_CHUNK
    ii = lax.broadcasted_iota(jnp.int32, (C, C), 0)
    jj = lax.broadcasted_iota(jnp.int32, (C, C), 1)
    diff = ii ^ jj
    level_id = jnp.full((C, C), -1, jnp.int32)
    n_levels = C.bit_length() - 1
    for level in range(n_levels):
        level_id = jnp.where((diff >> level) == 1, level, level_id)
    level_id = jnp.where((ii < jj) if reverse else (ii > jj), level_id, -1)
    rows = lax.broadcasted_iota(jnp.int32, (C, d), 0)
    row_in_group = lax.broadcasted_iota(jnp.int32, (SUBLANES, d), 0)
    signs = []
    for level in range(SUBLANES.bit_length() - 1):
        second = ((rows >> level) & 1) == 1
        is_query = jnp.logical_not(second) if reverse else second
        signs.append(jnp.where(is_query, 1.0, -1.0))
    return level_id, row_in_group, signs, _near_pair_masks(C, d, reverse, HGRN_VPU_LEVELS)


def _chunk_cumsum(x, reverse, row_in_group):
    rows, d = x.shape
    groups = rows // SUBLANES
    per_chunk = HGRN_CHUNK // SUBLANES
    y = x.reshape(groups, SUBLANES, d)
    step = 1
    while step < SUBLANES:
        rolled = pltpu.roll(y, (SUBLANES - step) if reverse else step, 1)
        valid = (row_in_group < SUBLANES - step) if reverse else (row_in_group >= step)
        y = y + jnp.where(valid, rolled, 0.0)
        step *= 2
    out = [None] * groups
    for c in range(rows // HGRN_CHUNK):
        order = range(c * per_chunk, (c + 1) * per_chunk)
        carry = None
        for g in (reversed(order) if reverse else order):
            yg = y[g] if carry is None else y[g] + carry
            out[g] = yg
            edge = 0 if reverse else SUBLANES - 1
            carry = jnp.broadcast_to(yg[edge:edge + 1, :], (SUBLANES, d))
    return jnp.concatenate(out, axis=0)


def _hgrn_kernel(xn_ref, wq_ref, wzf_ref, wzb_ref, wi_ref, wg_ref, lbf_ref, lbb_ref, nw_ref, o_ref,
                 q_s, v_s, g_s, acc_s, kf_s, kb_s, lff_s, lfb_s, cum_s, *, seq, dk, heads):
    C = HGRN_CHUNK
    R = seq // C
    row_tile = HGRN_ROW_TILE
    scale = dk ** -0.5

    def tile_rows(n):
        return pl.ds(pl.multiple_of(n * row_tile, row_tile), row_tile)

    def head_cols(h):
        return slice(h * dk, (h + 1) * dk)

    dirs = [((False, lbf_ref, kf_s.at[h], lff_s.at[h], cum_s.at[h, 0]),
             (True, lbb_ref, kb_s.at[h], lfb_s.at[h], cum_s.at[h, 1]))
            for h in range(heads)]
    row_in_group = lax.broadcasted_iota(jnp.int32, (1, SUBLANES, dk), 1)

    def proj(n, carry):
        rows = tile_rows(n)
        xc = xn_ref[rows, :]
        hq_all = _dot(xc, wq_ref[...])
        z_all = (_dot(xc, wzf_ref[...]), _dot(xc, wzb_ref[...]))
        v_all = _dot(xc, wi_ref[...])
        hg_all = _dot(xc, wg_ref[...])
        for h in range(heads):
            cols = head_cols(h)
            hq = hq_all[:, cols]
            q = _silu(hq, scale)
            q_s[h, rows, :] = q
            for idx, (reverse, lb_ref, k_s, lf_s, _) in enumerate(dirs[h]):
                lf, kk = _hgrn_gate(z_all[idx][:, cols], lb_ref[:, cols])
                k_s[rows, :] = kk
                lf_s[rows, :] = lf
            v_s[h, rows, :] = v_all[:, cols]
            hg = hg_all[:, cols]
            g_s[h, rows, :] = _silu(hg)
            acc_s[h, rows, :] = jnp.zeros((row_tile, dk), F32)
        return carry

    lax.fori_loop(0, seq // row_tile, proj, 0, unroll=2)

    consts = (_hgrn_consts(False, dk), _hgrn_consts(True, dk))

    def chunk_pair(h, cf, cb, states):
        data, rows, q_dec, k_dec, decs = [], [], [], [], []
        for c, (reverse, _, k_s, lf_s, cum_c) in zip((cf, cb), dirs[h]):
            r = pl.ds(pl.multiple_of(c * C, C), C)
            q, k = q_s[h, r, :], k_s[r, :]
            cum = _chunk_cumsum(lf_s[r, :], reverse, row_in_group)
            cum_c[...] = cum
            data.append((q, k, v_s[h, r, :], cum, cum_c, 0))
            rows.append(r)
            total = cum_c[pl.ds(0 if reverse else C - 1, 1), :]
            q_dec.append((q * jnp.exp2(cum)).astype(BF16))
            k_dec.append((k * jnp.exp2(total - cum)).astype(BF16))
            decs.append(jnp.exp2(total))
        o_f, o_b = _hgrn_intra_pair(data[0], data[1], consts)
        i_f, i_b = _paired_dot_nt(q_dec[0], states[0].astype(BF16), q_dec[1], states[1].astype(BF16))
        acc_s[h, rows[0], :] += o_f + i_f
        acc_s[h, rows[1], :] += o_b + i_b
        return tuple(st * dec + _dot_tn(d[2].astype(BF16), kd)
                     for st, dec, d, kd in zip(states, decs, data, k_dec))

    def step(i, states):
        return tuple(chunk_pair(h, i, R - 1 - i, states[h]) for h in range(heads))

    zero = jnp.zeros((dk, dk), F32)
    lax.fori_loop(0, R, step, tuple((zero, zero) for _ in range(heads)))

    def finish(n, carry):
        rows = tile_rows(n)
        for h in range(heads):
            o_ref[rows, head_cols(h)] = (_rms(acc_s[h, rows, :]) * nw_ref[...]
                                         * g_s[h, rows, :]).astype(o_ref.dtype)
        return carry

    lax.fori_loop(0, seq // row_tile, finish, 0)


def hgrn_branch(xn3, w_in_b, layer, hgrn_off, lb_f, lb_b, norm_w):
    B, S, D = xn3.shape
    dk = D // HGRN_HEADS
    hp = HGRN_HEADS_PER_STEP
    wide = hp * dk
    kern = functools.partial(_hgrn_kernel, seq=S, dk=dk, heads=hp)
    vec = lambda: pltpu.VMEM((hp, S, dk), F32)

    def w_spec(group):
        first = (hgrn_off + group * D) // wide
        return pl.BlockSpec((None, D, wide), lambda b, j: (layer, 0, first + j))

    lb_spec = pl.BlockSpec((1, wide), lambda b, j: (0, j))
    return pl.pallas_call(
        kern,
        out_shape=jax.ShapeDtypeStruct((B, S, D), BF16),
        grid=(B, HGRN_HEADS // hp),
        in_specs=[pl.BlockSpec((None, S, D), lambda b, j: (b, 0, 0)),
                  w_spec(0), w_spec(1), w_spec(2), w_spec(3), w_spec(4),
                  lb_spec, lb_spec,
                  pl.BlockSpec((1, dk), lambda b, j: (0, 0))],
        out_specs=pl.BlockSpec((None, S, wide), lambda b, j: (b, 0, j)),
        scratch_shapes=[vec(), vec(), vec(), vec(), vec(), vec(), vec(), vec(),
                        pltpu.VMEM((hp, 2, HGRN_CHUNK, dk), F32)],
        compiler_params=_params("parallel", "arbitrary"),
        name="hgrn2",
    )(xn3, w_in_b, w_in_b, w_in_b, w_in_b, w_in_b,
      lb_f.reshape(1, D), lb_b.reshape(1, D), norm_w.reshape(1, dk))


def _fnet_proj_kernel(xn_ref, w_ref, cs_ref, o_ref, *, gdim):
    fu = _dot(xn_ref[...], w_ref[...]).astype(BF16)
    for g in range(FNET_GROUPS):
        t = _dot(fu[:, g * gdim:(g + 1) * gdim], cs_ref[...])
        o_ref[0, :, g * gdim:(g + 1) * gdim] = t[:, :gdim].astype(o_ref.dtype)
        o_ref[1, :, g * gdim:(g + 1) * gdim] = t[:, gdim:].astype(o_ref.dtype)


def _seq_dft_kernel(cos_ref, sin_ref, perm_ref, rhs_ref, o_ref, fold_s, mir_s, *, seq):
    n = seq
    half = n // 2
    blk = perm_ref.shape[0]
    c0 = 1.0 / math.sqrt(n)

    for part, sign in ((0, 1.0), (1, -1.0)):
        base = part * n
        for j in range(half // blk):
            own = rhs_ref[pl.ds(base + j * blk, blk), :].astype(F32)
            if j == 0:
                mirror = _dot(perm_ref[:, :blk], rhs_ref[pl.ds(base + n - blk, blk), :])
            else:
                mirror = _dot(perm_ref[...], rhs_ref[pl.ds(base + n - (j + 1) * blk, 2 * blk), :])
            fold_s[pl.ds(part * half + j * blk, blk), :] = (own + sign * mirror).astype(fold_s.dtype)

    mid = rhs_ref[pl.ds(half, BF16_ROWS), :].astype(F32)[0:1] * c0
    p_ext = _dot(cos_ref[...], fold_s[pl.ds(0, half), :])
    q = _dot(sin_ref[...], fold_s[pl.ds(half, half), :])
    row = lax.broadcasted_iota(jnp.int32, q.shape, 0)
    p = p_ext[:half] + jnp.where((row & 1) == 0, 1.0, -1.0) * mid
    o_ref[pl.ds(0, half), :] = (p - q).astype(o_ref.dtype)

    mir_s[pl.ds(0, half), :] = (p + q).astype(mir_s.dtype)
    first = jnp.where(lax.broadcasted_iota(jnp.int32, (BF16_ROWS, q.shape[1]), 0) == 0, 1.0, 0.0)
    tail = (p_ext[half:half + BF16_ROWS] + mid) * first
    mir_s[pl.ds(half, BF16_ROWS), :] = tail.astype(mir_s.dtype)
    mir_s[pl.ds(half + BF16_ROWS, blk - BF16_ROWS), :] = jnp.zeros(
        (blk - BF16_ROWS, tail.shape[1]), mir_s.dtype)
    for j in range(half // blk):
        window = mir_s[pl.ds(half - (j + 1) * blk, 2 * blk), :]
        o_ref[pl.ds(half + j * blk, blk), :] = _dot(perm_ref[...], window).astype(o_ref.dtype)


DFT_FOLD_BLOCK = 128


def _half_dft_tables(n):
    c, s = _dft_tables(n)
    half = n // 2
    cos_ext = np.zeros((half + BF16_ROWS, half))
    cos_ext[:half + 1] = c[:half + 1, :half]
    return cos_ext, s[:half, :half]


def _mirror_permutation(blk):
    p = np.zeros((blk, 2 * blk), np.float32)
    i = np.arange(blk)
    p[i, blk - i] = 1.0
    return p


def _dft_tables(n):
    idx = np.arange(n, dtype=np.int64)
    ang = 2.0 * np.pi * ((idx[:, None] * idx[None, :]) % n).astype(np.float64) / n
    s = 1.0 / math.sqrt(n)
    return np.cos(ang) * s, np.sin(ang) * s


def fourier_branch(xn3, w_in_b, layer, fu_off_blocks, tm=512):
    B, S, D = xn3.shape
    W = D
    gdim = W // FNET_GROUPS
    c_small, s_small = _dft_tables(gdim)
    cs_small = jnp.asarray(np.concatenate([c_small, s_small], axis=1), dtype=BF16)
    half = S // 2
    assert half % 2 == 0 and half % DFT_FOLD_BLOCK == 0
    cos_np, sin_np = _half_dft_tables(S)
    dft_cos = jnp.asarray(cos_np, dtype=BF16)
    dft_sin = jnp.asarray(sin_np, dtype=BF16)
    perm = jnp.asarray(_mirror_permutation(DFT_FOLD_BLOCK), dtype=BF16)
    tiles = S // tm
    rhs = pl.pallas_call(
        functools.partial(_fnet_proj_kernel, gdim=gdim),
        out_shape=jax.ShapeDtypeStruct((B, 2, S, W), BF16),
        grid=(B, tiles),
        in_specs=[pl.BlockSpec((None, tm, D), lambda b, r: (b, r, 0)),
                  pl.BlockSpec((None, D, W), lambda b, r: (layer, 0, fu_off_blocks)),
                  pl.BlockSpec((gdim, 2 * gdim), lambda b, r: (0, 0))],
        out_specs=pl.BlockSpec((None, 2, tm, W), lambda b, r: (b, 0, r, 0)),
        compiler_params=_params("parallel", "parallel"),
        name="fnet_proj",
    )(xn3, w_in_b, cs_small)
    rhs = rhs.reshape(B, 2 * S, W)
    return pl.pallas_call(
        functools.partial(_seq_dft_kernel, seq=S),
        out_shape=jax.ShapeDtypeStruct((B, S, W), BF16),
        grid=(B,),
        in_specs=[_resident((half + BF16_ROWS, half), lambda b: (0, 0)),
                  _resident((half, half), lambda b: (0, 0)),
                  pl.BlockSpec((DFT_FOLD_BLOCK, 2 * DFT_FOLD_BLOCK), lambda b: (0, 0)),
                  pl.BlockSpec((None, 2 * S, W), lambda b: (b, 0, 0))],
        out_specs=pl.BlockSpec((None, S, W), lambda b: (b, 0, 0)),
        scratch_shapes=[pltpu.VMEM((S, W), BF16),
                        pltpu.VMEM((half + DFT_FOLD_BLOCK, W), BF16)],
        compiler_params=_params("parallel"),
        name="fnet_seq_dft",
    )(dft_cos, dft_sin, perm, rhs)


def _merge_kernel(x_ref, xn_ref, ro_ref, ho_ref, fo_ref, wga_ref, wro_ref, who_ref,
                  wf_ref, wout_ref, nw_ref, o_ref, *, d):
    xn = xn_ref[...]

    def gate(i):
        return jax.nn.sigmoid(_dot(xn, wga_ref[:, i * d:(i + 1) * d]))

    mix = gate(0) * _dot(ro_ref[...], wro_ref[...])
    mix += gate(1) * _dot(ho_ref[...], who_ref[...])
    mix += gate(2) * _dot(fo_ref[...], wf_ref[...])
    y = _dot(mix.astype(BF16), wout_ref[...])
    o_ref[...] = x_ref[...] + _rms(y) * nw_ref[...]


def merge_branches(x2, xn2, ro2, ho2, fo2, w_in_b, layer, ga_off_blocks,
                   w_ret_o, w_hgrn_o, w_fnet, w_out, norm_w, tm=512):
    T, D = x2.shape
    RV = ro2.shape[1]
    tile = lambda w: pl.BlockSpec((tm, w), lambda i: (i, 0))
    return pl.pallas_call(
        functools.partial(_merge_kernel, d=D),
        out_shape=jax.ShapeDtypeStruct((T, D), F32),
        grid=(T // tm,),
        in_specs=[tile(D), tile(D), tile(RV), tile(D), tile(D),
                  _resident((None, D, N_BRANCH * D), lambda i: (layer, 0, ga_off_blocks)),
                  _resident((None, RV, D), lambda i: (layer, 0, 0)),
                  _resident((None, D, D), lambda i: (layer, 0, 0)),
                  _resident((None, D, D), lambda i: (layer, 0, 0)),
                  _resident((None, D, D), lambda i: (layer, 0, 0)),
                  pl.BlockSpec((1, D), lambda i: (0, 0))],
        out_specs=tile(D),
        compiler_params=_params("parallel"),
        name="merge",
    )(x2, xn2, ro2, ho2, fo2, w_in_b, w_ret_o, w_hgrn_o, w_fnet, w_out, norm_w.reshape(1, D))


def _ffn_kernel(x_ref, xp_ref, xnx_ref, nw_in_ref, wup_ref, cw_ref, cb_ref, wdn_ref, nw_out_ref,
                *rest, tm, tiles_per_seq, d_ff, fc, emit_next):
    if emit_next:
        nw_next_ref, o_ref, xn_ref, hn_s, acc_s = rest
    else:
        (o_ref, hn_s, acc_s), nw_next_ref, xn_ref = rest, None, None
    i = pl.program_id(0)
    r = i % tiles_per_seq
    halo = BF16_ROWS
    x = x_ref[...]
    nw = nw_in_ref[...]
    hp = jnp.where(r == 0, 0.0, _rms(xp_ref[...]) * nw)
    hx = jnp.where(r == tiles_per_seq - 1, 0.0, _rms(xnx_ref[...]) * nw)
    hn = jnp.concatenate([hp, _rms(x) * nw, hx], axis=0).astype(BF16)
    n_ext = tm + 2 * halo
    hn_s[...] = hn
    acc_s[...] = jnp.zeros_like(acc_s)

    def conv(col, scale):
        cols = pl.ds(pl.multiple_of(col, fc), fc)
        h = _dot(hn_s[...], wup_ref[:, cols])
        cw = cw_ref[:, cols] * scale
        prev = pltpu.roll(h, 1, 0)[halo:halo + tm]
        nxt = pltpu.roll(h, n_ext - 1, 0)[halo:halo + tm]
        return (cb_ref[:, cols] * scale + prev * cw[0:1] + h[halo:halo + tm] * cw[1:2]
                + nxt * cw[2:3])

    def chunk(c, carry):
        gate = conv(c * fc, 1.0)
        half_up = conv(d_ff + c * fc, 0.5)
        inner = gate * (GELU_C0 + GELU_C1 * (gate * gate))
        act = (gate * (1.0 + jnp.tanh(inner)) * half_up).astype(BF16)
        acc_s[...] += _dot(act, wdn_ref[pl.ds(pl.multiple_of(c * fc, fc), fc), :])
        return carry

    for c in range(d_ff // fc):
        chunk(c, 0)
    y = x + _rms(acc_s[...]) * nw_out_ref[...]
    o_ref[...] = y
    if xn_ref is not None:
        xn_ref[...] = (_rms(y) * nw_next_ref[...]).astype(xn_ref.dtype)


def conv_ffn_block(x2, seq, w_up, conv_w, conv_b, w_down, nw_in, nw_out, layer, nw_next=None,
                   tm=1024, fc=256):
    T, D = x2.shape
    d_ff = w_down.shape[1]
    halo = BF16_ROWS
    tps = seq // tm
    hb = tm // halo
    n_hb = T // halo
    assert seq % tm == 0, "a row tile must not straddle two sequences (the conv zero-pads each)"
    emit_next = nw_next is not None
    kern = functools.partial(_ffn_kernel, tm=tm, tiles_per_seq=tps, d_ff=d_ff, fc=fc,
                             emit_next=emit_next)
    vec = pl.BlockSpec((1, D), lambda i: (0, 0))
    tile = pl.BlockSpec((tm, D), lambda i: (i, 0))
    in_specs = [tile,
                pl.BlockSpec((halo, D), lambda i: (jnp.maximum(i * hb - 1, 0), 0)),
                pl.BlockSpec((halo, D), lambda i: (jnp.minimum((i + 1) * hb, n_hb - 1), 0)),
                vec,
                _resident((None, D, 2 * d_ff), lambda i: (layer, 0, 0)),
                pl.BlockSpec((None, CONV_W, 2 * d_ff), lambda i: (layer, 0, 0)),
                pl.BlockSpec((None, 1, 2 * d_ff), lambda i: (layer, 0, 0)),
                _resident((None, d_ff, D), lambda i: (layer, 0, 0)),
                vec]
    args = [x2, x2, x2, nw_in.reshape(1, D), w_up, conv_w, conv_b, w_down, nw_out.reshape(1, D)]
    out_shape = jax.ShapeDtypeStruct((T, D), F32)
    out_specs = tile
    if emit_next:
        in_specs.append(vec)
        args.append(nw_next.reshape(1, D))
        out_shape = (out_shape, jax.ShapeDtypeStruct((T, D), BF16))
        out_specs = (tile, tile)
    return pl.pallas_call(
        kern,
        out_shape=out_shape,
        grid=(T // tm,),
        in_specs=in_specs,
        out_specs=out_specs,
        scratch_shapes=[pltpu.VMEM((tm + 2 * halo, D), BF16),
                        pltpu.VMEM((tm, D), F32)],
        compiler_params=_params("parallel"),
        name="conv_ffn",
    )(*args)


def kernel(x, positions, norm_w, w_in, hgrn_lb_logits, hgrn_norm_w, w_ret_o, w_hgrn_o,
           w_fnet, w_out, w_up, conv_w, conv_b, w_down):
    B, S, D = x.shape
    depth = w_in.shape[0]
    T = B * S

    hgrn_off = 2 * D + 2 * 2 * D
    fu_off = hgrn_off + 5 * D
    ga_off = fu_off + D

    w_in_b = w_in.astype(BF16)
    w_ret_o_b = w_ret_o.astype(BF16)
    w_hgrn_o_b = w_hgrn_o.astype(BF16)
    w_fnet_b = w_fnet.astype(BF16)
    w_out_b = w_out.astype(BF16)
    w_up_b = w_up.astype(BF16)
    w_down_b = w_down.astype(BF16)
    conv_b3 = conv_b.reshape(depth, 1, -1)

    log_gamma = jnp.log(1.0 - 2.0 ** (-5.0 - jnp.arange(RET_HEADS, dtype=F32)))
    p = jax.nn.softmax(hgrn_lb_logits.astype(F32), axis=1)
    lower_bounds = jnp.cumsum(p, axis=1) - p[:, :1]

    cos, sin, xn3 = rope_tables_and_norm(positions, D // RET_HEADS // 2, x, norm_w[0, 0])

    x2 = x.reshape(T, D)
    xn2 = xn3.reshape(T, D)
    for l in range(depth):
        xn3 = xn2.reshape(B, S, D)
        ro = retention_branch(xn3, w_in_b, l, cos, sin, log_gamma)
        ho = hgrn_branch(xn3, w_in_b, l, hgrn_off, lower_bounds[0, l], lower_bounds[1, l],
                         hgrn_norm_w[l])
        fo = fourier_branch(xn3, w_in_b, l, fu_off // D)
        x2 = merge_branches(x2, xn2, ro.reshape(T, -1), ho.reshape(T, D), fo.reshape(T, D),
                            w_in_b, l, ga_off // (N_BRANCH * D),
                            w_ret_o_b, w_hgrn_o_b, w_fnet_b, w_out_b, norm_w[l, 1])
        if l + 1 < depth:
            x2, xn2 = conv_ffn_block(x2, S, w_up_b, conv_w, conv_b3, w_down_b,
                                     norm_w[l, 2], norm_w[l, 3], l, nw_next=norm_w[l + 1, 0])
        else:
            x2 = conv_ffn_block(x2, S, w_up_b, conv_w, conv_b3, w_down_b,
                                norm_w[l, 2], norm_w[l, 3], l)
    return x2.reshape(B, S, D)
```

```python
import functools
import math

import numpy as np
import jax
import jax.numpy as jnp
from jax import lax
from jax.experimental import pallas as pl
from jax.experimental.pallas import tpu as pltpu

F32 = jnp.float32
BF16 = jnp.bfloat16

RET_HEADS = 4
HGRN_HEADS = 8
FNET_GROUPS = 4
N_BRANCH = 3
CONV_W = 3
ROPE_BASE = 10000.0
LB_FLOOR = 1e-30
EPS = 1e-6
LOG2_E = 1.4426950408889634
GELU_C0 = math.sqrt(2.0 / math.pi)
GELU_C1 = GELU_C0 * 0.044715

V7X_VMEM_LIMIT_BYTES = 56 * 1024 * 1024
SUBLANES = 8
BF16_ROWS = 16

RET_CHUNK = 256
HGRN_CHUNK = 128
HGRN_HEADS_PER_STEP = 2
HGRN_VPU_LEVELS = (0,)
ROW_TILE = 512
HGRN_ROW_TILE = 2048


def _dot(a, b):
    return jnp.dot(a, b, preferred_element_type=F32)


def _dot_nt(a, b):
    return lax.dot_general(a, b, (((1,), (1,)), ((), ())), preferred_element_type=F32)


def _dot_tn(a, b):
    return lax.dot_general(a, b, (((0,), (0,)), ((), ())), preferred_element_type=F32)


def _silu(x, scale=1.0):
    return (x * scale if scale != 1.0 else x) / (1.0 + jnp.exp2(x * (-LOG2_E)))


def _rms(x):
    return x * lax.rsqrt(jnp.mean(x * x, axis=-1, keepdims=True) + EPS)


def _params(*sem):
    return pltpu.CompilerParams(dimension_semantics=sem,
                                vmem_limit_bytes=V7X_VMEM_LIMIT_BYTES)


def _resident(shape, index_map):
    return pl.BlockSpec(shape, index_map, pipeline_mode=pl.Buffered(1))


def _rope_norm_kernel(pos_ref, invf_ref, x_ref, w_ref, cos_ref, sin_ref, xn_ref):
    ang = pos_ref[...] * invf_ref[...]
    cos_ref[...] = jnp.cos(ang)
    sin_ref[...] = jnp.sin(ang)
    xn_ref[...] = (_rms(x_ref[...]) * w_ref[...]).astype(xn_ref.dtype)


def rope_tables_and_norm(positions, half, x, w):
    B, S = positions.shape
    D = x.shape[-1]
    pos = positions.astype(F32).reshape(B, S, 1)
    inv_freq = (ROPE_BASE ** (-jnp.arange(half, dtype=F32) / half)).reshape(1, half)
    table = jax.ShapeDtypeStruct((B, S, half), F32)
    table_spec = pl.BlockSpec((None, S, half), lambda b: (b, 0, 0))
    rows_spec = pl.BlockSpec((None, S, D), lambda b: (b, 0, 0))
    return pl.pallas_call(
        _rope_norm_kernel,
        out_shape=(table, table, jax.ShapeDtypeStruct((B, S, D), BF16)),
        grid=(B,),
        in_specs=[pl.BlockSpec((None, S, 1), lambda b: (b, 0, 0)),
                  pl.BlockSpec((1, half), lambda b: (0, 0)),
                  rows_spec,
                  pl.BlockSpec((1, D), lambda b: (0, 0))],
        out_specs=(table_spec, table_spec, rows_spec),
        compiler_params=_params("parallel"),
        name="rope_tables_norm",
    )(pos, inv_freq, x, w.reshape(1, D))


def _ret_kernel(lg_ref, xn_ref, wq_ref, wk_ref, wv_ref, wg_ref, cos_ref, sin_ref,
                o_ref, qi_s, qd_s, ki_s, v_s, g_s, st_s, kvb_s, run_s, *, seq, dk, dv):
    C = RET_CHUNK
    R = seq // C
    half = dk // 2
    lg = lg_ref[pl.program_id(1)]
    ret_scale = dk ** -0.5

    def rows_of(n):
        return pl.ds(pl.multiple_of(n * C, C), C)

    pos = lax.broadcasted_iota(jnp.int32, (C, 1), 0).astype(F32)
    qdec_f = jnp.exp(lg * (pos + 1.0))
    qdec_b = jnp.exp(lg * (C - pos))
    kdec_f = jnp.exp(lg * (C - 1.0 - pos))
    kdec_b = jnp.exp(lg * pos)
    chunk_dec = jnp.exp(lg * C)
    ii = lax.broadcasted_iota(jnp.int32, (C, C), 0)
    jj = lax.broadcasted_iota(jnp.int32, (C, C), 1)
    decay = jnp.exp(lg * jnp.abs(ii - jj).astype(F32))

    run_s[...] = jnp.zeros_like(run_s)

    def proj(t, carry):
        rows = pl.ds(pl.multiple_of(t * ROW_TILE, ROW_TILE), ROW_TILE)
        xc = xn_ref[rows, :]
        cos = cos_ref[rows, :]
        sin = sin_ref[rows, :]
        q = _dot(xc, wq_ref[...])
        q1, q2 = q[:, :half], q[:, half:]
        q = jnp.concatenate([q1 * cos - q2 * sin, q1 * sin + q2 * cos], axis=-1)
        k = _dot(xc, wk_ref[...]) * ret_scale
        k1, k2 = k[:, :half], k[:, half:]
        k = jnp.concatenate([k1 * cos - k2 * sin, k1 * sin + k2 * cos], axis=-1)
        v = _dot(xc, wv_ref[...]).astype(BF16)
        g = _dot(xc, wg_ref[...])
        qi_s[rows, :] = q.astype(BF16)
        ki_s[rows, :] = k.astype(BF16)
        v_s[rows, :] = v
        g_s[rows, :] = _silu(g).astype(BF16)
        for j in range(ROW_TILE // C):
            n = t * (ROW_TILE // C) + j
            sl = slice(j * C, (j + 1) * C)
            qd_s[rows_of(n), :] = jnp.concatenate([q[sl] * qdec_f, q[sl] * qdec_b],
                                                  axis=-1).astype(BF16)
            st_s[n, pl.ds(0, dk), :] = run_s[...].astype(BF16)
            run_s[...] = run_s[...] * chunk_dec + _dot_tn((k[sl] * kdec_f).astype(BF16), v[sl])
            kvb_s[n] = _dot_tn((k[sl] * kdec_b).astype(BF16), v[sl])
        return carry

    lax.fori_loop(0, seq // ROW_TILE, proj, 0, unroll=2)

    run_s[...] = jnp.zeros_like(run_s)

    def out(t, carry):
        n = R - 1 - t
        rows = rows_of(n)
        st_s[n, pl.ds(dk, dk), :] = run_s[...].astype(BF16)
        s = _dot_nt(qi_s[rows, :], ki_s[rows, :]) * decay
        o = _dot(s.astype(BF16), v_s[rows, :]) + _dot(qd_s[rows, :], st_s[n])
        o_ref[rows, :] = (_rms(o) * g_s[rows, :].astype(F32)).astype(o_ref.dtype)
        run_s[...] = run_s[...] * chunk_dec + kvb_s[n]
        return carry

    lax.fori_loop(0, R, out, 0, unroll=4)


def retention_branch(xn3, w_in_b, layer, cos, sin, log_gamma):
    B, S, D = xn3.shape
    dk = D // RET_HEADS
    dv = 2 * dk
    H = RET_HEADS
    qk_blocks = D // dk
    v_off = 2 * D // dv
    g_off = v_off + H
    kern = functools.partial(_ret_kernel, seq=S, dk=dk, dv=dv)
    return pl.pallas_call(
        kern,
        out_shape=jax.ShapeDtypeStruct((B, S, H * dv), BF16),
        grid=(B, H),
        in_specs=[
            pl.BlockSpec(memory_space=pltpu.SMEM),
            pl.BlockSpec((None, S, D), lambda b, h: (b, 0, 0)),
            pl.BlockSpec((None, D, dk), lambda b, h: (layer, 0, h)),
            pl.BlockSpec((None, D, dk), lambda b, h: (layer, 0, qk_blocks + h)),
            pl.BlockSpec((None, D, dv), lambda b, h: (layer, 0, v_off + h)),
            pl.BlockSpec((None, D, dv), lambda b, h: (layer, 0, g_off + h)),
            pl.BlockSpec((None, S, dk // 2), lambda b, h: (b, 0, 0)),
            pl.BlockSpec((None, S, dk // 2), lambda b, h: (b, 0, 0)),
        ],
        out_specs=pl.BlockSpec((None, S, dv), lambda b, h: (b, 0, h)),
        scratch_shapes=[
            pltpu.VMEM((S, dk), BF16),
            pltpu.VMEM((S, 2 * dk), BF16),
            pltpu.VMEM((S, dk), BF16),
            pltpu.VMEM((S, dv), BF16),
            pltpu.VMEM((S, dv), BF16),
            pltpu.VMEM((S // RET_CHUNK, 2 * dk, dv), BF16),
            pltpu.VMEM((S // RET_CHUNK, dk, dv), F32),
            pltpu.VMEM((dk, dv), F32),
        ],
        compiler_params=_params("parallel", "arbitrary"),
        name="retention",
    )(log_gamma, xn3, w_in_b, w_in_b, w_in_b, w_in_b, cos, sin)


def _hgrn_gate(z, lb):
    e = jnp.exp2(jnp.abs(z) * (-LOG2_E))
    pos = z >= 0.0
    sig_neg_num = jnp.where(pos, e, 1.0)
    num = jnp.where(pos, 1.0, e) + jnp.maximum(lb, LB_FLOOR) * sig_neg_num
    inv = 1.0 / (1.0 + e)
    log2_f = jnp.log2(num * inv)
    return log2_f, (1.0 - lb) * sig_neg_num * inv


def _boundary_rows(cum_ref, base, m, reverse, row_in_group):
    C = HGRN_CHUNK
    d = cum_ref.shape[1]
    blk = 2 * m
    off = m if reverse else m - 1
    pieces = []
    if blk >= SUBLANES:
        for b in range(C // blk):
            pieces.append(jnp.broadcast_to(cum_ref[pl.ds(base + (b * blk + off), 1), :], (blk, d)))
    else:
        for g in range(C // SUBLANES):
            val = None
            for u in range(SUBLANES // blk):
                row = g * SUBLANES + u * blk + off
                piece = jnp.broadcast_to(cum_ref[pl.ds(base + row, 1), :], (SUBLANES, d))
                val = piece if val is None else jnp.where(row_in_group >= u * blk, piece, val)
            pieces.append(val)
    return jnp.concatenate(pieces, axis=0) if len(pieces) > 1 else pieces[0]


def _level_operands(level, q, k, cum, cum_ref, base, consts, reverse):
    C = HGRN_CHUNK
    _, row_in_group, signs, _ = consts
    m = 2 ** level
    if 2 * m <= SUBLANES:
        sign = signs[level]
        ref_pt = _boundary_rows(cum_ref, base, m, reverse, row_in_group)
        x = (jnp.where(sign > 0.0, q, k) * jnp.exp2((cum - ref_pt) * sign)).astype(BF16)
        return x, x, list(range(C // SUBLANES))
    xq, xall, q_groups = [], [], []
    for b in range(C // (2 * m)):
        first = slice(b * 2 * m, b * 2 * m + m)
        second = slice(b * 2 * m + m, (b + 1) * 2 * m)
        q_rows, k_rows = (first, second) if reverse else (second, first)
        edge = k_rows.start if reverse else k_rows.stop - 1
        ref_pt = cum_ref[pl.ds(base + edge, 1), :]
        xq_b = q[q_rows] * jnp.exp2(cum[q_rows] - ref_pt)
        xk_b = k[k_rows] * jnp.exp2(ref_pt - cum[k_rows])
        xq.append(xq_b)
        xall.extend([xq_b, xk_b] if reverse else [xk_b, xq_b])
        q_groups.extend(range(q_rows.start // SUBLANES, q_rows.stop // SUBLANES))
    return (jnp.concatenate(xq, axis=0).astype(BF16), jnp.concatenate(xall, axis=0).astype(BF16),
            q_groups)


def _paired_dot_nt(lhs_a, rhs_a, lhs_b, rhs_b):
    rhs = jnp.concatenate([rhs_a, rhs_b], axis=1)
    lhs = jnp.concatenate(
        [jnp.concatenate([lhs_a, jnp.zeros_like(lhs_a)], axis=1),
         jnp.concatenate([jnp.zeros_like(lhs_b), lhs_b], axis=1)], axis=0)
    s = _dot_nt(lhs, rhs)
    return s[:lhs_a.shape[0]], s[lhs_a.shape[0]:]


def _assemble_scores(scores, level_id):
    C = HGRN_CHUNK
    rows = [jnp.zeros((SUBLANES, C), F32) for _ in range(C // SUBLANES)]
    for level, entry in enumerate(scores):
        if entry is None:
            continue
        s, q_groups = entry
        for i, g in enumerate(q_groups):
            lid = level_id[g * SUBLANES:(g + 1) * SUBLANES]
            rows[g] = jnp.where(lid == level, s[i * SUBLANES:(i + 1) * SUBLANES], rows[g])
    return jnp.concatenate(rows, axis=0)


def _hgrn_intra_pair(fwd, bwd, consts):
    C = HGRN_CHUNK
    args = ((fwd, consts[0], False), (bwd, consts[1], True))
    n_vpu = len(HGRN_VPU_LEVELS)
    scores = ([None] * n_vpu, [None] * n_vpu)
    for level in range(n_vpu, C.bit_length() - 1):
        ops = [_level_operands(level, q, k, cum, cum_ref, base, cst, rev)
               for (q, k, _, cum, cum_ref, base), cst, rev in args]
        s_f, s_b = _paired_dot_nt(ops[0][0], ops[0][1], ops[1][0], ops[1][1])
        scores[0].append((s_f, ops[0][2]))
        scores[1].append((s_b, ops[1][2]))
    outs = []
    for idx, ((q, k, v, cum, _, _), cst, rev) in enumerate(args):
        attn = _assemble_scores(scores[idx], cst[0])
        o = _dot(attn.astype(BF16), v.astype(BF16))
        o += jnp.sum(q * k, axis=-1, keepdims=True) * v
        outs.append(o + _near_pairs(q, k, v, cum, rev, cst[3]))
    return outs


def _near_pairs(q, k, v, cum, reverse, masks):
    C, d = q.shape
    shape3 = (C // SUBLANES, SUBLANES, d)
    out = jnp.zeros((C, d), F32)
    for offset, mask in enumerate(masks, start=1):
        valid = mask != 0
        shift = (SUBLANES - offset) if reverse else offset

        def key_row(x):
            return pltpu.roll(x.reshape(shape3), shift, 1).reshape(C, d)

        w = jnp.exp2(jnp.where(valid, cum - key_row(cum), 0.0))
        score = jnp.sum(q * key_row(k) * w, axis=-1, keepdims=True)
        out += jnp.where(valid, score * key_row(v), 0.0)
    return out


def _near_pair_masks(C, d, reverse, levels):
    pos = lax.broadcasted_iota(jnp.int32, (C, d), 0) % SUBLANES
    masks = []
    for offset in range(1, 2 ** (max(levels) + 1)):
        valid = jnp.zeros((C, d), jnp.int32)
        for level in levels:
            m, blk = 2 ** level, 2 ** (level + 1)
            r = pos % blk
            key = (r + offset) if reverse else (r - offset)
            if reverse:
                ok = (r < m) & (key >= m) & (key < blk)
            else:
                ok = (r >= m) & (key >= 0) & (key < m)
            valid = jnp.where(ok, 1, valid)
        masks.append(valid)
    return masks


def _hgrn_consts(reverse, d):
    C = HGRN_CHUNK
    ii = lax.broadcasted_iota(jnp.int32, (C, C), 0)
    jj = lax.broadcasted_iota(jnp.int32, (C, C), 1)
    diff = ii ^ jj
    level_id = jnp.full((C, C), -1, jnp.int32)
    n_levels = C.bit_length() - 1
    for level in range(n_levels):
        level_id = jnp.where((diff >> level) == 1, level, level_id)
    level_id = jnp.where((ii < jj) if reverse else (ii > jj), level_id, -1)
    rows = lax.broadcasted_iota(jnp.int32, (C, d), 0)
    row_in_group = lax.broadcasted_iota(jnp.int32, (SUBLANES, d), 0)
    signs = []
    for level in range(SUBLANES.bit_length() - 1):
        second = ((rows >> level) & 1) == 1
        is_query = jnp.logical_not(second) if reverse else second
        signs.append(jnp.where(is_query, 1.0, -1.0))
    return level_id, row_in_group, signs, _near_pair_masks(C, d, reverse, HGRN_VPU_LEVELS)


def _chunk_cumsum(x, reverse, row_in_group):
    rows, d = x.shape
    groups = rows // SUBLANES
    per_chunk = HGRN_CHUNK // SUBLANES
    y = x.reshape(groups, SUBLANES, d)
    step = 1
    while step < SUBLANES:
        rolled = pltpu.roll(y, (SUBLANES - step) if reverse else step, 1)
        valid = (row_in_group < SUBLANES - step) if reverse else (row_in_group >= step)
        y = y + jnp.where(valid, rolled, 0.0)
        step *= 2
    out = [None] * groups
    for c in range(rows // HGRN_CHUNK):
        order = range(c * per_chunk, (c + 1) * per_chunk)
        carry = None
        for g in (reversed(order) if reverse else order):
            yg = y[g] if carry is None else y[g] + carry
            out[g] = yg
            edge = 0 if reverse else SUBLANES - 1
            carry = jnp.broadcast_to(yg[edge:edge + 1, :], (SUBLANES, d))
    return jnp.concatenate(out, axis=0)


def _hgrn_kernel(xn_ref, wq_ref, wzf_ref, wzb_ref, wi_ref, wg_ref, lbf_ref, lbb_ref, nw_ref, o_ref,
                 q_s, v_s, g_s, acc_s, kf_s, kb_s, lff_s, lfb_s, cum_s, *, seq, dk, heads):
    C = HGRN_CHUNK
    R = seq // C
    row_tile = HGRN_ROW_TILE
    scale = dk ** -0.5

    def tile_rows(n):
        return pl.ds(pl.multiple_of(n * row_tile, row_tile), row_tile)

    def head_cols(h):
        return slice(h * dk, (h + 1) * dk)

    dirs = [((False, lbf_ref, kf_s.at[h], lff_s.at[h], cum_s.at[h, 0]),
             (True, lbb_ref, kb_s.at[h], lfb_s.at[h], cum_s.at[h, 1]))
            for h in range(heads)]
    row_in_group = lax.broadcasted_iota(jnp.int32, (1, SUBLANES, dk), 1)

    def proj(n, carry):
        rows = tile_rows(n)
        xc = xn_ref[rows, :]
        hq_all = _dot(xc, wq_ref[...])
        z_all = (_dot(xc, wzf_ref[...]), _dot(xc, wzb_ref[...]))
        v_all = _dot(xc, wi_ref[...])
        hg_all = _dot(xc, wg_ref[...])
        for h in range(heads):
            cols = head_cols(h)
            hq = hq_all[:, cols]
            q = _silu(hq, scale)
            q_s[h, rows, :] = q
            for idx, (reverse, lb_ref, k_s, lf_s, _) in enumerate(dirs[h]):
                lf, kk = _hgrn_gate(z_all[idx][:, cols], lb_ref[:, cols])
                k_s[rows, :] = kk
                lf_s[rows, :] = lf
            v_s[h, rows, :] = v_all[:, cols]
            hg = hg_all[:, cols]
            g_s[h, rows, :] = _silu(hg)
            acc_s[h, rows, :] = jnp.zeros((row_tile, dk), F32)
        return carry

    lax.fori_loop(0, seq // row_tile, proj, 0, unroll=2)

    consts = (_hgrn_consts(False, dk), _hgrn_consts(True, dk))

    def chunk_pair(h, cf, cb, states):
        data, rows, q_dec, k_dec, decs = [], [], [], [], []
        for c, (reverse, _, k_s, lf_s, cum_c) in zip((cf, cb), dirs[h]):
            r = pl.ds(pl.multiple_of(c * C, C), C)
            q, k = q_s[h, r, :], k_s[r, :]
            cum = _chunk_cumsum(lf_s[r, :], reverse, row_in_group)
            cum_c[...] = cum
            data.append((q, k, v_s[h, r, :], cum, cum_c, 0))
            rows.append(r)
            total = cum_c[pl.ds(0 if reverse else C - 1, 1), :]
            q_dec.append((q * jnp.exp2(cum)).astype(BF16))
            k_dec.append((k * jnp.exp2(total - cum)).astype(BF16))
            decs.append(jnp.exp2(total))
        o_f, o_b = _hgrn_intra_pair(data[0], data[1], consts)
        i_f, i_b = _paired_dot_nt(q_dec[0], states[0].astype(BF16), q_dec[1], states[1].astype(BF16))
        acc_s[h, rows[0], :] += o_f + i_f
        acc_s[h, rows[1], :] += o_b + i_b
        return tuple(st * dec + _dot_tn(d[2].astype(BF16), kd)
                     for st, dec, d, kd in zip(states, decs, data, k_dec))

    def step(i, states):
        return tuple(chunk_pair(h, i, R - 1 - i, states[h]) for h in range(heads))

    zero = jnp.zeros((dk, dk), F32)
    lax.fori_loop(0, R, step, tuple((zero, zero) for _ in range(heads)))

    def finish(n, carry):
        rows = tile_rows(n)
        for h in range(heads):
            o_ref[rows, head_cols(h)] = (_rms(acc_s[h, rows, :]) * nw_ref[...]
                                         * g_s[h, rows, :]).astype(o_ref.dtype)
        return carry

    lax.fori_loop(0, seq // row_tile, finish, 0)


def hgrn_branch(xn3, w_in_b, layer, hgrn_off, lb_f, lb_b, norm_w):
    B, S, D = xn3.shape
    dk = D // HGRN_HEADS
    hp = HGRN_HEADS_PER_STEP
    wide = hp * dk
    kern = functools.partial(_hgrn_kernel, seq=S, dk=dk, heads=hp)
    vec = lambda: pltpu.VMEM((hp, S, dk), F32)

    def w_spec(group):
        first = (hgrn_off + group * D) // wide
        return pl.BlockSpec((None, D, wide), lambda b, j: (layer, 0, first + j))

    lb_spec = pl.BlockSpec((1, wide), lambda b, j: (0, j))
    return pl.pallas_call(
        kern,
        out_shape=jax.ShapeDtypeStruct((B, S, D), BF16),
        grid=(B, HGRN_HEADS // hp),
        in_specs=[pl.BlockSpec((None, S, D), lambda b, j: (b, 0, 0)),
                  w_spec(0), w_spec(1), w_spec(2), w_spec(3), w_spec(4),
                  lb_spec, lb_spec,
                  pl.BlockSpec((1, dk), lambda b, j: (0, 0))],
        out_specs=pl.BlockSpec((None, S, wide), lambda b, j: (b, 0, j)),
        scratch_shapes=[vec(), vec(), vec(), vec(), vec(), vec(), vec(), vec(),
                        pltpu.VMEM((hp, 2, HGRN_CHUNK, dk), F32)],
        compiler_params=_params("parallel", "arbitrary"),
        name="hgrn2",
    )(xn3, w_in_b, w_in_b, w_in_b, w_in_b, w_in_b,
      lb_f.reshape(1, D), lb_b.reshape(1, D), norm_w.reshape(1, dk))


def _fnet_proj_kernel(xn_ref, w_ref, cs_ref, o_ref, *, gdim):
    fu = _dot(xn_ref[...], w_ref[...]).astype(BF16)
    for g in range(FNET_GROUPS):
        t = _dot(fu[:, g * gdim:(g + 1) * gdim], cs_ref[...])
        o_ref[0, :, g * gdim:(g + 1) * gdim] = t[:, :gdim].astype(o_ref.dtype)
        o_ref[1, :, g * gdim:(g + 1) * gdim] = t[:, gdim:].astype(o_ref.dtype)


def _seq_dft_kernel(cos_ref, sin_ref, perm_ref, rhs_ref, o_ref, fold_s, mir_s, *, seq):
    n = seq
    half = n // 2
    blk = perm_ref.shape[0]
    c0 = 1.0 / math.sqrt(n)

    for part, sign in ((0, 1.0), (1, -1.0)):
        base = part * n
        for j in range(half // blk):
            own = rhs_ref[pl.ds(base + j * blk, blk), :].astype(F32)
            if j == 0:
                mirror = _dot(perm_ref[:, :blk], rhs_ref[pl.ds(base + n - blk, blk), :])
            else:
                mirror = _dot(perm_ref[...], rhs_ref[pl.ds(base + n - (j + 1) * blk, 2 * blk), :])
            fold_s[pl.ds(part * half + j * blk, blk), :] = (own + sign * mirror).astype(fold_s.dtype)

    mid = rhs_ref[pl.ds(half, BF16_ROWS), :].astype(F32)[0:1] * c0
    p_ext = _dot(cos_ref[...], fold_s[pl.ds(0, half), :])
    q = _dot(sin_ref[...], fold_s[pl.ds(half, half), :])
    row = lax.broadcasted_iota(jnp.int32, q.shape, 0)
    p = p_ext[:half] + jnp.where((row & 1) == 0, 1.0, -1.0) * mid
    o_ref[pl.ds(0, half), :] = (p - q).astype(o_ref.dtype)

    mir_s[pl.ds(0, half), :] = (p + q).astype(mir_s.dtype)
    first = jnp.where(lax.broadcasted_iota(jnp.int32, (BF16_ROWS, q.shape[1]), 0) == 0, 1.0, 0.0)
    tail = (p_ext[half:half + BF16_ROWS] + mid) * first
    mir_s[pl.ds(half, BF16_ROWS), :] = tail.astype(mir_s.dtype)
    mir_s[pl.ds(half + BF16_ROWS, blk - BF16_ROWS), :] = jnp.zeros(
        (blk - BF16_ROWS, tail.shape[1]), mir_s.dtype)
    for j in range(half // blk):
        window = mir_s[pl.ds(half - (j + 1) * blk, 2 * blk), :]
        o_ref[pl.ds(half + j * blk, blk), :] = _dot(perm_ref[...], window).astype(o_ref.dtype)


DFT_FOLD_BLOCK = 128


def _half_dft_tables(n):
    c, s = _dft_tables(n)
    half = n // 2
    cos_ext = np.zeros((half + BF16_ROWS, half))
    cos_ext[:half + 1] = c[:half + 1, :half]
    return cos_ext, s[:half, :half]


def _mirror_permutation(blk):
    p = np.zeros((blk, 2 * blk), np.float32)
    i = np.arange(blk)
    p[i, blk - i] = 1.0
    return p


def _dft_tables(n):
    idx = np.arange(n, dtype=np.int64)
    ang = 2.0 * np.pi * ((idx[:, None] * idx[None, :]) % n).astype(np.float64) / n
    s = 1.0 / math.sqrt(n)
    return np.cos(ang) * s, np.sin(ang) * s


def fourier_branch(xn3, w_in_b, layer, fu_off_blocks, tm=512):
    B, S, D = xn3.shape
    W = D
    gdim = W // FNET_GROUPS
    c_small, s_small = _dft_tables(gdim)
    cs_small = jnp.asarray(np.concatenate([c_small, s_small], axis=1), dtype=BF16)
    half = S // 2
    assert half % 2 == 0 and half % DFT_FOLD_BLOCK == 0
    cos_np, sin_np = _half_dft_tables(S)
    dft_cos = jnp.asarray(cos_np, dtype=BF16)
    dft_sin = jnp.asarray(sin_np, dtype=BF16)
    perm = jnp.asarray(_mirror_permutation(DFT_FOLD_BLOCK), dtype=BF16)
    tiles = S // tm
    rhs = pl.pallas_call(
        functools.partial(_fnet_proj_kernel, gdim=gdim),
        out_shape=jax.ShapeDtypeStruct((B, 2, S, W), BF16),
        grid=(B, tiles),
        in_specs=[pl.BlockSpec((None, tm, D), lambda b, r: (b, r, 0)),
                  pl.BlockSpec((None, D, W), lambda b, r: (layer, 0, fu_off_blocks)),
                  pl.BlockSpec((gdim, 2 * gdim), lambda b, r: (0, 0))],
        out_specs=pl.BlockSpec((None, 2, tm, W), lambda b, r: (b, 0, r, 0)),
        compiler_params=_params("parallel", "parallel"),
        name="fnet_proj",
    )(xn3, w_in_b, cs_small)
    rhs = rhs.reshape(B, 2 * S, W)
    return pl.pallas_call(
        functools.partial(_seq_dft_kernel, seq=S),
        out_shape=jax.ShapeDtypeStruct((B, S, W), BF16),
        grid=(B,),
        in_specs=[_resident((half + BF16_ROWS, half), lambda b: (0, 0)),
                  _resident((half, half), lambda b: (0, 0)),
                  pl.BlockSpec((DFT_FOLD_BLOCK, 2 * DFT_FOLD_BLOCK), lambda b: (0, 0)),
                  pl.BlockSpec((None, 2 * S, W), lambda b: (b, 0, 0))],
        out_specs=pl.BlockSpec((None, S, W), lambda b: (b, 0, 0)),
        scratch_shapes=[pltpu.VMEM((S, W), BF16),
                        pltpu.VMEM((half + DFT_FOLD_BLOCK, W), BF16)],
        compiler_params=_params("parallel"),
        name="fnet_seq_dft",
    )(dft_cos, dft_sin, perm, rhs)


def _merge_kernel(x_ref, xn_ref, ro_ref, ho_ref, fo_ref, wga_ref, wro_ref, who_ref,
                  wf_ref, wout_ref, nw_ref, o_ref, *, d):
    xn = xn_ref[...]

    def gate(i):
        return jax.nn.sigmoid(_dot(xn, wga_ref[:, i * d:(i + 1) * d]))

    mix = gate(0) * _dot(ro_ref[...], wro_ref[...])
    mix += gate(1) * _dot(ho_ref[...], who_ref[...])
    mix += gate(2) * _dot(fo_ref[...], wf_ref[...])
    y = _dot(mix.astype(BF16), wout_ref[...])
    o_ref[...] = x_ref[...] + _rms(y) * nw_ref[...]


def merge_branches(x2, xn2, ro2, ho2, fo2, w_in_b, layer, ga_off_blocks,
                   w_ret_o, w_hgrn_o, w_fnet, w_out, norm_w, tm=512):
    T, D = x2.shape
    RV = ro2.shape[1]
    tile = lambda w: pl.BlockSpec((tm, w), lambda i: (i, 0))
    return pl.pallas_call(
        functools.partial(_merge_kernel, d=D),
        out_shape=jax.ShapeDtypeStruct((T, D), F32),
        grid=(T // tm,),
        in_specs=[tile(D), tile(D), tile(RV), tile(D), tile(D),
                  _resident((None, D, N_BRANCH * D), lambda i: (layer, 0, ga_off_blocks)),
                  _resident((None, RV, D), lambda i: (layer, 0, 0)),
                  _resident((None, D, D), lambda i: (layer, 0, 0)),
                  _resident((None, D, D), lambda i: (layer, 0, 0)),
                  _resident((None, D, D), lambda i: (layer, 0, 0)),
                  pl.BlockSpec((1, D), lambda i: (0, 0))],
        out_specs=tile(D),
        compiler_params=_params("parallel"),
        name="merge",
    )(x2, xn2, ro2, ho2, fo2, w_in_b, w_ret_o, w_hgrn_o, w_fnet, w_out, norm_w.reshape(1, D))


def _ffn_kernel(x_ref, xp_ref, xnx_ref, nw_in_ref, wup_ref, cw_ref, cb_ref, wdn_ref, nw_out_ref,
                *rest, tm, tiles_per_seq, d_ff, fc, emit_next):
    if emit_next:
        nw_next_ref, o_ref, xn_ref, hn_s, acc_s = rest
    else:
        (o_ref, hn_s, acc_s), nw_next_ref, xn_ref = rest, None, None
    i = pl.program_id(0)
    r = i % tiles_per_seq
    halo = BF16_ROWS
    x = x_ref[...]
    nw = nw_in_ref[...]
    hp = jnp.where(r == 0, 0.0, _rms(xp_ref[...]) * nw)
    hx = jnp.where(r == tiles_per_seq - 1, 0.0, _rms(xnx_ref[...]) * nw)
    hn = jnp.concatenate([hp, _rms(x) * nw, hx], axis=0).astype(BF16)
    n_ext = tm + 2 * halo
    hn_s[...] = hn
    acc_s[...] = jnp.zeros_like(acc_s)

    def conv(col, scale):
        cols = pl.ds(pl.multiple_of(col, fc), fc)
        h = _dot(hn_s[...], wup_ref[:, cols])
        cw = cw_ref[:, cols] * scale
        prev = pltpu.roll(h, 1, 0)[halo:halo + tm]
        nxt = pltpu.roll(h, n_ext - 1, 0)[halo:halo + tm]
        return (cb_ref[:, cols] * scale + prev * cw[0:1] + h[halo:halo + tm] * cw[1:2]
                + nxt * cw[2:3])

    def chunk(c, carry):
        gate = conv(c * fc, 1.0)
        half_up = conv(d_ff + c * fc, 0.5)
        inner = gate * (GELU_C0 + GELU_C1 * (gate * gate))
        act = (gate * (1.0 + jnp.tanh(inner)) * half_up).astype(BF16)
        acc_s[...] += _dot(act, wdn_ref[pl.ds(pl.multiple_of(c * fc, fc), fc), :])
        return carry

    for c in range(d_ff // fc):
        chunk(c, 0)
    y = x + _rms(acc_s[...]) * nw_out_ref[...]
    o_ref[...] = y
    if xn_ref is not None:
        xn_ref[...] = (_rms(y) * nw_next_ref[...]).astype(xn_ref.dtype)


def conv_ffn_block(x2, seq, w_up, conv_w, conv_b, w_down, nw_in, nw_out, layer, nw_next=None,
                   tm=1024, fc=256):
    T, D = x2.shape
    d_ff = w_down.shape[1]
    halo = BF16_ROWS
    tps = seq // tm
    hb = tm // halo
    n_hb = T // halo
    assert seq % tm == 0, "a row tile must not straddle two sequences (the conv zero-pads each)"
    emit_next = nw_next is not None
    kern = functools.partial(_ffn_kernel, tm=tm, tiles_per_seq=tps, d_ff=d_ff, fc=fc,
                             emit_next=emit_next)
    vec = pl.BlockSpec((1, D), lambda i: (0, 0))
    tile = pl.BlockSpec((tm, D), lambda i: (i, 0))
    in_specs = [tile,
                pl.BlockSpec((halo, D), lambda i: (jnp.maximum(i * hb - 1, 0), 0)),
                pl.BlockSpec((halo, D), lambda i: (jnp.minimum((i + 1) * hb, n_hb - 1), 0)),
                vec,
                _resident((None, D, 2 * d_ff), lambda i: (layer, 0, 0)),
                pl.BlockSpec((None, CONV_W, 2 * d_ff), lambda i: (layer, 0, 0)),
                pl.BlockSpec((None, 1, 2 * d_ff), lambda i: (layer, 0, 0)),
                _resident((None, d_ff, D), lambda i: (layer, 0, 0)),
                vec]
    args = [x2, x2, x2, nw_in.reshape(1, D), w_up, conv_w, conv_b, w_down, nw_out.reshape(1, D)]
    out_shape = jax.ShapeDtypeStruct((T, D), F32)
    out_specs = tile
    if emit_next:
        in_specs.append(vec)
        args.append(nw_next.reshape(1, D))
        out_shape = (out_shape, jax.ShapeDtypeStruct((T, D), BF16))
        out_specs = (tile, tile)
    return pl.pallas_call(
        kern,
        out_shape=out_shape,
        grid=(T // tm,),
        in_specs=in_specs,
        out_specs=out_specs,
        scratch_shapes=[pltpu.VMEM((tm + 2 * halo, D), BF16),
                        pltpu.VMEM((tm, D), F32)],
        compiler_params=_params("parallel"),
        name="conv_ffn",
    )(*args)


def kernel(x, positions, norm_w, w_in, hgrn_lb_logits, hgrn_norm_w, w_ret_o, w_hgrn_o,
           w_fnet, w_out, w_up, conv_w, conv_b, w_down):
    B, S, D = x.shape
    depth = w_in.shape[0]
    T = B * S

    hgrn_off = 2 * D + 2 * 2 * D
    fu_off = hgrn_off + 5 * D
    ga_off = fu_off + D

    w_in_b = w_in.astype(BF16)
    w_ret_o_b = w_ret_o.astype(BF16)
    w_hgrn_o_b = w_hgrn_o.astype(BF16)
    w_fnet_b = w_fnet.astype(BF16)
    w_out_b = w_out.astype(BF16)
    w_up_b = w_up.astype(BF16)
    w_down_b = w_down.astype(BF16)
    conv_b3 = conv_b.reshape(depth, 1, -1)

    log_gamma = jnp.log(1.0 - 2.0 ** (-5.0 - jnp.arange(RET_HEADS, dtype=F32)))
    p = jax.nn.softmax(hgrn_lb_logits.astype(F32), axis=1)
    lower_bounds = jnp.cumsum(p, axis=1) - p[:, :1]

    cos, sin, xn3 = rope_tables_and_norm(positions, D // RET_HEADS // 2, x, norm_w[0, 0])

    x2 = x.reshape(T, D)
    xn2 = xn3.reshape(T, D)
    for l in range(depth):
        xn3 = xn2.reshape(B, S, D)
        ro = retention_branch(xn3, w_in_b, l, cos, sin, log_gamma)
        ho = hgrn_branch(xn3, w_in_b, l, hgrn_off, lower_bounds[0, l], lower_bounds[1, l],
                         hgrn_norm_w[l])
        fo = fourier_branch(xn3, w_in_b, l, fu_off // D)
        x2 = merge_branches(x2, xn2, ro.reshape(T, -1), ho.reshape(T, D), fo.reshape(T, D),
                            w_in_b, l, ga_off // (N_BRANCH * D),
                            w_ret_o_b, w_hgrn_o_b, w_fnet_b, w_out_b, norm_w[l, 1])
        if l + 1 < depth:
            x2, xn2 = conv_ffn_block(x2, S, w_up_b, conv_w, conv_b3, w_down_b,
                                     norm_w[l, 2], norm_w[l, 3], l, nw_next=norm_w[l + 1, 0])
        else:
            x2 = conv_ffn_block(x2, S, w_up_b, conv_w, conv_b3, w_down_b,
                                norm_w[l, 2], norm_w[l, 3], l)
    return x2.reshape(B, S, D)
```

```python
import functools
import math

import numpy as np
import jax
import jax.numpy as jnp
from jax import lax
from jax.experimental import pallas as pl
from jax.experimental.pallas import tpu as pltpu

F32 = jnp.float32
BF16 = jnp.bfloat16

RET_HEADS = 4
HGRN_HEADS = 8
FNET_GROUPS = 4
N_BRANCH = 3
CONV_W = 3
ROPE_BASE = 10000.0
LB_FLOOR = 1e-30
EPS = 1e-6
LOG2_E = 1.4426950408889634
GELU_C0 = math.sqrt(2.0 / math.pi)
GELU_C1 = GELU_C0 * 0.044715

V7X_VMEM_LIMIT_BYTES = 56 * 1024 * 1024
SUBLANES = 8
BF16_ROWS = 16

RET_CHUNK = 256
HGRN_CHUNK = 128
HGRN_HEADS_PER_STEP = 2
HGRN_VPU_LEVELS = (0,)
ROW_TILE = 512
HGRN_ROW_TILE = 1024


def _dot(a, b):
    return jnp.dot(a, b, preferred_element_type=F32)


def _dot_nt(a, b):
    return lax.dot_general(a, b, (((1,), (1,)), ((), ())), preferred_element_type=F32)


def _dot_tn(a, b):
    return lax.dot_general(a, b, (((0,), (0,)), ((), ())), preferred_element_type=F32)


def _silu(x, scale=1.0):
    return (x * scale if scale != 1.0 else x) / (1.0 + jnp.exp2(x * (-LOG2_E)))


def _rms(x):
    return x * lax.rsqrt(jnp.mean(x * x, axis=-1, keepdims=True) + EPS)


def _params(*sem):
    return pltpu.CompilerParams(dimension_semantics=sem,
                                vmem_limit_bytes=V7X_VMEM_LIMIT_BYTES)


def _resident(shape, index_map):
    return pl.BlockSpec(shape, index_map, pipeline_mode=pl.Buffered(1))


def _rope_norm_kernel(pos_ref, invf_ref, x_ref, w_ref, cos_ref, sin_ref, xn_ref):
    ang = pos_ref[...] * invf_ref[...]
    cos_ref[...] = jnp.cos(ang)
    sin_ref[...] = jnp.sin(ang)
    xn_ref[...] = (_rms(x_ref[...]) * w_ref[...]).astype(xn_ref.dtype)


def rope_tables_and_norm(positions, half, x, w):
    B, S = positions.shape
    D = x.shape[-1]
    pos = positions.astype(F32).reshape(B, S, 1)
    inv_freq = (ROPE_BASE ** (-jnp.arange(half, dtype=F32) / half)).reshape(1, half)
    table = jax.ShapeDtypeStruct((B, S, half), F32)
    table_spec = pl.BlockSpec((None, S, half), lambda b: (b, 0, 0))
    rows_spec = pl.BlockSpec((None, S, D), lambda b: (b, 0, 0))
    return pl.pallas_call(
        _rope_norm_kernel,
        out_shape=(table, table, jax.ShapeDtypeStruct((B, S, D), BF16)),
        grid=(B,),
        in_specs=[pl.BlockSpec((None, S, 1), lambda b: (b, 0, 0)),
                  pl.BlockSpec((1, half), lambda b: (0, 0)),
                  rows_spec,
                  pl.BlockSpec((1, D), lambda b: (0, 0))],
        out_specs=(table_spec, table_spec, rows_spec),
        compiler_params=_params("parallel"),
        name="rope_tables_norm",
    )(pos, inv_freq, x, w.reshape(1, D))


def _ret_kernel(lg_ref, xn_ref, wq_ref, wk_ref, wv_ref, wg_ref, cos_ref, sin_ref,
                o_ref, qi_s, qd_s, ki_s, v_s, g_s, st_s, kvb_s, run_s, *, seq, dk, dv):
    C = RET_CHUNK
    R = seq // C
    half = dk // 2
    lg = lg_ref[pl.program_id(1)]
    ret_scale = dk ** -0.5

    def rows_of(n):
        return pl.ds(pl.multiple_of(n * C, C), C)

    pos = lax.broadcasted_iota(jnp.int32, (C, 1), 0).astype(F32)
    qdec_f = jnp.exp(lg * (pos + 1.0))
    qdec_b = jnp.exp(lg * (C - pos))
    kdec_f = jnp.exp(lg * (C - 1.0 - pos))
    kdec_b = jnp.exp(lg * pos)
    chunk_dec = jnp.exp(lg * C)
    ii = lax.broadcasted_iota(jnp.int32, (C, C), 0)
    jj = lax.broadcasted_iota(jnp.int32, (C, C), 1)
    decay = jnp.exp(lg * jnp.abs(ii - jj).astype(F32))

    run_s[...] = jnp.zeros_like(run_s)

    def proj(t, carry):
        rows = pl.ds(pl.multiple_of(t * ROW_TILE, ROW_TILE), ROW_TILE)
        xc = xn_ref[rows, :]
        cos = cos_ref[rows, :]
        sin = sin_ref[rows, :]
        q = _dot(xc, wq_ref[...])
        q1, q2 = q[:, :half], q[:, half:]
        q = jnp.concatenate([q1 * cos - q2 * sin, q1 * sin + q2 * cos], axis=-1)
        k = _dot(xc, wk_ref[...]) * ret_scale
        k1, k2 = k[:, :half], k[:, half:]
        k = jnp.concatenate([k1 * cos - k2 * sin, k1 * sin + k2 * cos], axis=-1)
        v = _dot(xc, wv_ref[...]).astype(BF16)
        g = _dot(xc, wg_ref[...])
        qi_s[rows, :] = q.astype(BF16)
        ki_s[rows, :] = k.astype(BF16)
        v_s[rows, :] = v
        g_s[rows, :] = _silu(g).astype(BF16)
        for j in range(ROW_TILE // C):
            n = t * (ROW_TILE // C) + j
            sl = slice(j * C, (j + 1) * C)
            qd_s[rows_of(n), :] = jnp.concatenate([q[sl] * qdec_f, q[sl] * qdec_b],
                                                  axis=-1).astype(BF16)
            st_s[n, pl.ds(0, dk), :] = run_s[...].astype(BF16)
            run_s[...] = run_s[...] * chunk_dec + _dot_tn((k[sl] * kdec_f).astype(BF16), v[sl])
            kvb_s[n] = _dot_tn((k[sl] * kdec_b).astype(BF16), v[sl])
        return carry

    lax.fori_loop(0, seq // ROW_TILE, proj, 0, unroll=2)

    run_s[...] = jnp.zeros_like(run_s)

    def out(t, carry):
        n = R - 1 - t
        rows = rows_of(n)
        st_s[n, pl.ds(dk, dk), :] = run_s[...].astype(BF16)
        s = _dot_nt(qi_s[rows, :], ki_s[rows, :]) * decay
        o = _dot(s.astype(BF16), v_s[rows, :]) + _dot(qd_s[rows, :], st_s[n])
        o_ref[rows, :] = (_rms(o) * g_s[rows, :].astype(F32)).astype(o_ref.dtype)
        run_s[...] = run_s[...] * chunk_dec + kvb_s[n]
        return carry

    lax.fori_loop(0, R, out, 0, unroll=4)


def retention_branch(xn3, w_in_b, layer, cos, sin, log_gamma):
    B, S, D = xn3.shape
    dk = D // RET_HEADS
    dv = 2 * dk
    H = RET_HEADS
    qk_blocks = D // dk
    v_off = 2 * D // dv
    g_off = v_off + H
    kern = functools.partial(_ret_kernel, seq=S, dk=dk, dv=dv)
    return pl.pallas_call(
        kern,
        out_shape=jax.ShapeDtypeStruct((B, S, H * dv), BF16),
        grid=(B, H),
        in_specs=[
            pl.BlockSpec(memory_space=pltpu.SMEM),
            pl.BlockSpec((None, S, D), lambda b, h: (b, 0, 0)),
            pl.BlockSpec((None, D, dk), lambda b, h: (layer, 0, h)),
            pl.BlockSpec((None, D, dk), lambda b, h: (layer, 0, qk_blocks + h)),
            pl.BlockSpec((None, D, dv), lambda b, h: (layer, 0, v_off + h)),
            pl.BlockSpec((None, D, dv), lambda b, h: (layer, 0, g_off + h)),
            pl.BlockSpec((None, S, dk // 2), lambda b, h: (b, 0, 0)),
            pl.BlockSpec((None, S, dk // 2), lambda b, h: (b, 0, 0)),
        ],
        out_specs=pl.BlockSpec((None, S, dv), lambda b, h: (b, 0, h)),
        scratch_shapes=[
            pltpu.VMEM((S, dk), BF16),
            pltpu.VMEM((S, 2 * dk), BF16),
            pltpu.VMEM((S, dk), BF16),
            pltpu.VMEM((S, dv), BF16),
            pltpu.VMEM((S, dv), BF16),
            pltpu.VMEM((S // RET_CHUNK, 2 * dk, dv), BF16),
            pltpu.VMEM((S // RET_CHUNK, dk, dv), F32),
            pltpu.VMEM((dk, dv), F32),
        ],
        compiler_params=_params("parallel", "arbitrary"),
        name="retention",
    )(log_gamma, xn3, w_in_b, w_in_b, w_in_b, w_in_b, cos, sin)


def _hgrn_gate(z, lb):
    e = jnp.exp2(jnp.abs(z) * (-LOG2_E))
    pos = z >= 0.0
    sig_neg_num = jnp.where(pos, e, 1.0)
    num = jnp.where(pos, 1.0, e) + jnp.maximum(lb, LB_FLOOR) * sig_neg_num
    inv = 1.0 / (1.0 + e)
    log2_f = jnp.log2(num * inv)
    return log2_f, (1.0 - lb) * sig_neg_num * inv


def _boundary_rows(cum_ref, base, m, reverse, row_in_group):
    C = HGRN_CHUNK
    d = cum_ref.shape[1]
    blk = 2 * m
    off = m if reverse else m - 1
    pieces = []
    if blk >= SUBLANES:
        for b in range(C // blk):
            pieces.append(jnp.broadcast_to(cum_ref[pl.ds(base + (b * blk + off), 1), :], (blk, d)))
    else:
        for g in range(C // SUBLANES):
            val = None
            for u in range(SUBLANES // blk):
                row = g * SUBLANES + u * blk + off
                piece = jnp.broadcast_to(cum_ref[pl.ds(base + row, 1), :], (SUBLANES, d))
                val = piece if val is None else jnp.where(row_in_group >= u * blk, piece, val)
            pieces.append(val)
    return jnp.concatenate(pieces, axis=0) if len(pieces) > 1 else pieces[0]


def _level_operands(level, q, k, cum, cum_ref, base, consts, reverse):
    C = HGRN_CHUNK
    _, row_in_group, signs, _ = consts
    m = 2 ** level
    if 2 * m <= SUBLANES:
        sign = signs[level]
        ref_pt = _boundary_rows(cum_ref, base, m, reverse, row_in_group)
        x = (jnp.where(sign > 0.0, q, k) * jnp.exp2((cum - ref_pt) * sign)).astype(BF16)
        return x, x, list(range(C // SUBLANES))
    xq, xall, q_groups = [], [], []
    for b in range(C // (2 * m)):
        first = slice(b * 2 * m, b * 2 * m + m)
        second = slice(b * 2 * m + m, (b + 1) * 2 * m)
        q_rows, k_rows = (first, second) if reverse else (second, first)
        edge = k_rows.start if reverse else k_rows.stop - 1
        ref_pt = cum_ref[pl.ds(base + edge, 1), :]
        xq_b = q[q_rows] * jnp.exp2(cum[q_rows] - ref_pt)
        xk_b = k[k_rows] * jnp.exp2(ref_pt - cum[k_rows])
        xq.append(xq_b)
        xall.extend([xq_b, xk_b] if reverse else [xk_b, xq_b])
        q_groups.extend(range(q_rows.start // SUBLANES, q_rows.stop // SUBLANES))
    return (jnp.concatenate(xq, axis=0).astype(BF16), jnp.concatenate(xall, axis=0).astype(BF16),
            q_groups)


def _paired_dot_nt(lhs_a, rhs_a, lhs_b, rhs_b):
    rhs = jnp.concatenate([rhs_a, rhs_b], axis=1)
    lhs = jnp.concatenate(
        [jnp.concatenate([lhs_a, jnp.zeros_like(lhs_a)], axis=1),
         jnp.concatenate([jnp.zeros_like(lhs_b), lhs_b], axis=1)], axis=0)
    s = _dot_nt(lhs, rhs)
    return s[:lhs_a.shape[0]], s[lhs_a.shape[0]:]


def _assemble_scores(scores, level_id):
    C = HGRN_CHUNK
    rows = [jnp.zeros((SUBLANES, C), F32) for _ in range(C // SUBLANES)]
    for level, entry in enumerate(scores):
        if entry is None:
            continue
        s, q_groups = entry
        for i, g in enumerate(q_groups):
            lid = level_id[g * SUBLANES:(g + 1) * SUBLANES]
            rows[g] = jnp.where(lid == level, s[i * SUBLANES:(i + 1) * SUBLANES], rows[g])
    return jnp.concatenate(rows, axis=0)


def _hgrn_intra_pair(fwd, bwd, consts):
    C = HGRN_CHUNK
    args = ((fwd, consts[0], False), (bwd, consts[1], True))
    n_vpu = len(HGRN_VPU_LEVELS)
    scores = ([None] * n_vpu, [None] * n_vpu)
    for level in range(n_vpu, C.bit_length() - 1):
        ops = [_level_operands(level, q, k, cum, cum_ref, base, cst, rev)
               for (q, k, _, cum, cum_ref, base), cst, rev in args]
        s_f, s_b = _paired_dot_nt(ops[0][0], ops[0][1], ops[1][0], ops[1][1])
        scores[0].append((s_f, ops[0][2]))
        scores[1].append((s_b, ops[1][2]))
    outs = []
    for idx, ((q, k, v, cum, _, _), cst, rev) in enumerate(args):
        attn = _assemble_scores(scores[idx], cst[0])
        o = _dot(attn.astype(BF16), v.astype(BF16))
        o += jnp.sum(q * k, axis=-1, keepdims=True) * v
        outs.append(o + _near_pairs(q, k, v, cum, rev, cst[3]))
    return outs


def _near_pairs(q, k, v, cum, reverse, masks):
    C, d = q.shape
    shape3 = (C // SUBLANES, SUBLANES, d)
    out = jnp.zeros((C, d), F32)
    for offset, mask in enumerate(masks, start=1):
        valid = mask != 0
        shift = (SUBLANES - offset) if reverse else offset

        def key_row(x):
            return pltpu.roll(x.reshape(shape3), shift, 1).reshape(C, d)

        w = jnp.exp2(jnp.where(valid, cum - key_row(cum), 0.0))
        score = jnp.sum(q * key_row(k) * w, axis=-1, keepdims=True)
        out += jnp.where(valid, score * key_row(v), 0.0)
    return out


def _near_pair_masks(C, d, reverse, levels):
    pos = lax.broadcasted_iota(jnp.int32, (C, d), 0) % SUBLANES
    masks = []
    for offset in range(1, 2 ** (max(levels) + 1)):
        valid = jnp.zeros((C, d), jnp.int32)
        for level in levels:
            m, blk = 2 ** level, 2 ** (level + 1)
            r = pos % blk
            key = (r + offset) if reverse else (r - offset)
            if reverse:
                ok = (r < m) & (key >= m) & (key < blk)
            else:
                ok = (r >= m) & (key >= 0) & (key < m)
            valid = jnp.where(ok, 1, valid)
        masks.append(valid)
    return masks


def _hgrn_consts(reverse, d):
    C = HGRN_CHUNK
    ii = lax.broadcasted_iota(jnp.int32, (C, C), 0)
    jj = lax.broadcasted_iota(jnp.int32, (C, C), 1)
    diff = ii ^ jj
    level_id = jnp.full((C, C), -1, jnp.int32)
    n_levels = C.bit_length() - 1
    for level in range(n_levels):
        level_id = jnp.where((diff >> level) == 1, level, level_id)
    level_id = jnp.where((ii < jj) if reverse else (ii > jj), level_id, -1)
    rows = lax.broadcasted_iota(jnp.int32, (C, d), 0)
    row_in_group = lax.broadcasted_iota(jnp.int32, (SUBLANES, d), 0)
    signs = []
    for level in range(SUBLANES.bit_length() - 1):
        second = ((rows >> level) & 1) == 1
        is_query = jnp.logical_not(second) if reverse else second
        signs.append(jnp.where(is_query, 1.0, -1.0))
    return level_id, row_in_group, signs, _near_pair_masks(C, d, reverse, HGRN_VPU_LEVELS)


def _chunk_cumsum(x, reverse, row_in_group):
    rows, d = x.shape
    groups = rows // SUBLANES
    per_chunk = HGRN_CHUNK // SUBLANES
    y = x.reshape(groups, SUBLANES, d)
    step = 1
    while step < SUBLANES:
        rolled = pltpu.roll(y, (SUBLANES - step) if reverse else step, 1)
        valid = (row_in_group < SUBLANES - step) if reverse else (row_in_group >= step)
        y = y + jnp.where(valid, rolled, 0.0)
        step *= 2
    out = [None] * groups
    for c in range(rows // HGRN_CHUNK):
        order = range(c * per_chunk, (c + 1) * per_chunk)
        carry = None
        for g in (reversed(order) if reverse else order):
            yg = y[g] if carry is None else y[g] + carry
            out[g] = yg
            edge = 0 if reverse else SUBLANES - 1
            carry = jnp.broadcast_to(yg[edge:edge + 1, :], (SUBLANES, d))
    return jnp.concatenate(out, axis=0)


def _hgrn_kernel(xn_ref, wq_ref, wzf_ref, wzb_ref, wi_ref, wg_ref, lbf_ref, lbb_ref, nw_ref, o_ref,
                 q_s, v_s, g_s, acc_s, kf_s, kb_s, lff_s, lfb_s, cum_s, *, seq, dk, heads):
    C = HGRN_CHUNK
    R = seq // C
    row_tile = HGRN_ROW_TILE
    scale = dk ** -0.5

    def tile_rows(n):
        return pl.ds(pl.multiple_of(n * row_tile, row_tile), row_tile)

    def head_cols(h):
        return slice(h * dk, (h + 1) * dk)

    dirs = [((False, lbf_ref, kf_s.at[h], lff_s.at[h], cum_s.at[h, 0]),
             (True, lbb_ref, kb_s.at[h], lfb_s.at[h], cum_s.at[h, 1]))
            for h in range(heads)]
    row_in_group = lax.broadcasted_iota(jnp.int32, (1, SUBLANES, dk), 1)

    def proj(n, carry):
        rows = tile_rows(n)
        hq_all = _dot(xn_ref[rows, :], wq_ref[...])
        for h in range(heads):
            q_s[h, rows, :] = _silu(hq_all[:, head_cols(h)], scale)
        for idx, wz_ref in enumerate((wzf_ref, wzb_ref)):
            z_all = _dot(xn_ref[rows, :], wz_ref[...])
            for h in range(heads):
                cols = head_cols(h)
                _, lb_ref, k_s, lf_s, _ = dirs[h][idx]
                lf, kk = _hgrn_gate(z_all[:, cols], lb_ref[:, cols])
                k_s[rows, :] = kk
                lf_s[rows, :] = lf
        v_all = _dot(xn_ref[rows, :], wi_ref[...])
        for h in range(heads):
            v_s[h, rows, :] = v_all[:, head_cols(h)]
        hg_all = _dot(xn_ref[rows, :], wg_ref[...])
        for h in range(heads):
            g_s[h, rows, :] = _silu(hg_all[:, head_cols(h)])
            acc_s[h, rows, :] = jnp.zeros((row_tile, dk), F32)
        return carry

    lax.fori_loop(0, seq // row_tile, proj, 0, unroll=2)

    consts = (_hgrn_consts(False, dk), _hgrn_consts(True, dk))

    def chunk_pair(h, cf, cb, states):
        data, rows, q_dec, k_dec, decs = [], [], [], [], []
        for c, (reverse, _, k_s, lf_s, cum_c) in zip((cf, cb), dirs[h]):
            r = pl.ds(pl.multiple_of(c * C, C), C)
            q, k = q_s[h, r, :], k_s[r, :]
            cum = _chunk_cumsum(lf_s[r, :], reverse, row_in_group)
            cum_c[...] = cum
            data.append((q, k, v_s[h, r, :], cum, cum_c, 0))
            rows.append(r)
            total = cum_c[pl.ds(0 if reverse else C - 1, 1), :]
            q_dec.append((q * jnp.exp2(cum)).astype(BF16))
            k_dec.append((k * jnp.exp2(total - cum)).astype(BF16))
            decs.append(jnp.exp2(total))
        o_f, o_b = _hgrn_intra_pair(data[0], data[1], consts)
        i_f, i_b = _paired_dot_nt(q_dec[0], states[0].astype(BF16), q_dec[1], states[1].astype(BF16))
        acc_s[h, rows[0], :] += o_f + i_f
        acc_s[h, rows[1], :] += o_b + i_b
        return tuple(st * dec + _dot_tn(d[2].astype(BF16), kd)
                     for st, dec, d, kd in zip(states, decs, data, k_dec))

    def step(i, states):
        return tuple(chunk_pair(h, i, R - 1 - i, states[h]) for h in range(heads))

    zero = jnp.zeros((dk, dk), F32)
    lax.fori_loop(0, R, step, tuple((zero, zero) for _ in range(heads)))

    def finish(n, carry):
        rows = tile_rows(n)
        for h in range(heads):
            o_ref[rows, head_cols(h)] = (_rms(acc_s[h, rows, :]) * nw_ref[...]
                                         * g_s[h, rows, :]).astype(o_ref.dtype)
        return carry

    lax.fori_loop(0, seq // row_tile, finish, 0)


def hgrn_branch(xn3, w_in_b, layer, hgrn_off, lb_f, lb_b, norm_w):
    B, S, D = xn3.shape
    dk = D // HGRN_HEADS
    hp = HGRN_HEADS_PER_STEP
    wide = hp * dk
    kern = functools.partial(_hgrn_kernel, seq=S, dk=dk, heads=hp)
    vec = lambda: pltpu.VMEM((hp, S, dk), F32)

    def w_spec(group):
        first = (hgrn_off + group * D) // wide
        return pl.BlockSpec((None, D, wide), lambda b, j: (layer, 0, first + j))

    lb_spec = pl.BlockSpec((1, wide), lambda b, j: (0, j))
    return pl.pallas_call(
        kern,
        out_shape=jax.ShapeDtypeStruct((B, S, D), BF16),
        grid=(B, HGRN_HEADS // hp),
        in_specs=[pl.BlockSpec((None, S, D), lambda b, j: (b, 0, 0)),
                  w_spec(0), w_spec(1), w_spec(2), w_spec(3), w_spec(4),
                  lb_spec, lb_spec,
                  pl.BlockSpec((1, dk), lambda b, j: (0, 0))],
        out_specs=pl.BlockSpec((None, S, wide), lambda b, j: (b, 0, j)),
        scratch_shapes=[vec(), vec(), vec(), vec(), vec(), vec(), vec(), vec(),
                        pltpu.VMEM((hp, 2, HGRN_CHUNK, dk), F32)],
        compiler_params=_params("parallel", "arbitrary"),
        name="hgrn2",
    )(xn3, w_in_b, w_in_b, w_in_b, w_in_b, w_in_b,
      lb_f.reshape(1, D), lb_b.reshape(1, D), norm_w.reshape(1, dk))


def _fnet_proj_kernel(xn_ref, w_ref, cs_ref, o_ref, *, gdim):
    fu = _dot(xn_ref[...], w_ref[...]).astype(BF16)
    for g in range(FNET_GROUPS):
        t = _dot(fu[:, g * gdim:(g + 1) * gdim], cs_ref[...])
        o_ref[0, :, g * gdim:(g + 1) * gdim] = t[:, :gdim].astype(o_ref.dtype)
        o_ref[1, :, g * gdim:(g + 1) * gdim] = t[:, gdim:].astype(o_ref.dtype)


def _seq_dft_kernel(cos_ref, sin_ref, perm_ref, rhs_ref, o_ref, fold_s, mir_s, *, seq):
    n = seq
    half = n // 2
    blk = perm_ref.shape[0]
    c0 = 1.0 / math.sqrt(n)

    for part, sign in ((0, 1.0), (1, -1.0)):
        base = part * n
        for j in range(half // blk):
            own = rhs_ref[pl.ds(base + j * blk, blk), :].astype(F32)
            if j == 0:
                mirror = _dot(perm_ref[:, :blk], rhs_ref[pl.ds(base + n - blk, blk), :])
            else:
                mirror = _dot(perm_ref[...], rhs_ref[pl.ds(base + n - (j + 1) * blk, 2 * blk), :])
            fold_s[pl.ds(part * half + j * blk, blk), :] = (own + sign * mirror).astype(fold_s.dtype)

    mid = rhs_ref[pl.ds(half, BF16_ROWS), :].astype(F32)[0:1] * c0
    p_ext = _dot(cos_ref[...], fold_s[pl.ds(0, half), :])
    q = _dot(sin_ref[...], fold_s[pl.ds(half, half), :])
    row = lax.broadcasted_iota(jnp.int32, q.shape, 0)
    p = p_ext[:half] + jnp.where((row & 1) == 0, 1.0, -1.0) * mid
    o_ref[pl.ds(0, half), :] = (p - q).astype(o_ref.dtype)

    mir_s[pl.ds(0, half), :] = (p + q).astype(mir_s.dtype)
    first = jnp.where(lax.broadcasted_iota(jnp.int32, (BF16_ROWS, q.shape[1]), 0) == 0, 1.0, 0.0)
    tail = (p_ext[half:half + BF16_ROWS] + mid) * first
    mir_s[pl.ds(half, BF16_ROWS), :] = tail.astype(mir_s.dtype)
    mir_s[pl.ds(half + BF16_ROWS, blk - BF16_ROWS), :] = jnp.zeros(
        (blk - BF16_ROWS, tail.shape[1]), mir_s.dtype)
    for j in range(half // blk):
        window = mir_s[pl.ds(half - (j + 1) * blk, 2 * blk), :]
        o_ref[pl.ds(half + j * blk, blk), :] = _dot(perm_ref[...], window).astype(o_ref.dtype)


DFT_FOLD_BLOCK = 128


def _half_dft_tables(n):
    c, s = _dft_tables(n)
    half = n // 2
    cos_ext = np.zeros((half + BF16_ROWS, half))
    cos_ext[:half + 1] = c[:half + 1, :half]
    return cos_ext, s[:half, :half]


def _mirror_permutation(blk):
    p = np.zeros((blk, 2 * blk), np.float32)
    i = np.arange(blk)
    p[i, blk - i] = 1.0
    return p


def _dft_tables(n):
    idx = np.arange(n, dtype=np.int64)
    ang = 2.0 * np.pi * ((idx[:, None] * idx[None, :]) % n).astype(np.float64) / n
    s = 1.0 / math.sqrt(n)
    return np.cos(ang) * s, np.sin(ang) * s


def fourier_branch(xn3, w_in_b, layer, fu_off_blocks, tm=512):
    B, S, D = xn3.shape
    W = D
    gdim = W // FNET_GROUPS
    c_small, s_small = _dft_tables(gdim)
    cs_small = jnp.asarray(np.concatenate([c_small, s_small], axis=1), dtype=BF16)
    half = S // 2
    assert half % 2 == 0 and half % DFT_FOLD_BLOCK == 0
    cos_np, sin_np = _half_dft_tables(S)
    dft_cos = jnp.asarray(cos_np, dtype=BF16)
    dft_sin = jnp.asarray(sin_np, dtype=BF16)
    perm = jnp.asarray(_mirror_permutation(DFT_FOLD_BLOCK), dtype=BF16)
    tiles = S // tm
    rhs = pl.pallas_call(
        functools.partial(_fnet_proj_kernel, gdim=gdim),
        out_shape=jax.ShapeDtypeStruct((B, 2, S, W), BF16),
        grid=(B, tiles),
        in_specs=[pl.BlockSpec((None, tm, D), lambda b, r: (b, r, 0)),
                  pl.BlockSpec((None, D, W), lambda b, r: (layer, 0, fu_off_blocks)),
                  pl.BlockSpec((gdim, 2 * gdim), lambda b, r: (0, 0))],
        out_specs=pl.BlockSpec((None, 2, tm, W), lambda b, r: (b, 0, r, 0)),
        compiler_params=_params("parallel", "parallel"),
        name="fnet_proj",
    )(xn3, w_in_b, cs_small)
    rhs = rhs.reshape(B, 2 * S, W)
    return pl.pallas_call(
        functools.partial(_seq_dft_kernel, seq=S),
        out_shape=jax.ShapeDtypeStruct((B, S, W), BF16),
        grid=(B,),
        in_specs=[_resident((half + BF16_ROWS, half), lambda b: (0, 0)),
                  _resident((half, half), lambda b: (0, 0)),
                  pl.BlockSpec((DFT_FOLD_BLOCK, 2 * DFT_FOLD_BLOCK), lambda b: (0, 0)),
                  pl.BlockSpec((None, 2 * S, W), lambda b: (b, 0, 0))],
        out_specs=pl.BlockSpec((None, S, W), lambda b: (b, 0, 0)),
        scratch_shapes=[pltpu.VMEM((S, W), BF16),
                        pltpu.VMEM((half + DFT_FOLD_BLOCK, W), BF16)],
        compiler_params=_params("parallel"),
        name="fnet_seq_dft",
    )(dft_cos, dft_sin, perm, rhs)


def _merge_kernel(x_ref, xn_ref, ro_ref, ho_ref, fo_ref, wga_ref, wro_ref, who_ref,
                  wf_ref, wout_ref, nw_ref, o_ref, *, d):
    xn = xn_ref[...]

    def gate(i):
        return jax.nn.sigmoid(_dot(xn, wga_ref[:, i * d:(i + 1) * d]))

    mix = gate(0) * _dot(ro_ref[...], wro_ref[...])
    mix += gate(1) * _dot(ho_ref[...], who_ref[...])
    mix += gate(2) * _dot(fo_ref[...], wf_ref[...])
    y = _dot(mix.astype(BF16), wout_ref[...])
    o_ref[...] = x_ref[...] + _rms(y) * nw_ref[...]


def merge_branches(x2, xn2, ro2, ho2, fo2, w_in_b, layer, ga_off_blocks,
                   w_ret_o, w_hgrn_o, w_fnet, w_out, norm_w, tm=512):
    T, D = x2.shape
    RV = ro2.shape[1]
    tile = lambda w: pl.BlockSpec((tm, w), lambda i: (i, 0))
    return pl.pallas_call(
        functools.partial(_merge_kernel, d=D),
        out_shape=jax.ShapeDtypeStruct((T, D), F32),
        grid=(T // tm,),
        in_specs=[tile(D), tile(D), tile(RV), tile(D), tile(D),
                  _resident((None, D, N_BRANCH * D), lambda i: (layer, 0, ga_off_blocks)),
                  _resident((None, RV, D), lambda i: (layer, 0, 0)),
                  _resident((None, D, D), lambda i: (layer, 0, 0)),
                  _resident((None, D, D), lambda i: (layer, 0, 0)),
                  _resident((None, D, D), lambda i: (layer, 0, 0)),
                  pl.BlockSpec((1, D), lambda i: (0, 0))],
        out_specs=tile(D),
        compiler_params=_params("parallel"),
        name="merge",
    )(x2, xn2, ro2, ho2, fo2, w_in_b, w_ret_o, w_hgrn_o, w_fnet, w_out, norm_w.reshape(1, D))


def _ffn_kernel(x_ref, xp_ref, xnx_ref, nw_in_ref, wup_ref, cw_ref, cb_ref, wdn_ref, nw_out_ref,
                *rest, tm, tiles_per_seq, d_ff, fc, emit_next):
    if emit_next:
        nw_next_ref, o_ref, xn_ref, hn_s, acc_s = rest
    else:
        (o_ref, hn_s, acc_s), nw_next_ref, xn_ref = rest, None, None
    i = pl.program_id(0)
    r = i % tiles_per_seq
    halo = BF16_ROWS
    x = x_ref[...]
    nw = nw_in_ref[...]
    hp = jnp.where(r == 0, 0.0, _rms(xp_ref[...]) * nw)
    hx = jnp.where(r == tiles_per_seq - 1, 0.0, _rms(xnx_ref[...]) * nw)
    hn = jnp.concatenate([hp, _rms(x) * nw, hx], axis=0).astype(BF16)
    n_ext = tm + 2 * halo
    hn_s[...] = hn
    acc_s[...] = jnp.zeros_like(acc_s)

    def conv(col, scale):
        cols = pl.ds(pl.multiple_of(col, fc), fc)
        h = _dot(hn_s[...], wup_ref[:, cols])
        cw = cw_ref[:, cols] * scale
        prev = pltpu.roll(h, 1, 0)[halo:halo + tm]
        nxt = pltpu.roll(h, n_ext - 1, 0)[halo:halo + tm]
        return (cb_ref[:, cols] * scale + prev * cw[0:1] + h[halo:halo + tm] * cw[1:2]
                + nxt * cw[2:3])

    def chunk(c, carry):
        gate = conv(c * fc, 1.0)
        half_up = conv(d_ff + c * fc, 0.5)
        inner = gate * (GELU_C0 + GELU_C1 * (gate * gate))
        act = (gate * (1.0 + jnp.tanh(inner)) * half_up).astype(BF16)
        acc_s[...] += _dot(act, wdn_ref[pl.ds(pl.multiple_of(c * fc, fc), fc), :])
        return carry

    for c in range(d_ff // fc):
        chunk(c, 0)
    y = x + _rms(acc_s[...]) * nw_out_ref[...]
    o_ref[...] = y
    if xn_ref is not None:
        xn_ref[...] = (_rms(y) * nw_next_ref[...]).astype(xn_ref.dtype)


def conv_ffn_block(x2, seq, w_up, conv_w, conv_b, w_down, nw_in, nw_out, layer, nw_next=None,
                   tm=1024, fc=256):
    T, D = x2.shape
    d_ff = w_down.shape[1]
    halo = BF16_ROWS
    tps = seq // tm
    hb = tm // halo
    n_hb = T // halo
    assert seq % tm == 0, "a row tile must not straddle two sequences (the conv zero-pads each)"
    emit_next = nw_next is not None
    kern = functools.partial(_ffn_kernel, tm=tm, tiles_per_seq=tps, d_ff=d_ff, fc=fc,
                             emit_next=emit_next)
    vec = pl.BlockSpec((1, D), lambda i: (0, 0))
    tile = pl.BlockSpec((tm, D), lambda i: (i, 0))
    in_specs = [tile,
                pl.BlockSpec((halo, D), lambda i: (jnp.maximum(i * hb - 1, 0), 0)),
                pl.BlockSpec((halo, D), lambda i: (jnp.minimum((i + 1) * hb, n_hb - 1), 0)),
                vec,
                _resident((None, D, 2 * d_ff), lambda i: (layer, 0, 0)),
                pl.BlockSpec((None, CONV_W, 2 * d_ff), lambda i: (layer, 0, 0)),
                pl.BlockSpec((None, 1, 2 * d_ff), lambda i: (layer, 0, 0)),
                _resident((None, d_ff, D), lambda i: (layer, 0, 0)),
                vec]
    args = [x2, x2, x2, nw_in.reshape(1, D), w_up, conv_w, conv_b, w_down, nw_out.reshape(1, D)]
    out_shape = jax.ShapeDtypeStruct((T, D), F32)
    out_specs = tile
    if emit_next:
        in_specs.append(vec)
        args.append(nw_next.reshape(1, D))
        out_shape = (out_shape, jax.ShapeDtypeStruct((T, D), BF16))
        out_specs = (tile, tile)
    return pl.pallas_call(
        kern,
        out_shape=out_shape,
        grid=(T // tm,),
        in_specs=in_specs,
        out_specs=out_specs,
        scratch_shapes=[pltpu.VMEM((tm + 2 * halo, D), BF16),
                        pltpu.VMEM((tm, D), F32)],
        compiler_params=_params("parallel"),
        name="conv_ffn",
    )(*args)


def kernel(x, positions, norm_w, w_in, hgrn_lb_logits, hgrn_norm_w, w_ret_o, w_hgrn_o,
           w_fnet, w_out, w_up, conv_w, conv_b, w_down):
    B, S, D = x.shape
    depth = w_in.shape[0]
    T = B * S

    hgrn_off = 2 * D + 2 * 2 * D
    fu_off = hgrn_off + 5 * D
    ga_off = fu_off + D

    w_in_b = w_in.astype(BF16)
    w_ret_o_b = w_ret_o.astype(BF16)
    w_hgrn_o_b = w_hgrn_o.astype(BF16)
    w_fnet_b = w_fnet.astype(BF16)
    w_out_b = w_out.astype(BF16)
    w_up_b = w_up.astype(BF16)
    w_down_b = w_down.astype(BF16)
    conv_b3 = conv_b.reshape(depth, 1, -1)

    log_gamma = jnp.log(1.0 - 2.0 ** (-5.0 - jnp.arange(RET_HEADS, dtype=F32)))
    p = jax.nn.softmax(hgrn_lb_logits.astype(F32), axis=1)
    lower_bounds = jnp.cumsum(p, axis=1) - p[:, :1]

    cos, sin, xn3 = rope_tables_and_norm(positions, D // RET_HEADS // 2, x, norm_w[0, 0])

    x2 = x.reshape(T, D)
    xn2 = xn3.reshape(T, D)
    for l in range(depth):
        xn3 = xn2.reshape(B, S, D)
        ro = retention_branch(xn3, w_in_b, l, cos, sin, log_gamma)
        ho = hgrn_branch(xn3, w_in_b, l, hgrn_off, lower_bounds[0, l], lower_bounds[1, l],
                         hgrn_norm_w[l])
        fo = fourier_branch(xn3, w_in_b, l, fu_off // D)
        x2 = merge_branches(x2, xn2, ro.reshape(T, -1), ho.reshape(T, D), fo.reshape(T, D),
                            w_in_b, l, ga_off // (N_BRANCH * D),
                            w_ret_o_b, w_hgrn_o_b, w_fnet_b, w_out_b, norm_w[l, 1])
        if l + 1 < depth:
            x2, xn2 = conv_ffn_block(x2, S, w_up_b, conv_w, conv_b3, w_down_b,
                                     norm_w[l, 2], norm_w[l, 3], l, nw_next=norm_w[l + 1, 0])
        else:
            x2 = conv_ffn_block(x2, S, w_up_b, conv_w, conv_b3, w_down_b,
                                norm_w[l, 2], norm_w[l, 3], l)
    return x2.reshape(B, S, D)
```

```python
import functools
import math

import numpy as np
import jax
import jax.numpy as jnp
from jax import lax
from jax.experimental import pallas as pl
from jax.experimental.pallas import tpu as pltpu

F32 = jnp.float32
BF16 = jnp.bfloat16

RET_HEADS = 4
HGRN_HEADS = 8
FNET_GROUPS = 4
N_BRANCH = 3
CONV_W = 3
ROPE_BASE = 10000.0
LB_FLOOR = 1e-30
EPS = 1e-6
LOG2_E = 1.4426950408889634
GELU_C0 = math.sqrt(2.0 / math.pi)
GELU_C1 = GELU_C0 * 0.044715

V7X_VMEM_LIMIT_BYTES = 56 * 1024 * 1024
SUBLANES = 8
BF16_ROWS = 16

RET_CHUNK = 256
HGRN_CHUNK = 128
HGRN_HEADS_PER_STEP = 2
HGRN_VPU_LEVELS = (0,)
ROW_TILE = 512
HGRN_ROW_TILE = 1024


def _dot(a, b):
    return jnp.dot(a, b, preferred_element_type=F32)


def _dot_nt(a, b):
    return lax.dot_general(a, b, (((1,), (1,)), ((), ())), preferred_element_type=F32)


def _dot_tn(a, b):
    return lax.dot_general(a, b, (((0,), (0,)), ((), ())), preferred_element_type=F32)


def _silu(x, scale=1.0):
    return (x * scale if scale != 1.0 else x) / (1.0 + jnp.exp2(x * (-LOG2_E)))


def _rms(x):
    return x * lax.rsqrt(jnp.mean(x * x, axis=-1, keepdims=True) + EPS)


def _params(*sem):
    return pltpu.CompilerParams(dimension_semantics=sem,
                                vmem_limit_bytes=V7X_VMEM_LIMIT_BYTES)


def _resident(shape, index_map):
    return pl.BlockSpec(shape, index_map, pipeline_mode=pl.Buffered(1))


def _rope_norm_kernel(pos_ref, invf_ref, x_ref, w_ref, cos_ref, sin_ref, xn_ref):
    ang = pos_ref[...] * invf_ref[...]
    cos_ref[...] = jnp.cos(ang)
    sin_ref[...] = jnp.sin(ang)
    xn_ref[...] = (_rms(x_ref[...]) * w_ref[...]).astype(xn_ref.dtype)


def rope_tables_and_norm(positions, half, x, w):
    B, S = positions.shape
    D = x.shape[-1]
    pos = positions.astype(F32).reshape(B, S, 1)
    inv_freq = (ROPE_BASE ** (-jnp.arange(half, dtype=F32) / half)).reshape(1, half)
    table = jax.ShapeDtypeStruct((B, S, half), F32)
    table_spec = pl.BlockSpec((None, S, half), lambda b: (b, 0, 0))
    rows_spec = pl.BlockSpec((None, S, D), lambda b: (b, 0, 0))
    return pl.pallas_call(
        _rope_norm_kernel,
        out_shape=(table, table, jax.ShapeDtypeStruct((B, S, D), BF16)),
        grid=(B,),
        in_specs=[pl.BlockSpec((None, S, 1), lambda b: (b, 0, 0)),
                  pl.BlockSpec((1, half), lambda b: (0, 0)),
                  rows_spec,
                  pl.BlockSpec((1, D), lambda b: (0, 0))],
        out_specs=(table_spec, table_spec, rows_spec),
        compiler_params=_params("parallel"),
        name="rope_tables_norm",
    )(pos, inv_freq, x, w.reshape(1, D))


def _ret_kernel(lg_ref, xn_ref, wq_ref, wk_ref, wv_ref, wg_ref, cos_ref, sin_ref,
                o_ref, qi_s, qd_s, ki_s, v_s, g_s, st_s, kvb_s, run_s, *, seq, dk, dv):
    C = RET_CHUNK
    R = seq // C
    half = dk // 2
    lg = lg_ref[pl.program_id(1)]
    ret_scale = dk ** -0.5

    def rows_of(n):
        return pl.ds(pl.multiple_of(n * C, C), C)

    pos = lax.broadcasted_iota(jnp.int32, (C, 1), 0).astype(F32)
    qdec_f = jnp.exp(lg * (pos + 1.0))
    qdec_b = jnp.exp(lg * (C - pos))
    kdec_f = jnp.exp(lg * (C - 1.0 - pos))
    kdec_b = jnp.exp(lg * pos)
    chunk_dec = jnp.exp(lg * C)
    ii = lax.broadcasted_iota(jnp.int32, (C, C), 0)
    jj = lax.broadcasted_iota(jnp.int32, (C, C), 1)
    decay = jnp.exp(lg * jnp.abs(ii - jj).astype(F32))

    run_s[...] = jnp.zeros_like(run_s)

    def proj(t, carry):
        rows = pl.ds(pl.multiple_of(t * ROW_TILE, ROW_TILE), ROW_TILE)
        xc = xn_ref[rows, :]
        cos = cos_ref[rows, :]
        sin = sin_ref[rows, :]
        q = _dot(xc, wq_ref[...])
        q1, q2 = q[:, :half], q[:, half:]
        q = jnp.concatenate([q1 * cos - q2 * sin, q1 * sin + q2 * cos], axis=-1)
        k = _dot(xc, wk_ref[...]) * ret_scale
        k1, k2 = k[:, :half], k[:, half:]
        k = jnp.concatenate([k1 * cos - k2 * sin, k1 * sin + k2 * cos], axis=-1)
        v = _dot(xc, wv_ref[...]).astype(BF16)
        g = _dot(xc, wg_ref[...])
        qi_s[rows, :] = q.astype(BF16)
        ki_s[rows, :] = k.astype(BF16)
        v_s[rows, :] = v
        g_s[rows, :] = _silu(g).astype(BF16)
        for j in range(ROW_TILE // C):
            n = t * (ROW_TILE // C) + j
            sl = slice(j * C, (j + 1) * C)
            qd_s[rows_of(n), :] = jnp.concatenate([q[sl] * qdec_f, q[sl] * qdec_b],
                                                  axis=-1).astype(BF16)
            st_s[n, pl.ds(0, dk), :] = run_s[...].astype(BF16)
            run_s[...] = run_s[...] * chunk_dec + _dot_tn((k[sl] * kdec_f).astype(BF16), v[sl])
            kvb_s[n] = _dot_tn((k[sl] * kdec_b).astype(BF16), v[sl])
        return carry

    lax.fori_loop(0, seq // ROW_TILE, proj, 0, unroll=2)

    run_s[...] = jnp.zeros_like(run_s)

    def out(t, carry):
        n = R - 1 - t
        rows = rows_of(n)
        st_s[n, pl.ds(dk, dk), :] = run_s[...].astype(BF16)
        s = _dot_nt(qi_s[rows, :], ki_s[rows, :]) * decay
        o = _dot(s.astype(BF16), v_s[rows, :]) + _dot(qd_s[rows, :], st_s[n])
        o_ref[rows, :] = (_rms(o) * g_s[rows, :].astype(F32)).astype(o_ref.dtype)
        run_s[...] = run_s[...] * chunk_dec + kvb_s[n]
        return carry

    lax.fori_loop(0, R, out, 0, unroll=8)


def retention_branch(xn3, w_in_b, layer, cos, sin, log_gamma):
    B, S, D = xn3.shape
    dk = D // RET_HEADS
    dv = 2 * dk
    H = RET_HEADS
    qk_blocks = D // dk
    v_off = 2 * D // dv
    g_off = v_off + H
    kern = functools.partial(_ret_kernel, seq=S, dk=dk, dv=dv)
    return pl.pallas_call(
        kern,
        out_shape=jax.ShapeDtypeStruct((B, S, H * dv), BF16),
        grid=(B, H),
        in_specs=[
            pl.BlockSpec(memory_space=pltpu.SMEM),
            pl.BlockSpec((None, S, D), lambda b, h: (b, 0, 0)),
            pl.BlockSpec((None, D, dk), lambda b, h: (layer, 0, h)),
            pl.BlockSpec((None, D, dk), lambda b, h: (layer, 0, qk_blocks + h)),
            pl.BlockSpec((None, D, dv), lambda b, h: (layer, 0, v_off + h)),
            pl.BlockSpec((None, D, dv), lambda b, h: (layer, 0, g_off + h)),
            pl.BlockSpec((None, S, dk // 2), lambda b, h: (b, 0, 0)),
            pl.BlockSpec((None, S, dk // 2), lambda b, h: (b, 0, 0)),
        ],
        out_specs=pl.BlockSpec((None, S, dv), lambda b, h: (b, 0, h)),
        scratch_shapes=[
            pltpu.VMEM((S, dk), BF16),
            pltpu.VMEM((S, 2 * dk), BF16),
            pltpu.VMEM((S, dk), BF16),
            pltpu.VMEM((S, dv), BF16),
            pltpu.VMEM((S, dv), BF16),
            pltpu.VMEM((S // RET_CHUNK, 2 * dk, dv), BF16),
            pltpu.VMEM((S // RET_CHUNK, dk, dv), F32),
            pltpu.VMEM((dk, dv), F32),
        ],
        compiler_params=_params("parallel", "arbitrary"),
        name="retention",
    )(log_gamma, xn3, w_in_b, w_in_b, w_in_b, w_in_b, cos, sin)


def _hgrn_gate(z, lb):
    e = jnp.exp2(jnp.abs(z) * (-LOG2_E))
    pos = z >= 0.0
    sig_neg_num = jnp.where(pos, e, 1.0)
    num = jnp.where(pos, 1.0, e) + jnp.maximum(lb, LB_FLOOR) * sig_neg_num
    inv = 1.0 / (1.0 + e)
    log2_f = jnp.log2(num * inv)
    return log2_f, (1.0 - lb) * sig_neg_num * inv


def _boundary_rows(cum_ref, base, m, reverse, row_in_group):
    C = HGRN_CHUNK
    d = cum_ref.shape[1]
    blk = 2 * m
    off = m if reverse else m - 1
    pieces = []
    if blk >= SUBLANES:
        for b in range(C // blk):
            pieces.append(jnp.broadcast_to(cum_ref[pl.ds(base + (b * blk + off), 1), :], (blk, d)))
    else:
        for g in range(C // SUBLANES):
            val = None
            for u in range(SUBLANES // blk):
                row = g * SUBLANES + u * blk + off
                piece = jnp.broadcast_to(cum_ref[pl.ds(base + row, 1), :], (SUBLANES, d))
                val = piece if val is None else jnp.where(row_in_group >= u * blk, piece, val)
            pieces.append(val)
    return jnp.concatenate(pieces, axis=0) if len(pieces) > 1 else pieces[0]


def _level_operands(level, q, k, cum, cum_ref, base, consts, reverse):
    C = HGRN_CHUNK
    _, row_in_group, signs, _ = consts
    m = 2 ** level
    if 2 * m <= SUBLANES:
        sign = signs[level]
        ref_pt = _boundary_rows(cum_ref, base, m, reverse, row_in_group)
        x = (jnp.where(sign > 0.0, q, k) * jnp.exp2((cum - ref_pt) * sign)).astype(BF16)
        return x, x, list(range(C // SUBLANES))
    xq, xall, q_groups = [], [], []
    for b in range(C // (2 * m)):
        first = slice(b * 2 * m, b * 2 * m + m)
        second = slice(b * 2 * m + m, (b + 1) * 2 * m)
        q_rows, k_rows = (first, second) if reverse else (second, first)
        edge = k_rows.start if reverse else k_rows.stop - 1
        ref_pt = cum_ref[pl.ds(base + edge, 1), :]
        xq_b = q[q_rows] * jnp.exp2(cum[q_rows] - ref_pt)
        xk_b = k[k_rows] * jnp.exp2(ref_pt - cum[k_rows])
        xq.append(xq_b)
        xall.extend([xq_b, xk_b] if reverse else [xk_b, xq_b])
        q_groups.extend(range(q_rows.start // SUBLANES, q_rows.stop // SUBLANES))
    return (jnp.concatenate(xq, axis=0).astype(BF16), jnp.concatenate(xall, axis=0).astype(BF16),
            q_groups)


def _paired_dot_nt(lhs_a, rhs_a, lhs_b, rhs_b):
    rhs = jnp.concatenate([rhs_a, rhs_b], axis=1)
    lhs = jnp.concatenate(
        [jnp.concatenate([lhs_a, jnp.zeros_like(lhs_a)], axis=1),
         jnp.concatenate([jnp.zeros_like(lhs_b), lhs_b], axis=1)], axis=0)
    s = _dot_nt(lhs, rhs)
    return s[:lhs_a.shape[0]], s[lhs_a.shape[0]:]


def _assemble_scores(scores, level_id):
    C = HGRN_CHUNK
    rows = [jnp.zeros((SUBLANES, C), F32) for _ in range(C // SUBLANES)]
    for level, entry in enumerate(scores):
        if entry is None:
            continue
        s, q_groups = entry
        for i, g in enumerate(q_groups):
            lid = level_id[g * SUBLANES:(g + 1) * SUBLANES]
            rows[g] = jnp.where(lid == level, s[i * SUBLANES:(i + 1) * SUBLANES], rows[g])
    return jnp.concatenate(rows, axis=0)


def _hgrn_intra_pair(fwd, bwd, consts):
    C = HGRN_CHUNK
    args = ((fwd, consts[0], False), (bwd, consts[1], True))
    n_vpu = len(HGRN_VPU_LEVELS)
    scores = ([None] * n_vpu, [None] * n_vpu)
    for level in range(n_vpu, C.bit_length() - 1):
        ops = [_level_operands(level, q, k, cum, cum_ref, base, cst, rev)
               for (q, k, _, cum, cum_ref, base), cst, rev in args]
        s_f, s_b = _paired_dot_nt(ops[0][0], ops[0][1], ops[1][0], ops[1][1])
        scores[0].append((s_f, ops[0][2]))
        scores[1].append((s_b, ops[1][2]))
    outs = []
    for idx, ((q, k, v, cum, _, _), cst, rev) in enumerate(args):
        attn = _assemble_scores(scores[idx], cst[0])
        o = _dot(attn.astype(BF16), v.astype(BF16))
        o += jnp.sum(q * k, axis=-1, keepdims=True) * v
        outs.append(o + _near_pairs(q, k, v, cum, rev, cst[3]))
    return outs


def _near_pairs(q, k, v, cum, reverse, masks):
    C, d = q.shape
    shape3 = (C // SUBLANES, SUBLANES, d)
    out = jnp.zeros((C, d), F32)
    for offset, mask in enumerate(masks, start=1):
        valid = mask != 0
        shift = (SUBLANES - offset) if reverse else offset

        def key_row(x):
            return pltpu.roll(x.reshape(shape3), shift, 1).reshape(C, d)

        w = jnp.exp2(jnp.where(valid, cum - key_row(cum), 0.0))
        score = jnp.sum(q * key_row(k) * w, axis=-1, keepdims=True)
        out += jnp.where(valid, score * key_row(v), 0.0)
    return out


def _near_pair_masks(C, d, reverse, levels):
    pos = lax.broadcasted_iota(jnp.int32, (C, d), 0) % SUBLANES
    masks = []
    for offset in range(1, 2 ** (max(levels) + 1)):
        valid = jnp.zeros((C, d), jnp.int32)
        for level in levels:
            m, blk = 2 ** level, 2 ** (level + 1)
            r = pos % blk
            key = (r + offset) if reverse else (r - offset)
            if reverse:
                ok = (r < m) & (key >= m) & (key < blk)
            else:
                ok = (r >= m) & (key >= 0) & (key < m)
            valid = jnp.where(ok, 1, valid)
        masks.append(valid)
    return masks


def _hgrn_consts(reverse, d):
    C = HGRN_CHUNK
    ii = lax.broadcasted_iota(jnp.int32, (C, C), 0)
    jj = lax.broadcasted_iota(jnp.int32, (C, C), 1)
    diff = ii ^ jj
    level_id = jnp.full((C, C), -1, jnp.int32)
    n_levels = C.bit_length() - 1
    for level in range(n_levels):
        level_id = jnp.where((diff >> level) == 1, level, level_id)
    level_id = jnp.where((ii < jj) if reverse else (ii > jj), level_id, -1)
    rows = lax.broadcasted_iota(jnp.int32, (C, d), 0)
    row_in_group = lax.broadcasted_iota(jnp.int32, (SUBLANES, d), 0)
    signs = []
    for level in range(SUBLANES.bit_length() - 1):
        second = ((rows >> level) & 1) == 1
        is_query = jnp.logical_not(second) if reverse else second
        signs.append(jnp.where(is_query, 1.0, -1.0))
    return level_id, row_in_group, signs, _near_pair_masks(C, d, reverse, HGRN_VPU_LEVELS)


def _chunk_cumsum(x, reverse, row_in_group):
    rows, d = x.shape
    groups = rows // SUBLANES
    per_chunk = HGRN_CHUNK // SUBLANES
    y = x.reshape(groups, SUBLANES, d)
    step = 1
    while step < SUBLANES:
        rolled = pltpu.roll(y, (SUBLANES - step) if reverse else step, 1)
        valid = (row_in_group < SUBLANES - step) if reverse else (row_in_group >= step)
        y = y + jnp.where(valid, rolled, 0.0)
        step *= 2
    out = [None] * groups
    for c in range(rows // HGRN_CHUNK):
        order = range(c * per_chunk, (c + 1) * per_chunk)
        carry = None
        for g in (reversed(order) if reverse else order):
            yg = y[g] if carry is None else y[g] + carry
            out[g] = yg
            edge = 0 if reverse else SUBLANES - 1
            carry = jnp.broadcast_to(yg[edge:edge + 1, :], (SUBLANES, d))
    return jnp.concatenate(out, axis=0)


def _hgrn_kernel(xn_ref, wq_ref, wzf_ref, wzb_ref, wi_ref, wg_ref, lbf_ref, lbb_ref, nw_ref, o_ref,
                 q_s, v_s, g_s, acc_s, kf_s, kb_s, lff_s, lfb_s, cum_s, *, seq, dk, heads):
    C = HGRN_CHUNK
    R = seq // C
    row_tile = HGRN_ROW_TILE
    scale = dk ** -0.5

    def tile_rows(n):
        return pl.ds(pl.multiple_of(n * row_tile, row_tile), row_tile)

    def head_cols(h):
        return slice(h * dk, (h + 1) * dk)

    dirs = [((False, lbf_ref, kf_s.at[h], lff_s.at[h], cum_s.at[h, 0]),
             (True, lbb_ref, kb_s.at[h], lfb_s.at[h], cum_s.at[h, 1]))
            for h in range(heads)]
    row_in_group = lax.broadcasted_iota(jnp.int32, (1, SUBLANES, dk), 1)

    def proj(n, carry):
        rows = tile_rows(n)
        xc = xn_ref[rows, :]
        hq_all = _dot(xc, wq_ref[...])
        z_all = (_dot(xc, wzf_ref[...]), _dot(xc, wzb_ref[...]))
        v_all = _dot(xc, wi_ref[...])
        hg_all = _dot(xc, wg_ref[...])
        for h in range(heads):
            cols = head_cols(h)
            hq = hq_all[:, cols]
            q = _silu(hq, scale)
            q_s[h, rows, :] = q
            for idx, (reverse, lb_ref, k_s, lf_s, _) in enumerate(dirs[h]):
                lf, kk = _hgrn_gate(z_all[idx][:, cols], lb_ref[:, cols])
                k_s[rows, :] = kk
                lf_s[rows, :] = lf
            v_s[h, rows, :] = v_all[:, cols]
            hg = hg_all[:, cols]
            g_s[h, rows, :] = hg
            acc_s[h, rows, :] = jnp.zeros((row_tile, dk), F32)
        return carry

    lax.fori_loop(0, seq // row_tile, proj, 0, unroll=2)

    consts = (_hgrn_consts(False, dk), _hgrn_consts(True, dk))

    def chunk_pair(h, cf, cb, states):
        data, rows, q_dec, k_dec, decs = [], [], [], [], []
        for c, (reverse, _, k_s, lf_s, cum_c) in zip((cf, cb), dirs[h]):
            r = pl.ds(pl.multiple_of(c * C, C), C)
            q, k = q_s[h, r, :], k_s[r, :]
            cum = _chunk_cumsum(lf_s[r, :], reverse, row_in_group)
            cum_c[...] = cum
            data.append((q, k, v_s[h, r, :], cum, cum_c, 0))
            rows.append(r)
            total = cum_c[pl.ds(0 if reverse else C - 1, 1), :]
            q_dec.append((q * jnp.exp2(cum)).astype(BF16))
            k_dec.append((k * jnp.exp2(total - cum)).astype(BF16))
            decs.append(jnp.exp2(total))
        o_f, o_b = _hgrn_intra_pair(data[0], data[1], consts)
        i_f, i_b = _paired_dot_nt(q_dec[0], states[0].astype(BF16), q_dec[1], states[1].astype(BF16))
        acc_s[h, rows[0], :] += o_f + i_f
        acc_s[h, rows[1], :] += o_b + i_b
        return tuple(st * dec + _dot_tn(d[2].astype(BF16), kd)
                     for st, dec, d, kd in zip(states, decs, data, k_dec))

    def step(i, states):
        return tuple(chunk_pair(h, i, R - 1 - i, states[h]) for h in range(heads))

    zero = jnp.zeros((dk, dk), F32)
    lax.fori_loop(0, R, step, tuple((zero, zero) for _ in range(heads)))

    def finish(n, carry):
        rows = tile_rows(n)
        for h in range(heads):
            o_ref[rows, head_cols(h)] = (_rms(acc_s[h, rows, :]) * nw_ref[...]
                                         * _silu(g_s[h, rows, :])).astype(o_ref.dtype)
        return carry

    lax.fori_loop(0, seq // row_tile, finish, 0)


def hgrn_branch(xn3, w_in_b, layer, hgrn_off, lb_f, lb_b, norm_w):
    B, S, D = xn3.shape
    dk = D // HGRN_HEADS
    hp = HGRN_HEADS_PER_STEP
    wide = hp * dk
    kern = functools.partial(_hgrn_kernel, seq=S, dk=dk, heads=hp)
    vec = lambda: pltpu.VMEM((hp, S, dk), F32)

    def w_spec(group):
        first = (hgrn_off + group * D) // wide
        return pl.BlockSpec((None, D, wide), lambda b, j: (layer, 0, first + j))

    lb_spec = pl.BlockSpec((1, wide), lambda b, j: (0, j))
    return pl.pallas_call(
        kern,
        out_shape=jax.ShapeDtypeStruct((B, S, D), BF16),
        grid=(B, HGRN_HEADS // hp),
        in_specs=[pl.BlockSpec((None, S, D), lambda b, j: (b, 0, 0)),
                  w_spec(0), w_spec(1), w_spec(2), w_spec(3), w_spec(4),
                  lb_spec, lb_spec,
                  pl.BlockSpec((1, dk), lambda b, j: (0, 0))],
        out_specs=pl.BlockSpec((None, S, wide), lambda b, j: (b, 0, j)),
        scratch_shapes=[vec(), vec(), vec(), vec(), vec(), vec(), vec(), vec(),
                        pltpu.VMEM((hp, 2, HGRN_CHUNK, dk), F32)],
        compiler_params=_params("parallel", "arbitrary"),
        name="hgrn2",
    )(xn3, w_in_b, w_in_b, w_in_b, w_in_b, w_in_b,
      lb_f.reshape(1, D), lb_b.reshape(1, D), norm_w.reshape(1, dk))


def _fnet_proj_kernel(xn_ref, w_ref, cs_ref, o_ref, *, gdim):
    fu = _dot(xn_ref[...], w_ref[...]).astype(BF16)
    for g in range(FNET_GROUPS):
        t = _dot(fu[:, g * gdim:(g + 1) * gdim], cs_ref[...])
        o_ref[0, :, g * gdim:(g + 1) * gdim] = t[:, :gdim].astype(o_ref.dtype)
        o_ref[1, :, g * gdim:(g + 1) * gdim] = t[:, gdim:].astype(o_ref.dtype)


def _seq_dft_kernel(cos_ref, sin_ref, perm_ref, rhs_ref, o_ref, fold_s, mir_s, *, seq):
    n = seq
    half = n // 2
    blk = perm_ref.shape[0]
    c0 = 1.0 / math.sqrt(n)

    for part, sign in ((0, 1.0), (1, -1.0)):
        base = part * n
        for j in range(half // blk):
            own = rhs_ref[pl.ds(base + j * blk, blk), :].astype(F32)
            if j == 0:
                mirror = _dot(perm_ref[:, :blk], rhs_ref[pl.ds(base + n - blk, blk), :])
            else:
                mirror = _dot(perm_ref[...], rhs_ref[pl.ds(base + n - (j + 1) * blk, 2 * blk), :])
            fold_s[pl.ds(part * half + j * blk, blk), :] = (own + sign * mirror).astype(fold_s.dtype)

    mid = rhs_ref[pl.ds(half, BF16_ROWS), :].astype(F32)[0:1] * c0
    p_ext = _dot(cos_ref[...], fold_s[pl.ds(0, half), :])
    q = _dot(sin_ref[...], fold_s[pl.ds(half, half), :])
    row = lax.broadcasted_iota(jnp.int32, q.shape, 0)
    p = p_ext[:half] + jnp.where((row & 1) == 0, 1.0, -1.0) * mid
    o_ref[pl.ds(0, half), :] = (p - q).astype(o_ref.dtype)

    mir_s[pl.ds(0, half), :] = (p + q).astype(mir_s.dtype)
    first = jnp.where(lax.broadcasted_iota(jnp.int32, (BF16_ROWS, q.shape[1]), 0) == 0, 1.0, 0.0)
    tail = (p_ext[half:half + BF16_ROWS] + mid) * first
    mir_s[pl.ds(half, BF16_ROWS), :] = tail.astype(mir_s.dtype)
    mir_s[pl.ds(half + BF16_ROWS, blk - BF16_ROWS), :] = jnp.zeros(
        (blk - BF16_ROWS, tail.shape[1]), mir_s.dtype)
    for j in range(half // blk):
        window = mir_s[pl.ds(half - (j + 1) * blk, 2 * blk), :]
        o_ref[pl.ds(half + j * blk, blk), :] = _dot(perm_ref[...], window).astype(o_ref.dtype)


DFT_FOLD_BLOCK = 128


def _half_dft_tables(n):
    c, s = _dft_tables(n)
    half = n // 2
    cos_ext = np.zeros((half + BF16_ROWS, half))
    cos_ext[:half + 1] = c[:half + 1, :half]
    return cos_ext, s[:half, :half]


def _mirror_permutation(blk):
    p = np.zeros((blk, 2 * blk), np.float32)
    i = np.arange(blk)
    p[i, blk - i] = 1.0
    return p


def _dft_tables(n):
    idx = np.arange(n, dtype=np.int64)
    ang = 2.0 * np.pi * ((idx[:, None] * idx[None, :]) % n).astype(np.float64) / n
    s = 1.0 / math.sqrt(n)
    return np.cos(ang) * s, np.sin(ang) * s


def fourier_branch(xn3, w_in_b, layer, fu_off_blocks, tm=512):
    B, S, D = xn3.shape
    W = D
    gdim = W // FNET_GROUPS
    c_small, s_small = _dft_tables(gdim)
    cs_small = jnp.asarray(np.concatenate([c_small, s_small], axis=1), dtype=BF16)
    half = S // 2
    assert half % 2 == 0 and half % DFT_FOLD_BLOCK == 0
    cos_np, sin_np = _half_dft_tables(S)
    dft_cos = jnp.asarray(cos_np, dtype=BF16)
    dft_sin = jnp.asarray(sin_np, dtype=BF16)
    perm = jnp.asarray(_mirror_permutation(DFT_FOLD_BLOCK), dtype=BF16)
    tiles = S // tm
    rhs = pl.pallas_call(
        functools.partial(_fnet_proj_kernel, gdim=gdim),
        out_shape=jax.ShapeDtypeStruct((B, 2, S, W), BF16),
        grid=(B, tiles),
        in_specs=[pl.BlockSpec((None, tm, D), lambda b, r: (b, r, 0)),
                  pl.BlockSpec((None, D, W), lambda b, r: (layer, 0, fu_off_blocks)),
                  pl.BlockSpec((gdim, 2 * gdim), lambda b, r: (0, 0))],
        out_specs=pl.BlockSpec((None, 2, tm, W), lambda b, r: (b, 0, r, 0)),
        compiler_params=_params("parallel", "parallel"),
        name="fnet_proj",
    )(xn3, w_in_b, cs_small)
    rhs = rhs.reshape(B, 2 * S, W)
    return pl.pallas_call(
        functools.partial(_seq_dft_kernel, seq=S),
        out_shape=jax.ShapeDtypeStruct((B, S, W), BF16),
        grid=(B,),
        in_specs=[_resident((half + BF16_ROWS, half), lambda b: (0, 0)),
                  _resident((half, half), lambda b: (0, 0)),
                  pl.BlockSpec((DFT_FOLD_BLOCK, 2 * DFT_FOLD_BLOCK), lambda b: (0, 0)),
                  pl.BlockSpec((None, 2 * S, W), lambda b: (b, 0, 0))],
        out_specs=pl.BlockSpec((None, S, W), lambda b: (b, 0, 0)),
        scratch_shapes=[pltpu.VMEM((S, W), BF16),
                        pltpu.VMEM((half + DFT_FOLD_BLOCK, W), BF16)],
        compiler_params=_params("parallel"),
        name="fnet_seq_dft",
    )(dft_cos, dft_sin, perm, rhs)


def _merge_kernel(x_ref, xn_ref, ro_ref, ho_ref, fo_ref, wga_ref, wro_ref, who_ref,
                  wf_ref, wout_ref, nw_ref, o_ref, *, d):
    xn = xn_ref[...]

    def gate(i):
        return jax.nn.sigmoid(_dot(xn, wga_ref[:, i * d:(i + 1) * d]))

    mix = gate(0) * _dot(ro_ref[...], wro_ref[...])
    mix += gate(1) * _dot(ho_ref[...], who_ref[...])
    mix += gate(2) * _dot(fo_ref[...], wf_ref[...])
    y = _dot(mix.astype(BF16), wout_ref[...])
    o_ref[...] = x_ref[...] + _rms(y) * nw_ref[...]


def merge_branches(x2, xn2, ro2, ho2, fo2, w_in_b, layer, ga_off_blocks,
                   w_ret_o, w_hgrn_o, w_fnet, w_out, norm_w, tm=512):
    T, D = x2.shape
    RV = ro2.shape[1]
    tile = lambda w: pl.BlockSpec((tm, w), lambda i: (i, 0))
    return pl.pallas_call(
        functools.partial(_merge_kernel, d=D),
        out_shape=jax.ShapeDtypeStruct((T, D), F32),
        grid=(T // tm,),
        in_specs=[tile(D), tile(D), tile(RV), tile(D), tile(D),
                  _resident((None, D, N_BRANCH * D), lambda i: (layer, 0, ga_off_blocks)),
                  _resident((None, RV, D), lambda i: (layer, 0, 0)),
                  _resident((None, D, D), lambda i: (layer, 0, 0)),
                  _resident((None, D, D), lambda i: (layer, 0, 0)),
                  _resident((None, D, D), lambda i: (layer, 0, 0)),
                  pl.BlockSpec((1, D), lambda i: (0, 0))],
        out_specs=tile(D),
        compiler_params=_params("parallel"),
        name="merge",
    )(x2, xn2, ro2, ho2, fo2, w_in_b, w_ret_o, w_hgrn_o, w_fnet, w_out, norm_w.reshape(1, D))


def _ffn_kernel(x_ref, xp_ref, xnx_ref, nw_in_ref, wup_ref, cw_ref, cb_ref, wdn_ref, nw_out_ref,
                *rest, tm, tiles_per_seq, d_ff, fc, emit_next):
    if emit_next:
        nw_next_ref, o_ref, xn_ref, hn_s, acc_s = rest
    else:
        (o_ref, hn_s, acc_s), nw_next_ref, xn_ref = rest, None, None
    i = pl.program_id(0)
    r = i % tiles_per_seq
    halo = BF16_ROWS
    x = x_ref[...]
    nw = nw_in_ref[...]
    hp = jnp.where(r == 0, 0.0, _rms(xp_ref[...]) * nw)
    hx = jnp.where(r == tiles_per_seq - 1, 0.0, _rms(xnx_ref[...]) * nw)
    hn = jnp.concatenate([hp, _rms(x) * nw, hx], axis=0).astype(BF16)
    n_ext = tm + 2 * halo
    hn_s[...] = hn
    acc_s[...] = jnp.zeros_like(acc_s)

    def conv(col, scale):
        cols = pl.ds(pl.multiple_of(col, fc), fc)
        h = _dot(hn_s[...], wup_ref[:, cols])
        cw = cw_ref[:, cols] * scale
        prev = pltpu.roll(h, 1, 0)[halo:halo + tm]
        nxt = pltpu.roll(h, n_ext - 1, 0)[halo:halo + tm]
        return (cb_ref[:, cols] * scale + prev * cw[0:1] + h[halo:halo + tm] * cw[1:2]
                + nxt * cw[2:3])

    def chunk(c, carry):
        gate = conv(c * fc, 1.0)
        half_up = conv(d_ff + c * fc, 0.5)
        inner = gate * (GELU_C0 + GELU_C1 * (gate * gate))
        act = (gate * (1.0 + jnp.tanh(inner)) * half_up).astype(BF16)
        acc_s[...] += _dot(act, wdn_ref[pl.ds(pl.multiple_of(c * fc, fc), fc), :])
        return carry

    for c in range(d_ff // fc):
        chunk(c, 0)
    y = x + _rms(acc_s[...]) * nw_out_ref[...]
    o_ref[...] = y
    if xn_ref is not None:
        xn_ref[...] = (_rms(y) * nw_next_ref[...]).astype(xn_ref.dtype)


def conv_ffn_block(x2, seq, w_up, conv_w, conv_b, w_down, nw_in, nw_out, layer, nw_next=None,
                   tm=1024, fc=256):
    T, D = x2.shape
    d_ff = w_down.shape[1]
    halo = BF16_ROWS
    tps = seq // tm
    hb = tm // halo
    n_hb = T // halo
    assert seq % tm == 0, "a row tile must not straddle two sequences (the conv zero-pads each)"
    emit_next = nw_next is not None
    kern = functools.partial(_ffn_kernel, tm=tm, tiles_per_seq=tps, d_ff=d_ff, fc=fc,
                             emit_next=emit_next)
    vec = pl.BlockSpec((1, D), lambda i: (0, 0))
    tile = pl.BlockSpec((tm, D), lambda i: (i, 0))
    in_specs = [tile,
                pl.BlockSpec((halo, D), lambda i: (jnp.maximum(i * hb - 1, 0), 0)),
                pl.BlockSpec((halo, D), lambda i: (jnp.minimum((i + 1) * hb, n_hb - 1), 0)),
                vec,
                _resident((None, D, 2 * d_ff), lambda i: (layer, 0, 0)),
                pl.BlockSpec((None, CONV_W, 2 * d_ff), lambda i: (layer, 0, 0)),
                pl.BlockSpec((None, 1, 2 * d_ff), lambda i: (layer, 0, 0)),
                _resident((None, d_ff, D), lambda i: (layer, 0, 0)),
                vec]
    args = [x2, x2, x2, nw_in.reshape(1, D), w_up, conv_w, conv_b, w_down, nw_out.reshape(1, D)]
    out_shape = jax.ShapeDtypeStruct((T, D), F32)
    out_specs = tile
    if emit_next:
        in_specs.append(vec)
        args.append(nw_next.reshape(1, D))
        out_shape = (out_shape, jax.ShapeDtypeStruct((T, D), BF16))
        out_specs = (tile, tile)
    return pl.pallas_call(
        kern,
        out_shape=out_shape,
        grid=(T // tm,),
        in_specs=in_specs,
        out_specs=out_specs,
        scratch_shapes=[pltpu.VMEM((tm + 2 * halo, D), BF16),
                        pltpu.VMEM((tm, D), F32)],
        compiler_params=_params("parallel"),
        name="conv_ffn",
    )(*args)


def kernel(x, positions, norm_w, w_in, hgrn_lb_logits, hgrn_norm_w, w_ret_o, w_hgrn_o,
           w_fnet, w_out, w_up, conv_w, conv_b, w_down):
    B, S, D = x.shape
    depth = w_in.shape[0]
    T = B * S

    hgrn_off = 2 * D + 2 * 2 * D
    fu_off = hgrn_off + 5 * D
    ga_off = fu_off + D

    w_in_b = w_in.astype(BF16)
    w_ret_o_b = w_ret_o.astype(BF16)
    w_hgrn_o_b = w_hgrn_o.astype(BF16)
    w_fnet_b = w_fnet.astype(BF16)
    w_out_b = w_out.astype(BF16)
    w_up_b = w_up.astype(BF16)
    w_down_b = w_down.astype(BF16)
    conv_b3 = conv_b.reshape(depth, 1, -1)

    log_gamma = jnp.log(1.0 - 2.0 ** (-5.0 - jnp.arange(RET_HEADS, dtype=F32)))
    p = jax.nn.softmax(hgrn_lb_logits.astype(F32), axis=1)
    lower_bounds = jnp.cumsum(p, axis=1) - p[:, :1]

    cos, sin, xn3 = rope_tables_and_norm(positions, D // RET_HEADS // 2, x, norm_w[0, 0])

    x2 = x.reshape(T, D)
    xn2 = xn3.reshape(T, D)
    for l in range(depth):
        xn3 = xn2.reshape(B, S, D)
        ro = retention_branch(xn3, w_in_b, l, cos, sin, log_gamma)
        ho = hgrn_branch(xn3, w_in_b, l, hgrn_off, lower_bounds[0, l], lower_bounds[1, l],
                         hgrn_norm_w[l])
        fo = fourier_branch(xn3, w_in_b, l, fu_off // D)
        x2 = merge_branches(x2, xn2, ro.reshape(T, -1), ho.reshape(T, D), fo.reshape(T, D),
                            w_in_b, l, ga_off // (N_BRANCH * D),
                            w_ret_o_b, w_hgrn_o_b, w_fnet_b, w_out_b, norm_w[l, 1])
        if l + 1 < depth:
            x2, xn2 = conv_ffn_block(x2, S, w_up_b, conv_w, conv_b3, w_down_b,
                                     norm_w[l, 2], norm_w[l, 3], l, nw_next=norm_w[l + 1, 0])
        else:
            x2 = conv_ffn_block(x2, S, w_up_b, conv_w, conv_b3, w_down_b,
                                norm_w[l, 2], norm_w[l, 3], l)
    return x2.reshape(B, S, D)
```

```python
import functools
import math

import numpy as np
import jax
import jax.numpy as jnp
from jax import lax
from jax.experimental import pallas as pl
from jax.experimental.pallas import tpu as pltpu

F32 = jnp.float32
BF16 = jnp.bfloat16

RET_HEADS = 4
HGRN_HEADS = 8
FNET_GROUPS = 4
N_BRANCH = 3
CONV_W = 3
ROPE_BASE = 10000.0
LB_FLOOR = 1e-30
EPS = 1e-6
LOG2_E = 1.4426950408889634
GELU_C0 = math.sqrt(2.0 / math.pi)
GELU_C1 = GELU_C0 * 0.044715

V7X_VMEM_LIMIT_BYTES = 56 * 1024 * 1024
SUBLANES = 8
BF16_ROWS = 16

RET_CHUNK = 256
HGRN_CHUNK = 128
HGRN_HEADS_PER_STEP = 2
HGRN_VPU_LEVELS = (0,)
ROW_TILE = 512
HGRN_ROW_TILE = 1024


def _dot(a, b):
    return jnp.dot(a, b, preferred_element_type=F32)


def _dot_nt(a, b):
    return lax.dot_general(a, b, (((1,), (1,)), ((), ())), preferred_element_type=F32)


def _dot_tn(a, b):
    return lax.dot_general(a, b, (((0,), (0,)), ((), ())), preferred_element_type=F32)


def _silu(x, scale=1.0):
    return (x * scale if scale != 1.0 else x) / (1.0 + jnp.exp2(x * (-LOG2_E)))


def _rms(x):
    return x * lax.rsqrt(jnp.mean(x * x, axis=-1, keepdims=True) + EPS)


def _params(*sem):
    return pltpu.CompilerParams(dimension_semantics=sem,
                                vmem_limit_bytes=V7X_VMEM_LIMIT_BYTES)


def _resident(shape, index_map):
    return pl.BlockSpec(shape, index_map, pipeline_mode=pl.Buffered(1))


def _rope_norm_kernel(pos_ref, invf_ref, x_ref, w_ref, cos_ref, sin_ref, xn_ref):
    ang = pos_ref[...] * invf_ref[...]
    cos_ref[...] = jnp.cos(ang)
    sin_ref[...] = jnp.sin(ang)
    xn_ref[...] = (_rms(x_ref[...]) * w_ref[...]).astype(xn_ref.dtype)


def rope_tables_and_norm(positions, half, x, w):
    B, S = positions.shape
    D = x.shape[-1]
    pos = positions.astype(F32).reshape(B, S, 1)
    inv_freq = (ROPE_BASE ** (-jnp.arange(half, dtype=F32) / half)).reshape(1, half)
    table = jax.ShapeDtypeStruct((B, S, half), F32)
    table_spec = pl.BlockSpec((None, S, half), lambda b: (b, 0, 0))
    rows_spec = pl.BlockSpec((None, S, D), lambda b: (b, 0, 0))
    return pl.pallas_call(
        _rope_norm_kernel,
        out_shape=(table, table, jax.ShapeDtypeStruct((B, S, D), BF16)),
        grid=(B,),
        in_specs=[pl.BlockSpec((None, S, 1), lambda b: (b, 0, 0)),
                  pl.BlockSpec((1, half), lambda b: (0, 0)),
                  rows_spec,
                  pl.BlockSpec((1, D), lambda b: (0, 0))],
        out_specs=(table_spec, table_spec, rows_spec),
        compiler_params=_params("parallel"),
        name="rope_tables_norm",
    )(pos, inv_freq, x, w.reshape(1, D))


def _ret_kernel(lg_ref, xn_ref, wq_ref, wk_ref, wv_ref, wg_ref, cos_ref, sin_ref,
                o_ref, qi_s, qd_s, ki_s, v_s, g_s, st_s, kvb_s, run_s, *, seq, dk, dv):
    C = RET_CHUNK
    R = seq // C
    half = dk // 2
    lg = lg_ref[pl.program_id(1)]
    ret_scale = dk ** -0.5

    def rows_of(n):
        return pl.ds(pl.multiple_of(n * C, C), C)

    pos = lax.broadcasted_iota(jnp.int32, (C, 1), 0).astype(F32)
    qdec_f = jnp.exp(lg * (pos + 1.0))
    qdec_b = jnp.exp(lg * (C - pos))
    kdec_f = jnp.exp(lg * (C - 1.0 - pos))
    kdec_b = jnp.exp(lg * pos)
    chunk_dec = jnp.exp(lg * C)
    ii = lax.broadcasted_iota(jnp.int32, (C, C), 0)
    jj = lax.broadcasted_iota(jnp.int32, (C, C), 1)
    decay = jnp.exp(lg * jnp.abs(ii - jj).astype(F32))

    run_s[...] = jnp.zeros_like(run_s)

    def proj(t, carry):
        rows = pl.ds(pl.multiple_of(t * ROW_TILE, ROW_TILE), ROW_TILE)
        xc = xn_ref[rows, :]
        cos = cos_ref[rows, :]
        sin = sin_ref[rows, :]
        q = _dot(xc, wq_ref[...])
        q1, q2 = q[:, :half], q[:, half:]
        q = jnp.concatenate([q1 * cos - q2 * sin, q1 * sin + q2 * cos], axis=-1)
        k = _dot(xc, wk_ref[...]) * ret_scale
        k1, k2 = k[:, :half], k[:, half:]
        k = jnp.concatenate([k1 * cos - k2 * sin, k1 * sin + k2 * cos], axis=-1)
        v = _dot(xc, wv_ref[...]).astype(BF16)
        g = _dot(xc, wg_ref[...])
        qi_s[rows, :] = q.astype(BF16)
        ki_s[rows, :] = k.astype(BF16)
        v_s[rows, :] = v
        g_s[rows, :] = _silu(g).astype(BF16)
        for j in range(ROW_TILE // C):
            n = t * (ROW_TILE // C) + j
            sl = slice(j * C, (j + 1) * C)
            qd_s[rows_of(n), :] = jnp.concatenate([q[sl] * qdec_f, q[sl] * qdec_b],
                                                  axis=-1).astype(BF16)
            st_s[n, pl.ds(0, dk), :] = run_s[...].astype(BF16)
            run_s[...] = run_s[...] * chunk_dec + _dot_tn((k[sl] * kdec_f).astype(BF16), v[sl])
            kvb_s[n] = _dot_tn((k[sl] * kdec_b).astype(BF16), v[sl])
        return carry

    lax.fori_loop(0, seq // ROW_TILE, proj, 0, unroll=2)

    run_s[...] = jnp.zeros_like(run_s)

    def out(t, carry):
        n = R - 1 - t
        rows = rows_of(n)
        st_s[n, pl.ds(dk, dk), :] = run_s[...].astype(BF16)
        s = _dot_nt(qi_s[rows, :], ki_s[rows, :]) * decay
        o = _dot(s.astype(BF16), v_s[rows, :]) + _dot(qd_s[rows, :], st_s[n])
        o_ref[rows, :] = (_rms(o) * g_s[rows, :].astype(F32)).astype(o_ref.dtype)
        run_s[...] = run_s[...] * chunk_dec + kvb_s[n]
        return carry

    lax.fori_loop(0, R, out, 0, unroll=8)


def retention_branch(xn3, w_in_b, layer, cos, sin, log_gamma):
    B, S, D = xn3.shape
    dk = D // RET_HEADS
    dv = 2 * dk
    H = RET_HEADS
    qk_blocks = D // dk
    v_off = 2 * D // dv
    g_off = v_off + H
    kern = functools.partial(_ret_kernel, seq=S, dk=dk, dv=dv)
    return pl.pallas_call(
        kern,
        out_shape=jax.ShapeDtypeStruct((B, S, H * dv), BF16),
        grid=(B, H),
        in_specs=[
            pl.BlockSpec(memory_space=pltpu.SMEM),
            pl.BlockSpec((None, S, D), lambda b, h: (b, 0, 0)),
            pl.BlockSpec((None, D, dk), lambda b, h: (layer, 0, h)),
            pl.BlockSpec((None, D, dk), lambda b, h: (layer, 0, qk_blocks + h)),
            pl.BlockSpec((None, D, dv), lambda b, h: (layer, 0, v_off + h)),
            pl.BlockSpec((None, D, dv), lambda b, h: (layer, 0, g_off + h)),
            pl.BlockSpec((None, S, dk // 2), lambda b, h: (b, 0, 0)),
            pl.BlockSpec((None, S, dk // 2), lambda b, h: (b, 0, 0)),
        ],
        out_specs=pl.BlockSpec((None, S, dv), lambda b, h: (b, 0, h)),
        scratch_shapes=[
            pltpu.VMEM((S, dk), BF16),
            pltpu.VMEM((S, 2 * dk), BF16),
            pltpu.VMEM((S, dk), BF16),
            pltpu.VMEM((S, dv), BF16),
            pltpu.VMEM((S, dv), BF16),
            pltpu.VMEM((S // RET_CHUNK, 2 * dk, dv), BF16),
            pltpu.VMEM((S // RET_CHUNK, dk, dv), F32),
            pltpu.VMEM((dk, dv), F32),
        ],
        compiler_params=_params("parallel", "arbitrary"),
        name="retention",
    )(log_gamma, xn3, w_in_b, w_in_b, w_in_b, w_in_b, cos, sin)


def _hgrn_gate(z, lb):
    e = jnp.exp2(jnp.abs(z) * (-LOG2_E))
    pos = z >= 0.0
    sig_neg_num = jnp.where(pos, e, 1.0)
    num = jnp.where(pos, 1.0, e) + jnp.maximum(lb, LB_FLOOR) * sig_neg_num
    inv = 1.0 / (1.0 + e)
    log2_f = jnp.log2(num * inv)
    return log2_f, (1.0 - lb) * sig_neg_num * inv


def _boundary_rows(cum_ref, base, m, reverse, row_in_group):
    C = HGRN_CHUNK
    d = cum_ref.shape[1]
    blk = 2 * m
    off = m if reverse else m - 1
    pieces = []
    if blk >= SUBLANES:
        for b in range(C // blk):
            pieces.append(jnp.broadcast_to(cum_ref[pl.ds(base + (b * blk + off), 1), :], (blk, d)))
    else:
        for g in range(C // SUBLANES):
            val = None
            for u in range(SUBLANES // blk):
                row = g * SUBLANES + u * blk + off
                piece = jnp.broadcast_to(cum_ref[pl.ds(base + row, 1), :], (SUBLANES, d))
                val = piece if val is None else jnp.where(row_in_group >= u * blk, piece, val)
            pieces.append(val)
    return jnp.concatenate(pieces, axis=0) if len(pieces) > 1 else pieces[0]


def _level_operands(level, q, k, cum, cum_ref, base, consts, reverse):
    C = HGRN_CHUNK
    _, row_in_group, signs, _ = consts
    m = 2 ** level
    if 2 * m <= SUBLANES:
        sign = signs[level]
        ref_pt = _boundary_rows(cum_ref, base, m, reverse, row_in_group)
        x = (jnp.where(sign > 0.0, q, k) * jnp.exp2((cum - ref_pt) * sign)).astype(BF16)
        return x, x, list(range(C // SUBLANES))
    xq, xall, q_groups = [], [], []
    for b in range(C // (2 * m)):
        first = slice(b * 2 * m, b * 2 * m + m)
        second = slice(b * 2 * m + m, (b + 1) * 2 * m)
        q_rows, k_rows = (first, second) if reverse else (second, first)
        edge = k_rows.start if reverse else k_rows.stop - 1
        ref_pt = cum_ref[pl.ds(base + edge, 1), :]
        xq_b = q[q_rows] * jnp.exp2(cum[q_rows] - ref_pt)
        xk_b = k[k_rows] * jnp.exp2(ref_pt - cum[k_rows])
        xq.append(xq_b)
        xall.extend([xq_b, xk_b] if reverse else [xk_b, xq_b])
        q_groups.extend(range(q_rows.start // SUBLANES, q_rows.stop // SUBLANES))
    return (jnp.concatenate(xq, axis=0).astype(BF16), jnp.concatenate(xall, axis=0).astype(BF16),
            q_groups)


def _paired_dot_nt(lhs_a, rhs_a, lhs_b, rhs_b):
    rhs = jnp.concatenate([rhs_a, rhs_b], axis=1)
    lhs = jnp.concatenate(
        [jnp.concatenate([lhs_a, jnp.zeros_like(lhs_a)], axis=1),
         jnp.concatenate([jnp.zeros_like(lhs_b), lhs_b], axis=1)], axis=0)
    s = _dot_nt(lhs, rhs)
    return s[:lhs_a.shape[0]], s[lhs_a.shape[0]:]


def _assemble_scores(scores, level_id):
    C = HGRN_CHUNK
    rows = [jnp.zeros((SUBLANES, C), F32) for _ in range(C // SUBLANES)]
    for level, entry in enumerate(scores):
        if entry is None:
            continue
        s, q_groups = entry
        for i, g in enumerate(q_groups):
            lid = level_id[g * SUBLANES:(g + 1) * SUBLANES]
            rows[g] = jnp.where(lid == level, s[i * SUBLANES:(i + 1) * SUBLANES], rows[g])
    return jnp.concatenate(rows, axis=0)


def _hgrn_intra_pair(fwd, bwd, consts):
    C = HGRN_CHUNK
    args = ((fwd, consts[0], False), (bwd, consts[1], True))
    n_vpu = len(HGRN_VPU_LEVELS)
    scores = ([None] * n_vpu, [None] * n_vpu)
    for level in range(n_vpu, C.bit_length() - 1):
        ops = [_level_operands(level, q, k, cum, cum_ref, base, cst, rev)
               for (q, k, _, cum, cum_ref, base), cst, rev in args]
        s_f, s_b = _paired_dot_nt(ops[0][0], ops[0][1], ops[1][0], ops[1][1])
        scores[0].append((s_f, ops[0][2]))
        scores[1].append((s_b, ops[1][2]))
    outs = []
    for idx, ((q, k, v, cum, _, _), cst, rev) in enumerate(args):
        attn = _assemble_scores(scores[idx], cst[0])
        o = _dot(attn.astype(BF16), v.astype(BF16))
        o += jnp.sum(q * k, axis=-1, keepdims=True) * v
        outs.append(o + _near_pairs(q, k, v, cum, rev, cst[3]))
    return outs


def _near_pairs(q, k, v, cum, reverse, masks):
    C, d = q.shape
    shape3 = (C // SUBLANES, SUBLANES, d)
    out = jnp.zeros((C, d), F32)
    for offset, mask in enumerate(masks, start=1):
        valid = mask != 0
        shift = (SUBLANES - offset) if reverse else offset

        def key_row(x):
            return pltpu.roll(x.reshape(shape3), shift, 1).reshape(C, d)

        w = jnp.exp2(jnp.where(valid, cum - key_row(cum), 0.0))
        score = jnp.sum(q * key_row(k) * w, axis=-1, keepdims=True)
        out += jnp.where(valid, score * key_row(v), 0.0)
    return out


def _near_pair_masks(C, d, reverse, levels):
    pos = lax.broadcasted_iota(jnp.int32, (C, d), 0) % SUBLANES
    masks = []
    for offset in range(1, 2 ** (max(levels) + 1)):
        valid = jnp.zeros((C, d), jnp.int32)
        for level in levels:
            m, blk = 2 ** level, 2 ** (level + 1)
            r = pos % blk
            key = (r + offset) if reverse else (r - offset)
            if reverse:
                ok = (r < m) & (key >= m) & (key < blk)
            else:
                ok = (r >= m) & (key >= 0) & (key < m)
            valid = jnp.where(ok, 1, valid)
        masks.append(valid)
    return masks


def _hgrn_consts(reverse, d):
    C = HGRN_CHUNK
    ii = lax.broadcasted_iota(jnp.int32, (C, C), 0)
    jj = lax.broadcasted_iota(jnp.int32, (C, C), 1)
    diff = ii ^ jj
    level_id = jnp.full((C, C), -1, jnp.int32)
    n_levels = C.bit_length() - 1
    for level in range(n_levels):
        level_id = jnp.where((diff >> level) == 1, level, level_id)
    level_id = jnp.where((ii < jj) if reverse else (ii > jj), level_id, -1)
    rows = lax.broadcasted_iota(jnp.int32, (C, d), 0)
    row_in_group = lax.broadcasted_iota(jnp.int32, (SUBLANES, d), 0)
    signs = []
    for level in range(SUBLANES.bit_length() - 1):
        second = ((rows >> level) & 1) == 1
        is_query = jnp.logical_not(second) if reverse else second
        signs.append(jnp.where(is_query, 1.0, -1.0))
    return level_id, row_in_group, signs, _near_pair_masks(C, d, reverse, HGRN_VPU_LEVELS)


def _chunk_cumsum(x, reverse, row_in_group):
    rows, d = x.shape
    groups = rows // SUBLANES
    per_chunk = HGRN_CHUNK // SUBLANES
    y = x.reshape(groups, SUBLANES, d)
    step = 1
    while step < SUBLANES:
        rolled = pltpu.roll(y, (SUBLANES - step) if reverse else step, 1)
        valid = (row_in_group < SUBLANES - step) if reverse else (row_in_group >= step)
        y = y + jnp.where(valid, rolled, 0.0)
        step *= 2
    out = [None] * groups
    for c in range(rows // HGRN_CHUNK):
        order = range(c * per_chunk, (c + 1) * per_chunk)
        carry = None
        for g in (reversed(order) if reverse else order):
            yg = y[g] if carry is None else y[g] + carry
            out[g] = yg
            edge = 0 if reverse else SUBLANES - 1
            carry = jnp.broadcast_to(yg[edge:edge + 1, :], (SUBLANES, d))
    return jnp.concatenate(out, axis=0)


def _hgrn_kernel(xn_ref, wq_ref, wzf_ref, wzb_ref, wi_ref, wg_ref, lbf_ref, lbb_ref, nw_ref, o_ref,
                 q_s, v_s, g_s, acc_s, kf_s, kb_s, lff_s, lfb_s, cum_s, *, seq, dk, heads):
    C = HGRN_CHUNK
    R = seq // C
    row_tile = HGRN_ROW_TILE
    scale = dk ** -0.5

    def tile_rows(n):
        return pl.ds(pl.multiple_of(n * row_tile, row_tile), row_tile)

    def head_cols(h):
        return slice(h * dk, (h + 1) * dk)

    dirs = [((False, lbf_ref, kf_s.at[h], lff_s.at[h], cum_s.at[h, 0]),
             (True, lbb_ref, kb_s.at[h], lfb_s.at[h], cum_s.at[h, 1]))
            for h in range(heads)]
    row_in_group = lax.broadcasted_iota(jnp.int32, (1, SUBLANES, dk), 1)

    def proj(n, carry):
        rows = tile_rows(n)
        xc = xn_ref[rows, :]
        hq_all = _dot(xc, wq_ref[...])
        z_all = (_dot(xc, wzf_ref[...]), _dot(xc, wzb_ref[...]))
        v_all = _dot(xc, wi_ref[...])
        hg_all = _dot(xc, wg_ref[...])
        for h in range(heads):
            cols = head_cols(h)
            hq = hq_all[:, cols]
            q = _silu(hq, scale)
            q_s[h, rows, :] = q
            for idx, (reverse, lb_ref, k_s, lf_s, _) in enumerate(dirs[h]):
                lf, kk = _hgrn_gate(z_all[idx][:, cols], lb_ref[:, cols])
                k_s[rows, :] = kk
                lf_s[rows, :] = lf
            v_s[h, rows, :] = v_all[:, cols]
            hg = hg_all[:, cols]
            g_s[h, rows, :] = _silu(hg)
            acc_s[h, rows, :] = jnp.zeros((row_tile, dk), F32)
        return carry

    lax.fori_loop(0, seq // row_tile, proj, 0, unroll=2)

    consts = (_hgrn_consts(False, dk), _hgrn_consts(True, dk))

    def chunk_pair(h, cf, cb, states):
        data, rows, q_dec, k_dec, decs = [], [], [], [], []
        for c, (reverse, _, k_s, lf_s, cum_c) in zip((cf, cb), dirs[h]):
            r = pl.ds(pl.multiple_of(c * C, C), C)
            q, k = q_s[h, r, :], k_s[r, :]
            cum = _chunk_cumsum(lf_s[r, :], reverse, row_in_group)
            cum_c[...] = cum
            data.append((q, k, v_s[h, r, :], cum, cum_c, 0))
            rows.append(r)
            total = cum_c[pl.ds(0 if reverse else C - 1, 1), :]
            q_dec.append((q * jnp.exp2(cum)).astype(BF16))
            k_dec.append((k * jnp.exp2(total - cum)).astype(BF16))
            decs.append(jnp.exp2(total))
        o_f, o_b = _hgrn_intra_pair(data[0], data[1], consts)
        i_f, i_b = _paired_dot_nt(q_dec[0], states[0].astype(BF16), q_dec[1], states[1].astype(BF16))
        acc_s[h, rows[0], :] += o_f + i_f
        acc_s[h, rows[1], :] += o_b + i_b
        return tuple(st * dec + _dot_tn(d[2].astype(BF16), kd)
                     for st, dec, d, kd in zip(states, decs, data, k_dec))

    def step(i, states):
        return tuple(chunk_pair(h, i, R - 1 - i, states[h]) for h in range(heads))

    zero = jnp.zeros((dk, dk), F32)
    lax.fori_loop(0, R, step, tuple((zero, zero) for _ in range(heads)))

    def finish(n, carry):
        rows = tile_rows(n)
        for h in range(heads):
            o_ref[rows, head_cols(h)] = (_rms(acc_s[h, rows, :]) * nw_ref[...]
                                         * g_s[h, rows, :]).astype(o_ref.dtype)
        return carry

    lax.fori_loop(0, seq // row_tile, finish, 0)


def hgrn_branch(xn3, w_in_b, layer, hgrn_off, lb_f, lb_b, norm_w):
    B, S, D = xn3.shape
    dk = D // HGRN_HEADS
    hp = HGRN_HEADS_PER_STEP
    wide = hp * dk
    kern = functools.partial(_hgrn_kernel, seq=S, dk=dk, heads=hp)
    vec = lambda: pltpu.VMEM((hp, S, dk), F32)

    def w_spec(group):
        first = (hgrn_off + group * D) // wide
        return pl.BlockSpec((None, D, wide), lambda b, j: (layer, 0, first + j))

    lb_spec = pl.BlockSpec((1, wide), lambda b, j: (0, j))
    return pl.pallas_call(
        kern,
        out_shape=jax.ShapeDtypeStruct((B, S, D), BF16),
        grid=(B, HGRN_HEADS // hp),
        in_specs=[pl.BlockSpec((None, S, D), lambda b, j: (b, 0, 0)),
                  w_spec(0), w_spec(1), w_spec(2), w_spec(3), w_spec(4),
                  lb_spec, lb_spec,
                  pl.BlockSpec((1, dk), lambda b, j: (0, 0))],
        out_specs=pl.BlockSpec((None, S, wide), lambda b, j: (b, 0, j)),
        scratch_shapes=[vec(), vec(), vec(), vec(), vec(), vec(), vec(), vec(),
                        pltpu.VMEM((hp, 2, HGRN_CHUNK, dk), F32)],
        compiler_params=_params("parallel", "arbitrary"),
        name="hgrn2",
    )(xn3, w_in_b, w_in_b, w_in_b, w_in_b, w_in_b,
      lb_f.reshape(1, D), lb_b.reshape(1, D), norm_w.reshape(1, dk))


def _fnet_proj_kernel(xn_ref, w_ref, cs_ref, o_ref, *, gdim):
    fu = _dot(xn_ref[...], w_ref[...]).astype(BF16)
    for g in range(FNET_GROUPS):
        t = _dot(fu[:, g * gdim:(g + 1) * gdim], cs_ref[...])
        o_ref[0, :, g * gdim:(g + 1) * gdim] = t[:, :gdim].astype(o_ref.dtype)
        o_ref[1, :, g * gdim:(g + 1) * gdim] = t[:, gdim:].astype(o_ref.dtype)


def _seq_dft_kernel(cos_ref, sin_ref, perm_ref, rhs_ref, o_ref, fold_s, mir_s, *, seq):
    n = seq
    half = n // 2
    blk = perm_ref.shape[0]
    c0 = 1.0 / math.sqrt(n)

    for part, sign in ((0, 1.0), (1, -1.0)):
        base = part * n
        for j in range(half // blk):
            own = rhs_ref[pl.ds(base + j * blk, blk), :].astype(F32)
            if j == 0:
                mirror = _dot(perm_ref[:, :blk], rhs_ref[pl.ds(base + n - blk, blk), :])
            else:
                mirror = _dot(perm_ref[...], rhs_ref[pl.ds(base + n - (j + 1) * blk, 2 * blk), :])
            fold_s[pl.ds(part * half + j * blk, blk), :] = (own + sign * mirror).astype(fold_s.dtype)

    mid = rhs_ref[pl.ds(half, BF16_ROWS), :].astype(F32)[0:1] * c0
    p_ext = _dot(cos_ref[...], fold_s[pl.ds(0, half), :])
    q = _dot(sin_ref[...], fold_s[pl.ds(half, half), :])
    row = lax.broadcasted_iota(jnp.int32, q.shape, 0)
    p = p_ext[:half] + jnp.where((row & 1) == 0, 1.0, -1.0) * mid
    o_ref[pl.ds(0, half), :] = (p - q).astype(o_ref.dtype)

    mir_s[pl.ds(0, half), :] = (p + q).astype(mir_s.dtype)
    first = jnp.where(lax.broadcasted_iota(jnp.int32, (BF16_ROWS, q.shape[1]), 0) == 0, 1.0, 0.0)
    tail = (p_ext[half:half + BF16_ROWS] + mid) * first
    mir_s[pl.ds(half, BF16_ROWS), :] = tail.astype(mir_s.dtype)
    mir_s[pl.ds(half + BF16_ROWS, blk - BF16_ROWS), :] = jnp.zeros(
        (blk - BF16_ROWS, tail.shape[1]), mir_s.dtype)
    for j in range(half // blk):
        window = mir_s[pl.ds(half - (j + 1) * blk, 2 * blk), :]
        o_ref[pl.ds(half + j * blk, blk), :] = _dot(perm_ref[...], window).astype(o_ref.dtype)


DFT_FOLD_BLOCK = 128


def _half_dft_tables(n):
    c, s = _dft_tables(n)
    half = n // 2
    cos_ext = np.zeros((half + BF16_ROWS, half))
    cos_ext[:half + 1] = c[:half + 1, :half]
    return cos_ext, s[:half, :half]


def _mirror_permutation(blk):
    p = np.zeros((blk, 2 * blk), np.float32)
    i = np.arange(blk)
    p[i, blk - i] = 1.0
    return p


def _dft_tables(n):
    idx = np.arange(n, dtype=np.int64)
    ang = 2.0 * np.pi * ((idx[:, None] * idx[None, :]) % n).astype(np.float64) / n
    s = 1.0 / math.sqrt(n)
    return np.cos(ang) * s, np.sin(ang) * s


def fourier_branch(xn3, w_in_b, layer, fu_off_blocks, tm=512):
    B, S, D = xn3.shape
    W = D
    gdim = W // FNET_GROUPS
    c_small, s_small = _dft_tables(gdim)
    cs_small = jnp.asarray(np.concatenate([c_small, s_small], axis=1), dtype=BF16)
    half = S // 2
    assert half % 2 == 0 and half % DFT_FOLD_BLOCK == 0
    cos_np, sin_np = _half_dft_tables(S)
    dft_cos = jnp.asarray(cos_np, dtype=BF16)
    dft_sin = jnp.asarray(sin_np, dtype=BF16)
    perm = jnp.asarray(_mirror_permutation(DFT_FOLD_BLOCK), dtype=BF16)
    tiles = S // tm
    rhs = pl.pallas_call(
        functools.partial(_fnet_proj_kernel, gdim=gdim),
        out_shape=jax.ShapeDtypeStruct((B, 2, S, W), BF16),
        grid=(B, tiles),
        in_specs=[pl.BlockSpec((None, tm, D), lambda b, r: (b, r, 0)),
                  pl.BlockSpec((None, D, W), lambda b, r: (layer, 0, fu_off_blocks)),
                  pl.BlockSpec((gdim, 2 * gdim), lambda b, r: (0, 0))],
        out_specs=pl.BlockSpec((None, 2, tm, W), lambda b, r: (b, 0, r, 0)),
        compiler_params=_params("parallel", "parallel"),
        name="fnet_proj",
    )(xn3, w_in_b, cs_small)
    rhs = rhs.reshape(B, 2 * S, W)
    return pl.pallas_call(
        functools.partial(_seq_dft_kernel, seq=S),
        out_shape=jax.ShapeDtypeStruct((B, S, W), BF16),
        grid=(B,),
        in_specs=[_resident((half + BF16_ROWS, half), lambda b: (0, 0)),
                  _resident((half, half), lambda b: (0, 0)),
                  pl.BlockSpec((DFT_FOLD_BLOCK, 2 * DFT_FOLD_BLOCK), lambda b: (0, 0)),
                  pl.BlockSpec((None, 2 * S, W), lambda b: (b, 0, 0))],
        out_specs=pl.BlockSpec((None, S, W), lambda b: (b, 0, 0)),
        scratch_shapes=[pltpu.VMEM((S, W), BF16),
                        pltpu.VMEM((half + DFT_FOLD_BLOCK, W), BF16)],
        compiler_params=_params("parallel"),
        name="fnet_seq_dft",
    )(dft_cos, dft_sin, perm, rhs)


def _merge_kernel(x_ref, xn_ref, ro_ref, ho_ref, fo_ref, wga_ref, wro_ref, who_ref,
                  wf_ref, wout_ref, nw_ref, o_ref, *, d):
    xn = xn_ref[...]

    def gate(i):
        return jax.nn.sigmoid(_dot(xn, wga_ref[:, i * d:(i + 1) * d]))

    mix = gate(0) * _dot(ro_ref[...], wro_ref[...])
    mix += gate(1) * _dot(ho_ref[...], who_ref[...])
    mix += gate(2) * _dot(fo_ref[...], wf_ref[...])
    y = _dot(mix.astype(BF16), wout_ref[...])
    o_ref[...] = x_ref[...] + _rms(y) * nw_ref[...]


def merge_branches(x2, xn2, ro2, ho2, fo2, w_in_b, layer, ga_off_blocks,
                   w_ret_o, w_hgrn_o, w_fnet, w_out, norm_w, tm=512):
    T, D = x2.shape
    RV = ro2.shape[1]
    tile = lambda w: pl.BlockSpec((tm, w), lambda i: (i, 0))
    return pl.pallas_call(
        functools.partial(_merge_kernel, d=D),
        out_shape=jax.ShapeDtypeStruct((T, D), F32),
        grid=(T // tm,),
        in_specs=[tile(D), tile(D), tile(RV), tile(D), tile(D),
                  _resident((None, D, N_BRANCH * D), lambda i: (layer, 0, ga_off_blocks)),
                  _resident((None, RV, D), lambda i: (layer, 0, 0)),
                  _resident((None, D, D), lambda i: (layer, 0, 0)),
                  _resident((None, D, D), lambda i: (layer, 0, 0)),
                  _resident((None, D, D), lambda i: (layer, 0, 0)),
                  pl.BlockSpec((1, D), lambda i: (0, 0))],
        out_specs=tile(D),
        compiler_params=_params("parallel"),
        name="merge",
    )(x2, xn2, ro2, ho2, fo2, w_in_b, w_ret_o, w_hgrn_o, w_fnet, w_out, norm_w.reshape(1, D))


def _ffn_kernel(x_ref, xp_ref, xnx_ref, nw_in_ref, wup_ref, cw_ref, cb_ref, wdn_ref, nw_out_ref,
                *rest, tm, tiles_per_seq, d_ff, fc, emit_next):
    if emit_next:
        nw_next_ref, o_ref, xn_ref, hn_s, acc_s = rest
    else:
        (o_ref, hn_s, acc_s), nw_next_ref, xn_ref = rest, None, None
    i = pl.program_id(0)
    r = i % tiles_per_seq
    halo = BF16_ROWS
    x = x_ref[...]
    nw = nw_in_ref[...]
    hp = jnp.where(r == 0, 0.0, _rms(xp_ref[...]) * nw)
    hx = jnp.where(r == tiles_per_seq - 1, 0.0, _rms(xnx_ref[...]) * nw)
    hn = jnp.concatenate([hp, _rms(x) * nw, hx], axis=0).astype(BF16)
    n_ext = tm + 2 * halo
    hn_s[...] = hn
    acc_s[...] = jnp.zeros_like(acc_s)

    def conv(col, scale):
        cols = pl.ds(pl.multiple_of(col, fc), fc)
        h = _dot(hn_s[...], wup_ref[:, cols])
        cw = cw_ref[:, cols] * scale
        prev = pltpu.roll(h, 1, 0)[halo:halo + tm]
        nxt = pltpu.roll(h, n_ext - 1, 0)[halo:halo + tm]
        return (cb_ref[:, cols] * scale + prev * cw[0:1] + h[halo:halo + tm] * cw[1:2]
                + nxt * cw[2:3])

    def chunk(c, carry):
        gate = conv(c * fc, 1.0)
        half_up = conv(d_ff + c * fc, 0.5)
        inner = gate * (GELU_C0 + GELU_C1 * (gate * gate))
        act = (gate * (1.0 + jnp.tanh(inner)) * half_up).astype(BF16)
        acc_s[...] += _dot(act, wdn_ref[pl.ds(pl.multiple_of(c * fc, fc), fc), :])
        return carry

    for c in range(d_ff // fc):
        chunk(c, 0)
    y = x + _rms(acc_s[...]) * nw_out_ref[...]
    o_ref[...] = y
    if xn_ref is not None:
        xn_ref[...] = (_rms(y) * nw_next_ref[...]).astype(xn_ref.dtype)


def conv_ffn_block(x2, seq, w_up, conv_w, conv_b, w_down, nw_in, nw_out, layer, nw_next=None,
                   tm=1024, fc=256):
    T, D = x2.shape
    d_ff = w_down.shape[1]
    halo = BF16_ROWS
    tps = seq // tm
    hb = tm // halo
    n_hb = T // halo
    assert seq % tm == 0, "a row tile must not straddle two sequences (the conv zero-pads each)"
    emit_next = nw_next is not None
    kern = functools.partial(_ffn_kernel, tm=tm, tiles_per_seq=tps, d_ff=d_ff, fc=fc,
                             emit_next=emit_next)
    vec = pl.BlockSpec((1, D), lambda i: (0, 0))
    tile = pl.BlockSpec((tm, D), lambda i: (i, 0))
    in_specs = [tile,
                pl.BlockSpec((halo, D), lambda i: (jnp.maximum(i * hb - 1, 0), 0)),
                pl.BlockSpec((halo, D), lambda i: (jnp.minimum((i + 1) * hb, n_hb - 1), 0)),
                vec,
                _resident((None, D, 2 * d_ff), lambda i: (layer, 0, 0)),
                pl.BlockSpec((None, CONV_W, 2 * d_ff), lambda i: (layer, 0, 0)),
                pl.BlockSpec((None, 1, 2 * d_ff), lambda i: (layer, 0, 0)),
                _resident((None, d_ff, D), lambda i: (layer, 0, 0)),
                vec]
    args = [x2, x2, x2, nw_in.reshape(1, D), w_up, conv_w, conv_b, w_down, nw_out.reshape(1, D)]
    out_shape = jax.ShapeDtypeStruct((T, D), F32)
    out_specs = tile
    if emit_next:
        in_specs.append(vec)
        args.append(nw_next.reshape(1, D))
        out_shape = (out_shape, jax.ShapeDtypeStruct((T, D), BF16))
        out_specs = (tile, tile)
    return pl.pallas_call(
        kern,
        out_shape=out_shape,
        grid=(T // tm,),
        in_specs=in_specs,
        out_specs=out_specs,
        scratch_shapes=[pltpu.VMEM((tm + 2 * halo, D), BF16),
                        pltpu.VMEM((tm, D), F32)],
        compiler_params=_params("parallel"),
        name="conv_ffn",
    )(*args)


def kernel(x, positions, norm_w, w_in, hgrn_lb_logits, hgrn_norm_w, w_ret_o, w_hgrn_o,
           w_fnet, w_out, w_up, conv_w, conv_b, w_down):
    B, S, D = x.shape
    depth = w_in.shape[0]
    T = B * S

    hgrn_off = 2 * D + 2 * 2 * D
    fu_off = hgrn_off + 5 * D
    ga_off = fu_off + D

    w_in_b = w_in.astype(BF16)
    w_ret_o_b = w_ret_o.astype(BF16)
    w_hgrn_o_b = w_hgrn_o.astype(BF16)
    w_fnet_b = w_fnet.astype(BF16)
    w_out_b = w_out.astype(BF16)
    w_up_b = w_up.astype(BF16)
    w_down_b = w_down.astype(BF16)
    conv_b3 = conv_b.reshape(depth, 1, -1)

    log_gamma = jnp.log(1.0 - 2.0 ** (-5.0 - jnp.arange(RET_HEADS, dtype=F32)))
    p = jax.nn.softmax(hgrn_lb_logits.astype(F32), axis=1)
    lower_bounds = jnp.cumsum(p, axis=1) - p[:, :1]

    cos, sin, xn3 = rope_tables_and_norm(positions, D // RET_HEADS // 2, x, norm_w[0, 0])

    x2 = x.reshape(T, D)
    xn2 = xn3.reshape(T, D)
    for l in range(depth):
        xn3 = xn2.reshape(B, S, D)
        ro = retention_branch(xn3, w_in_b, l, cos, sin, log_gamma)
        ho = hgrn_branch(xn3, w_in_b, l, hgrn_off, lower_bounds[0, l], lower_bounds[1, l],
                         hgrn_norm_w[l])
        fo = fourier_branch(xn3, w_in_b, l, fu_off // D)
        x2 = merge_branches(x2, xn2, ro.reshape(T, -1), ho.reshape(T, D), fo.reshape(T, D),
                            w_in_b, l, ga_off // (N_BRANCH * D),
                            w_ret_o_b, w_hgrn_o_b, w_fnet_b, w_out_b, norm_w[l, 1])
        if l + 1 < depth:
            x2, xn2 = conv_ffn_block(x2, S, w_up_b, conv_w, conv_b3, w_down_b,
                                     norm_w[l, 2], norm_w[l, 3], l, nw_next=norm_w[l + 1, 0])
        else:
            x2 = conv_ffn_block(x2, S, w_up_b, conv_w, conv_b3, w_down_b,
                                norm_w[l, 2], norm_w[l, 3], l)
    return x2.reshape(B, S, D)
```

```python
import functools
import math

import numpy as np
import jax
import jax.numpy as jnp
from jax import lax
from jax.experimental import pallas as pl
from jax.experimental.pallas import tpu as pltpu

F32 = jnp.float32
BF16 = jnp.bfloat16

RET_HEADS = 4
HGRN_HEADS = 8
FNET_GROUPS = 4
N_BRANCH = 3
CONV_W = 3
ROPE_BASE = 10000.0
LB_FLOOR = 1e-30
EPS = 1e-6
LOG2_E = 1.4426950408889634
GELU_C0 = math.sqrt(2.0 / math.pi)
GELU_C1 = GELU_C0 * 0.044715

V7X_VMEM_LIMIT_BYTES = 56 * 1024 * 1024
SUBLANES = 8
BF16_ROWS = 16

RET_CHUNK = 256
HGRN_CHUNK = 128
HGRN_HEADS_PER_STEP = 2
HGRN_VPU_LEVELS = (0,)
ROW_TILE = 512
HGRN_ROW_TILE = 1024


def _dot(a, b):
    return jnp.dot(a, b, preferred_element_type=F32)


def _dot_nt(a, b):
    return lax.dot_general(a, b, (((1,), (1,)), ((), ())), preferred_element_type=F32)


def _dot_tn(a, b):
    return lax.dot_general(a, b, (((0,), (0,)), ((), ())), preferred_element_type=F32)


def _silu(x, scale=1.0):
    return (x * scale if scale != 1.0 else x) / (1.0 + jnp.exp2(x * (-LOG2_E)))


def _rms(x):
    return x * lax.rsqrt(jnp.mean(x * x, axis=-1, keepdims=True) + EPS)


def _params(*sem):
    return pltpu.CompilerParams(dimension_semantics=sem,
                                vmem_limit_bytes=V7X_VMEM_LIMIT_BYTES)


def _resident(shape, index_map):
    return pl.BlockSpec(shape, index_map, pipeline_mode=pl.Buffered(1))


def _rope_norm_kernel(pos_ref, invf_ref, x_ref, w_ref, cos_ref, sin_ref, xn_ref):
    ang = pos_ref[...] * invf_ref[...]
    cos_ref[...] = jnp.cos(ang)
    sin_ref[...] = jnp.sin(ang)
    xn_ref[...] = (_rms(x_ref[...]) * w_ref[...]).astype(xn_ref.dtype)


def rope_tables_and_norm(positions, half, x, w):
    B, S = positions.shape
    D = x.shape[-1]
    pos = positions.astype(F32).reshape(B, S, 1)
    inv_freq = (ROPE_BASE ** (-jnp.arange(half, dtype=F32) / half)).reshape(1, half)
    table = jax.ShapeDtypeStruct((B, S, half), F32)
    table_spec = pl.BlockSpec((None, S, half), lambda b: (b, 0, 0))
    rows_spec = pl.BlockSpec((None, S, D), lambda b: (b, 0, 0))
    return pl.pallas_call(
        _rope_norm_kernel,
        out_shape=(table, table, jax.ShapeDtypeStruct((B, S, D), BF16)),
        grid=(B,),
        in_specs=[pl.BlockSpec((None, S, 1), lambda b: (b, 0, 0)),
                  pl.BlockSpec((1, half), lambda b: (0, 0)),
                  rows_spec,
                  pl.BlockSpec((1, D), lambda b: (0, 0))],
        out_specs=(table_spec, table_spec, rows_spec),
        compiler_params=_params("parallel"),
        name="rope_tables_norm",
    )(pos, inv_freq, x, w.reshape(1, D))


def _ret_kernel(lg_ref, xn_ref, wq_ref, wk_ref, wv_ref, wg_ref, cos_ref, sin_ref,
                o_ref, qi_s, qd_s, ki_s, v_s, g_s, st_s, kvb_s, run_s, *, seq, dk, dv):
    C = RET_CHUNK
    R = seq // C
    half = dk // 2
    lg = lg_ref[pl.program_id(1)]
    ret_scale = dk ** -0.5

    def rows_of(n):
        return pl.ds(pl.multiple_of(n * C, C), C)

    pos = lax.broadcasted_iota(jnp.int32, (C, 1), 0).astype(F32)
    qdec_f = jnp.exp(lg * (pos + 1.0))
    qdec_b = jnp.exp(lg * (C - pos))
    kdec_f = jnp.exp(lg * (C - 1.0 - pos))
    kdec_b = jnp.exp(lg * pos)
    chunk_dec = jnp.exp(lg * C)
    ii = lax.broadcasted_iota(jnp.int32, (C, C), 0)
    jj = lax.broadcasted_iota(jnp.int32, (C, C), 1)
    decay = jnp.exp(lg * jnp.abs(ii - jj).astype(F32))

    run_s[...] = jnp.zeros_like(run_s)

    def proj(t, carry):
        rows = pl.ds(pl.multiple_of(t * ROW_TILE, ROW_TILE), ROW_TILE)
        xc = xn_ref[rows, :]
        cos = cos_ref[rows, :]
        sin = sin_ref[rows, :]
        q = _dot(xc, wq_ref[...])
        q1, q2 = q[:, :half], q[:, half:]
        q = jnp.concatenate([q1 * cos - q2 * sin, q1 * sin + q2 * cos], axis=-1)
        k = _dot(xc, wk_ref[...]) * ret_scale
        k1, k2 = k[:, :half], k[:, half:]
        k = jnp.concatenate([k1 * cos - k2 * sin, k1 * sin + k2 * cos], axis=-1)
        v = _dot(xc, wv_ref[...]).astype(BF16)
        g = _dot(xc, wg_ref[...])
        qi_s[rows, :] = q.astype(BF16)
        ki_s[rows, :] = k.astype(BF16)
        v_s[rows, :] = v
        g_s[rows, :] = _silu(g).astype(BF16)
        for j in range(ROW_TILE // C):
            n = t * (ROW_TILE // C) + j
            sl = slice(j * C, (j + 1) * C)
            qd_s[rows_of(n), :] = jnp.concatenate([q[sl] * qdec_f, q[sl] * qdec_b],
                                                  axis=-1).astype(BF16)
            st_s[n, pl.ds(0, dk), :] = run_s[...].astype(BF16)
            run_s[...] = run_s[...] * chunk_dec + _dot_tn((k[sl] * kdec_f).astype(BF16), v[sl])
            kvb_s[n] = _dot_tn((k[sl] * kdec_b).astype(BF16), v[sl])
        return carry

    lax.fori_loop(0, seq // ROW_TILE, proj, 0, unroll=4)

    run_s[...] = jnp.zeros_like(run_s)

    def out(t, carry):
        n = R - 1 - t
        rows = rows_of(n)
        st_s[n, pl.ds(dk, dk), :] = run_s[...].astype(BF16)
        s = _dot_nt(qi_s[rows, :], ki_s[rows, :]) * decay
        o = _dot(s.astype(BF16), v_s[rows, :]) + _dot(qd_s[rows, :], st_s[n])
        o_ref[rows, :] = (_rms(o) * g_s[rows, :].astype(F32)).astype(o_ref.dtype)
        run_s[...] = run_s[...] * chunk_dec + kvb_s[n]
        return carry

    lax.fori_loop(0, R, out, 0, unroll=8)


def retention_branch(xn3, w_in_b, layer, cos, sin, log_gamma):
    B, S, D = xn3.shape
    dk = D // RET_HEADS
    dv = 2 * dk
    H = RET_HEADS
    qk_blocks = D // dk
    v_off = 2 * D // dv
    g_off = v_off + H
    kern = functools.partial(_ret_kernel, seq=S, dk=dk, dv=dv)
    return pl.pallas_call(
        kern,
        out_shape=jax.ShapeDtypeStruct((B, S, H * dv), BF16),
        grid=(B, H),
        in_specs=[
            pl.BlockSpec(memory_space=pltpu.SMEM),
            pl.BlockSpec((None, S, D), lambda b, h: (b, 0, 0)),
            pl.BlockSpec((None, D, dk), lambda b, h: (layer, 0, h)),
            pl.BlockSpec((None, D, dk), lambda b, h: (layer, 0, qk_blocks + h)),
            pl.BlockSpec((None, D, dv), lambda b, h: (layer, 0, v_off + h)),
            pl.BlockSpec((None, D, dv), lambda b, h: (layer, 0, g_off + h)),
            pl.BlockSpec((None, S, dk // 2), lambda b, h: (b, 0, 0)),
            pl.BlockSpec((None, S, dk // 2), lambda b, h: (b, 0, 0)),
        ],
        out_specs=pl.BlockSpec((None, S, dv), lambda b, h: (b, 0, h)),
        scratch_shapes=[
            pltpu.VMEM((S, dk), BF16),
            pltpu.VMEM((S, 2 * dk), BF16),
            pltpu.VMEM((S, dk), BF16),
            pltpu.VMEM((S, dv), BF16),
            pltpu.VMEM((S, dv), BF16),
            pltpu.VMEM((S // RET_CHUNK, 2 * dk, dv), BF16),
            pltpu.VMEM((S // RET_CHUNK, dk, dv), F32),
            pltpu.VMEM((dk, dv), F32),
        ],
        compiler_params=_params("parallel", "arbitrary"),
        name="retention",
    )(log_gamma, xn3, w_in_b, w_in_b, w_in_b, w_in_b, cos, sin)


def _hgrn_gate(z, lb):
    e = jnp.exp2(jnp.abs(z) * (-LOG2_E))
    pos = z >= 0.0
    sig_neg_num = jnp.where(pos, e, 1.0)
    num = jnp.where(pos, 1.0, e) + jnp.maximum(lb, LB_FLOOR) * sig_neg_num
    inv = 1.0 / (1.0 + e)
    log2_f = jnp.log2(num * inv)
    return log2_f, (1.0 - lb) * sig_neg_num * inv


def _boundary_rows(cum_ref, base, m, reverse, row_in_group):
    C = HGRN_CHUNK
    d = cum_ref.shape[1]
    blk = 2 * m
    off = m if reverse else m - 1
    pieces = []
    if blk >= SUBLANES:
        for b in range(C // blk):
            pieces.append(jnp.broadcast_to(cum_ref[pl.ds(base + (b * blk + off), 1), :], (blk, d)))
    else:
        for g in range(C // SUBLANES):
            val = None
            for u in range(SUBLANES // blk):
                row = g * SUBLANES + u * blk + off
                piece = jnp.broadcast_to(cum_ref[pl.ds(base + row, 1), :], (SUBLANES, d))
                val = piece if val is None else jnp.where(row_in_group >= u * blk, piece, val)
            pieces.append(val)
    return jnp.concatenate(pieces, axis=0) if len(pieces) > 1 else pieces[0]


def _level_operands(level, q, k, cum, cum_ref, base, consts, reverse):
    C = HGRN_CHUNK
    _, row_in_group, signs, _ = consts
    m = 2 ** level
    if 2 * m <= SUBLANES:
        sign = signs[level]
        ref_pt = _boundary_rows(cum_ref, base, m, reverse, row_in_group)
        x = (jnp.where(sign > 0.0, q, k) * jnp.exp2((cum - ref_pt) * sign)).astype(BF16)
        return x, x, list(range(C // SUBLANES))
    xq, xall, q_groups = [], [], []
    for b in range(C // (2 * m)):
        first = slice(b * 2 * m, b * 2 * m + m)
        second = slice(b * 2 * m + m, (b + 1) * 2 * m)
        q_rows, k_rows = (first, second) if reverse else (second, first)
        edge = k_rows.start if reverse else k_rows.stop - 1
        ref_pt = cum_ref[pl.ds(base + edge, 1), :]
        xq_b = q[q_rows] * jnp.exp2(cum[q_rows] - ref_pt)
        xk_b = k[k_rows] * jnp.exp2(ref_pt - cum[k_rows])
        xq.append(xq_b)
        xall.extend([xq_b, xk_b] if reverse else [xk_b, xq_b])
        q_groups.extend(range(q_rows.start // SUBLANES, q_rows.stop // SUBLANES))
    return (jnp.concatenate(xq, axis=0).astype(BF16), jnp.concatenate(xall, axis=0).astype(BF16),
            q_groups)


def _paired_dot_nt(lhs_a, rhs_a, lhs_b, rhs_b):
    rhs = jnp.concatenate([rhs_a, rhs_b], axis=1)
    lhs = jnp.concatenate(
        [jnp.concatenate([lhs_a, jnp.zeros_like(lhs_a)], axis=1),
         jnp.concatenate([jnp.zeros_like(lhs_b), lhs_b], axis=1)], axis=0)
    s = _dot_nt(lhs, rhs)
    return s[:lhs_a.shape[0]], s[lhs_a.shape[0]:]


def _assemble_scores(scores, level_id):
    C = HGRN_CHUNK
    rows = [jnp.zeros((SUBLANES, C), F32) for _ in range(C // SUBLANES)]
    for level, entry in enumerate(scores):
        if entry is None:
            continue
        s, q_groups = entry
        for i, g in enumerate(q_groups):
            lid = level_id[g * SUBLANES:(g + 1) * SUBLANES]
            rows[g] = jnp.where(lid == level, s[i * SUBLANES:(i + 1) * SUBLANES], rows[g])
    return jnp.concatenate(rows, axis=0)


def _hgrn_intra_pair(fwd, bwd, consts):
    C = HGRN_CHUNK
    args = ((fwd, consts[0], False), (bwd, consts[1], True))
    n_vpu = len(HGRN_VPU_LEVELS)
    scores = ([None] * n_vpu, [None] * n_vpu)
    for level in range(n_vpu, C.bit_length() - 1):
        ops = [_level_operands(level, q, k, cum, cum_ref, base, cst, rev)
               for (q, k, _, cum, cum_ref, base), cst, rev in args]
        s_f, s_b = _paired_dot_nt(ops[0][0], ops[0][1], ops[1][0], ops[1][1])
        scores[0].append((s_f, ops[0][2]))
        scores[1].append((s_b, ops[1][2]))
    outs = []
    for idx, ((q, k, v, cum, _, _), cst, rev) in enumerate(args):
        attn = _assemble_scores(scores[idx], cst[0])
        o = _dot(attn.astype(BF16), v.astype(BF16))
        o += jnp.sum(q * k, axis=-1, keepdims=True) * v
        outs.append(o + _near_pairs(q, k, v, cum, rev, cst[3]))
    return outs


def _near_pairs(q, k, v, cum, reverse, masks):
    C, d = q.shape
    shape3 = (C // SUBLANES, SUBLANES, d)
    out = jnp.zeros((C, d), F32)
    for offset, mask in enumerate(masks, start=1):
        valid = mask != 0
        shift = (SUBLANES - offset) if reverse else offset

        def key_row(x):
            return pltpu.roll(x.reshape(shape3), shift, 1).reshape(C, d)

        w = jnp.exp2(jnp.where(valid, cum - key_row(cum), 0.0))
        score = jnp.sum(q * key_row(k) * w, axis=-1, keepdims=True)
        out += jnp.where(valid, score * key_row(v), 0.0)
    return out


def _near_pair_masks(C, d, reverse, levels):
    pos = lax.broadcasted_iota(jnp.int32, (C, d), 0) % SUBLANES
    masks = []
    for offset in range(1, 2 ** (max(levels) + 1)):
        valid = jnp.zeros((C, d), jnp.int32)
        for level in levels:
            m, blk = 2 ** level, 2 ** (level + 1)
            r = pos % blk
            key = (r + offset) if reverse else (r - offset)
            if reverse:
                ok = (r < m) & (key >= m) & (key < blk)
            else:
                ok = (r >= m) & (key >= 0) & (key < m)
            valid = jnp.where(ok, 1, valid)
        masks.append(valid)
    return masks


def _hgrn_consts(reverse, d):
    C = HGRN_CHUNK
    ii = lax.broadcasted_iota(jnp.int32, (C, C), 0)
    jj = lax.broadcasted_iota(jnp.int32, (C, C), 1)
    diff = ii ^ jj
    level_id = jnp.full((C, C), -1, jnp.int32)
    n_levels = C.bit_length() - 1
    for level in range(n_levels):
        level_id = jnp.where((diff >> level) == 1, level, level_id)
    level_id = jnp.where((ii < jj) if reverse else (ii > jj), level_id, -1)
    rows = lax.broadcasted_iota(jnp.int32, (C, d), 0)
    row_in_group = lax.broadcasted_iota(jnp.int32, (SUBLANES, d), 0)
    signs = []
    for level in range(SUBLANES.bit_length() - 1):
        second = ((rows >> level) & 1) == 1
        is_query = jnp.logical_not(second) if reverse else second
        signs.append(jnp.where(is_query, 1.0, -1.0))
    return level_id, row_in_group, signs, _near_pair_masks(C, d, reverse, HGRN_VPU_LEVELS)


def _chunk_cumsum(x, reverse, row_in_group):
    rows, d = x.shape
    groups = rows // SUBLANES
    per_chunk = HGRN_CHUNK // SUBLANES
    y = x.reshape(groups, SUBLANES, d)
    step = 1
    while step < SUBLANES:
        rolled = pltpu.roll(y, (SUBLANES - step) if reverse else step, 1)
        valid = (row_in_group < SUBLANES - step) if reverse else (row_in_group >= step)
        y = y + jnp.where(valid, rolled, 0.0)
        step *= 2
    out = [None] * groups
    for c in range(rows // HGRN_CHUNK):
        order = range(c * per_chunk, (c + 1) * per_chunk)
        carry = None
        for g in (reversed(order) if reverse else order):
            yg = y[g] if carry is None else y[g] + carry
            out[g] = yg
            edge = 0 if reverse else SUBLANES - 1
            carry = jnp.broadcast_to(yg[edge:edge + 1, :], (SUBLANES, d))
    return jnp.concatenate(out, axis=0)


def _hgrn_kernel(xn_ref, wq_ref, wzf_ref, wzb_ref, wi_ref, wg_ref, lbf_ref, lbb_ref, nw_ref, o_ref,
                 q_s, v_s, g_s, acc_s, kf_s, kb_s, lff_s, lfb_s, cum_s, *, seq, dk, heads):
    C = HGRN_CHUNK
    R = seq // C
    row_tile = HGRN_ROW_TILE
    scale = dk ** -0.5

    def tile_rows(n):
        return pl.ds(pl.multiple_of(n * row_tile, row_tile), row_tile)

    def head_cols(h):
        return slice(h * dk, (h + 1) * dk)

    dirs = [((False, lbf_ref, kf_s.at[h], lff_s.at[h], cum_s.at[h, 0]),
             (True, lbb_ref, kb_s.at[h], lfb_s.at[h], cum_s.at[h, 1]))
            for h in range(heads)]
    row_in_group = lax.broadcasted_iota(jnp.int32, (1, SUBLANES, dk), 1)

    def proj(n, carry):
        rows = tile_rows(n)
        xc = xn_ref[rows, :]
        hq_all = _dot(xc, wq_ref[...])
        z_all = (_dot(xc, wzf_ref[...]), _dot(xc, wzb_ref[...]))
        v_all = _dot(xc, wi_ref[...])
        hg_all = _dot(xc, wg_ref[...])
        for h in range(heads):
            cols = head_cols(h)
            hq = hq_all[:, cols]
            q = _silu(hq, scale)
            q_s[h, rows, :] = q
            for idx, (reverse, lb_ref, k_s, lf_s, _) in enumerate(dirs[h]):
                lf, kk = _hgrn_gate(z_all[idx][:, cols], lb_ref[:, cols])
                k_s[rows, :] = kk
                lf_s[rows, :] = lf
            v_s[h, rows, :] = v_all[:, cols]
            hg = hg_all[:, cols]
            g_s[h, rows, :] = _silu(hg)
            acc_s[h, rows, :] = jnp.zeros((row_tile, dk), F32)
        return carry

    lax.fori_loop(0, seq // row_tile, proj, 0, unroll=2)

    consts = (_hgrn_consts(False, dk), _hgrn_consts(True, dk))

    def chunk_pair(h, cf, cb, states):
        data, rows, q_dec, k_dec, decs = [], [], [], [], []
        for c, (reverse, _, k_s, lf_s, cum_c) in zip((cf, cb), dirs[h]):
            r = pl.ds(pl.multiple_of(c * C, C), C)
            q, k = q_s[h, r, :], k_s[r, :]
            cum = _chunk_cumsum(lf_s[r, :], reverse, row_in_group)
            cum_c[...] = cum
            data.append((q, k, v_s[h, r, :], cum, cum_c, 0))
            rows.append(r)
            total = cum_c[pl.ds(0 if reverse else C - 1, 1), :]
            q_dec.append((q * jnp.exp2(cum)).astype(BF16))
            k_dec.append((k * jnp.exp2(total - cum)).astype(BF16))
            decs.append(jnp.exp2(total))
        o_f, o_b = _hgrn_intra_pair(data[0], data[1], consts)
        i_f, i_b = _paired_dot_nt(q_dec[0], states[0].astype(BF16), q_dec[1], states[1].astype(BF16))
        acc_s[h, rows[0], :] += o_f + i_f
        acc_s[h, rows[1], :] += o_b + i_b
        return tuple(st * dec + _dot_tn(d[2].astype(BF16), kd)
                     for st, dec, d, kd in zip(states, decs, data, k_dec))

    def step(i, states):
        return tuple(chunk_pair(h, i, R - 1 - i, states[h]) for h in range(heads))

    zero = jnp.zeros((dk, dk), F32)
    lax.fori_loop(0, R, step, tuple((zero, zero) for _ in range(heads)))

    def finish(n, carry):
        rows = tile_rows(n)
        for h in range(heads):
            o_ref[rows, head_cols(h)] = (_rms(acc_s[h, rows, :]) * nw_ref[...]
                                         * g_s[h, rows, :]).astype(o_ref.dtype)
        return carry

    lax.fori_loop(0, seq // row_tile, finish, 0)


def hgrn_branch(xn3, w_in_b, layer, hgrn_off, lb_f, lb_b, norm_w):
    B, S, D = xn3.shape
    dk = D // HGRN_HEADS
    hp = HGRN_HEADS_PER_STEP
    wide = hp * dk
    kern = functools.partial(_hgrn_kernel, seq=S, dk=dk, heads=hp)
    vec = lambda: pltpu.VMEM((hp, S, dk), F32)

    def w_spec(group):
        first = (hgrn_off + group * D) // wide
        return pl.BlockSpec((None, D, wide), lambda b, j: (layer, 0, first + j))

    lb_spec = pl.BlockSpec((1, wide), lambda b, j: (0, j))
    return pl.pallas_call(
        kern,
        out_shape=jax.ShapeDtypeStruct((B, S, D), BF16),
        grid=(B, HGRN_HEADS // hp),
        in_specs=[pl.BlockSpec((None, S, D), lambda b, j: (b, 0, 0)),
                  w_spec(0), w_spec(1), w_spec(2), w_spec(3), w_spec(4),
                  lb_spec, lb_spec,
                  pl.BlockSpec((1, dk), lambda b, j: (0, 0))],
        out_specs=pl.BlockSpec((None, S, wide), lambda b, j: (b, 0, j)),
        scratch_shapes=[vec(), vec(), vec(), vec(), vec(), vec(), vec(), vec(),
                        pltpu.VMEM((hp, 2, HGRN_CHUNK, dk), F32)],
        compiler_params=_params("parallel", "arbitrary"),
        name="hgrn2",
    )(xn3, w_in_b, w_in_b, w_in_b, w_in_b, w_in_b,
      lb_f.reshape(1, D), lb_b.reshape(1, D), norm_w.reshape(1, dk))


def _fnet_proj_kernel(xn_ref, w_ref, cs_ref, o_ref, *, gdim):
    fu = _dot(xn_ref[...], w_ref[...]).astype(BF16)
    for g in range(FNET_GROUPS):
        t = _dot(fu[:, g * gdim:(g + 1) * gdim], cs_ref[...])
        o_ref[0, :, g * gdim:(g + 1) * gdim] = t[:, :gdim].astype(o_ref.dtype)
        o_ref[1, :, g * gdim:(g + 1) * gdim] = t[:, gdim:].astype(o_ref.dtype)


def _seq_dft_kernel(cos_ref, sin_ref, perm_ref, rhs_ref, o_ref, fold_s, mir_s, *, seq):
    n = seq
    half = n // 2
    blk = perm_ref.shape[0]
    c0 = 1.0 / math.sqrt(n)

    for part, sign in ((0, 1.0), (1, -1.0)):
        base = part * n
        for j in range(half // blk):
            own = rhs_ref[pl.ds(base + j * blk, blk), :].astype(F32)
            if j == 0:
                mirror = _dot(perm_ref[:, :blk], rhs_ref[pl.ds(base + n - blk, blk), :])
            else:
                mirror = _dot(perm_ref[...], rhs_ref[pl.ds(base + n - (j + 1) * blk, 2 * blk), :])
            fold_s[pl.ds(part * half + j * blk, blk), :] = (own + sign * mirror).astype(fold_s.dtype)

    mid = rhs_ref[pl.ds(half, BF16_ROWS), :].astype(F32)[0:1] * c0
    p_ext = _dot(cos_ref[...], fold_s[pl.ds(0, half), :])
    q = _dot(sin_ref[...], fold_s[pl.ds(half, half), :])
    row = lax.broadcasted_iota(jnp.int32, q.shape, 0)
    p = p_ext[:half] + jnp.where((row & 1) == 0, 1.0, -1.0) * mid
    o_ref[pl.ds(0, half), :] = (p - q).astype(o_ref.dtype)

    mir_s[pl.ds(0, half), :] = (p + q).astype(mir_s.dtype)
    first = jnp.where(lax.broadcasted_iota(jnp.int32, (BF16_ROWS, q.shape[1]), 0) == 0, 1.0, 0.0)
    tail = (p_ext[half:half + BF16_ROWS] + mid) * first
    mir_s[pl.ds(half, BF16_ROWS), :] = tail.astype(mir_s.dtype)
    mir_s[pl.ds(half + BF16_ROWS, blk - BF16_ROWS), :] = jnp.zeros(
        (blk - BF16_ROWS, tail.shape[1]), mir_s.dtype)
    for j in range(half // blk):
        window = mir_s[pl.ds(half - (j + 1) * blk, 2 * blk), :]
        o_ref[pl.ds(half + j * blk, blk), :] = _dot(perm_ref[...], window).astype(o_ref.dtype)


DFT_FOLD_BLOCK = 128


def _half_dft_tables(n):
    c, s = _dft_tables(n)
    half = n // 2
    cos_ext = np.zeros((half + BF16_ROWS, half))
    cos_ext[:half + 1] = c[:half + 1, :half]
    return cos_ext, s[:half, :half]


def _mirror_permutation(blk):
    p = np.zeros((blk, 2 * blk), np.float32)
    i = np.arange(blk)
    p[i, blk - i] = 1.0
    return p


def _dft_tables(n):
    idx = np.arange(n, dtype=np.int64)
    ang = 2.0 * np.pi * ((idx[:, None] * idx[None, :]) % n).astype(np.float64) / n
    s = 1.0 / math.sqrt(n)
    return np.cos(ang) * s, np.sin(ang) * s


def fourier_branch(xn3, w_in_b, layer, fu_off_blocks, tm=512):
    B, S, D = xn3.shape
    W = D
    gdim = W // FNET_GROUPS
    c_small, s_small = _dft_tables(gdim)
    cs_small = jnp.asarray(np.concatenate([c_small, s_small], axis=1), dtype=BF16)
    half = S // 2
    assert half % 2 == 0 and half % DFT_FOLD_BLOCK == 0
    cos_np, sin_np = _half_dft_tables(S)
    dft_cos = jnp.asarray(cos_np, dtype=BF16)
    dft_sin = jnp.asarray(sin_np, dtype=BF16)
    perm = jnp.asarray(_mirror_permutation(DFT_FOLD_BLOCK), dtype=BF16)
    tiles = S // tm
    rhs = pl.pallas_call(
        functools.partial(_fnet_proj_kernel, gdim=gdim),
        out_shape=jax.ShapeDtypeStruct((B, 2, S, W), BF16),
        grid=(B, tiles),
        in_specs=[pl.BlockSpec((None, tm, D), lambda b, r: (b, r, 0)),
                  pl.BlockSpec((None, D, W), lambda b, r: (layer, 0, fu_off_blocks)),
                  pl.BlockSpec((gdim, 2 * gdim), lambda b, r: (0, 0))],
        out_specs=pl.BlockSpec((None, 2, tm, W), lambda b, r: (b, 0, r, 0)),
        compiler_params=_params("parallel", "parallel"),
        name="fnet_proj",
    )(xn3, w_in_b, cs_small)
    rhs = rhs.reshape(B, 2 * S, W)
    return pl.pallas_call(
        functools.partial(_seq_dft_kernel, seq=S),
        out_shape=jax.ShapeDtypeStruct((B, S, W), BF16),
        grid=(B,),
        in_specs=[_resident((half + BF16_ROWS, half), lambda b: (0, 0)),
                  _resident((half, half), lambda b: (0, 0)),
                  pl.BlockSpec((DFT_FOLD_BLOCK, 2 * DFT_FOLD_BLOCK), lambda b: (0, 0)),
                  pl.BlockSpec((None, 2 * S, W), lambda b: (b, 0, 0))],
        out_specs=pl.BlockSpec((None, S, W), lambda b: (b, 0, 0)),
        scratch_shapes=[pltpu.VMEM((S, W), BF16),
                        pltpu.VMEM((half + DFT_FOLD_BLOCK, W), BF16)],
        compiler_params=_params("parallel"),
        name="fnet_seq_dft",
    )(dft_cos, dft_sin, perm, rhs)


def _merge_kernel(x_ref, xn_ref, ro_ref, ho_ref, fo_ref, wga_ref, wro_ref, who_ref,
                  wf_ref, wout_ref, nw_ref, o_ref, *, d):
    xn = xn_ref[...]

    def gate(i):
        return jax.nn.sigmoid(_dot(xn, wga_ref[:, i * d:(i + 1) * d]))

    mix = gate(0) * _dot(ro_ref[...], wro_ref[...])
    mix += gate(1) * _dot(ho_ref[...], who_ref[...])
    mix += gate(2) * _dot(fo_ref[...], wf_ref[...])
    y = _dot(mix.astype(BF16), wout_ref[...])
    o_ref[...] = x_ref[...] + _rms(y) * nw_ref[...]


def merge_branches(x2, xn2, ro2, ho2, fo2, w_in_b, layer, ga_off_blocks,
                   w_ret_o, w_hgrn_o, w_fnet, w_out, norm_w, tm=512):
    T, D = x2.shape
    RV = ro2.shape[1]
    tile = lambda w: pl.BlockSpec((tm, w), lambda i: (i, 0))
    return pl.pallas_call(
        functools.partial(_merge_kernel, d=D),
        out_shape=jax.ShapeDtypeStruct((T, D), F32),
        grid=(T // tm,),
        in_specs=[tile(D), tile(D), tile(RV), tile(D), tile(D),
                  _resident((None, D, N_BRANCH * D), lambda i: (layer, 0, ga_off_blocks)),
                  _resident((None, RV, D), lambda i: (layer, 0, 0)),
                  _resident((None, D, D), lambda i: (layer, 0, 0)),
                  _resident((None, D, D), lambda i: (layer, 0, 0)),
                  _resident((None, D, D), lambda i: (layer, 0, 0)),
                  pl.BlockSpec((1, D), lambda i: (0, 0))],
        out_specs=tile(D),
        compiler_params=_params("parallel"),
        name="merge",
    )(x2, xn2, ro2, ho2, fo2, w_in_b, w_ret_o, w_hgrn_o, w_fnet, w_out, norm_w.reshape(1, D))


def _ffn_kernel(x_ref, xp_ref, xnx_ref, nw_in_ref, wup_ref, cw_ref, cb_ref, wdn_ref, nw_out_ref,
                *rest, tm, tiles_per_seq, d_ff, fc, emit_next):
    if emit_next:
        nw_next_ref, o_ref, xn_ref, hn_s, acc_s = rest
    else:
        (o_ref, hn_s, acc_s), nw_next_ref, xn_ref = rest, None, None
    i = pl.program_id(0)
    r = i % tiles_per_seq
    halo = BF16_ROWS
    x = x_ref[...]
    nw = nw_in_ref[...]
    hp = jnp.where(r == 0, 0.0, _rms(xp_ref[...]) * nw)
    hx = jnp.where(r == tiles_per_seq - 1, 0.0, _rms(xnx_ref[...]) * nw)
    hn = jnp.concatenate([hp, _rms(x) * nw, hx], axis=0).astype(BF16)
    n_ext = tm + 2 * halo
    hn_s[...] = hn
    acc_s[...] = jnp.zeros_like(acc_s)

    def conv(col, scale):
        cols = pl.ds(pl.multiple_of(col, fc), fc)
        h = _dot(hn_s[...], wup_ref[:, cols])
        cw = cw_ref[:, cols] * scale
        prev = pltpu.roll(h, 1, 0)[halo:halo + tm]
        nxt = pltpu.roll(h, n_ext - 1, 0)[halo:halo + tm]
        return (cb_ref[:, cols] * scale + prev * cw[0:1] + h[halo:halo + tm] * cw[1:2]
                + nxt * cw[2:3])

    def chunk(c, carry):
        gate = conv(c * fc, 1.0)
        half_up = conv(d_ff + c * fc, 0.5)
        inner = gate * (GELU_C0 + GELU_C1 * (gate * gate))
        act = (gate * (1.0 + jnp.tanh(inner)) * half_up).astype(BF16)
        acc_s[...] += _dot(act, wdn_ref[pl.ds(pl.multiple_of(c * fc, fc), fc), :])
        return carry

    for c in range(d_ff // fc):
        chunk(c, 0)
    y = x + _rms(acc_s[...]) * nw_out_ref[...]
    o_ref[...] = y
    if xn_ref is not None:
        xn_ref[...] = (_rms(y) * nw_next_ref[...]).astype(xn_ref.dtype)


def conv_ffn_block(x2, seq, w_up, conv_w, conv_b, w_down, nw_in, nw_out, layer, nw_next=None,
                   tm=1024, fc=256):
    T, D = x2.shape
    d_ff = w_down.shape[1]
    halo = BF16_ROWS
    tps = seq // tm
    hb = tm // halo
    n_hb = T // halo
    assert seq % tm == 0, "a row tile must not straddle two sequences (the conv zero-pads each)"
    emit_next = nw_next is not None
    kern = functools.partial(_ffn_kernel, tm=tm, tiles_per_seq=tps, d_ff=d_ff, fc=fc,
                             emit_next=emit_next)
    vec = pl.BlockSpec((1, D), lambda i: (0, 0))
    tile = pl.BlockSpec((tm, D), lambda i: (i, 0))
    in_specs = [tile,
                pl.BlockSpec((halo, D), lambda i: (jnp.maximum(i * hb - 1, 0), 0)),
                pl.BlockSpec((halo, D), lambda i: (jnp.minimum((i + 1) * hb, n_hb - 1), 0)),
                vec,
                _resident((None, D, 2 * d_ff), lambda i: (layer, 0, 0)),
                pl.BlockSpec((None, CONV_W, 2 * d_ff), lambda i: (layer, 0, 0)),
                pl.BlockSpec((None, 1, 2 * d_ff), lambda i: (layer, 0, 0)),
                _resident((None, d_ff, D), lambda i: (layer, 0, 0)),
                vec]
    args = [x2, x2, x2, nw_in.reshape(1, D), w_up, conv_w, conv_b, w_down, nw_out.reshape(1, D)]
    out_shape = jax.ShapeDtypeStruct((T, D), F32)
    out_specs = tile
    if emit_next:
        in_specs.append(vec)
        args.append(nw_next.reshape(1, D))
        out_shape = (out_shape, jax.ShapeDtypeStruct((T, D), BF16))
        out_specs = (tile, tile)
    return pl.pallas_call(
        kern,
        out_shape=out_shape,
        grid=(T // tm,),
        in_specs=in_specs,
        out_specs=out_specs,
        scratch_shapes=[pltpu.VMEM((tm + 2 * halo, D), BF16),
                        pltpu.VMEM((tm, D), F32)],
        compiler_params=_params("parallel"),
        name="conv_ffn",
    )(*args)


def kernel(x, positions, norm_w, w_in, hgrn_lb_logits, hgrn_norm_w, w_ret_o, w_hgrn_o,
           w_fnet, w_out, w_up, conv_w, conv_b, w_down):
    B, S, D = x.shape
    depth = w_in.shape[0]
    T = B * S

    hgrn_off = 2 * D + 2 * 2 * D
    fu_off = hgrn_off + 5 * D
    ga_off = fu_off + D

    w_in_b = w_in.astype(BF16)
    w_ret_o_b = w_ret_o.astype(BF16)
    w_hgrn_o_b = w_hgrn_o.astype(BF16)
    w_fnet_b = w_fnet.astype(BF16)
    w_out_b = w_out.astype(BF16)
    w_up_b = w_up.astype(BF16)
    w_down_b = w_down.astype(BF16)
    conv_b3 = conv_b.reshape(depth, 1, -1)

    log_gamma = jnp.log(1.0 - 2.0 ** (-5.0 - jnp.arange(RET_HEADS, dtype=F32)))
    p = jax.nn.softmax(hgrn_lb_logits.astype(F32), axis=1)
    lower_bounds = jnp.cumsum(p, axis=1) - p[:, :1]

    cos, sin, xn3 = rope_tables_and_norm(positions, D // RET_HEADS // 2, x, norm_w[0, 0])

    x2 = x.reshape(T, D)
    xn2 = xn3.reshape(T, D)
    for l in range(depth):
        xn3 = xn2.reshape(B, S, D)
        ro = retention_branch(xn3, w_in_b, l, cos, sin, log_gamma)
        ho = hgrn_branch(xn3, w_in_b, l, hgrn_off, lower_bounds[0, l], lower_bounds[1, l],
                         hgrn_norm_w[l])
        fo = fourier_branch(xn3, w_in_b, l, fu_off // D)
        x2 = merge_branches(x2, xn2, ro.reshape(T, -1), ho.reshape(T, D), fo.reshape(T, D),
                            w_in_b, l, ga_off // (N_BRANCH * D),
                            w_ret_o_b, w_hgrn_o_b, w_fnet_b, w_out_b, norm_w[l, 1])
        if l + 1 < depth:
            x2, xn2 = conv_ffn_block(x2, S, w_up_b, conv_w, conv_b3, w_down_b,
                                     norm_w[l, 2], norm_w[l, 3], l, nw_next=norm_w[l + 1, 0])
        else:
            x2 = conv_ffn_block(x2, S, w_up_b, conv_w, conv_b3, w_down_b,
                                norm_w[l, 2], norm_w[l, 3], l)
    return x2.reshape(B, S, D)
```
